```python
import math
import jax, jax.numpy as jnp
from jax import lax
import numpy as np

D_MODEL = 2048
BATCH = 1
SEQ = 8192
DEPTH = 2
DEC_BATCH = 16
DEC_SEQ = 32
PAST_LEN = 4096

CHUNK = 64
N_META = 16
N_MIXERS = 2
EXPAND = 2
D_INNER = EXPAND * D_MODEL
CONV_WIDTH = 3
SSM_GROUP = 16
N_SSM_GROUPS = D_INNER // SSM_GROUP
SSM_STATE = 64
N_CONV_LAYERS = (DEPTH + 1) // 2
N_SSM_LAYERS = DEPTH // 2
RMS_EPS = 1e-6
DT_MIN = 1e-3
DT_MAX = 1e-1

kernel_name = "hybrid_shortconv_s5_stream_step"


def rmsnorm(x, w):
    x32 = x.astype(jnp.float32)
    y = x32 * lax.rsqrt(jnp.mean(x32 * x32, axis=-1, keepdims=True) + RMS_EPS)
    return (y * w.astype(jnp.float32)).astype(x.dtype)


def conv_mixer(x, conv_state, w_in, conv_w, conv_b, w_out):
    T = x.shape[1]
    proj = x @ w_in
    b_gate, c_gate, v, z = jnp.split(proj, 4, axis=-1)
    cv = c_gate * v
    padded = jnp.concatenate([conv_state.astype(cv.dtype), cv], axis=1)
    conv = conv_b
    for k in range(CONV_WIDTH):
        conv = conv + conv_w[k] * padded[:, k:k + T]
    y = b_gate * conv * jax.nn.silu(z)
    return y @ w_out, padded[:, -(CONV_WIDTH - 1):]


def s5_discretize(a_re, a_im, b_re, b_im, log_dt):
    lam = lax.complex(a_re.astype(jnp.float32), a_im.astype(jnp.float32))
    dt = jnp.exp(log_dt.astype(jnp.float32))[:, None]
    a_bar = jnp.exp(lam * dt)
    b_bar = ((a_bar - 1.0) / lam)[..., None] * lax.complex(
        b_re.astype(jnp.float32), b_im.astype(jnp.float32))
    return a_bar, b_bar


def _lin_combine(left, right):
    a_l, b_l = left
    a_r, b_r = right
    return a_r * a_l, a_r * b_l + b_r


def s5_block(h0, u_blk, a_bar, b_bar, c_mat, d_vec):
    bu = lax.complex(jnp.einsum('btgc,gpc->btgp', u_blk, jnp.real(b_bar)),
                     jnp.einsum('btgc,gpc->btgp', u_blk, jnp.imag(b_bar)))
    bu = bu.at[:, 0].add(a_bar * h0)
    a = jnp.broadcast_to(a_bar, bu.shape)
    _, hs = lax.associative_scan(_lin_combine, (a, bu), axis=1)
    y = jnp.real(jnp.einsum('gcp,btgp->btgc', c_mat, hs)) + d_vec * u_blk
    return hs[:, -1], y


def s5_mixer(x, h0, fresh, w_in, a_bar, b_bar, c_mat, d_vec, w_glu, b_glu, w_out):
    bsz, T, _ = x.shape
    proj = x @ w_in
    u, z = jnp.split(proj, 2, axis=-1)
    u32 = u.astype(jnp.float32).reshape(bsz, T, N_SSM_GROUPS, SSM_GROUP)

    def body(h, u_blk):
        return s5_block(h, u_blk, a_bar, b_bar, c_mat, d_vec)

    if fresh:
        pad = (-T) % CHUNK
        u_pad = jnp.pad(u32, ((0, 0), (pad, 0), (0, 0), (0, 0)))
        nb = (T + pad) // CHUNK
        xs = jnp.swapaxes(u_pad.reshape(bsz, nb, CHUNK, N_SSM_GROUPS, SSM_GROUP), 0, 1)
        h_last, ys = lax.scan(body, h0, xs)
        y = jnp.swapaxes(ys, 0, 1).reshape(bsz, nb * CHUNK, N_SSM_GROUPS, SSM_GROUP)[:, pad:]
    else:
        h_last, y = body(h0, u32)
    y = y.reshape(bsz, T, D_INNER).astype(x.dtype)
    g = jax.nn.gelu(y)
    y = g * jax.nn.sigmoid(g @ w_glu + b_glu)
    out = (y * jax.nn.silu(z)) @ w_out
    return out, h_last


def setup_inputs(seed: int = 0) -> dict:
    key = jax.random.key(seed)
    ks = jax.random.split(key, 24)
    f32 = jnp.float32
    nrm = lambda k, shape, s: jax.random.normal(k, shape, f32) * s
    G, P, C, E, D = N_SSM_GROUPS, SSM_STATE, SSM_GROUP, D_INNER, D_MODEL
    a_im = jnp.broadcast_to(jnp.pi * jnp.arange(P, dtype=f32), (N_SSM_LAYERS, G, P))
    return {
        'x_prompt': nrm(ks[0], (BATCH, SEQ, D), 1.0),
        'x_sample': nrm(ks[1], (DEC_BATCH, DEC_SEQ, D), 1.0),
        'cache_conv': nrm(ks[2], (N_CONV_LAYERS, DEC_BATCH, CONV_WIDTH - 1, E), 1.0),
        'state_ssm_re': nrm(ks[3], (N_SSM_LAYERS, DEC_BATCH, G, P), 0.5),
        'state_ssm_im': nrm(ks[4], (N_SSM_LAYERS, DEC_BATCH, G, P), 0.5),
        'meta_tokens': nrm(ks[5], (N_META, D), 1.0),
        'norm_w': 1.0 + nrm(ks[6], (DEPTH, D), 0.02),
        'final_norm_w': 1.0 + nrm(ks[7], (D,), 0.02),
        'conv_w_in': nrm(ks[8], (N_CONV_LAYERS, D, 4 * E), D ** -0.5),
        'conv_w': nrm(ks[9], (N_CONV_LAYERS, CONV_WIDTH, E), CONV_WIDTH ** -0.5),
        'conv_b': nrm(ks[10], (N_CONV_LAYERS, E), 0.01),
        'conv_w_out': nrm(ks[11], (N_CONV_LAYERS, E, D), E ** -0.5),
        'ssm_w_in': nrm(ks[12], (N_SSM_LAYERS, D, 2 * E), D ** -0.5),
        'ssm_a_re': -0.5 * jnp.exp(nrm(ks[13], (N_SSM_LAYERS, G, P), 0.05)),
        'ssm_a_im': a_im + nrm(ks[14], (N_SSM_LAYERS, G, P), 0.01),
        'ssm_b_re': nrm(ks[15], (N_SSM_LAYERS, G, P, C), (2 * C) ** -0.5),
        'ssm_b_im': nrm(ks[16], (N_SSM_LAYERS, G, P, C), (2 * C) ** -0.5),
        'ssm_c_re': nrm(ks[17], (N_SSM_LAYERS, G, C, P), (2 * P) ** -0.5),
        'ssm_c_im': nrm(ks[18], (N_SSM_LAYERS, G, C, P), (2 * P) ** -0.5),
        'ssm_d': nrm(ks[19], (N_SSM_LAYERS, E), 0.5),
        'ssm_log_dt': jax.random.uniform(ks[20], (N_SSM_LAYERS, G), f32,
                                         math.log(DT_MIN), math.log(DT_MAX)),
        'ssm_w_glu': nrm(ks[21], (N_SSM_LAYERS, E, E), E ** -0.5),
        'ssm_b_glu': nrm(ks[22], (N_SSM_LAYERS, E), 0.01),
        'ssm_w_out': nrm(ks[23], (N_SSM_LAYERS, E, D), E ** -0.5),
    }


def reference(x_prompt, x_sample, cache_conv, state_ssm_re, state_ssm_im, meta_tokens,
              norm_w, final_norm_w, conv_w_in, conv_w, conv_b, conv_w_out, ssm_w_in,
              ssm_a_re, ssm_a_im, ssm_b_re, ssm_b_im, ssm_c_re, ssm_c_im, ssm_d,
              ssm_log_dt, ssm_w_glu, ssm_b_glu, ssm_w_out):
    n_prompt = x_prompt.shape[0]
    meta = jnp.broadcast_to(meta_tokens.astype(x_prompt.dtype)[None], (n_prompt, N_META, D_MODEL))
    hp = jnp.concatenate([meta, x_prompt], axis=1)
    hs = x_sample
    conv_p, conv_s, ssm_p, ssm_s = [], [], [], []
    for i in range(DEPTH):
        j = i // N_MIXERS
        xp_n = rmsnorm(hp, norm_w[i])
        xs_n = rmsnorm(hs, norm_w[i])
        if i % N_MIXERS == 0:
            zero_state = jnp.zeros((n_prompt, CONV_WIDTH - 1, D_INNER), hp.dtype)
            op, st_p = conv_mixer(xp_n, zero_state, conv_w_in[j], conv_w[j], conv_b[j], conv_w_out[j])
            os_, st_s = conv_mixer(xs_n, cache_conv[j], conv_w_in[j], conv_w[j], conv_b[j], conv_w_out[j])
            conv_p.append(st_p)
            conv_s.append(st_s)
        else:
            a_bar, b_bar = s5_discretize(ssm_a_re[j], ssm_a_im[j], ssm_b_re[j], ssm_b_im[j], ssm_log_dt[j])
            c_mat = lax.complex(ssm_c_re[j].astype(jnp.float32), ssm_c_im[j].astype(jnp.float32))
            d_vec = ssm_d[j].astype(jnp.float32).reshape(N_SSM_GROUPS, SSM_GROUP)
            h0p = jnp.zeros((n_prompt, N_SSM_GROUPS, SSM_STATE), jnp.complex64)
            h0s = lax.complex(state_ssm_re[j].astype(jnp.float32), state_ssm_im[j].astype(jnp.float32))
            op, st_p = s5_mixer(xp_n, h0p, True, ssm_w_in[j], a_bar, b_bar, c_mat, d_vec,
                                ssm_w_glu[j], ssm_b_glu[j], ssm_w_out[j])
            os_, st_s = s5_mixer(xs_n, h0s, False, ssm_w_in[j], a_bar, b_bar, c_mat, d_vec,
                                 ssm_w_glu[j], ssm_b_glu[j], ssm_w_out[j])
            ssm_p.append(st_p)
            ssm_s.append(st_s)
        hp = hp + op
        hs = hs + os_
    y_prompt = rmsnorm(hp, final_norm_w)[:, N_META:]
    y_sample = rmsnorm(hs, final_norm_w)
    new_conv_prompt = jnp.stack(conv_p)
    new_conv_sample = jnp.stack(conv_s)
    h_p = jnp.stack(ssm_p)
    h_s = jnp.stack(ssm_s)
    return (y_prompt, y_sample, new_conv_prompt, new_conv_sample,
            jnp.real(h_p), jnp.imag(h_p), jnp.real(h_s), jnp.imag(h_s))
```

```python
import functools

import jax
import jax.numpy as jnp
from jax import lax
from jax.experimental import pallas as pl
from jax.experimental.pallas import tpu as pltpu

F32 = jnp.float32
BF16 = jnp.bfloat16

D_MODEL = 2048
D_INNER = 4096
N_META = 16
SSM_GROUP = 16
N_GROUPS = D_INNER // SSM_GROUP
SSM_STATE = 64
RMS_EPS = 1e-6

CHUNK = 16
LANES = 128
TILE = CHUNK * LANES
SPECIAL_ROWS = 768
SPECIAL_CHUNKS = SPECIAL_ROWS // CHUNK

TE_CONV = 256
TK_OUT = 512
TE_SSM = 256
TE_GLU = 512
LB_GLU = 512

VMEM_LIMIT = 56 * 1024 * 1024


def _params(n_axes, vmem=VMEM_LIMIT):
    return pltpu.CompilerParams(dimension_semantics=("arbitrary",) * n_axes,
                                vmem_limit_bytes=vmem)


def _conv_proj_kernel(*refs, tm, te, special):
    if special:
        (x_ref, nw_ref, w_ref, cw_ref, cb_ref, init_ref, c0_ref, c1_ref,
         y_ref, cv_ref, xn_scr, s_scr, carry_scr, p1_scr, p2_scr) = refs
    else:
        (x_ref, nw_ref, w_ref, cw_ref, cb_ref, init_ref,
         y_ref, tail_ref, xn_scr, s_scr, carry_scr) = refs
    i = pl.program_id(0)
    j = pl.program_id(1)

    @pl.when(j == 0)
    def _():
        x = x_ref[...]
        ms = jnp.mean(x * x, axis=-1, keepdims=True)
        xn_scr[...] = (x * lax.rsqrt(ms + RMS_EPS) * nw_ref[...]).astype(BF16)

    proj = jnp.dot(xn_scr[...], w_ref[...], preferred_element_type=F32)
    bg = proj[:, 0 * te:1 * te]
    cg = proj[:, 1 * te:2 * te]
    vv = proj[:, 2 * te:3 * te]
    zz = proj[:, 3 * te:4 * te]
    cv = cg * vv

    @pl.when(i == 0)
    def _():
        s_scr[0:8, :] = init_ref[...]

    @pl.when(i > 0)
    def _():
        s_scr[0:8, :] = carry_scr[j]

    s_scr[8:8 + tm, :] = cv
    if special:
        cv_ref[...] = cv
        p1_scr[...] = s_scr[7:7 + tm, :]
        p2_scr[...] = s_scr[6:6 + tm, :]
        for q in range(c0_ref.shape[0]):
            r0 = N_META + 32 * q
            p1_scr[r0:r0 + 1, :] = c1_ref[q:q + 1, :]
            p2_scr[r0:r0 + 1, :] = c0_ref[q:q + 1, :]
            p2_scr[r0 + 1:r0 + 2, :] = c1_ref[q:q + 1, :]
        p1 = p1_scr[...]
        p2 = p2_scr[...]
    else:
        p1 = s_scr[7:7 + tm, :]
        p2 = s_scr[6:6 + tm, :]
        tail = s_scr[tm:tm + 8, :]
        carry_scr[j] = tail
        tail_ref[...] = tail

    conv = cb_ref[...] + cw_ref[0:1, :] * p2
    conv = conv + cw_ref[1:2, :] * p1
    conv = conv + cw_ref[2:3, :] * cv
    y_ref[...] = (bg * conv * jax.nn.silu(zz)).astype(BF16)


def _conv_proj(x, norm_w, w_r, conv_w8, conv_b, init8, cache0=None, cache1=None, *, tm):
    rows = x.shape[0]
    te = TE_CONV
    n_j = D_INNER // te
    n_i = rows // tm
    special = cache0 is not None
    in_specs = [
        pl.BlockSpec((tm, D_MODEL), lambda i, j: (i, 0)),
        pl.BlockSpec((1, D_MODEL), lambda i, j: (0, 0)),
        pl.BlockSpec((None, D_MODEL, 4 * te), lambda i, j: (j, 0, 0)),
        pl.BlockSpec((8, te), lambda i, j: (0, j)),
        pl.BlockSpec((1, te), lambda i, j: (0, j)),
        pl.BlockSpec((8, te), lambda i, j: (0, j)),
    ]
    args = [x, norm_w, w_r, conv_w8, conv_b, init8]
    scratch = [pltpu.VMEM((tm, D_MODEL), BF16),
               pltpu.VMEM((tm + 8, te), F32),
               pltpu.VMEM((n_j, 8, te), F32)]
    if special:
        assert n_i == 1
        nq = cache0.shape[0]
        in_specs += [pl.BlockSpec((nq, te), lambda i, j: (0, j)),
                     pl.BlockSpec((nq, te), lambda i, j: (0, j))]
        args += [cache0, cache1]
        out_shape = (jax.ShapeDtypeStruct((rows, D_INNER), BF16),
                     jax.ShapeDtypeStruct((rows, D_INNER), F32))
        out_specs = (pl.BlockSpec((tm, te), lambda i, j: (i, j)),
                     pl.BlockSpec((tm, te), lambda i, j: (i, j)))
        scratch += [pltpu.VMEM((tm, te), F32), pltpu.VMEM((tm, te), F32)]
    else:
        out_shape = (jax.ShapeDtypeStruct((rows, D_INNER), BF16),
                     jax.ShapeDtypeStruct((n_i, 8, D_INNER), F32))
        out_specs = (pl.BlockSpec((tm, te), lambda i, j: (i, j)),
                     pl.BlockSpec((None, 8, te), lambda i, j: (i, 0, j)))
    return pl.pallas_call(
        functools.partial(_conv_proj_kernel, tm=tm, te=te, special=special),
        grid=(n_i, n_j), in_specs=in_specs, out_specs=out_specs, out_shape=out_shape,
        scratch_shapes=scratch, compiler_params=_params(2),
        name="conv_proj_special" if special else "conv_proj",
    )(*args)


def _out_proj_kernel(y_ref, w_ref, x_ref, nw_ref, h_ref, xs_ref, acc_scr, hn_scr, *, n_chunks):
    k = pl.program_id(1)

    @pl.when(k == 0)
    def _():
        acc_scr[...] = jnp.zeros_like(acc_scr)

    acc_scr[...] += jnp.dot(y_ref[...], w_ref[...], preferred_element_type=F32)

    @pl.when(k == pl.num_programs(1) - 1)
    def _():
        h = x_ref[...] + acc_scr[...]
        h_ref[...] = h
        ms = jnp.mean(h * h, axis=-1, keepdims=True)
        hn = h * lax.rsqrt(ms + RMS_EPS) * nw_ref[...]
        for k in range(D_MODEL // LANES):
            hn_scr[k] = hn[:, LANES * k:LANES * (k + 1)]
        for s in range(CHUNK):
            for k in range(D_MODEL // LANES):
                xs_ref[s, :, LANES * k:LANES * (k + 1)] = (
                    hn_scr[k, pl.ds(s, n_chunks, stride=CHUNK), :].astype(BF16))


def _out_proj(y, w_out, x, norm_w, *, tm, lanes):
    rows = x.shape[0]
    n_i = rows // tm
    n_k = D_INNER // TK_OUT
    n_chunks = tm // CHUNK
    per_tile = lanes // n_chunks
    n_tiles = n_i // per_tile
    xs_block = (None, CHUNK, n_chunks, D_MODEL)
    xs_map = lambda i, k: (i // per_tile, 0, i % per_tile, 0)
    return pl.pallas_call(
        functools.partial(_out_proj_kernel, n_chunks=n_chunks),
        grid=(n_i, n_k),
        in_specs=[
            pl.BlockSpec((tm, TK_OUT), lambda i, k: (i, k)),
            pl.BlockSpec((TK_OUT, D_MODEL), lambda i, k: (k, 0)),
            pl.BlockSpec((tm, D_MODEL), lambda i, k: (i, 0)),
            pl.BlockSpec((1, D_MODEL), lambda i, k: (0, 0)),
        ],
        out_specs=(pl.BlockSpec((tm, D_MODEL), lambda i, k: (i, 0)),
                   pl.BlockSpec(xs_block, xs_map)),
        out_shape=(jax.ShapeDtypeStruct((rows, D_MODEL), F32),
                   jax.ShapeDtypeStruct((n_tiles, CHUNK, lanes, D_MODEL), BF16)),
        scratch_shapes=[pltpu.VMEM((tm, D_MODEL), F32),
                        pltpu.VMEM((D_MODEL // LANES, tm, LANES), F32)],
        compiler_params=_params(2),
        name="out_proj",
    )(y, w_out, x, norm_w)


def _cmul(ar, ai, br, bi):
    return ar * br - ai * bi, ar * bi + ai * br


def _ssm_kernel(xs_ref, w_ref, tkx_ref, cout_ref, are_ref, aim_ref, sre_ref, sim_ref,
                gt_ref, szt_ref, h0re_ref, h0im_ref, hlre_ref, hlim_ref,
                zt_scr, cre_scr, cim_scr, *, te, n_tiles):
    i = pl.program_id(1)
    gb = te // SSM_GROUP
    rows = gb * SSM_STATE

    uz = lax.dot_general(w_ref[...], xs_ref[...], (((1,), (1,)), ((), ())),
                         preferred_element_type=F32)
    zt = uz[te:, :]
    szt_ref[...] = jax.nn.silu(zt).astype(BF16)
    ub = uz[:te, :].astype(BF16)
    for s in range(CHUNK):
        zt_scr[:, CHUNK * s:CHUNK * (s + 1), :] = (
            ub[:, LANES * s:LANES * (s + 1)].reshape(gb, SSM_GROUP, LANES))

    yx = lax.dot_general(tkx_ref[...], zt_scr[...], (((2,), (1,)), ((0,), (0,))),
                         preferred_element_type=F32)
    xre = yx[:, 256:320, :].reshape(rows, LANES)
    xim = yx[:, 320:384, :].reshape(rows, LANES)

    lane = lax.broadcasted_iota(jnp.int32, (1, LANES), 1)
    first = i == 0
    n_sample_lanes = 2 * sre_ref.shape[1]
    last_valid = jnp.where(first, n_sample_lanes, LANES - 1)
    start = (lane == 0) | (((lane & 1) == 1) & (lane < jnp.where(first, n_sample_lanes, 0)))
    valid = lane <= last_valid

    are = jnp.broadcast_to(are_ref[...], (rows, LANES))
    aim = jnp.broadcast_to(aim_ref[...], (rows, LANES))

    @pl.when(first)
    def _():
        ipre = jnp.zeros((rows, LANES), F32)
        ipim = jnp.zeros((rows, LANES), F32)
        for q in range(sre_ref.shape[1]):
            sel = lane == 2 * q + 1
            ipre = jnp.where(sel, sre_ref[:, q:q + 1], ipre)
            ipim = jnp.where(sel, sim_ref[:, q:q + 1], ipim)
        cre_scr[...] = ipre
        cim_scr[...] = ipim

    ipre = jnp.where(start, cre_scr[...], 0.0)
    ipim = jnp.where(start, cim_scr[...], 0.0)
    tr, ti = _cmul(are, aim, ipre, ipim)
    hre = xre + tr
    him = xim + ti

    pre, pim = are, aim
    shift = 1
    while shift < LANES:
        if shift == 1:
            ok = (lane >= 1) & valid & ((lane & 1) <= jnp.where(first, 0, 1))
        else:
            ok = lane >= jnp.where(first, LANES, shift)
        rre = pltpu.roll(hre, shift, 1)
        rim = pltpu.roll(him, shift, 1)
        tr, ti = _cmul(pre, pim, rre, rim)
        hre = hre + jnp.where(ok, tr, 0.0)
        him = him + jnp.where(ok, ti, 0.0)
        pre, pim = _cmul(pre, pim, pre, pim)
        shift *= 2

    pvre = jnp.where(start, ipre, pltpu.roll(hre, 1, 1))
    pvim = jnp.where(start, ipim, pltpu.roll(him, 1, 1))
    pvre = jnp.where(valid, pvre, 0.0)
    pvim = jnp.where(valid, pvim, 0.0)

    @pl.when(first)
    def _():
        h0re_ref[...] = hre
        h0im_ref[...] = him
        cre_scr[...] = hre
        cim_scr[...] = him

    @pl.when(jnp.logical_not(first))
    def _():
        cre_scr[...] = pltpu.roll(hre, 1, 1)
        cim_scr[...] = pltpu.roll(him, 1, 1)

    @pl.when(i == n_tiles - 1)
    def _():
        hlre_ref[...] = hre
        hlim_ref[...] = him

    hp = jnp.concatenate([pvre.reshape(gb, SSM_STATE, LANES),
                          pvim.reshape(gb, SSM_STATE, LANES)], axis=1).astype(BF16)
    ycorr = lax.dot_general(cout_ref[...], hp, (((2,), (1,)), ((0,), (0,))),
                            preferred_element_type=F32)
    y = yx[:, 0:256, :] + ycorr
    g = jax.nn.gelu(y)
    for t in range(CHUNK):
        gt_ref[:, LANES * t:LANES * (t + 1)] = (
            g[:, SSM_GROUP * t:SSM_GROUP * (t + 1), :].reshape(te, LANES).astype(BF16))


def _ssm(xs_all, wuz, tkx, cout, are, aim, sre, sim):
    n_tiles = xs_all.shape[0]
    te = TE_SSM
    n_j = D_INNER // te
    gb = te // SSM_GROUP
    rows = gb * SSM_STATE
    nq = sre.shape[1]
    st_shape = jax.ShapeDtypeStruct((N_GROUPS * SSM_STATE, LANES), F32)
    st_spec = pl.BlockSpec((rows, LANES), lambda j, i: (j, 0))
    act_shape = jax.ShapeDtypeStruct((n_tiles, D_INNER, TILE), BF16)
    act_spec = pl.BlockSpec((None, te, TILE), lambda j, i: (i, j, 0))
    once = pl.Buffered(1)
    return pl.pallas_call(
        functools.partial(_ssm_kernel, te=te, n_tiles=n_tiles),
        grid=(n_j, n_tiles),
        in_specs=[
            pl.BlockSpec((None, TILE, D_MODEL), lambda j, i: (i, 0, 0)),
            pl.BlockSpec((None, 2 * te, D_MODEL), lambda j, i: (j, 0, 0), pipeline_mode=once),
            pl.BlockSpec((gb, 384, 256), lambda j, i: (j, 0, 0), pipeline_mode=once),
            pl.BlockSpec((gb, 256, 128), lambda j, i: (j, 0, 0), pipeline_mode=once),
            pl.BlockSpec((rows, 1), lambda j, i: (j, 0), pipeline_mode=once),
            pl.BlockSpec((rows, 1), lambda j, i: (j, 0), pipeline_mode=once),
            pl.BlockSpec((rows, nq), lambda j, i: (j, 0), pipeline_mode=once),
            pl.BlockSpec((rows, nq), lambda j, i: (j, 0), pipeline_mode=once),
        ],
        out_specs=(act_spec, act_spec, st_spec, st_spec, st_spec, st_spec),
        out_shape=(act_shape, act_shape, st_shape, st_shape, st_shape, st_shape),
        scratch_shapes=[pltpu.VMEM((gb, 256, LANES), BF16),
                        pltpu.VMEM((rows, LANES), F32),
                        pltpu.VMEM((rows, LANES), F32)],
        compiler_params=_params(2),
        name="s5_scan",
    )(xs_all, wuz, tkx, cout, are, aim, sre, sim)


def _glu_out_kernel(gt_ref, szt_ref, wg_ref, bg_ref, wo_ref, h_ref, nw_ref, o_ref, acc_scr, *,
                    te, lb):
    e = pl.program_id(2)

    @pl.when(e == 0)
    def _():
        acc_scr[...] = jnp.zeros_like(acc_scr)

    gate = jnp.dot(wg_ref[...], gt_ref[...], preferred_element_type=F32) + bg_ref[...]
    ge = gt_ref[pl.ds(pl.multiple_of(e * te, te), te), :].astype(F32)
    y3 = (ge * jax.nn.sigmoid(gate)) * szt_ref[...].astype(F32)
    acc_scr[...] += jnp.dot(wo_ref[...], y3.astype(BF16), preferred_element_type=F32)

    @pl.when(e == pl.num_programs(2) - 1)
    def _():
        o = acc_scr[...].T
        for t in range(lb // LANES):
            h = h_ref[:, D_MODEL * t:D_MODEL * (t + 1)] + o[LANES * t:LANES * (t + 1), :]
            ms = jnp.mean(h * h, axis=-1, keepdims=True)
            o_ref[:, D_MODEL * t:D_MODEL * (t + 1)] = h * lax.rsqrt(ms + RMS_EPS) * nw_ref[...]


def _glu_out(gt, szt, wg_t, b_glu, wo_t, h_view, norm_w, *, n_tiles, tile_off):
    te = TE_GLU
    lb = LB_GLU
    n_l = TILE // lb
    n_e = D_INNER // te
    hw = (lb // LANES) * D_MODEL
    return pl.pallas_call(
        functools.partial(_glu_out_kernel, te=te, lb=lb),
        grid=(n_tiles, n_l, n_e),
        in_specs=[
            pl.BlockSpec((None, D_INNER, lb), lambda i, l, e: (i + tile_off, 0, l)),
            pl.BlockSpec((None, te, lb), lambda i, l, e: (i + tile_off, e, l)),
            pl.BlockSpec((te, D_INNER), lambda i, l, e: (e, 0)),
            pl.BlockSpec((te, 1), lambda i, l, e: (e, 0)),
            pl.BlockSpec((D_MODEL, te), lambda i, l, e: (0, e)),
            pl.BlockSpec((LANES, hw), lambda i, l, e: (i, l)),
            pl.BlockSpec((1, D_MODEL), lambda i, l, e: (0, 0)),
        ],
        out_specs=pl.BlockSpec((LANES, hw), lambda i, l, e: (i, l)),
        out_shape=jax.ShapeDtypeStruct((n_tiles * LANES, CHUNK * D_MODEL), F32),
        scratch_shapes=[pltpu.VMEM((D_MODEL, lb), F32)],
        compiler_params=_params(3),
        name="glu_out",
    )(gt, szt, wg_t, b_glu, wo_t, h_view, norm_w)


def _ssm_operators(a_re, a_im, b_re, b_im, c_re, c_im, d_vec, log_dt):
    hi = lax.Precision.HIGHEST
    lam = lax.complex(a_re.astype(F32), a_im.astype(F32))
    dt = jnp.exp(log_dt.astype(F32))[:, None]
    a_bar = jnp.exp(lam * dt)
    b_bar = ((a_bar - 1.0) / lam)[..., None] * lax.complex(b_re.astype(F32), b_im.astype(F32))
    c_mat = lax.complex(c_re.astype(F32), c_im.astype(F32))
    taus = jnp.arange(CHUNK + 1, dtype=F32)
    apow = jnp.exp((lam * dt)[None] * taus[:, None, None])

    ca = c_mat[:, None] * apow[:CHUNK].transpose(1, 0, 2)[:, :, None, :]
    kk = (jnp.einsum('gtcp,gpd->gtcd', jnp.real(ca), jnp.real(b_bar), precision=hi)
          - jnp.einsum('gtcp,gpd->gtcd', jnp.imag(ca), jnp.imag(b_bar), precision=hi))
    eye = jnp.eye(SSM_GROUP, dtype=F32)
    kk = kk.at[:, 0].add(eye[None] * d_vec.astype(F32).reshape(N_GROUPS, SSM_GROUP)[:, :, None])
    tt = jnp.arange(CHUNK)[:, None]
    ss = jnp.arange(CHUNK)[None, :]
    lag = tt - ss
    kt = kk[:, jnp.clip(lag, 0, CHUNK - 1)]
    kt = jnp.where((lag >= 0)[None, :, :, None, None], kt, 0.0)
    tk_t = kt.transpose(0, 1, 3, 2, 4).reshape(N_GROUPS, 256, 256)

    wb = apow[CHUNK - 1 - jnp.arange(CHUNK)][..., None] * b_bar[None]
    wb = wb.transpose(1, 2, 0, 3).reshape(N_GROUPS, SSM_STATE, 256)
    w_t = jnp.concatenate([jnp.real(wb), jnp.imag(wb)], axis=1)
    tkx = jnp.concatenate([tk_t, w_t], axis=1).astype(BF16)

    co = c_mat[None] * apow[1:CHUNK + 1][:, :, None, :]
    co = co.transpose(1, 0, 2, 3).reshape(N_GROUPS, 256, SSM_STATE)
    cout = jnp.concatenate([jnp.real(co), -jnp.imag(co)], axis=2).astype(BF16)

    alpha = apow[CHUNK]
    n_rows = N_GROUPS * SSM_STATE
    return (tkx, cout, jnp.real(alpha).reshape(n_rows, 1), jnp.imag(alpha).reshape(n_rows, 1))


def kernel(x_prompt, x_sample, cache_conv, state_ssm_re, state_ssm_im, meta_tokens, norm_w,
           final_norm_w, conv_w_in, conv_w, conv_b, conv_w_out, ssm_w_in, ssm_a_re, ssm_a_im,
           ssm_b_re, ssm_b_im, ssm_c_re, ssm_c_im, ssm_d, ssm_log_dt, ssm_w_glu, ssm_b_glu,
           ssm_w_out):
    n_seq, seq_len = x_sample.shape[0], x_sample.shape[1]
    n_prompt_rows = x_prompt.shape[1]
    n_ptiles = n_prompt_rows // TILE
    n_sample_rows = n_seq * seq_len
    assert x_prompt.shape[0] == 1 and seq_len == 2 * CHUNK and N_META == CHUNK
    assert n_prompt_rows % TILE == 0 and N_META + n_sample_rows <= SPECIAL_ROWS

    n_jc = D_INNER // TE_CONV
    w_in_r = (conv_w_in[0].reshape(D_MODEL, 4, n_jc, TE_CONV).transpose(2, 0, 1, 3)
              .reshape(n_jc, D_MODEL, 4 * TE_CONV).astype(BF16))
    w_out0 = conv_w_out[0].astype(BF16)
    conv_w8 = jnp.zeros((8, D_INNER), F32).at[0:3].set(conv_w[0].astype(F32))
    conv_b2 = conv_b[0].astype(F32).reshape(1, D_INNER)
    n_js = D_INNER // TE_SSM
    wuz = (ssm_w_in[0].T.reshape(2, n_js, TE_SSM, D_MODEL).transpose(1, 0, 2, 3)
           .reshape(n_js, 2 * TE_SSM, D_MODEL).astype(BF16))
    wg_t = ssm_w_glu[0].T.astype(BF16)
    wo_t = ssm_w_out[0].T.astype(BF16)
    b_glu = ssm_b_glu[0].astype(F32).reshape(D_INNER, 1)
    nw0 = norm_w[0].astype(F32).reshape(1, D_MODEL)
    nw1 = norm_w[1].astype(F32).reshape(1, D_MODEL)
    nwf = final_norm_w.astype(F32).reshape(1, D_MODEL)
    tkx, cout, are, aim = _ssm_operators(ssm_a_re[0], ssm_a_im[0], ssm_b_re[0], ssm_b_im[0],
                                         ssm_c_re[0], ssm_c_im[0], ssm_d[0], ssm_log_dt[0])
    n_rows = N_GROUPS * SSM_STATE
    sre = state_ssm_re[0].astype(F32).reshape(n_seq, n_rows).T
    sim = state_ssm_im[0].astype(F32).reshape(n_seq, n_rows).T

    xp = x_prompt[0].astype(F32)
    x_sp = jnp.concatenate([
        meta_tokens.astype(F32), x_sample.astype(F32).reshape(n_sample_rows, D_MODEL),
        jnp.zeros((SPECIAL_ROWS - N_META - n_sample_rows, D_MODEL), F32)], axis=0)

    zeros8 = jnp.zeros((8, D_INNER), F32)
    y_sp, cv_sp = _conv_proj(x_sp, nw0, w_in_r, conv_w8, conv_b2, zeros8,
                             cache_conv[0, :, 0, :].astype(F32), cache_conv[0, :, 1, :].astype(F32),
                             tm=SPECIAL_ROWS)
    init8 = zeros8.at[6:8].set(cv_sp[N_META - 2:N_META])
    y_p, tail_p = _conv_proj(xp, nw0, w_in_r, conv_w8, conv_b2, init8, tm=512)
    h1_sp, xs_sp = _out_proj(y_sp, w_out0, x_sp, nw1, tm=SPECIAL_ROWS, lanes=SPECIAL_CHUNKS)
    h1_p, xs_p = _out_proj(y_p, w_out0, xp, nw1, tm=512, lanes=LANES)

    xs_sp = jnp.pad(xs_sp, ((0, 0), (0, 0), (0, LANES - SPECIAL_CHUNKS), (0, 0)))
    xs_all = jnp.concatenate([xs_sp, xs_p], axis=0).reshape(1 + n_ptiles, TILE, D_MODEL)
    gt, szt, h0re, h0im, hlre, hlim = _ssm(xs_all, wuz, tkx, cout, are, aim, sre, sim)
    h1_sp_view = jnp.pad(h1_sp, ((0, TILE - SPECIAL_ROWS), (0, 0))).reshape(LANES, CHUNK * D_MODEL)
    o_sp = _glu_out(gt, szt, wg_t, b_glu, wo_t, h1_sp_view, nwf, n_tiles=1, tile_off=0)
    o_p = _glu_out(gt, szt, wg_t, b_glu, wo_t, h1_p.reshape(n_ptiles * LANES, CHUNK * D_MODEL),
                   nwf, n_tiles=n_ptiles, tile_off=1)

    y_prompt = o_p.reshape(1, n_prompt_rows, D_MODEL)
    n_sc = n_sample_rows // CHUNK
    y_sample = o_sp[1:1 + n_sc].reshape(n_seq, seq_len, D_MODEL)
    new_conv_prompt = tail_p[-1, 6:8].reshape(1, 1, 2, D_INNER)
    cv_s = cv_sp[N_META:N_META + n_sample_rows].reshape(n_seq, seq_len, D_INNER)
    new_conv_sample = cv_s[:, seq_len - 2:].reshape(1, n_seq, 2, D_INNER)
    st_shape = (1, 1, N_GROUPS, SSM_STATE)
    ssm_re_p = hlre[:, LANES - 1].reshape(st_shape)
    ssm_im_p = hlim[:, LANES - 1].reshape(st_shape)
    ssm_re_s = h0re[:, 2:2 * n_seq + 1:2].T.reshape(1, n_seq, N_GROUPS, SSM_STATE)
    ssm_im_s = h0im[:, 2:2 * n_seq + 1:2].T.reshape(1, n_seq, N_GROUPS, SSM_STATE)
    return (y_prompt, y_sample, new_conv_prompt, new_conv_sample,
            ssm_re_p, ssm_im_p, ssm_re_s, ssm_im_s)
```

```python
import functools

import jax
import jax.numpy as jnp
from jax import lax
from jax.experimental import pallas as pl
from jax.experimental.pallas import tpu as pltpu

F32 = jnp.float32
BF16 = jnp.bfloat16

D_MODEL = 2048
D_INNER = 4096
N_META = 16
SSM_GROUP = 16
N_GROUPS = D_INNER // SSM_GROUP
SSM_STATE = 64
N_STATE_ROWS = N_GROUPS * SSM_STATE
RMS_EPS = 1e-6

CHUNK = 16
LANES = 128
TILE = CHUNK * LANES
SPECIAL_ROWS = 1024

TM_CONV = 1024
TE_CONV = 256
TM_OUT = 512
TE_SSM = 256
TE_GLU = 512
LB_GLU = 512
PREP_BLOCK = 1024

VMEM_LIMIT = 56 * 1024 * 1024


def _params(n_axes, vmem=VMEM_LIMIT):
    return pltpu.CompilerParams(dimension_semantics=("arbitrary",) * n_axes,
                                vmem_limit_bytes=vmem)


def _rms(x, w):
    ms = jnp.mean(x * x, axis=-1, keepdims=True)
    return x * lax.rsqrt(ms + RMS_EPS) * w


def _cast_kernel(x_ref, o_ref, *, transpose):
    x = x_ref[...]
    o_ref[...] = (x.T if transpose else x).astype(BF16)


def _cast_bf16(w, *, transpose):
    r, c = w.shape
    b = PREP_BLOCK
    out_shape = (c, r) if transpose else (r, c)
    out_map = (lambda i, j: (j, i)) if transpose else (lambda i, j: (i, j))
    return pl.pallas_call(
        functools.partial(_cast_kernel, transpose=transpose),
        grid=(r // b, c // b),
        in_specs=[pl.BlockSpec((b, b), lambda i, j: (i, j))],
        out_specs=pl.BlockSpec((b, b), out_map),
        out_shape=jax.ShapeDtypeStruct(out_shape, BF16),
        compiler_params=_params(2),
        name="cast_t" if transpose else "cast",
    )(w)


def _rmsnorm_kernel(x_ref, w_ref, o_ref):
    o_ref[...] = _rms(x_ref[...], w_ref[...]).astype(BF16)


def _rmsnorm(x, w, *, tm):
    rows = x.shape[0]
    return pl.pallas_call(
        _rmsnorm_kernel, grid=(rows // tm,),
        in_specs=[pl.BlockSpec((tm, D_MODEL), lambda i: (i, 0)),
                  pl.BlockSpec((1, D_MODEL), lambda i: (0, 0))],
        out_specs=pl.BlockSpec((tm, D_MODEL), lambda i: (i, 0)),
        out_shape=jax.ShapeDtypeStruct((rows, D_MODEL), BF16),
        compiler_params=_params(1), name="rmsnorm",
    )(x, w)


def _conv_proj_kernel(*refs, tm, te, n_pieces, special):
    if special:
        (xn_ref, wb_ref, wc_ref, wv_ref, wz_ref, cw_ref, cb_ref, init_ref, c0_ref, c1_ref,
         y_ref, cv_ref, w_scr, s_scr, p1_scr, p2_scr) = refs
    else:
        (xn_ref, wb_ref, wc_ref, wv_ref, wz_ref, cw_ref, cb_ref, init_ref,
         y_ref, tail_ref, w_scr, s_scr) = refs
    i = pl.program_id(1)

    @pl.when(i == 0)
    def _():
        for p, w_ref in enumerate((wb_ref, wc_ref, wv_ref, wz_ref)):
            w_scr[:, p * te:(p + 1) * te] = w_ref[...].astype(BF16)
        s_scr[0:8, :] = init_ref[...]

    pm = tm // n_pieces
    for r in range(n_pieces):
        lo = r * pm
        proj = jnp.dot(xn_ref[lo:lo + pm, :], w_scr[...], preferred_element_type=F32)
        bg = proj[:, 0 * te:1 * te]
        cg = proj[:, 1 * te:2 * te]
        vv = proj[:, 2 * te:3 * te]
        zz = proj[:, 3 * te:4 * te]
        cv = cg * vv
        s_scr[8 + lo:8 + lo + pm, :] = cv
        if special:
            cv_ref[lo:lo + pm, :] = cv
            p1_scr[...] = s_scr[7 + lo:7 + lo + pm, :]
            p2_scr[...] = s_scr[6 + lo:6 + lo + pm, :]
            for q in range(c0_ref.shape[0]):
                r0 = N_META + 32 * q
                p1_scr[r0:r0 + 1, :] = c1_ref[q:q + 1, :]
                p2_scr[r0:r0 + 1, :] = c0_ref[q:q + 1, :]
                p2_scr[r0 + 1:r0 + 2, :] = c1_ref[q:q + 1, :]
            p1 = p1_scr[...]
            p2 = p2_scr[...]
        else:
            p1 = s_scr[7 + lo:7 + lo + pm, :]
            p2 = s_scr[6 + lo:6 + lo + pm, :]
        conv = cb_ref[...] + cw_ref[0:1, :] * p2
        conv = conv + cw_ref[1:2, :] * p1
        conv = conv + cw_ref[2:3, :] * cv
        y_ref[lo:lo + pm, :] = (bg * conv * jax.nn.silu(zz)).astype(BF16)

    if not special:
        tail = s_scr[tm:tm + 8, :]
        s_scr[0:8, :] = tail
        tail_ref[...] = tail


def _conv_proj(xn, w_in, conv_w8, conv_b, init8, cache0=None, cache1=None):
    rows = xn.shape[0]
    tm, te = TM_CONV, TE_CONV
    n_j = D_INNER // te
    n_i = rows // tm
    special = cache0 is not None
    w_spec = lambda p: pl.BlockSpec((D_MODEL, te), lambda j, i, p=p: (0, p * n_j + j))
    in_specs = [pl.BlockSpec((tm, D_MODEL), lambda j, i: (i, 0)),
                w_spec(0), w_spec(1), w_spec(2), w_spec(3),
                pl.BlockSpec((8, te), lambda j, i: (0, j)),
                pl.BlockSpec((1, te), lambda j, i: (0, j)),
                pl.BlockSpec((8, te), lambda j, i: (0, j))]
    args = [xn, w_in, w_in, w_in, w_in, conv_w8, conv_b, init8]
    scratch = [pltpu.VMEM((D_MODEL, 4 * te), BF16), pltpu.VMEM((tm + 8, te), F32)]
    y_shape = jax.ShapeDtypeStruct((rows, D_INNER), BF16)
    y_spec = pl.BlockSpec((tm, te), lambda j, i: (i, j))
    if special:
        assert n_i == 1
        nq = cache0.shape[0]
        in_specs += [pl.BlockSpec((nq, te), lambda j, i: (0, j)),
                     pl.BlockSpec((nq, te), lambda j, i: (0, j))]
        args += [cache0, cache1]
        out_shape = (y_shape, jax.ShapeDtypeStruct((rows, D_INNER), F32))
        out_specs = (y_spec, pl.BlockSpec((tm, te), lambda j, i: (i, j)))
        scratch += [pltpu.VMEM((tm, te), F32), pltpu.VMEM((tm, te), F32)]
    else:
        out_shape = (y_shape, jax.ShapeDtypeStruct((n_i, 8, D_INNER), F32))
        out_specs = (y_spec, pl.BlockSpec((None, 8, te), lambda j, i: (i, 0, j)))
    return pl.pallas_call(
        functools.partial(_conv_proj_kernel, tm=tm, te=te, n_pieces=1 if special else 2,
                          special=special),
        grid=(n_j, n_i), in_specs=in_specs, out_specs=out_specs, out_shape=out_shape,
        scratch_shapes=scratch, compiler_params=_params(2),
        name="conv_proj_special" if special else "conv_proj",
    )(*args)


def _out_proj_kernel(*refs, n_real, aliased):
    if aliased:
        y_ref, w_ref, x_ref, nw_ref, _, h_ref, xs_ref = refs
    else:
        y_ref, w_ref, x_ref, nw_ref, h_ref, xs_ref = refs
    nc = y_ref.shape[0]

    def compute():
        ycat = jnp.concatenate(
            [y_ref[:, s * D_INNER:(s + 1) * D_INNER] for s in range(CHUNK)], axis=0)
        out = jnp.dot(ycat, w_ref[...], preferred_element_type=F32)
        for s in range(CHUNK):
            h = x_ref[:, s * D_MODEL:(s + 1) * D_MODEL] + out[s * nc:(s + 1) * nc, :]
            h_ref[:, s * D_MODEL:(s + 1) * D_MODEL] = h
            xs_ref[s] = _rms(h, nw_ref[...]).astype(BF16)

    i = pl.program_id(0)
    pl.when(i < n_real)(compute)

    @pl.when(i >= n_real)
    def _():
        xs_ref[...] = jnp.zeros_like(xs_ref)
        if aliased:
            h_ref[...] = jnp.zeros_like(h_ref)


def _out_proj(y, w_out, x, norm_w, *, n_tiles_total, tile_off, xs_all=None):
    n_chunks_real = y.shape[0] // CHUNK
    nc = TM_OUT // CHUNK
    per_tile = LANES // nc
    aliased = xs_all is not None
    n_real = n_chunks_real // nc
    clamp = lambda i: jnp.minimum(i, n_real - 1)
    if aliased:
        n_i = per_tile
        n_h = per_tile
        xs_map = lambda i: (0, 0, i, 0)
    else:
        n_i = n_real + per_tile
        n_h = n_real
        xs_map = lambda i: (jnp.where(i < n_real, i // per_tile + tile_off, 0), 0,
                            jnp.where(i < n_real, i % per_tile, i - n_real), 0)
    yv = y.reshape(n_chunks_real, CHUNK * D_INNER)
    xv = x.reshape(n_chunks_real, CHUNK * D_MODEL)
    in_specs = [
        pl.BlockSpec((nc, CHUNK * D_INNER), lambda i: (clamp(i), 0)),
        pl.BlockSpec((D_INNER, D_MODEL), lambda i: (0, 0), pipeline_mode=pl.Buffered(1)),
        pl.BlockSpec((nc, CHUNK * D_MODEL), lambda i: (clamp(i), 0)),
        pl.BlockSpec((1, D_MODEL), lambda i: (0, 0)),
    ]
    args = [yv, w_out, xv, norm_w]
    aliases = {}
    if aliased:
        in_specs.append(pl.BlockSpec(memory_space=pl.ANY))
        args.append(xs_all)
        aliases = {4: 1}
    return pl.pallas_call(
        functools.partial(_out_proj_kernel, n_real=n_real, aliased=aliased),
        grid=(n_i,),
        in_specs=in_specs,
        out_specs=(pl.BlockSpec((nc, CHUNK * D_MODEL), lambda i: (jnp.minimum(i, n_h - 1), 0)),
                   pl.BlockSpec((None, CHUNK, nc, D_MODEL), xs_map)),
        out_shape=(jax.ShapeDtypeStruct((n_h * nc, CHUNK * D_MODEL), F32),
                   jax.ShapeDtypeStruct((n_tiles_total, CHUNK, LANES, D_MODEL), BF16)),
        input_output_aliases=aliases,
        compiler_params=_params(1, 60 * 1024 * 1024),
        name="out_proj_special" if aliased else "out_proj",
    )(*args)


def _ssm_kernel(xs_ref, wu_ref, wz_ref, kall_ref, win_ref, cout_ref, are_ref, aim_ref,
                sre_ref, sim_ref,
                gt_ref, szt_ref, hsre_ref, hsim_ref, hpre_ref, hpim_ref,
                w_scr, tk_scr, zt_scr, xre_scr, xim_scr, pre_scr, pim_scr, cre_scr, cim_scr,
                *, te, n_tiles):
    i = pl.program_id(1)
    gb = te // SSM_GROUP
    rows = gb * SSM_STATE
    n_seq = sre_ref.shape[0]

    @pl.when(i == 0)
    def _():
        w_scr[0:te, :] = wu_ref[...]
        w_scr[te:2 * te, :] = wz_ref[...]
        rep = (lax.broadcasted_iota(jnp.int32, (SSM_GROUP, 256), 0)
               == (lax.broadcasted_iota(jnp.int32, (SSM_GROUP, 256), 1) & (SSM_GROUP - 1)))
        kw = jnp.dot(kall_ref[...].reshape(gb * 256, SSM_GROUP).astype(BF16),
                     rep.astype(BF16), preferred_element_type=F32)
        kw = kw.astype(BF16).reshape(gb, 256, 256)
        tk_scr[...] = jnp.zeros_like(tk_scr)
        for s in range(CHUNK):
            lo = SSM_GROUP * s
            tk_scr[:, lo:256, lo:lo + SSM_GROUP] = kw[:, 0:256 - lo, lo:lo + SSM_GROUP]

    uz = lax.dot_general(w_scr[...], xs_ref[...], (((1,), (1,)), ((), ())),
                         preferred_element_type=F32)
    szt_ref[...] = jax.nn.silu(uz[te:, :]).astype(BF16)
    ub = uz[:te, :].astype(BF16)
    for s in range(CHUNK):
        zt_scr[:, CHUNK * s:CHUNK * (s + 1), :] = (
            ub[:, LANES * s:LANES * (s + 1)].reshape(gb, SSM_GROUP, LANES))
    zt = zt_scr[...]
    bdims = (((2,), (1,)), ((0,), (0,)))
    yt = lax.dot_general(tk_scr[...], zt, bdims, preferred_element_type=F32)
    xt = lax.dot_general(win_ref[...], zt, bdims, preferred_element_type=F32)
    xre_scr[...] = xt[:, 0:SSM_STATE, :].reshape(rows, LANES).T
    xim_scr[...] = xt[:, SSM_STATE:, :].reshape(rows, LANES).T

    ar = are_ref[...]
    ai = aim_ref[...]

    def step(hr, hi, n):
        xr = xre_scr[pl.ds(n, 1), :]
        xi = xim_scr[pl.ds(n, 1), :]
        return ar * hr - ai * hi + xr, ar * hi + ai * hr + xi

    @pl.when(i == 0)
    def _():
        pre_scr[...] = jnp.zeros_like(pre_scr)
        pim_scr[...] = jnp.zeros_like(pim_scr)
        zero = jnp.zeros((1, rows), F32)
        hr, hi = step(zero, zero, 0)
        cre_scr[0:1, :] = hr
        cim_scr[0:1, :] = hi
        for q in range(n_seq):
            hr = sre_ref[q:q + 1, :]
            hi = sim_ref[q:q + 1, :]
            for e in range(2):
                n = 1 + 2 * q + e
                pre_scr[n:n + 1, :] = hr
                pim_scr[n:n + 1, :] = hi
                hr, hi = step(hr, hi, n)
            hsre_ref[q:q + 1, :] = hr
            hsim_ref[q:q + 1, :] = hi

    @pl.when(i > 0)
    def _():
        def body(n, carry):
            hr, hi = carry
            pre_scr[pl.ds(n, 1), :] = hr
            pim_scr[pl.ds(n, 1), :] = hi
            return step(hr, hi, n)

        hr, hi = lax.fori_loop(0, LANES, body, (cre_scr[0:1, :], cim_scr[0:1, :]), unroll=8)
        cre_scr[0:1, :] = hr
        cim_scr[0:1, :] = hi

    @pl.when(i == n_tiles - 1)
    def _():
        hpre_ref[...] = jnp.broadcast_to(cre_scr[0:1, :], hpre_ref.shape)
        hpim_ref[...] = jnp.broadcast_to(cim_scr[0:1, :], hpim_ref.shape)

    hp = jnp.concatenate([pre_scr[...].T.reshape(gb, SSM_STATE, LANES),
                          pim_scr[...].T.reshape(gb, SSM_STATE, LANES)], axis=1).astype(BF16)
    ycorr = lax.dot_general(cout_ref[...], hp, bdims, preferred_element_type=F32)
    g = jax.nn.gelu(yt + ycorr)
    for t in range(CHUNK):
        gt_ref[:, LANES * t:LANES * (t + 1)] = (
            g[:, SSM_GROUP * t:SSM_GROUP * (t + 1), :].reshape(te, LANES).astype(BF16))


def _ssm(xs_all, wuz_t, kall, w_in, cout, are, aim, sre, sim):
    n_tiles = xs_all.shape[0]
    te = TE_SSM
    n_j = D_INNER // te
    gb = te // SSM_GROUP
    rows = gb * SSM_STATE
    n_seq = sre.shape[0]
    act_shape = jax.ShapeDtypeStruct((n_tiles, D_INNER, TILE), BF16)
    act_spec = pl.BlockSpec((None, te, TILE), lambda j, i: (i, j, 0))
    hs_shape = jax.ShapeDtypeStruct((n_seq, N_STATE_ROWS), F32)
    hs_spec = pl.BlockSpec((n_seq, rows), lambda j, i: (0, j))
    hp_shape = jax.ShapeDtypeStruct((8, N_STATE_ROWS), F32)
    hp_spec = pl.BlockSpec((8, rows), lambda j, i: (0, j))
    row_spec = pl.BlockSpec((1, rows), lambda j, i: (0, j))
    return pl.pallas_call(
        functools.partial(_ssm_kernel, te=te, n_tiles=n_tiles),
        grid=(n_j, n_tiles),
        in_specs=[
            pl.BlockSpec((None, TILE, D_MODEL), lambda j, i: (i, 0, 0)),
            pl.BlockSpec((te, D_MODEL), lambda j, i: (j, 0)),
            pl.BlockSpec((te, D_MODEL), lambda j, i: (n_j + j, 0)),
            pl.BlockSpec((gb, 256, SSM_GROUP), lambda j, i: (j, 0, 0)),
            pl.BlockSpec((gb, 2 * SSM_STATE, 256), lambda j, i: (j, 0, 0)),
            pl.BlockSpec((gb, 256, 2 * SSM_STATE), lambda j, i: (j, 0, 0)),
            row_spec, row_spec, hs_spec, hs_spec,
        ],
        out_specs=(act_spec, act_spec, hs_spec, hs_spec, hp_spec, hp_spec),
        out_shape=(act_shape, act_shape, hs_shape, hs_shape, hp_shape, hp_shape),
        scratch_shapes=[pltpu.VMEM((2 * te, D_MODEL), BF16),
                        pltpu.VMEM((gb, 256, 256), BF16),
                        pltpu.VMEM((gb, 256, LANES), BF16),
                        pltpu.VMEM((LANES, rows), F32), pltpu.VMEM((LANES, rows), F32),
                        pltpu.VMEM((LANES, rows), F32), pltpu.VMEM((LANES, rows), F32),
                        pltpu.VMEM((8, rows), F32), pltpu.VMEM((8, rows), F32)],
        compiler_params=_params(2),
        name="s5_scan",
    )(xs_all, wuz_t, wuz_t, kall, w_in, cout, are, aim, sre, sim)


def _glu_out_kernel(gt_ref, szt_ref, wg_ref, bg_ref, wo_ref, h_ref, nw_ref, o_ref, acc_scr, *,
                    te, lb):
    e = pl.program_id(2)

    @pl.when(e == 0)
    def _():
        acc_scr[...] = jnp.zeros_like(acc_scr)

    gate = jnp.dot(wg_ref[...], gt_ref[...], preferred_element_type=F32) + bg_ref[...]
    ge = gt_ref[pl.ds(pl.multiple_of(e * te, te), te), :].astype(F32)
    y3 = (ge * jax.nn.sigmoid(gate)) * szt_ref[...].astype(F32)
    acc_scr[...] += jnp.dot(wo_ref[...], y3.astype(BF16), preferred_element_type=F32)

    @pl.when(e == pl.num_programs(2) - 1)
    def _():
        o = acc_scr[...].T
        for t in range(lb // LANES):
            h = h_ref[:, D_MODEL * t:D_MODEL * (t + 1)] + o[LANES * t:LANES * (t + 1), :]
            o_ref[:, D_MODEL * t:D_MODEL * (t + 1)] = _rms(h, nw_ref[...])


def _glu_out(gt, szt, wg_t, b_glu, wo_t, h_view, norm_w, *, n_tiles, tile_off):
    te = TE_GLU
    lb = LB_GLU
    n_l = TILE // lb
    n_e = D_INNER // te
    hw = (lb // LANES) * D_MODEL
    return pl.pallas_call(
        functools.partial(_glu_out_kernel, te=te, lb=lb),
        grid=(n_tiles, n_l, n_e),
        in_specs=[
            pl.BlockSpec((None, D_INNER, lb), lambda i, l, e: (i + tile_off, 0, l)),
            pl.BlockSpec((None, te, lb), lambda i, l, e: (i + tile_off, e, l)),
            pl.BlockSpec((te, D_INNER), lambda i, l, e: (e, 0)),
            pl.BlockSpec((te, 1), lambda i, l, e: (e, 0)),
            pl.BlockSpec((D_MODEL, te), lambda i, l, e: (0, e)),
            pl.BlockSpec((LANES, hw), lambda i, l, e: (i, l)),
            pl.BlockSpec((1, D_MODEL), lambda i, l, e: (0, 0)),
        ],
        out_specs=pl.BlockSpec((LANES, hw), lambda i, l, e: (i, l)),
        out_shape=jax.ShapeDtypeStruct((n_tiles * LANES, CHUNK * D_MODEL), F32),
        scratch_shapes=[pltpu.VMEM((D_MODEL, lb), F32)],
        compiler_params=_params(3),
        name="glu_out",
    )(gt, szt, wg_t, b_glu, wo_t, h_view, norm_w)


def _ssm_operators(a_re, a_im, b_re, b_im, c_re, c_im, d_vec, log_dt):
    hi = lax.Precision.HIGHEST
    lam = lax.complex(a_re.astype(F32), a_im.astype(F32))
    dt = jnp.exp(log_dt.astype(F32))[:, None]
    a_bar = jnp.exp(lam * dt)
    b_bar = ((a_bar - 1.0) / lam)[..., None] * lax.complex(b_re.astype(F32), b_im.astype(F32))
    c_mat = lax.complex(c_re.astype(F32), c_im.astype(F32))
    taus = jnp.arange(CHUNK + 1, dtype=F32)
    apow = jnp.exp((lam * dt)[:, None, :] * taus[None, :, None])

    ca = c_mat[:, None] * apow[:, :CHUNK, None, :]
    kk = (jnp.einsum('gtcp,gpd->gtcd', jnp.real(ca), jnp.real(b_bar), precision=hi)
          - jnp.einsum('gtcp,gpd->gtcd', jnp.imag(ca), jnp.imag(b_bar), precision=hi))
    eye = jnp.eye(SSM_GROUP, dtype=F32)
    kk = kk.at[:, 0].add(eye[None] * d_vec.astype(F32).reshape(N_GROUPS, SSM_GROUP)[:, :, None])
    kall = kk.reshape(N_GROUPS, CHUNK * SSM_GROUP, SSM_GROUP)

    wb = apow[:, CHUNK - 1::-1, :].transpose(0, 2, 1)[..., None] * b_bar[:, :, None, :]
    wb = wb.reshape(N_GROUPS, SSM_STATE, CHUNK * SSM_GROUP)
    w_in = jnp.concatenate([jnp.real(wb), jnp.imag(wb)], axis=1).astype(BF16)

    co = (c_mat[:, None] * apow[:, 1:CHUNK + 1, None, :]).reshape(N_GROUPS, 256, SSM_STATE)
    cout = jnp.concatenate([jnp.real(co), -jnp.imag(co)], axis=2).astype(BF16)

    alpha = apow[:, CHUNK, :]
    return (kall, w_in, cout, jnp.real(alpha).reshape(1, N_STATE_ROWS),
            jnp.imag(alpha).reshape(1, N_STATE_ROWS))


def kernel(x_prompt, x_sample, cache_conv, state_ssm_re, state_ssm_im, meta_tokens, norm_w,
           final_norm_w, conv_w_in, conv_w, conv_b, conv_w_out, ssm_w_in, ssm_a_re, ssm_a_im,
           ssm_b_re, ssm_b_im, ssm_c_re, ssm_c_im, ssm_d, ssm_log_dt, ssm_w_glu, ssm_b_glu,
           ssm_w_out):
    n_seq, seq_len = x_sample.shape[0], x_sample.shape[1]
    n_prompt_rows = x_prompt.shape[1]
    n_ptiles = n_prompt_rows // TILE
    n_sample_rows = n_seq * seq_len
    assert x_prompt.shape[0] == 1 and seq_len == 2 * CHUNK and N_META == CHUNK
    assert n_prompt_rows % TILE == 0 and N_META + n_sample_rows <= SPECIAL_ROWS

    w_out0 = _cast_bf16(conv_w_out[0].astype(F32), transpose=False)
    wuz_t = _cast_bf16(ssm_w_in[0].astype(F32), transpose=True)
    wg_t = _cast_bf16(ssm_w_glu[0].astype(F32), transpose=True)
    wo_t = _cast_bf16(ssm_w_out[0].astype(F32), transpose=True)
    w_in0 = conv_w_in[0].astype(F32)
    conv_w8 = jnp.zeros((8, D_INNER), F32).at[0:3].set(conv_w[0].astype(F32))
    conv_b2 = conv_b[0].astype(F32).reshape(1, D_INNER)
    b_glu = ssm_b_glu[0].astype(F32).reshape(D_INNER, 1)
    nw0 = norm_w[0].astype(F32).reshape(1, D_MODEL)
    nw1 = norm_w[1].astype(F32).reshape(1, D_MODEL)
    nwf = final_norm_w.astype(F32).reshape(1, D_MODEL)
    kall, w_in_s, cout, are, aim = _ssm_operators(
        ssm_a_re[0], ssm_a_im[0], ssm_b_re[0], ssm_b_im[0], ssm_c_re[0], ssm_c_im[0],
        ssm_d[0], ssm_log_dt[0])
    sre = state_ssm_re[0].astype(F32).reshape(n_seq, N_STATE_ROWS)
    sim = state_ssm_im[0].astype(F32).reshape(n_seq, N_STATE_ROWS)

    xp = x_prompt[0].astype(F32)
    x_sp = jnp.concatenate([
        meta_tokens.astype(F32), x_sample.astype(F32).reshape(n_sample_rows, D_MODEL),
        jnp.zeros((SPECIAL_ROWS - N_META - n_sample_rows, D_MODEL), F32)], axis=0)

    zeros8 = jnp.zeros((8, D_INNER), F32)
    xn_sp = _rmsnorm(x_sp, nw0, tm=SPECIAL_ROWS)
    xn_p = _rmsnorm(xp, nw0, tm=TM_CONV)
    y_sp, cv_sp = _conv_proj(xn_sp, w_in0, conv_w8, conv_b2, zeros8,
                             cache_conv[0, :, 0, :].astype(F32), cache_conv[0, :, 1, :].astype(F32))
    init8 = zeros8.at[6:8].set(cv_sp[N_META - 2:N_META])
    y_p, tail_p = _conv_proj(xn_p, w_in0, conv_w8, conv_b2, init8)
    h1_p, xs_all = _out_proj(y_p, w_out0, xp, nw1, n_tiles_total=1 + n_ptiles, tile_off=1)
    h1_sp, xs_all = _out_proj(y_sp, w_out0, x_sp, nw1, n_tiles_total=1 + n_ptiles, tile_off=0,
                              xs_all=xs_all)

    gt, szt, hs_re, hs_im, hp_re, hp_im = _ssm(
        xs_all.reshape(1 + n_ptiles, TILE, D_MODEL), wuz_t, kall, w_in_s, cout, are, aim, sre, sim)
    o_sp = _glu_out(gt, szt, wg_t, b_glu, wo_t, h1_sp, nwf, n_tiles=1, tile_off=0)
    o_p = _glu_out(gt, szt, wg_t, b_glu, wo_t, h1_p, nwf, n_tiles=n_ptiles, tile_off=1)

    y_prompt = o_p.reshape(1, n_prompt_rows, D_MODEL)
    n_sc = n_sample_rows // CHUNK
    y_sample = o_sp[1:1 + n_sc].reshape(n_seq, seq_len, D_MODEL)
    new_conv_prompt = tail_p[-1, 6:8].reshape(1, 1, 2, D_INNER)
    cv_s = cv_sp[N_META:N_META + n_sample_rows].reshape(n_seq, seq_len, D_INNER)
    new_conv_sample = cv_s[:, seq_len - 2:].reshape(1, n_seq, 2, D_INNER)
    p_shape = (1, 1, N_GROUPS, SSM_STATE)
    s_shape = (1, n_seq, N_GROUPS, SSM_STATE)
    return (y_prompt, y_sample, new_conv_prompt, new_conv_sample,
            hp_re[0].reshape(p_shape), hp_im[0].reshape(p_shape),
            hs_re.reshape(s_shape), hs_im.reshape(s_shape))
```

```python
import functools

import jax
import jax.numpy as jnp
from jax import lax
from jax.experimental import pallas as pl
from jax.experimental.pallas import tpu as pltpu

F32 = jnp.float32
BF16 = jnp.bfloat16

D_MODEL = 2048
D_INNER = 4096
N_META = 16
SSM_GROUP = 16
N_GROUPS = D_INNER // SSM_GROUP
SSM_STATE = 64
N_STATE_ROWS = N_GROUPS * SSM_STATE
RMS_EPS = 1e-6

CHUNK = 16
LANES = 128
TILE = CHUNK * LANES
SPECIAL_ROWS = 1024

TM_CONV = 1024
TE_CONV = 256
TM_OUT = 512
TE_SSM = 256
TE_GLU = 512
LB_GLU = 512
PREP_BLOCK = 1024

VMEM_LIMIT = 56 * 1024 * 1024


def _params(n_axes, vmem=VMEM_LIMIT):
    return pltpu.CompilerParams(dimension_semantics=("arbitrary",) * n_axes,
                                vmem_limit_bytes=vmem)


def _rms(x, w):
    ms = jnp.mean(x * x, axis=-1, keepdims=True)
    return x * lax.rsqrt(ms + RMS_EPS) * w


def _cast_kernel(x_ref, o_ref, *, transpose):
    x = x_ref[...]
    o_ref[...] = (x.T if transpose else x).astype(BF16)


def _cast_bf16(w, *, transpose):
    _, r, c = w.shape
    b = PREP_BLOCK
    out_shape = (c, r) if transpose else (r, c)
    out_map = (lambda i, j: (j, i)) if transpose else (lambda i, j: (i, j))
    return pl.pallas_call(
        functools.partial(_cast_kernel, transpose=transpose),
        grid=(r // b, c // b),
        in_specs=[pl.BlockSpec((None, b, b), lambda i, j: (0, i, j))],
        out_specs=pl.BlockSpec((b, b), out_map),
        out_shape=jax.ShapeDtypeStruct(out_shape, BF16),
        compiler_params=_params(2),
        name="cast_t" if transpose else "cast",
    )(w)


def _rmsnorm_kernel(x_ref, w_ref, o_ref):
    o_ref[...] = _rms(x_ref[...], w_ref[...]).astype(BF16)


def _rmsnorm(x, w, *, tm):
    rows = x.shape[1]
    return pl.pallas_call(
        _rmsnorm_kernel, grid=(rows // tm,),
        in_specs=[pl.BlockSpec((None, tm, D_MODEL), lambda i: (0, i, 0)),
                  pl.BlockSpec((1, D_MODEL), lambda i: (0, 0))],
        out_specs=pl.BlockSpec((tm, D_MODEL), lambda i: (i, 0)),
        out_shape=jax.ShapeDtypeStruct((rows, D_MODEL), BF16),
        compiler_params=_params(1), name="rmsnorm",
    )(x, w)


def _conv_proj_kernel(*refs, tm, te, n_pieces, special):
    if special:
        (xn_ref, wb_ref, wc_ref, wv_ref, wz_ref, cw_ref, cb_ref, init_ref, c0_ref, c1_ref,
         y_ref, cv_ref, w_scr, s_scr, p1_scr, p2_scr) = refs
    else:
        (xn_ref, wb_ref, wc_ref, wv_ref, wz_ref, cw_ref, cb_ref, init_ref,
         y_ref, tail_ref, w_scr, s_scr) = refs
    i = pl.program_id(1)

    @pl.when(i == 0)
    def _():
        for p, w_ref in enumerate((wb_ref, wc_ref, wv_ref, wz_ref)):
            w_scr[:, p * te:(p + 1) * te] = w_ref[...].astype(BF16)
        s_scr[0:8, :] = init_ref[...]

    pm = tm // n_pieces
    for r in range(n_pieces):
        lo = r * pm
        proj = jnp.dot(xn_ref[lo:lo + pm, :], w_scr[...], preferred_element_type=F32)
        bg = proj[:, 0 * te:1 * te]
        cg = proj[:, 1 * te:2 * te]
        vv = proj[:, 2 * te:3 * te]
        zz = proj[:, 3 * te:4 * te]
        cv = cg * vv
        s_scr[8 + lo:8 + lo + pm, :] = cv
        if special:
            cv_ref[lo:lo + pm, :] = cv
            p1_scr[...] = s_scr[7 + lo:7 + lo + pm, :]
            p2_scr[...] = s_scr[6 + lo:6 + lo + pm, :]
            for q in range(c0_ref.shape[0]):
                r0 = N_META + 32 * q
                p1_scr[r0:r0 + 1, :] = c1_ref[q:q + 1, :]
                p2_scr[r0:r0 + 1, :] = c0_ref[q:q + 1, :]
                p2_scr[r0 + 1:r0 + 2, :] = c1_ref[q:q + 1, :]
            p1 = p1_scr[...]
            p2 = p2_scr[...]
        else:
            p1 = s_scr[7 + lo:7 + lo + pm, :]
            p2 = s_scr[6 + lo:6 + lo + pm, :]
        conv = cb_ref[...] + cw_ref[0:1, :] * p2
        conv = conv + cw_ref[1:2, :] * p1
        conv = conv + cw_ref[2:3, :] * cv
        y_ref[lo:lo + pm, :] = (bg * conv * jax.nn.silu(zz)).astype(BF16)

    if not special:
        tail = s_scr[tm:tm + 8, :]
        s_scr[0:8, :] = tail
        tail_ref[...] = tail


def _conv_proj(xn, w_in, conv_w8, conv_b, init8, cache0=None, cache1=None):
    rows = xn.shape[0]
    tm, te = TM_CONV, TE_CONV
    n_j = D_INNER // te
    n_i = rows // tm
    special = cache0 is not None
    w_spec = lambda p: pl.BlockSpec((None, D_MODEL, te),
                                    lambda j, i, p=p: (0, 0, p * n_j + j))
    in_specs = [pl.BlockSpec((tm, D_MODEL), lambda j, i: (i, 0)),
                w_spec(0), w_spec(1), w_spec(2), w_spec(3),
                pl.BlockSpec((8, te), lambda j, i: (0, j)),
                pl.BlockSpec((1, te), lambda j, i: (0, j)),
                pl.BlockSpec((8, te), lambda j, i: (0, j))]
    args = [xn, w_in, w_in, w_in, w_in, conv_w8, conv_b, init8]
    scratch = [pltpu.VMEM((D_MODEL, 4 * te), BF16), pltpu.VMEM((tm + 8, te), F32)]
    y_shape = jax.ShapeDtypeStruct((rows, D_INNER), BF16)
    y_spec = pl.BlockSpec((tm, te), lambda j, i: (i, j))
    if special:
        assert n_i == 1
        nq = cache0.shape[0]
        in_specs += [pl.BlockSpec((nq, te), lambda j, i: (0, j)),
                     pl.BlockSpec((nq, te), lambda j, i: (0, j))]
        args += [cache0, cache1]
        out_shape = (y_shape, jax.ShapeDtypeStruct((rows, D_INNER), F32))
        out_specs = (y_spec, pl.BlockSpec((tm, te), lambda j, i: (i, j)))
        scratch += [pltpu.VMEM((tm, te), F32), pltpu.VMEM((tm, te), F32)]
    else:
        out_shape = (y_shape, jax.ShapeDtypeStruct((n_i, 8, D_INNER), F32))
        out_specs = (y_spec, pl.BlockSpec((None, 8, te), lambda j, i: (i, 0, j)))
    return pl.pallas_call(
        functools.partial(_conv_proj_kernel, tm=tm, te=te, n_pieces=1 if special else 2,
                          special=special),
        grid=(n_j, n_i), in_specs=in_specs, out_specs=out_specs, out_shape=out_shape,
        scratch_shapes=scratch, compiler_params=_params(2),
        name="conv_proj_special" if special else "conv_proj",
    )(*args)


def _out_proj_kernel(*refs, n_real, aliased):
    if aliased:
        y_ref, w_ref, x_ref, nw_ref, _, _, h_ref, xs_ref, slab_scr = refs
    else:
        y_ref, w_ref, x_ref, nw_ref, h_ref, xs_ref, slab_scr = refs
    nc = y_ref.shape[0] // CHUNK
    n_slabs = D_MODEL // LANES

    def compute():
        h = x_ref[...] + jnp.dot(y_ref[...], w_ref[...], preferred_element_type=F32)
        for k in range(n_slabs):
            slab_scr[k] = h[:, LANES * k:LANES * (k + 1)]
        for s in range(CHUNK):
            hs = jnp.concatenate(
                [slab_scr[k, pl.ds(s, nc, stride=CHUNK), :] for k in range(n_slabs)], axis=1)
            h_ref[s] = hs
            xs_ref[s] = _rms(hs, nw_ref[...]).astype(BF16)

    i = pl.program_id(0)
    pl.when(i < n_real)(compute)

    @pl.when(i >= n_real)
    def _():
        h_ref[...] = jnp.zeros_like(h_ref)
        xs_ref[...] = jnp.zeros_like(xs_ref)


def _out_proj(y, w_out, x, norm_w, *, n_tiles_total, h_all=None, xs_all=None):
    tm = TM_OUT
    nc = tm // CHUNK
    per_tile = LANES // nc
    aliased = xs_all is not None
    n_real = x.shape[1] // tm
    if aliased:
        n_i = n_real
        tile_map = lambda i: (0, 0, i, 0)
    else:
        n_i = n_real + per_tile
        tile_map = lambda i: (jnp.where(i < n_real, i // per_tile + 1, 0), 0,
                              jnp.where(i < n_real, i % per_tile, i - n_real), 0)
    clamp = lambda i: jnp.minimum(i, n_real - 1)
    in_specs = [
        pl.BlockSpec((tm, D_INNER), lambda i: (clamp(i), 0)),
        pl.BlockSpec((D_INNER, D_MODEL), lambda i: (0, 0), pipeline_mode=pl.Buffered(1)),
        pl.BlockSpec((None, tm, D_MODEL), lambda i: (0, clamp(i), 0)),
        pl.BlockSpec((1, D_MODEL), lambda i: (0, 0)),
    ]
    args = [y, w_out, x, norm_w]
    aliases = {}
    if aliased:
        in_specs += [pl.BlockSpec(memory_space=pl.ANY), pl.BlockSpec(memory_space=pl.ANY)]
        args += [h_all, xs_all]
        aliases = {4: 0, 5: 1}
    tile_shape = (n_tiles_total, CHUNK, LANES, D_MODEL)
    return pl.pallas_call(
        functools.partial(_out_proj_kernel, n_real=n_real, aliased=aliased),
        grid=(n_i,),
        in_specs=in_specs,
        out_specs=(pl.BlockSpec((None, CHUNK, nc, D_MODEL), tile_map),
                   pl.BlockSpec((None, CHUNK, nc, D_MODEL), tile_map)),
        out_shape=(jax.ShapeDtypeStruct(tile_shape, F32),
                   jax.ShapeDtypeStruct(tile_shape, BF16)),
        scratch_shapes=[pltpu.VMEM((D_MODEL // LANES, tm, LANES), F32)],
        input_output_aliases=aliases,
        compiler_params=_params(1, 60 * 1024 * 1024),
        name="out_proj_special" if aliased else "out_proj",
    )(*args)


def _ssm_kernel(xs_ref, wu_ref, wz_ref, kall_ref, win_ref, cout_ref, are_ref, aim_ref,
                sre_ref, sim_ref,
                gt_ref, szt_ref, hsre_ref, hsim_ref, hpre_ref, hpim_ref,
                w_scr, tk_scr, zt_scr, xre_scr, xim_scr, pre_scr, pim_scr, cre_scr, cim_scr,
                *, te, n_tiles):
    i = pl.program_id(1)
    gb = te // SSM_GROUP
    rows = gb * SSM_STATE
    n_seq = sre_ref.shape[0]

    @pl.when(i == 0)
    def _():
        w_scr[0:te, :] = wu_ref[...]
        w_scr[te:2 * te, :] = wz_ref[...]
        rep = (lax.broadcasted_iota(jnp.int32, (SSM_GROUP, 256), 0)
               == (lax.broadcasted_iota(jnp.int32, (SSM_GROUP, 256), 1) & (SSM_GROUP - 1)))
        kw = jnp.dot(kall_ref[...].reshape(gb * 256, SSM_GROUP).astype(BF16),
                     rep.astype(BF16), preferred_element_type=F32)
        kw = kw.astype(BF16).reshape(gb, 256, 256)
        tk_scr[...] = jnp.zeros_like(tk_scr)
        for s in range(CHUNK):
            lo = SSM_GROUP * s
            tk_scr[:, lo:256, lo:lo + SSM_GROUP] = kw[:, 0:256 - lo, lo:lo + SSM_GROUP]

    uz = lax.dot_general(w_scr[...], xs_ref[...], (((1,), (1,)), ((), ())),
                         preferred_element_type=F32)
    szt_ref[...] = jax.nn.silu(uz[te:, :]).astype(BF16)
    ub = uz[:te, :].astype(BF16)
    for s in range(CHUNK):
        zt_scr[:, CHUNK * s:CHUNK * (s + 1), :] = (
            ub[:, LANES * s:LANES * (s + 1)].reshape(gb, SSM_GROUP, LANES))
    zt = zt_scr[...]
    bdims = (((2,), (1,)), ((0,), (0,)))
    yt = lax.dot_general(tk_scr[...], zt, bdims, preferred_element_type=F32)
    xt = lax.dot_general(win_ref[...], zt, bdims, preferred_element_type=F32)
    xre_scr[...] = xt[:, 0:SSM_STATE, :].reshape(rows, LANES).T
    xim_scr[...] = xt[:, SSM_STATE:, :].reshape(rows, LANES).T

    ar = are_ref[...]
    ai = aim_ref[...]

    def step(hr, hi, n):
        xr = xre_scr[pl.ds(n, 1), :]
        xi = xim_scr[pl.ds(n, 1), :]
        return ar * hr - ai * hi + xr, ar * hi + ai * hr + xi

    @pl.when(i == 0)
    def _():
        pre_scr[...] = jnp.zeros_like(pre_scr)
        pim_scr[...] = jnp.zeros_like(pim_scr)
        zero = jnp.zeros((1, rows), F32)
        hr, hi = step(zero, zero, 0)
        cre_scr[0:1, :] = hr
        cim_scr[0:1, :] = hi
        for q in range(n_seq):
            hr = sre_ref[q:q + 1, :]
            hi = sim_ref[q:q + 1, :]
            for e in range(2):
                n = 1 + 2 * q + e
                pre_scr[n:n + 1, :] = hr
                pim_scr[n:n + 1, :] = hi
                hr, hi = step(hr, hi, n)
            hsre_ref[q:q + 1, :] = hr
            hsim_ref[q:q + 1, :] = hi

    @pl.when(i > 0)
    def _():
        def body(n, carry):
            hr, hi = carry
            pre_scr[pl.ds(n, 1), :] = hr
            pim_scr[pl.ds(n, 1), :] = hi
            return step(hr, hi, n)

        hr, hi = lax.fori_loop(0, LANES, body, (cre_scr[0:1, :], cim_scr[0:1, :]), unroll=8)
        cre_scr[0:1, :] = hr
        cim_scr[0:1, :] = hi

    @pl.when(i == n_tiles - 1)
    def _():
        hpre_ref[...] = jnp.broadcast_to(cre_scr[0:1, :], hpre_ref.shape)
        hpim_ref[...] = jnp.broadcast_to(cim_scr[0:1, :], hpim_ref.shape)

    hp = jnp.concatenate([pre_scr[...].T.reshape(gb, SSM_STATE, LANES),
                          pim_scr[...].T.reshape(gb, SSM_STATE, LANES)], axis=1).astype(BF16)
    ycorr = lax.dot_general(cout_ref[...], hp, bdims, preferred_element_type=F32)
    g = jax.nn.gelu(yt + ycorr)
    for t in range(CHUNK):
        gt_ref[:, LANES * t:LANES * (t + 1)] = (
            g[:, SSM_GROUP * t:SSM_GROUP * (t + 1), :].reshape(te, LANES).astype(BF16))


def _ssm(xs_all, wuz_t, kall, w_in, cout, are, aim, sre, sim):
    n_tiles = xs_all.shape[0]
    te = TE_SSM
    n_j = D_INNER // te
    gb = te // SSM_GROUP
    rows = gb * SSM_STATE
    n_seq = sre.shape[0]
    act_shape = jax.ShapeDtypeStruct((n_tiles, D_INNER, TILE), BF16)
    act_spec = pl.BlockSpec((None, te, TILE), lambda j, i: (i, j, 0))
    hs_shape = jax.ShapeDtypeStruct((n_seq, N_STATE_ROWS), F32)
    hs_spec = pl.BlockSpec((n_seq, rows), lambda j, i: (0, j))
    hp_shape = jax.ShapeDtypeStruct((8, N_STATE_ROWS), F32)
    hp_spec = pl.BlockSpec((8, rows), lambda j, i: (0, j))
    row_spec = pl.BlockSpec((1, rows), lambda j, i: (0, j))
    return pl.pallas_call(
        functools.partial(_ssm_kernel, te=te, n_tiles=n_tiles),
        grid=(n_j, n_tiles),
        in_specs=[
            pl.BlockSpec((None, TILE, D_MODEL), lambda j, i: (i, 0, 0)),
            pl.BlockSpec((te, D_MODEL), lambda j, i: (j, 0)),
            pl.BlockSpec((te, D_MODEL), lambda j, i: (n_j + j, 0)),
            pl.BlockSpec((gb, 256, SSM_GROUP), lambda j, i: (j, 0, 0)),
            pl.BlockSpec((gb, 2 * SSM_STATE, 256), lambda j, i: (j, 0, 0)),
            pl.BlockSpec((gb, 256, 2 * SSM_STATE), lambda j, i: (j, 0, 0)),
            row_spec, row_spec, hs_spec, hs_spec,
        ],
        out_specs=(act_spec, act_spec, hs_spec, hs_spec, hp_spec, hp_spec),
        out_shape=(act_shape, act_shape, hs_shape, hs_shape, hp_shape, hp_shape),
        scratch_shapes=[pltpu.VMEM((2 * te, D_MODEL), BF16),
                        pltpu.VMEM((gb, 256, 256), BF16),
                        pltpu.VMEM((gb, 256, LANES), BF16),
                        pltpu.VMEM((LANES, rows), F32), pltpu.VMEM((LANES, rows), F32),
                        pltpu.VMEM((LANES, rows), F32), pltpu.VMEM((LANES, rows), F32),
                        pltpu.VMEM((8, rows), F32), pltpu.VMEM((8, rows), F32)],
        compiler_params=_params(2),
        name="s5_scan",
    )(xs_all, wuz_t, wuz_t, kall, w_in, cout, are, aim, sre, sim)


def _glu_out_kernel(gt_ref, szt_ref, wg_ref, bg_ref, wo_ref, h_ref, nw_ref, o_ref, acc_scr, *,
                    te, lb):
    e = pl.program_id(2)

    @pl.when(e == 0)
    def _():
        acc_scr[...] = jnp.zeros_like(acc_scr)

    gate = jnp.dot(wg_ref[...], gt_ref[...], preferred_element_type=F32) + bg_ref[...]
    ge = gt_ref[pl.ds(pl.multiple_of(e * te, te), te), :].astype(F32)
    y3 = (ge * jax.nn.sigmoid(gate)) * szt_ref[...].astype(F32)
    acc_scr[...] += jnp.dot(wo_ref[...], y3.astype(BF16), preferred_element_type=F32)

    @pl.when(e == pl.num_programs(2) - 1)
    def _():
        o = acc_scr[...].T
        for t in range(lb // LANES):
            h = h_ref[t] + o[LANES * t:LANES * (t + 1), :]
            o_ref[:, D_MODEL * t:D_MODEL * (t + 1)] = _rms(h, nw_ref[...])


def _glu_out(gt, szt, wg_t, b_glu, wo_t, h_all, norm_w, *, n_tiles, tile_off):
    te = TE_GLU
    lb = LB_GLU
    n_l = TILE // lb
    n_e = D_INNER // te
    hw = (lb // LANES) * D_MODEL
    return pl.pallas_call(
        functools.partial(_glu_out_kernel, te=te, lb=lb),
        grid=(n_tiles, n_l, n_e),
        in_specs=[
            pl.BlockSpec((None, D_INNER, lb), lambda i, l, e: (i + tile_off, 0, l)),
            pl.BlockSpec((None, te, lb), lambda i, l, e: (i + tile_off, e, l)),
            pl.BlockSpec((te, D_INNER), lambda i, l, e: (e, 0)),
            pl.BlockSpec((te, 1), lambda i, l, e: (e, 0)),
            pl.BlockSpec((D_MODEL, te), lambda i, l, e: (0, e)),
            pl.BlockSpec((None, lb // LANES, LANES, D_MODEL),
                         lambda i, l, e: (i + tile_off, l, 0, 0)),
            pl.BlockSpec((1, D_MODEL), lambda i, l, e: (0, 0)),
        ],
        out_specs=pl.BlockSpec((LANES, hw), lambda i, l, e: (i, l)),
        out_shape=jax.ShapeDtypeStruct((n_tiles * LANES, CHUNK * D_MODEL), F32),
        scratch_shapes=[pltpu.VMEM((D_MODEL, lb), F32)],
        compiler_params=_params(3),
        name="glu_out",
    )(gt, szt, wg_t, b_glu, wo_t, h_all, norm_w)


def _ssm_operators(a_re, a_im, b_re, b_im, c_re, c_im, d_vec, log_dt):
    hi = lax.Precision.HIGHEST
    lam = lax.complex(a_re.astype(F32), a_im.astype(F32))
    dt = jnp.exp(log_dt.astype(F32))[:, None]
    a_bar = jnp.exp(lam * dt)
    b_bar = ((a_bar - 1.0) / lam)[..., None] * lax.complex(b_re.astype(F32), b_im.astype(F32))
    c_mat = lax.complex(c_re.astype(F32), c_im.astype(F32))
    taus = jnp.arange(CHUNK + 1, dtype=F32)
    apow = jnp.exp((lam * dt)[:, None, :] * taus[None, :, None])

    ca = c_mat[:, None] * apow[:, :CHUNK, None, :]
    kk = (jnp.einsum('gtcp,gpd->gtcd', jnp.real(ca), jnp.real(b_bar), precision=hi)
          - jnp.einsum('gtcp,gpd->gtcd', jnp.imag(ca), jnp.imag(b_bar), precision=hi))
    eye = jnp.eye(SSM_GROUP, dtype=F32)
    kk = kk.at[:, 0].add(eye[None] * d_vec.astype(F32).reshape(N_GROUPS, SSM_GROUP)[:, :, None])
    kall = kk.reshape(N_GROUPS, CHUNK * SSM_GROUP, SSM_GROUP)

    wb = apow[:, CHUNK - 1::-1, :].transpose(0, 2, 1)[..., None] * b_bar[:, :, None, :]
    wb = wb.reshape(N_GROUPS, SSM_STATE, CHUNK * SSM_GROUP)
    w_in = jnp.concatenate([jnp.real(wb), jnp.imag(wb)], axis=1).astype(BF16)

    co = (c_mat[:, None] * apow[:, 1:CHUNK + 1, None, :]).reshape(N_GROUPS, 256, SSM_STATE)
    cout = jnp.concatenate([jnp.real(co), -jnp.imag(co)], axis=2).astype(BF16)

    alpha = apow[:, CHUNK, :]
    return (kall, w_in, cout, jnp.real(alpha).reshape(1, N_STATE_ROWS),
            jnp.imag(alpha).reshape(1, N_STATE_ROWS))


def kernel(x_prompt, x_sample, cache_conv, state_ssm_re, state_ssm_im, meta_tokens, norm_w,
           final_norm_w, conv_w_in, conv_w, conv_b, conv_w_out, ssm_w_in, ssm_a_re, ssm_a_im,
           ssm_b_re, ssm_b_im, ssm_c_re, ssm_c_im, ssm_d, ssm_log_dt, ssm_w_glu, ssm_b_glu,
           ssm_w_out):
    n_seq, seq_len = x_sample.shape[0], x_sample.shape[1]
    n_prompt_rows = x_prompt.shape[1]
    n_ptiles = n_prompt_rows // TILE
    n_sample_rows = n_seq * seq_len
    assert x_prompt.shape[0] == 1 and seq_len == 2 * CHUNK and N_META == CHUNK
    assert n_prompt_rows % TILE == 0 and N_META + n_sample_rows <= SPECIAL_ROWS

    w_out0 = _cast_bf16(conv_w_out.astype(F32), transpose=False)
    wuz_t = _cast_bf16(ssm_w_in.astype(F32), transpose=True)
    wg_t = _cast_bf16(ssm_w_glu.astype(F32), transpose=True)
    wo_t = _cast_bf16(ssm_w_out.astype(F32), transpose=True)
    w_in0 = conv_w_in.astype(F32)
    conv_w8 = jnp.zeros((8, D_INNER), F32).at[0:3].set(conv_w[0].astype(F32))
    conv_b2 = conv_b[0].astype(F32).reshape(1, D_INNER)
    b_glu = ssm_b_glu[0].astype(F32).reshape(D_INNER, 1)
    nw0 = norm_w[0].astype(F32).reshape(1, D_MODEL)
    nw1 = norm_w[1].astype(F32).reshape(1, D_MODEL)
    nwf = final_norm_w.astype(F32).reshape(1, D_MODEL)
    kall, w_in_s, cout, are, aim = _ssm_operators(
        ssm_a_re[0], ssm_a_im[0], ssm_b_re[0], ssm_b_im[0], ssm_c_re[0], ssm_c_im[0],
        ssm_d[0], ssm_log_dt[0])
    sre = state_ssm_re[0].astype(F32).reshape(n_seq, N_STATE_ROWS)
    sim = state_ssm_im[0].astype(F32).reshape(n_seq, N_STATE_ROWS)

    xp = x_prompt.astype(F32)
    x_sp = jnp.concatenate([
        meta_tokens.astype(F32), x_sample.astype(F32).reshape(n_sample_rows, D_MODEL),
        jnp.zeros((SPECIAL_ROWS - N_META - n_sample_rows, D_MODEL), F32)], axis=0)[None]

    zeros8 = jnp.zeros((8, D_INNER), F32)
    xn_sp = _rmsnorm(x_sp, nw0, tm=SPECIAL_ROWS)
    xn_p = _rmsnorm(xp, nw0, tm=TM_CONV)
    y_sp, cv_sp = _conv_proj(xn_sp, w_in0, conv_w8, conv_b2, zeros8,
                             cache_conv[0, :, 0, :].astype(F32), cache_conv[0, :, 1, :].astype(F32))
    init8 = zeros8.at[6:8].set(cv_sp[N_META - 2:N_META])
    y_p, tail_p = _conv_proj(xn_p, w_in0, conv_w8, conv_b2, init8)
    h1_all, xs_all = _out_proj(y_p, w_out0, xp, nw1, n_tiles_total=1 + n_ptiles)
    h1_all, xs_all = _out_proj(y_sp, w_out0, x_sp, nw1, n_tiles_total=1 + n_ptiles,
                               h_all=h1_all, xs_all=xs_all)

    gt, szt, hs_re, hs_im, hp_re, hp_im = _ssm(
        xs_all.reshape(1 + n_ptiles, TILE, D_MODEL), wuz_t, kall, w_in_s, cout, are, aim, sre, sim)
    o_sp = _glu_out(gt, szt, wg_t, b_glu, wo_t, h1_all, nwf, n_tiles=1, tile_off=0)
    o_p = _glu_out(gt, szt, wg_t, b_glu, wo_t, h1_all, nwf, n_tiles=n_ptiles, tile_off=1)

    y_prompt = o_p.reshape(1, n_prompt_rows, D_MODEL)
    n_sc = n_sample_rows // CHUNK
    y_sample = o_sp[1:1 + n_sc].reshape(n_seq, seq_len, D_MODEL)
    new_conv_prompt = tail_p[-1, 6:8].reshape(1, 1, 2, D_INNER)
    cv_s = cv_sp[N_META:N_META + n_sample_rows].reshape(n_seq, seq_len, D_INNER)
    new_conv_sample = cv_s[:, seq_len - 2:].reshape(1, n_seq, 2, D_INNER)
    p_shape = (1, 1, N_GROUPS, SSM_STATE)
    s_shape = (1, n_seq, N_GROUPS, SSM_STATE)
    return (y_prompt, y_sample, new_conv_prompt, new_conv_sample,
            hp_re[0].reshape(p_shape), hp_im[0].reshape(p_shape),
            hs_re.reshape(s_shape), hs_im.reshape(s_shape))
```

```python
import functools

import jax
import jax.numpy as jnp
from jax import lax
from jax.experimental import pallas as pl
from jax.experimental.pallas import tpu as pltpu

F32 = jnp.float32
BF16 = jnp.bfloat16

D_MODEL = 2048
D_INNER = 4096
N_META = 16
SSM_GROUP = 16
N_GROUPS = D_INNER // SSM_GROUP
SSM_STATE = 64
N_STATE_ROWS = N_GROUPS * SSM_STATE
RMS_EPS = 1e-6

CHUNK = 16
LANES = 128
TILE = CHUNK * LANES
SPECIAL_ROWS = 1024

TM_CONV = 1024
TE_CONV = 256
TM_OUT = 512
TE_SSM = 256
TE_GLU = 512
LB_GLU = 512
PREP_BLOCK = 1024

VMEM_LIMIT = 56 * 1024 * 1024


def _params(n_axes, vmem=VMEM_LIMIT):
    return pltpu.CompilerParams(dimension_semantics=("arbitrary",) * n_axes,
                                vmem_limit_bytes=vmem)


def _rms(x, w):
    ms = jnp.mean(x * x, axis=-1, keepdims=True)
    return x * lax.rsqrt(ms + RMS_EPS) * w


def _cast_kernel(x_ref, o_ref, *, transpose):
    x = x_ref[...]
    o_ref[...] = (x.T if transpose else x).astype(BF16)


def _cast_bf16(w, *, transpose):
    _, r, c = w.shape
    b = PREP_BLOCK
    out_shape = (c, r) if transpose else (r, c)
    out_map = (lambda i, j: (j, i)) if transpose else (lambda i, j: (i, j))
    return pl.pallas_call(
        functools.partial(_cast_kernel, transpose=transpose),
        grid=(r // b, c // b),
        in_specs=[pl.BlockSpec((None, b, b), lambda i, j: (0, i, j))],
        out_specs=pl.BlockSpec((b, b), out_map),
        out_shape=jax.ShapeDtypeStruct(out_shape, BF16),
        compiler_params=_params(2),
        name="cast_t" if transpose else "cast",
    )(w)


def _rmsnorm_kernel(x_ref, w_ref, o_ref):
    o_ref[...] = _rms(x_ref[...], w_ref[...]).astype(BF16)


def _rmsnorm(x, w, *, tm):
    rows = x.shape[1]
    return pl.pallas_call(
        _rmsnorm_kernel, grid=(rows // tm,),
        in_specs=[pl.BlockSpec((None, tm, D_MODEL), lambda i: (0, i, 0)),
                  pl.BlockSpec((1, D_MODEL), lambda i: (0, 0))],
        out_specs=pl.BlockSpec((tm, D_MODEL), lambda i: (i, 0)),
        out_shape=jax.ShapeDtypeStruct((rows, D_MODEL), BF16),
        compiler_params=_params(1), name="rmsnorm",
    )(x, w)


def _conv_proj_kernel(*refs, tm, te, n_pieces, special):
    if special:
        (xn_ref, wb_ref, wc_ref, wv_ref, wz_ref, cw_ref, cb_ref, init_ref, c0_ref, c1_ref,
         y_ref, cv_ref, w_scr, s_scr, p1_scr, p2_scr) = refs
    else:
        (xn_ref, wb_ref, wc_ref, wv_ref, wz_ref, cw_ref, cb_ref, init_ref,
         y_ref, tail_ref, w_scr, s_scr) = refs
    i = pl.program_id(1)

    @pl.when(i == 0)
    def _():
        for p, w_ref in enumerate((wb_ref, wc_ref, wv_ref, wz_ref)):
            w_scr[:, p * te:(p + 1) * te] = w_ref[...].astype(BF16)
        s_scr[0:8, :] = init_ref[...]

    pm = tm // n_pieces
    for r in range(n_pieces):
        lo = r * pm
        proj = jnp.dot(xn_ref[lo:lo + pm, :], w_scr[...], preferred_element_type=F32)
        bg = proj[:, 0 * te:1 * te]
        cg = proj[:, 1 * te:2 * te]
        vv = proj[:, 2 * te:3 * te]
        zz = proj[:, 3 * te:4 * te]
        cv = cg * vv
        s_scr[8 + lo:8 + lo + pm, :] = cv
        if special:
            cv_ref[lo:lo + pm, :] = cv
            p1_scr[...] = s_scr[7 + lo:7 + lo + pm, :]
            p2_scr[...] = s_scr[6 + lo:6 + lo + pm, :]
            for q in range(c0_ref.shape[0]):
                r0 = N_META + 32 * q
                p1_scr[r0:r0 + 1, :] = c1_ref[q:q + 1, :]
                p2_scr[r0:r0 + 1, :] = c0_ref[q:q + 1, :]
                p2_scr[r0 + 1:r0 + 2, :] = c1_ref[q:q + 1, :]
            p1 = p1_scr[...]
            p2 = p2_scr[...]
        else:
            p1 = s_scr[7 + lo:7 + lo + pm, :]
            p2 = s_scr[6 + lo:6 + lo + pm, :]
        conv = cb_ref[...] + cw_ref[0:1, :] * p2
        conv = conv + cw_ref[1:2, :] * p1
        conv = conv + cw_ref[2:3, :] * cv
        y_ref[lo:lo + pm, :] = (bg * conv * jax.nn.silu(zz)).astype(BF16)

    if not special:
        tail = s_scr[tm:tm + 8, :]
        s_scr[0:8, :] = tail
        tail_ref[...] = tail


def _conv_proj(xn, w_in, conv_w8, conv_b, init8, cache0=None, cache1=None):
    rows = xn.shape[0]
    tm, te = TM_CONV, TE_CONV
    n_j = D_INNER // te
    n_i = rows // tm
    special = cache0 is not None
    w_spec = lambda p: pl.BlockSpec((None, D_MODEL, te),
                                    lambda j, i, p=p: (0, 0, p * n_j + j))
    in_specs = [pl.BlockSpec((tm, D_MODEL), lambda j, i: (i, 0)),
                w_spec(0), w_spec(1), w_spec(2), w_spec(3),
                pl.BlockSpec((8, te), lambda j, i: (0, j)),
                pl.BlockSpec((1, te), lambda j, i: (0, j)),
                pl.BlockSpec((8, te), lambda j, i: (0, j))]
    args = [xn, w_in, w_in, w_in, w_in, conv_w8, conv_b, init8]
    scratch = [pltpu.VMEM((D_MODEL, 4 * te), BF16), pltpu.VMEM((tm + 8, te), F32)]
    y_shape = jax.ShapeDtypeStruct((rows, D_INNER), BF16)
    y_spec = pl.BlockSpec((tm, te), lambda j, i: (i, j))
    if special:
        assert n_i == 1
        nq = cache0.shape[0]
        in_specs += [pl.BlockSpec((nq, te), lambda j, i: (0, j)),
                     pl.BlockSpec((nq, te), lambda j, i: (0, j))]
        args += [cache0, cache1]
        out_shape = (y_shape, jax.ShapeDtypeStruct((rows, D_INNER), F32))
        out_specs = (y_spec, pl.BlockSpec((tm, te), lambda j, i: (i, j)))
        scratch += [pltpu.VMEM((tm, te), F32), pltpu.VMEM((tm, te), F32)]
    else:
        out_shape = (y_shape, jax.ShapeDtypeStruct((n_i, 8, D_INNER), F32))
        out_specs = (y_spec, pl.BlockSpec((None, 8, te), lambda j, i: (i, 0, j)))
    return pl.pallas_call(
        functools.partial(_conv_proj_kernel, tm=tm, te=te, n_pieces=1 if special else 2,
                          special=special),
        grid=(n_j, n_i), in_specs=in_specs, out_specs=out_specs, out_shape=out_shape,
        scratch_shapes=scratch, compiler_params=_params(2),
        name="conv_proj_special" if special else "conv_proj",
    )(*args)


def _out_proj_kernel(*refs, n_real, aliased):
    if aliased:
        y_ref, w_ref, x_ref, nw_ref, _, _, h_ref, xs_ref, slab_scr = refs
    else:
        y_ref, w_ref, x_ref, nw_ref, h_ref, xs_ref, slab_scr = refs
    nc = y_ref.shape[0] // CHUNK
    n_slabs = D_MODEL // LANES

    def compute():
        h = x_ref[...] + jnp.dot(y_ref[...], w_ref[...], preferred_element_type=F32)
        for k in range(n_slabs):
            slab_scr[k] = h[:, LANES * k:LANES * (k + 1)]
        for s in range(CHUNK):
            hs = jnp.concatenate(
                [slab_scr[k, pl.ds(s, nc, stride=CHUNK), :] for k in range(n_slabs)], axis=1)
            h_ref[s] = hs
            xs_ref[s] = _rms(hs, nw_ref[...]).astype(BF16)

    i = pl.program_id(0)
    pl.when(i < n_real)(compute)

    @pl.when(i >= n_real)
    def _():
        h_ref[...] = jnp.zeros_like(h_ref)
        xs_ref[...] = jnp.zeros_like(xs_ref)


def _out_proj(y, w_out, x, norm_w, *, n_tiles_total, h_all=None, xs_all=None):
    tm = TM_OUT
    nc = tm // CHUNK
    per_tile = LANES // nc
    aliased = xs_all is not None
    n_real = x.shape[1] // tm
    if aliased:
        n_i = n_real
        tile_map = lambda i: (0, 0, i, 0)
    else:
        n_i = n_real + per_tile
        tile_map = lambda i: (jnp.where(i < n_real, i // per_tile + 1, 0), 0,
                              jnp.where(i < n_real, i % per_tile, i - n_real), 0)
    clamp = lambda i: jnp.minimum(i, n_real - 1)
    in_specs = [
        pl.BlockSpec((tm, D_INNER), lambda i: (clamp(i), 0)),
        pl.BlockSpec((D_INNER, D_MODEL), lambda i: (0, 0), pipeline_mode=pl.Buffered(1)),
        pl.BlockSpec((None, tm, D_MODEL), lambda i: (0, clamp(i), 0)),
        pl.BlockSpec((1, D_MODEL), lambda i: (0, 0)),
    ]
    args = [y, w_out, x, norm_w]
    aliases = {}
    if aliased:
        in_specs += [pl.BlockSpec(memory_space=pl.ANY), pl.BlockSpec(memory_space=pl.ANY)]
        args += [h_all, xs_all]
        aliases = {4: 0, 5: 1}
    tile_shape = (n_tiles_total, CHUNK, LANES, D_MODEL)
    return pl.pallas_call(
        functools.partial(_out_proj_kernel, n_real=n_real, aliased=aliased),
        grid=(n_i,),
        in_specs=in_specs,
        out_specs=(pl.BlockSpec((None, CHUNK, nc, D_MODEL), tile_map),
                   pl.BlockSpec((None, CHUNK, nc, D_MODEL), tile_map)),
        out_shape=(jax.ShapeDtypeStruct(tile_shape, F32),
                   jax.ShapeDtypeStruct(tile_shape, BF16)),
        scratch_shapes=[pltpu.VMEM((D_MODEL // LANES, tm, LANES), F32)],
        input_output_aliases=aliases,
        compiler_params=_params(1, 60 * 1024 * 1024),
        name="out_proj_special" if aliased else "out_proj",
    )(*args)


def _split_bf16(x):
    hi = x.astype(BF16)
    return hi, (x - hi.astype(F32)).astype(BF16)


def _ssm_kernel(xs_ref, wu_ref, wz_ref, ca_ref, bb_ref, dv_ref, win_ref, cout_ref,
                are_ref, aim_ref, sre_ref, sim_ref,
                gt_ref, szt_ref, hsre_ref, hsim_ref, hpre_ref, hpim_ref,
                w_scr, tk_scr, zt_scr, xre_scr, xim_scr, pre_scr, pim_scr, cre_scr, cim_scr,
                *, te, n_tiles):
    i = pl.program_id(1)
    gb = te // SSM_GROUP
    rows = gb * SSM_STATE
    n_seq = sre_ref.shape[0]
    bdims = (((2,), (1,)), ((0,), (0,)))

    @pl.when(i == 0)
    def _():
        w_scr[0:te, :] = wu_ref[...]
        w_scr[te:2 * te, :] = wz_ref[...]
        lane_c = lax.broadcasted_iota(jnp.int32, (SSM_GROUP, 256), 1) & (SSM_GROUP - 1)
        diag = lax.broadcasted_iota(jnp.int32, (SSM_GROUP, 256), 0) == lane_c
        rep = diag.astype(BF16)
        bhi, blo = _split_bf16(bb_ref[...].reshape(gb * 2 * SSM_STATE, SSM_GROUP))
        tile16 = lambda b: jnp.dot(b, rep, preferred_element_type=F32).astype(BF16).reshape(
            gb, 2 * SSM_STATE, 256)
        bhi, blo = tile16(bhi), tile16(blo)
        ahi, alo = _split_bf16(ca_ref[...])
        kw = (lax.dot_general(ahi, bhi, bdims, preferred_element_type=F32)
              + lax.dot_general(ahi, blo, bdims, preferred_element_type=F32)
              + lax.dot_general(alo, bhi, bdims, preferred_element_type=F32))
        kw0 = (kw[:, 0:SSM_GROUP, :] + jnp.where(diag, dv_ref[...], 0.0)).astype(BF16)
        kw = kw.astype(BF16)
        tk_scr[...] = jnp.zeros_like(tk_scr)
        for s in range(CHUNK):
            lo = SSM_GROUP * s
            hi = lo + SSM_GROUP
            tk_scr[:, lo:hi, lo:hi] = kw0[:, :, lo:hi]
            if hi < 256:
                tk_scr[:, hi:256, lo:hi] = kw[:, SSM_GROUP:256 - lo, lo:hi]

    uz = lax.dot_general(w_scr[...], xs_ref[...], (((1,), (1,)), ((), ())),
                         preferred_element_type=F32)
    szt_ref[...] = jax.nn.silu(uz[te:, :]).astype(BF16)
    ub = uz[:te, :].astype(BF16)
    for s in range(CHUNK):
        zt_scr[:, CHUNK * s:CHUNK * (s + 1), :] = (
            ub[:, LANES * s:LANES * (s + 1)].reshape(gb, SSM_GROUP, LANES))
    zt = zt_scr[...]
    yt = lax.dot_general(tk_scr[...], zt, bdims, preferred_element_type=F32)
    xt = lax.dot_general(win_ref[...], zt, bdims, preferred_element_type=F32)
    xre_scr[...] = xt[:, 0:SSM_STATE, :].reshape(rows, LANES).T
    xim_scr[...] = xt[:, SSM_STATE:, :].reshape(rows, LANES).T

    ar = are_ref[...]
    ai = aim_ref[...]

    def step(hr, hi, n):
        xr = xre_scr[pl.ds(n, 1), :]
        xi = xim_scr[pl.ds(n, 1), :]
        return ar * hr - ai * hi + xr, ar * hi + ai * hr + xi

    @pl.when(i == 0)
    def _():
        pre_scr[...] = jnp.zeros_like(pre_scr)
        pim_scr[...] = jnp.zeros_like(pim_scr)
        zero = jnp.zeros((1, rows), F32)
        hr, hi = step(zero, zero, 0)
        cre_scr[0:1, :] = hr
        cim_scr[0:1, :] = hi
        for q in range(n_seq):
            hr = sre_ref[q:q + 1, :]
            hi = sim_ref[q:q + 1, :]
            for e in range(2):
                n = 1 + 2 * q + e
                pre_scr[n:n + 1, :] = hr
                pim_scr[n:n + 1, :] = hi
                hr, hi = step(hr, hi, n)
            hsre_ref[q:q + 1, :] = hr
            hsim_ref[q:q + 1, :] = hi

    @pl.when(i > 0)
    def _():
        def body(n, carry):
            hr, hi = carry
            pre_scr[pl.ds(n, 1), :] = hr
            pim_scr[pl.ds(n, 1), :] = hi
            return step(hr, hi, n)

        hr, hi = lax.fori_loop(0, LANES, body, (cre_scr[0:1, :], cim_scr[0:1, :]), unroll=8)
        cre_scr[0:1, :] = hr
        cim_scr[0:1, :] = hi

    @pl.when(i == n_tiles - 1)
    def _():
        hpre_ref[...] = jnp.broadcast_to(cre_scr[0:1, :], hpre_ref.shape)
        hpim_ref[...] = jnp.broadcast_to(cim_scr[0:1, :], hpim_ref.shape)

    hp = jnp.concatenate([pre_scr[...].T.reshape(gb, SSM_STATE, LANES),
                          pim_scr[...].T.reshape(gb, SSM_STATE, LANES)], axis=1).astype(BF16)
    ycorr = lax.dot_general(cout_ref[...], hp, bdims, preferred_element_type=F32)
    g = jax.nn.gelu(yt + ycorr)
    for t in range(CHUNK):
        gt_ref[:, LANES * t:LANES * (t + 1)] = (
            g[:, SSM_GROUP * t:SSM_GROUP * (t + 1), :].reshape(te, LANES).astype(BF16))


def _ssm(xs_all, wuz_t, ca, bb, dv, w_in, cout, are, aim, sre, sim):
    n_tiles = xs_all.shape[0]
    te = TE_SSM
    n_j = D_INNER // te
    gb = te // SSM_GROUP
    rows = gb * SSM_STATE
    n_seq = sre.shape[0]
    act_shape = jax.ShapeDtypeStruct((n_tiles, D_INNER, TILE), BF16)
    act_spec = pl.BlockSpec((None, te, TILE), lambda j, i: (i, j, 0))
    hs_shape = jax.ShapeDtypeStruct((n_seq, N_STATE_ROWS), F32)
    hs_spec = pl.BlockSpec((n_seq, rows), lambda j, i: (0, j))
    hp_shape = jax.ShapeDtypeStruct((8, N_STATE_ROWS), F32)
    hp_spec = pl.BlockSpec((8, rows), lambda j, i: (0, j))
    row_spec = pl.BlockSpec((1, rows), lambda j, i: (0, j))
    return pl.pallas_call(
        functools.partial(_ssm_kernel, te=te, n_tiles=n_tiles),
        grid=(n_j, n_tiles),
        in_specs=[
            pl.BlockSpec((None, TILE, D_MODEL), lambda j, i: (i, 0, 0)),
            pl.BlockSpec((te, D_MODEL), lambda j, i: (j, 0)),
            pl.BlockSpec((te, D_MODEL), lambda j, i: (n_j + j, 0)),
            pl.BlockSpec((gb, 256, 2 * SSM_STATE), lambda j, i: (j, 0, 0)),
            pl.BlockSpec((gb, 2 * SSM_STATE, SSM_GROUP), lambda j, i: (j, 0, 0)),
            pl.BlockSpec((gb, SSM_GROUP, 1), lambda j, i: (j, 0, 0)),
            pl.BlockSpec((gb, 2 * SSM_STATE, 256), lambda j, i: (j, 0, 0)),
            pl.BlockSpec((gb, 256, 2 * SSM_STATE), lambda j, i: (j, 0, 0)),
            row_spec, row_spec, hs_spec, hs_spec,
        ],
        out_specs=(act_spec, act_spec, hs_spec, hs_spec, hp_spec, hp_spec),
        out_shape=(act_shape, act_shape, hs_shape, hs_shape, hp_shape, hp_shape),
        scratch_shapes=[pltpu.VMEM((2 * te, D_MODEL), BF16),
                        pltpu.VMEM((gb, 256, 256), BF16),
                        pltpu.VMEM((gb, 256, LANES), BF16),
                        pltpu.VMEM((LANES, rows), F32), pltpu.VMEM((LANES, rows), F32),
                        pltpu.VMEM((LANES, rows), F32), pltpu.VMEM((LANES, rows), F32),
                        pltpu.VMEM((8, rows), F32), pltpu.VMEM((8, rows), F32)],
        compiler_params=_params(2),
        name="s5_scan",
    )(xs_all, wuz_t, wuz_t, ca, bb, dv, w_in, cout, are, aim, sre, sim)


def _glu_out_kernel(gt_ref, szt_ref, wg_ref, bg_ref, wo_ref, h_ref, nw_ref, o_ref, acc_scr, *,
                    te, lb):
    e = pl.program_id(2)

    @pl.when(e == 0)
    def _():
        acc_scr[...] = jnp.zeros_like(acc_scr)

    gate = jnp.dot(wg_ref[...], gt_ref[...], preferred_element_type=F32) + bg_ref[...]
    ge = gt_ref[pl.ds(pl.multiple_of(e * te, te), te), :].astype(F32)
    y3 = (ge * jax.nn.sigmoid(gate)) * szt_ref[...].astype(F32)
    acc_scr[...] += jnp.dot(wo_ref[...], y3.astype(BF16), preferred_element_type=F32)

    @pl.when(e == pl.num_programs(2) - 1)
    def _():
        o = acc_scr[...].T
        for t in range(lb // LANES):
            h = h_ref[t] + o[LANES * t:LANES * (t + 1), :]
            o_ref[:, D_MODEL * t:D_MODEL * (t + 1)] = _rms(h, nw_ref[...])


def _glu_out(gt, szt, wg_t, b_glu, wo_t, h_all, norm_w, *, n_tiles, tile_off):
    te = TE_GLU
    lb = LB_GLU
    n_l = TILE // lb
    n_e = D_INNER // te
    hw = (lb // LANES) * D_MODEL
    return pl.pallas_call(
        functools.partial(_glu_out_kernel, te=te, lb=lb),
        grid=(n_tiles, n_l, n_e),
        in_specs=[
            pl.BlockSpec((None, D_INNER, lb), lambda i, l, e: (i + tile_off, 0, l)),
            pl.BlockSpec((None, te, lb), lambda i, l, e: (i + tile_off, e, l)),
            pl.BlockSpec((te, D_INNER), lambda i, l, e: (e, 0)),
            pl.BlockSpec((te, 1), lambda i, l, e: (e, 0)),
            pl.BlockSpec((D_MODEL, te), lambda i, l, e: (0, e)),
            pl.BlockSpec((None, lb // LANES, LANES, D_MODEL),
                         lambda i, l, e: (i + tile_off, l, 0, 0)),
            pl.BlockSpec((1, D_MODEL), lambda i, l, e: (0, 0)),
        ],
        out_specs=pl.BlockSpec((LANES, hw), lambda i, l, e: (i, l)),
        out_shape=jax.ShapeDtypeStruct((n_tiles * LANES, CHUNK * D_MODEL), F32),
        scratch_shapes=[pltpu.VMEM((D_MODEL, lb), F32)],
        compiler_params=_params(3),
        name="glu_out",
    )(gt, szt, wg_t, b_glu, wo_t, h_all, norm_w)


def _ssm_operators(a_re, a_im, b_re, b_im, c_re, c_im, d_vec, log_dt):
    lam = lax.complex(a_re.astype(F32), a_im.astype(F32))
    dt = jnp.exp(log_dt.astype(F32))[:, None]
    a_bar = jnp.exp(lam * dt)
    b_bar = ((a_bar - 1.0) / lam)[..., None] * lax.complex(b_re.astype(F32), b_im.astype(F32))
    c_mat = lax.complex(c_re.astype(F32), c_im.astype(F32))
    taus = jnp.arange(CHUNK + 1, dtype=F32)
    apow = jnp.exp((lam * dt)[:, None, :] * taus[None, :, None])
    cap = c_mat[:, None] * apow[:, :, None, :]

    def re_im_cols(z):
        z = z.reshape(N_GROUPS, CHUNK * SSM_GROUP, SSM_STATE)
        return jnp.concatenate([jnp.real(z), -jnp.imag(z)], axis=2)

    ca = re_im_cols(cap[:, :CHUNK])
    cout = re_im_cols(cap[:, 1:]).astype(BF16)
    bb = jnp.concatenate([jnp.real(b_bar), jnp.imag(b_bar)], axis=1)
    dv = d_vec.astype(F32).reshape(N_GROUPS, SSM_GROUP, 1)

    wb = apow[:, CHUNK - 1::-1, :].transpose(0, 2, 1)[..., None] * b_bar[:, :, None, :]
    wb = wb.reshape(N_GROUPS, SSM_STATE, CHUNK * SSM_GROUP)
    w_in = jnp.concatenate([jnp.real(wb), jnp.imag(wb)], axis=1).astype(BF16)

    alpha = apow[:, CHUNK, :]
    return (ca, bb, dv, w_in, cout, jnp.real(alpha).reshape(1, N_STATE_ROWS),
            jnp.imag(alpha).reshape(1, N_STATE_ROWS))


def kernel(x_prompt, x_sample, cache_conv, state_ssm_re, state_ssm_im, meta_tokens, norm_w,
           final_norm_w, conv_w_in, conv_w, conv_b, conv_w_out, ssm_w_in, ssm_a_re, ssm_a_im,
           ssm_b_re, ssm_b_im, ssm_c_re, ssm_c_im, ssm_d, ssm_log_dt, ssm_w_glu, ssm_b_glu,
           ssm_w_out):
    n_seq, seq_len = x_sample.shape[0], x_sample.shape[1]
    n_prompt_rows = x_prompt.shape[1]
    n_ptiles = n_prompt_rows // TILE
    n_sample_rows = n_seq * seq_len
    assert x_prompt.shape[0] == 1 and seq_len == 2 * CHUNK and N_META == CHUNK
    assert n_prompt_rows % TILE == 0 and N_META + n_sample_rows <= SPECIAL_ROWS

    w_out0 = _cast_bf16(conv_w_out.astype(F32), transpose=False)
    wuz_t = _cast_bf16(ssm_w_in.astype(F32), transpose=True)
    wg_t = _cast_bf16(ssm_w_glu.astype(F32), transpose=True)
    wo_t = _cast_bf16(ssm_w_out.astype(F32), transpose=True)
    w_in0 = conv_w_in.astype(F32)
    conv_w8 = jnp.zeros((8, D_INNER), F32).at[0:3].set(conv_w[0].astype(F32))
    conv_b2 = conv_b[0].astype(F32).reshape(1, D_INNER)
    b_glu = ssm_b_glu[0].astype(F32).reshape(D_INNER, 1)
    nw0 = norm_w[0].astype(F32).reshape(1, D_MODEL)
    nw1 = norm_w[1].astype(F32).reshape(1, D_MODEL)
    nwf = final_norm_w.astype(F32).reshape(1, D_MODEL)
    ca, bb, dv, w_in_s, cout, are, aim = _ssm_operators(
        ssm_a_re[0], ssm_a_im[0], ssm_b_re[0], ssm_b_im[0], ssm_c_re[0], ssm_c_im[0],
        ssm_d[0], ssm_log_dt[0])
    sre = state_ssm_re[0].astype(F32).reshape(n_seq, N_STATE_ROWS)
    sim = state_ssm_im[0].astype(F32).reshape(n_seq, N_STATE_ROWS)

    xp = x_prompt.astype(F32)
    x_sp = jnp.concatenate([
        meta_tokens.astype(F32), x_sample.astype(F32).reshape(n_sample_rows, D_MODEL),
        jnp.zeros((SPECIAL_ROWS - N_META - n_sample_rows, D_MODEL), F32)], axis=0)[None]

    zeros8 = jnp.zeros((8, D_INNER), F32)
    xn_sp = _rmsnorm(x_sp, nw0, tm=SPECIAL_ROWS)
    xn_p = _rmsnorm(xp, nw0, tm=TM_CONV)
    y_sp, cv_sp = _conv_proj(xn_sp, w_in0, conv_w8, conv_b2, zeros8,
                             cache_conv[0, :, 0, :].astype(F32), cache_conv[0, :, 1, :].astype(F32))
    init8 = zeros8.at[6:8].set(cv_sp[N_META - 2:N_META])
    y_p, tail_p = _conv_proj(xn_p, w_in0, conv_w8, conv_b2, init8)
    h1_all, xs_all = _out_proj(y_p, w_out0, xp, nw1, n_tiles_total=1 + n_ptiles)
    h1_all, xs_all = _out_proj(y_sp, w_out0, x_sp, nw1, n_tiles_total=1 + n_ptiles,
                               h_all=h1_all, xs_all=xs_all)

    gt, szt, hs_re, hs_im, hp_re, hp_im = _ssm(
        xs_all.reshape(1 + n_ptiles, TILE, D_MODEL), wuz_t, ca, bb, dv, w_in_s, cout, are, aim,
        sre, sim)
    o_sp = _glu_out(gt, szt, wg_t, b_glu, wo_t, h1_all, nwf, n_tiles=1, tile_off=0)
    o_p = _glu_out(gt, szt, wg_t, b_glu, wo_t, h1_all, nwf, n_tiles=n_ptiles, tile_off=1)

    y_prompt = o_p.reshape(1, n_prompt_rows, D_MODEL)
    n_sc = n_sample_rows // CHUNK
    y_sample = o_sp[1:1 + n_sc].reshape(n_seq, seq_len, D_MODEL)
    new_conv_prompt = tail_p[-1, 6:8].reshape(1, 1, 2, D_INNER)
    cv_s = cv_sp[N_META:N_META + n_sample_rows].reshape(n_seq, seq_len, D_INNER)
    new_conv_sample = cv_s[:, seq_len - 2:].reshape(1, n_seq, 2, D_INNER)
    p_shape = (1, 1, N_GROUPS, SSM_STATE)
    s_shape = (1, n_seq, N_GROUPS, SSM_STATE)
    return (y_prompt, y_sample, new_conv_prompt, new_conv_sample,
            hp_re[0].reshape(p_shape), hp_im[0].reshape(p_shape),
            hs_re.reshape(s_shape), hs_im.reshape(s_shape))
```

```python
import functools

import jax
import jax.numpy as jnp
from jax import lax
from jax.experimental import pallas as pl
from jax.experimental.pallas import tpu as pltpu

F32 = jnp.float32
BF16 = jnp.bfloat16

D_MODEL = 2048
D_INNER = 4096
N_META = 16
SSM_GROUP = 16
N_GROUPS = D_INNER // SSM_GROUP
SSM_STATE = 64
N_STATE_ROWS = N_GROUPS * SSM_STATE
RMS_EPS = 1e-6

CHUNK = 16
LANES = 128
TILE = CHUNK * LANES
SPECIAL_ROWS = 1024

TM_CONV = 1024
TE_CONV = 256
TM_OUT = 512
TE_SSM = 256
TE_GLU = 512
LB_GLU = 512
PREP_BLOCK = 1024

VMEM_LIMIT = 56 * 1024 * 1024


def _params(n_axes, vmem=VMEM_LIMIT):
    return pltpu.CompilerParams(dimension_semantics=("arbitrary",) * n_axes,
                                vmem_limit_bytes=vmem)


def _rms(x, w):
    ms = jnp.mean(x * x, axis=-1, keepdims=True)
    return x * lax.rsqrt(ms + RMS_EPS) * w


def _cast_kernel(x_ref, o_ref, *, transpose):
    x = x_ref[...]
    o_ref[...] = (x.T if transpose else x).astype(BF16)


def _cast_bf16(w, *, transpose):
    _, r, c = w.shape
    b = PREP_BLOCK
    out_shape = (c, r) if transpose else (r, c)
    out_map = (lambda i, j: (j, i)) if transpose else (lambda i, j: (i, j))
    return pl.pallas_call(
        functools.partial(_cast_kernel, transpose=transpose),
        grid=(r // b, c // b),
        in_specs=[pl.BlockSpec((None, b, b), lambda i, j: (0, i, j))],
        out_specs=pl.BlockSpec((b, b), out_map),
        out_shape=jax.ShapeDtypeStruct(out_shape, BF16),
        compiler_params=_params(2),
        name="cast_t" if transpose else "cast",
    )(w)


def _rmsnorm_kernel(x_ref, w_ref, o_ref):
    o_ref[...] = _rms(x_ref[...], w_ref[...]).astype(BF16)


def _rmsnorm(x, w, *, tm):
    rows = x.shape[1]
    return pl.pallas_call(
        _rmsnorm_kernel, grid=(rows // tm,),
        in_specs=[pl.BlockSpec((None, tm, D_MODEL), lambda i: (0, i, 0)),
                  pl.BlockSpec((1, D_MODEL), lambda i: (0, 0))],
        out_specs=pl.BlockSpec((tm, D_MODEL), lambda i: (i, 0)),
        out_shape=jax.ShapeDtypeStruct((rows, D_MODEL), BF16),
        compiler_params=_params(1), name="rmsnorm",
    )(x, w)


def _conv_proj_kernel(*refs, tm, te, n_pieces, special):
    if special:
        (xn_ref, wb_ref, wc_ref, wv_ref, wz_ref, cw_ref, cb_ref, init_ref, c0_ref, c1_ref,
         y_ref, cv_ref, w_scr, s_scr, p1_scr, p2_scr) = refs
    else:
        (xn_ref, wb_ref, wc_ref, wv_ref, wz_ref, cw_ref, cb_ref, init_ref,
         y_ref, tail_ref, w_scr, s_scr) = refs
    i = pl.program_id(1)

    @pl.when(i == 0)
    def _():
        for p, w_ref in enumerate((wb_ref, wc_ref, wv_ref, wz_ref)):
            w_scr[:, p * te:(p + 1) * te] = w_ref[...].astype(BF16)
        s_scr[0:8, :] = init_ref[...]

    pm = tm // n_pieces
    for r in range(n_pieces):
        lo = r * pm
        proj = jnp.dot(xn_ref[lo:lo + pm, :], w_scr[...], preferred_element_type=F32)
        bg = proj[:, 0 * te:1 * te]
        cg = proj[:, 1 * te:2 * te]
        vv = proj[:, 2 * te:3 * te]
        zz = proj[:, 3 * te:4 * te]
        cv = cg * vv
        s_scr[8 + lo:8 + lo + pm, :] = cv
        if special:
            cv_ref[lo:lo + pm, :] = cv
            p1_scr[...] = s_scr[7 + lo:7 + lo + pm, :]
            p2_scr[...] = s_scr[6 + lo:6 + lo + pm, :]
            for q in range(c0_ref.shape[0]):
                r0 = N_META + 32 * q
                p1_scr[r0:r0 + 1, :] = c1_ref[q:q + 1, :]
                p2_scr[r0:r0 + 1, :] = c0_ref[q:q + 1, :]
                p2_scr[r0 + 1:r0 + 2, :] = c1_ref[q:q + 1, :]
            p1 = p1_scr[...]
            p2 = p2_scr[...]
        else:
            p1 = s_scr[7 + lo:7 + lo + pm, :]
            p2 = s_scr[6 + lo:6 + lo + pm, :]
        conv = cb_ref[...] + cw_ref[0:1, :] * p2
        conv = conv + cw_ref[1:2, :] * p1
        conv = conv + cw_ref[2:3, :] * cv
        y_ref[lo:lo + pm, :] = (bg * conv * jax.nn.silu(zz)).astype(BF16)

    if not special:
        tail = s_scr[tm:tm + 8, :]
        s_scr[0:8, :] = tail
        tail_ref[...] = tail


def _conv_proj(xn, w_in, conv_w8, conv_b, init8, cache0=None, cache1=None):
    rows = xn.shape[0]
    tm, te = TM_CONV, TE_CONV
    n_j = D_INNER // te
    n_i = rows // tm
    special = cache0 is not None
    w_spec = lambda p: pl.BlockSpec((None, D_MODEL, te),
                                    lambda j, i, p=p: (0, 0, p * n_j + j))
    in_specs = [pl.BlockSpec((tm, D_MODEL), lambda j, i: (i, 0)),
                w_spec(0), w_spec(1), w_spec(2), w_spec(3),
                pl.BlockSpec((8, te), lambda j, i: (0, j)),
                pl.BlockSpec((1, te), lambda j, i: (0, j)),
                pl.BlockSpec((8, te), lambda j, i: (0, j))]
    args = [xn, w_in, w_in, w_in, w_in, conv_w8, conv_b, init8]
    scratch = [pltpu.VMEM((D_MODEL, 4 * te), BF16), pltpu.VMEM((tm + 8, te), F32)]
    y_shape = jax.ShapeDtypeStruct((rows, D_INNER), BF16)
    y_spec = pl.BlockSpec((tm, te), lambda j, i: (i, j))
    if special:
        assert n_i == 1
        nq = cache0.shape[0]
        in_specs += [pl.BlockSpec((nq, te), lambda j, i: (0, j)),
                     pl.BlockSpec((nq, te), lambda j, i: (0, j))]
        args += [cache0, cache1]
        out_shape = (y_shape, jax.ShapeDtypeStruct((rows, D_INNER), F32))
        out_specs = (y_spec, pl.BlockSpec((tm, te), lambda j, i: (i, j)))
        scratch += [pltpu.VMEM((tm, te), F32), pltpu.VMEM((tm, te), F32)]
    else:
        out_shape = (y_shape, jax.ShapeDtypeStruct((n_i, 8, D_INNER), F32))
        out_specs = (y_spec, pl.BlockSpec((None, 8, te), lambda j, i: (i, 0, j)))
    return pl.pallas_call(
        functools.partial(_conv_proj_kernel, tm=tm, te=te, n_pieces=1 if special else 2,
                          special=special),
        grid=(n_j, n_i), in_specs=in_specs, out_specs=out_specs, out_shape=out_shape,
        scratch_shapes=scratch, compiler_params=_params(2),
        name="conv_proj_special" if special else "conv_proj",
    )(*args)


def _out_proj_kernel(*refs, n_real, aliased):
    if aliased:
        y_ref, w_ref, x_ref, nw_ref, _, _, h_ref, xs_ref, slab_scr = refs
    else:
        y_ref, w_ref, x_ref, nw_ref, h_ref, xs_ref, slab_scr = refs
    nc = y_ref.shape[0] // CHUNK
    n_slabs = D_MODEL // LANES

    def compute():
        h = x_ref[...] + jnp.dot(y_ref[...], w_ref[...], preferred_element_type=F32)
        for k in range(n_slabs):
            slab_scr[k] = h[:, LANES * k:LANES * (k + 1)]
        for s in range(CHUNK):
            hs = jnp.concatenate(
                [slab_scr[k, pl.ds(s, nc, stride=CHUNK), :] for k in range(n_slabs)], axis=1)
            h_ref[s] = hs
            xs_ref[s] = _rms(hs, nw_ref[...]).astype(BF16)

    i = pl.program_id(0)
    pl.when(i < n_real)(compute)

    @pl.when(i >= n_real)
    def _():
        h_ref[...] = jnp.zeros_like(h_ref)
        xs_ref[...] = jnp.zeros_like(xs_ref)


def _out_proj(y, w_out, x, norm_w, *, n_tiles_total, h_all=None, xs_all=None):
    tm = TM_OUT
    nc = tm // CHUNK
    per_tile = LANES // nc
    aliased = xs_all is not None
    n_real = x.shape[1] // tm
    if aliased:
        n_i = n_real
        tile_map = lambda i: (0, 0, i, 0)
    else:
        n_i = n_real + per_tile
        tile_map = lambda i: (jnp.where(i < n_real, i // per_tile + 1, 0), 0,
                              jnp.where(i < n_real, i % per_tile, i - n_real), 0)
    clamp = lambda i: jnp.minimum(i, n_real - 1)
    in_specs = [
        pl.BlockSpec((tm, D_INNER), lambda i: (clamp(i), 0)),
        pl.BlockSpec((D_INNER, D_MODEL), lambda i: (0, 0), pipeline_mode=pl.Buffered(1)),
        pl.BlockSpec((None, tm, D_MODEL), lambda i: (0, clamp(i), 0)),
        pl.BlockSpec((1, D_MODEL), lambda i: (0, 0)),
    ]
    args = [y, w_out, x, norm_w]
    aliases = {}
    if aliased:
        in_specs += [pl.BlockSpec(memory_space=pl.ANY), pl.BlockSpec(memory_space=pl.ANY)]
        args += [h_all, xs_all]
        aliases = {4: 0, 5: 1}
    tile_shape = (n_tiles_total, CHUNK, LANES, D_MODEL)
    return pl.pallas_call(
        functools.partial(_out_proj_kernel, n_real=n_real, aliased=aliased),
        grid=(n_i,),
        in_specs=in_specs,
        out_specs=(pl.BlockSpec((None, CHUNK, nc, D_MODEL), tile_map),
                   pl.BlockSpec((None, CHUNK, nc, D_MODEL), tile_map)),
        out_shape=(jax.ShapeDtypeStruct(tile_shape, F32),
                   jax.ShapeDtypeStruct(tile_shape, BF16)),
        scratch_shapes=[pltpu.VMEM((D_MODEL // LANES, tm, LANES), F32)],
        input_output_aliases=aliases,
        compiler_params=_params(1, 60 * 1024 * 1024),
        name="out_proj_special" if aliased else "out_proj",
    )(*args)


def _split_bf16(x):
    hi = x.astype(BF16)
    return hi, (x - hi.astype(F32)).astype(BF16)


def _ssm_kernel(xs_ref, wu_ref, wz_ref, e1_ref, e2_ref, c1_ref, c2_ref, apt_ref, bb_ref, dv_ref,
                are_ref, aim_ref, sre_ref, sim_ref,
                gt_ref, szt_ref, hsre_ref, hsim_ref, hpre_ref, hpim_ref,
                w_scr, tk_scr, win_scr, cout_scr, zt_scr, xre_scr, xim_scr, pre_scr, pim_scr,
                cre_scr, cim_scr, *, te, n_tiles):
    i = pl.program_id(1)
    gb = te // SSM_GROUP
    rows = gb * SSM_STATE
    n_seq = sre_ref.shape[0]
    bdims = (((2,), (1,)), ((0,), (0,)))

    @pl.when(i == 0)
    def _():
        w_scr[0:te, :] = wu_ref[...]
        w_scr[te:2 * te, :] = wz_ref[...]

        c1 = c1_ref[...]
        c2 = c2_ref[...]
        cap = [c1 * e1_ref[:, t:t + 1, :] + c2 * e2_ref[:, t:t + 1, :] for t in range(CHUNK + 1)]
        ca = jnp.concatenate(cap[:CHUNK], axis=1)
        cout_scr[...] = jnp.concatenate(cap[1:], axis=1).astype(BF16)

        lane = lax.broadcasted_iota(jnp.int32, (SSM_GROUP, 256), 1)
        row = lax.broadcasted_iota(jnp.int32, (SSM_GROUP, 256), 0)
        diag = row == (lane & (SSM_GROUP - 1))
        tile_c = diag.astype(BF16)
        spread_s = (row == (lane >> 4)).astype(BF16)
        expand = lambda b, m, n_rows: jnp.dot(b, m, preferred_element_type=F32).reshape(
            gb, n_rows, 256)
        bhi, blo = _split_bf16(bb_ref[...].reshape(gb * 2 * SSM_STATE, SSM_GROUP))
        bhi_t = expand(bhi, tile_c, 2 * SSM_STATE)
        blo_t = expand(blo, tile_c, 2 * SSM_STATE)

        phi, plo = _split_bf16(apt_ref[...].reshape(gb * 2 * SSM_STATE, SSM_GROUP))
        ap_t = expand(phi, spread_s, 2 * SSM_STATE) + expand(plo, spread_s, 2 * SSM_STATE)
        b_t = bhi_t + blo_t
        pr, pi = ap_t[:, :SSM_STATE], ap_t[:, SSM_STATE:]
        br, bi = b_t[:, :SSM_STATE], b_t[:, SSM_STATE:]
        win_scr[:, 0:SSM_STATE, :] = (pr * br - pi * bi).astype(BF16)
        win_scr[:, SSM_STATE:, :] = (pr * bi + pi * br).astype(BF16)

        bhi_t, blo_t = bhi_t.astype(BF16), blo_t.astype(BF16)
        ahi, alo = _split_bf16(ca)
        kw = (lax.dot_general(ahi, bhi_t, bdims, preferred_element_type=F32)
              + lax.dot_general(ahi, blo_t, bdims, preferred_element_type=F32)
              + lax.dot_general(alo, bhi_t, bdims, preferred_element_type=F32))
        kw0 = (kw[:, 0:SSM_GROUP, :] + jnp.where(diag, dv_ref[...], 0.0)).astype(BF16)
        kw = kw.astype(BF16)
        tk_scr[...] = jnp.zeros_like(tk_scr)
        for s in range(CHUNK):
            lo = SSM_GROUP * s
            hi = lo + SSM_GROUP
            tk_scr[:, lo:hi, lo:hi] = kw0[:, :, lo:hi]
            if hi < 256:
                tk_scr[:, hi:256, lo:hi] = kw[:, SSM_GROUP:256 - lo, lo:hi]

    uz = lax.dot_general(w_scr[...], xs_ref[...], (((1,), (1,)), ((), ())),
                         preferred_element_type=F32)
    szt_ref[...] = jax.nn.silu(uz[te:, :]).astype(BF16)
    ub = uz[:te, :].astype(BF16)
    for s in range(CHUNK):
        zt_scr[:, CHUNK * s:CHUNK * (s + 1), :] = (
            ub[:, LANES * s:LANES * (s + 1)].reshape(gb, SSM_GROUP, LANES))
    zt = zt_scr[...]
    yt = lax.dot_general(tk_scr[...], zt, bdims, preferred_element_type=F32)
    xt = lax.dot_general(win_scr[...], zt, bdims, preferred_element_type=F32)
    xre_scr[...] = xt[:, 0:SSM_STATE, :].reshape(rows, LANES).T
    xim_scr[...] = xt[:, SSM_STATE:, :].reshape(rows, LANES).T

    ar = are_ref[...]
    ai = aim_ref[...]

    def step(hr, hi, n):
        xr = xre_scr[pl.ds(n, 1), :]
        xi = xim_scr[pl.ds(n, 1), :]
        return ar * hr - ai * hi + xr, ar * hi + ai * hr + xi

    @pl.when(i == 0)
    def _():
        pre_scr[...] = jnp.zeros_like(pre_scr)
        pim_scr[...] = jnp.zeros_like(pim_scr)
        zero = jnp.zeros((1, rows), F32)
        hr, hi = step(zero, zero, 0)
        cre_scr[0:1, :] = hr
        cim_scr[0:1, :] = hi
        for q in range(n_seq):
            hr = sre_ref[q:q + 1, :]
            hi = sim_ref[q:q + 1, :]
            for e in range(2):
                n = 1 + 2 * q + e
                pre_scr[n:n + 1, :] = hr
                pim_scr[n:n + 1, :] = hi
                hr, hi = step(hr, hi, n)
            hsre_ref[q:q + 1, :] = hr
            hsim_ref[q:q + 1, :] = hi

    @pl.when(i > 0)
    def _():
        def body(n, carry):
            hr, hi = carry
            pre_scr[pl.ds(n, 1), :] = hr
            pim_scr[pl.ds(n, 1), :] = hi
            return step(hr, hi, n)

        hr, hi = lax.fori_loop(0, LANES, body, (cre_scr[0:1, :], cim_scr[0:1, :]), unroll=8)
        cre_scr[0:1, :] = hr
        cim_scr[0:1, :] = hi

    @pl.when(i == n_tiles - 1)
    def _():
        hpre_ref[...] = jnp.broadcast_to(cre_scr[0:1, :], hpre_ref.shape)
        hpim_ref[...] = jnp.broadcast_to(cim_scr[0:1, :], hpim_ref.shape)

    hp = jnp.concatenate([pre_scr[...].T.reshape(gb, SSM_STATE, LANES),
                          pim_scr[...].T.reshape(gb, SSM_STATE, LANES)], axis=1).astype(BF16)
    ycorr = lax.dot_general(cout_scr[...], hp, bdims, preferred_element_type=F32)
    g = jax.nn.gelu(yt + ycorr)
    for t in range(CHUNK):
        gt_ref[:, LANES * t:LANES * (t + 1)] = (
            g[:, SSM_GROUP * t:SSM_GROUP * (t + 1), :].reshape(te, LANES).astype(BF16))


def _ssm(xs_all, wuz_t, e1, e2, c1, c2, apt, bb, dv, are, aim, sre, sim):
    n_tiles = xs_all.shape[0]
    te = TE_SSM
    n_j = D_INNER // te
    gb = te // SSM_GROUP
    rows = gb * SSM_STATE
    n_seq = sre.shape[0]
    act_shape = jax.ShapeDtypeStruct((n_tiles, D_INNER, TILE), BF16)
    act_spec = pl.BlockSpec((None, te, TILE), lambda j, i: (i, j, 0))
    hs_shape = jax.ShapeDtypeStruct((n_seq, N_STATE_ROWS), F32)
    hs_spec = pl.BlockSpec((n_seq, rows), lambda j, i: (0, j))
    hp_shape = jax.ShapeDtypeStruct((8, N_STATE_ROWS), F32)
    hp_spec = pl.BlockSpec((8, rows), lambda j, i: (0, j))
    row_spec = pl.BlockSpec((1, rows), lambda j, i: (0, j))
    grp_spec = lambda r, c: pl.BlockSpec((gb, r, c), lambda j, i: (j, 0, 0))
    return pl.pallas_call(
        functools.partial(_ssm_kernel, te=te, n_tiles=n_tiles),
        grid=(n_j, n_tiles),
        in_specs=[
            pl.BlockSpec((None, TILE, D_MODEL), lambda j, i: (i, 0, 0)),
            pl.BlockSpec((te, D_MODEL), lambda j, i: (j, 0)),
            pl.BlockSpec((te, D_MODEL), lambda j, i: (n_j + j, 0)),
            grp_spec(e1.shape[1], 2 * SSM_STATE), grp_spec(e2.shape[1], 2 * SSM_STATE),
            grp_spec(SSM_GROUP, 2 * SSM_STATE), grp_spec(SSM_GROUP, 2 * SSM_STATE),
            grp_spec(2 * SSM_STATE, SSM_GROUP), grp_spec(2 * SSM_STATE, SSM_GROUP),
            grp_spec(SSM_GROUP, 1),
            row_spec, row_spec, hs_spec, hs_spec,
        ],
        out_specs=(act_spec, act_spec, hs_spec, hs_spec, hp_spec, hp_spec),
        out_shape=(act_shape, act_shape, hs_shape, hs_shape, hp_shape, hp_shape),
        scratch_shapes=[pltpu.VMEM((2 * te, D_MODEL), BF16),
                        pltpu.VMEM((gb, 256, 256), BF16),
                        pltpu.VMEM((gb, 2 * SSM_STATE, 256), BF16),
                        pltpu.VMEM((gb, 256, 2 * SSM_STATE), BF16),
                        pltpu.VMEM((gb, 256, LANES), BF16),
                        pltpu.VMEM((LANES, rows), F32), pltpu.VMEM((LANES, rows), F32),
                        pltpu.VMEM((LANES, rows), F32), pltpu.VMEM((LANES, rows), F32),
                        pltpu.VMEM((8, rows), F32), pltpu.VMEM((8, rows), F32)],
        compiler_params=_params(2),
        name="s5_scan",
    )(xs_all, wuz_t, wuz_t, e1, e2, c1, c2, apt, bb, dv, are, aim, sre, sim)


def _glu_out_kernel(gt_ref, szt_ref, wg_ref, bg_ref, wo_ref, h_ref, nw_ref, o_ref, acc_scr, *,
                    te, lb):
    e = pl.program_id(2)

    @pl.when(e == 0)
    def _():
        acc_scr[...] = jnp.zeros_like(acc_scr)

    gate = jnp.dot(wg_ref[...], gt_ref[...], preferred_element_type=F32) + bg_ref[...]
    ge = gt_ref[pl.ds(pl.multiple_of(e * te, te), te), :].astype(F32)
    y3 = (ge * jax.nn.sigmoid(gate)) * szt_ref[...].astype(F32)
    acc_scr[...] += jnp.dot(wo_ref[...], y3.astype(BF16), preferred_element_type=F32)

    @pl.when(e == pl.num_programs(2) - 1)
    def _():
        o = acc_scr[...].T
        for t in range(lb // LANES):
            h = h_ref[t] + o[LANES * t:LANES * (t + 1), :]
            o_ref[:, D_MODEL * t:D_MODEL * (t + 1)] = _rms(h, nw_ref[...])


def _glu_out(gt, szt, wg_t, b_glu, wo_t, h_all, norm_w, *, n_tiles, tile_off):
    te = TE_GLU
    lb = LB_GLU
    n_l = TILE // lb
    n_e = D_INNER // te
    hw = (lb // LANES) * D_MODEL
    return pl.pallas_call(
        functools.partial(_glu_out_kernel, te=te, lb=lb),
        grid=(n_tiles, n_l, n_e),
        in_specs=[
            pl.BlockSpec((None, D_INNER, lb), lambda i, l, e: (i + tile_off, 0, l)),
            pl.BlockSpec((None, te, lb), lambda i, l, e: (i + tile_off, e, l)),
            pl.BlockSpec((te, D_INNER), lambda i, l, e: (e, 0)),
            pl.BlockSpec((te, 1), lambda i, l, e: (e, 0)),
            pl.BlockSpec((D_MODEL, te), lambda i, l, e: (0, e)),
            pl.BlockSpec((None, lb // LANES, LANES, D_MODEL),
                         lambda i, l, e: (i + tile_off, l, 0, 0)),
            pl.BlockSpec((1, D_MODEL), lambda i, l, e: (0, 0)),
        ],
        out_specs=pl.BlockSpec((LANES, hw), lambda i, l, e: (i, l)),
        out_shape=jax.ShapeDtypeStruct((n_tiles * LANES, CHUNK * D_MODEL), F32),
        scratch_shapes=[pltpu.VMEM((D_MODEL, lb), F32)],
        compiler_params=_params(3),
        name="glu_out",
    )(gt, szt, wg_t, b_glu, wo_t, h_all, norm_w)


def _ssm_operators(a_re, a_im, b_re, b_im, c_re, c_im, d_vec, log_dt):
    a_re, a_im = a_re.astype(F32), a_im.astype(F32)
    dt = jnp.exp(log_dt.astype(F32))[:, None]
    th_re = jnp.concatenate([a_re * dt] * 2, axis=1)[:, None, :]
    th_im = jnp.concatenate([a_im * dt] * 2, axis=1)[:, None, :]
    n_tau = 24
    taus = jnp.arange(n_tau, dtype=F32)[None, :, None]
    used = taus <= CHUNK
    mag = jnp.exp(jnp.where(used, th_re * taus, 0.0))
    ang = jnp.where(used, th_im * taus, 0.0)
    ap_re = mag * jnp.cos(ang)
    ap_im = mag * jnp.sin(ang)
    sign = jnp.concatenate([jnp.ones((SSM_STATE,), F32), -jnp.ones((SSM_STATE,), F32)])
    e1 = ap_re * sign
    e2 = -ap_im
    c_re, c_im = c_re.astype(F32), c_im.astype(F32)
    c1 = jnp.concatenate([c_re, c_im], axis=2)
    c2 = jnp.concatenate([c_im, c_re], axis=2)

    ab_re, ab_im = ap_re[:, 1, :SSM_STATE], ap_im[:, 1, :SSM_STATE]
    nr, ni = ab_re - 1.0, ab_im
    den = a_re * a_re + a_im * a_im
    q_re = ((nr * a_re + ni * a_im) / den)[..., None]
    q_im = ((ni * a_re - nr * a_im) / den)[..., None]
    b_re, b_im = b_re.astype(F32), b_im.astype(F32)
    bb = jnp.concatenate([q_re * b_re - q_im * b_im, q_re * b_im + q_im * b_re], axis=1)

    rev = slice(CHUNK - 1, None, -1)
    apt = jnp.concatenate([ap_re[:, rev, :SSM_STATE].transpose(0, 2, 1),
                           ap_im[:, rev, :SSM_STATE].transpose(0, 2, 1)], axis=1)
    dv = d_vec.astype(F32).reshape(N_GROUPS, SSM_GROUP, 1)
    are = ap_re[:, CHUNK, :SSM_STATE].reshape(1, N_STATE_ROWS)
    aim = ap_im[:, CHUNK, :SSM_STATE].reshape(1, N_STATE_ROWS)
    return e1, e2, c1, c2, apt, bb, dv, are, aim


def kernel(x_prompt, x_sample, cache_conv, state_ssm_re, state_ssm_im, meta_tokens, norm_w,
           final_norm_w, conv_w_in, conv_w, conv_b, conv_w_out, ssm_w_in, ssm_a_re, ssm_a_im,
           ssm_b_re, ssm_b_im, ssm_c_re, ssm_c_im, ssm_d, ssm_log_dt, ssm_w_glu, ssm_b_glu,
           ssm_w_out):
    n_seq, seq_len = x_sample.shape[0], x_sample.shape[1]
    n_prompt_rows = x_prompt.shape[1]
    n_ptiles = n_prompt_rows // TILE
    n_sample_rows = n_seq * seq_len
    assert x_prompt.shape[0] == 1 and seq_len == 2 * CHUNK and N_META == CHUNK
    assert n_prompt_rows % TILE == 0 and N_META + n_sample_rows <= SPECIAL_ROWS

    w_out0 = _cast_bf16(conv_w_out.astype(F32), transpose=False)
    wuz_t = _cast_bf16(ssm_w_in.astype(F32), transpose=True)
    wg_t = _cast_bf16(ssm_w_glu.astype(F32), transpose=True)
    wo_t = _cast_bf16(ssm_w_out.astype(F32), transpose=True)
    w_in0 = conv_w_in.astype(F32)
    conv_w8 = jnp.zeros((8, D_INNER), F32).at[0:3].set(conv_w[0].astype(F32))
    conv_b2 = conv_b[0].astype(F32).reshape(1, D_INNER)
    b_glu = ssm_b_glu[0].astype(F32).reshape(D_INNER, 1)
    nw0 = norm_w[0].astype(F32).reshape(1, D_MODEL)
    nw1 = norm_w[1].astype(F32).reshape(1, D_MODEL)
    nwf = final_norm_w.astype(F32).reshape(1, D_MODEL)
    ssm_ops = _ssm_operators(
        ssm_a_re[0], ssm_a_im[0], ssm_b_re[0], ssm_b_im[0], ssm_c_re[0], ssm_c_im[0],
        ssm_d[0], ssm_log_dt[0])
    sre = state_ssm_re[0].astype(F32).reshape(n_seq, N_STATE_ROWS)
    sim = state_ssm_im[0].astype(F32).reshape(n_seq, N_STATE_ROWS)

    xp = x_prompt.astype(F32)
    x_sp = jnp.concatenate([
        meta_tokens.astype(F32), x_sample.astype(F32).reshape(n_sample_rows, D_MODEL),
        jnp.zeros((SPECIAL_ROWS - N_META - n_sample_rows, D_MODEL), F32)], axis=0)[None]

    zeros8 = jnp.zeros((8, D_INNER), F32)
    xn_sp = _rmsnorm(x_sp, nw0, tm=SPECIAL_ROWS)
    xn_p = _rmsnorm(xp, nw0, tm=TM_CONV)
    y_sp, cv_sp = _conv_proj(xn_sp, w_in0, conv_w8, conv_b2, zeros8,
                             cache_conv[0, :, 0, :].astype(F32), cache_conv[0, :, 1, :].astype(F32))
    init8 = zeros8.at[6:8].set(cv_sp[N_META - 2:N_META])
    y_p, tail_p = _conv_proj(xn_p, w_in0, conv_w8, conv_b2, init8)
    h1_all, xs_all = _out_proj(y_p, w_out0, xp, nw1, n_tiles_total=1 + n_ptiles)
    h1_all, xs_all = _out_proj(y_sp, w_out0, x_sp, nw1, n_tiles_total=1 + n_ptiles,
                               h_all=h1_all, xs_all=xs_all)

    gt, szt, hs_re, hs_im, hp_re, hp_im = _ssm(
        xs_all.reshape(1 + n_ptiles, TILE, D_MODEL), wuz_t, *ssm_ops, sre, sim)
    o_sp = _glu_out(gt, szt, wg_t, b_glu, wo_t, h1_all, nwf, n_tiles=1, tile_off=0)
    o_p = _glu_out(gt, szt, wg_t, b_glu, wo_t, h1_all, nwf, n_tiles=n_ptiles, tile_off=1)

    y_prompt = o_p.reshape(1, n_prompt_rows, D_MODEL)
    n_sc = n_sample_rows // CHUNK
    y_sample = o_sp[1:1 + n_sc].reshape(n_seq, seq_len, D_MODEL)
    new_conv_prompt = tail_p[-1, 6:8].reshape(1, 1, 2, D_INNER)
    cv_s = cv_sp[N_META:N_META + n_sample_rows].reshape(n_seq, seq_len, D_INNER)
    new_conv_sample = cv_s[:, seq_len - 2:].reshape(1, n_seq, 2, D_INNER)
    p_shape = (1, 1, N_GROUPS, SSM_STATE)
    s_shape = (1, n_seq, N_GROUPS, SSM_STATE)
    return (y_prompt, y_sample, new_conv_prompt, new_conv_sample,
            hp_re[0].reshape(p_shape), hp_im[0].reshape(p_shape),
            hs_re.reshape(s_shape), hs_im.reshape(s_shape))
```

```python
import functools

import jax
import jax.numpy as jnp
from jax import lax
from jax.experimental import pallas as pl
from jax.experimental.pallas import tpu as pltpu

F32 = jnp.float32
BF16 = jnp.bfloat16

D_MODEL = 2048
D_INNER = 4096
N_META = 16
SSM_GROUP = 16
N_GROUPS = D_INNER // SSM_GROUP
SSM_STATE = 64
N_STATE_ROWS = N_GROUPS * SSM_STATE
RMS_EPS = 1e-6

CHUNK = 16
LANES = 128
TILE = CHUNK * LANES
SPECIAL_ROWS = 1024

TM_CONV = 1024
TE_CONV = 256
TM_OUT = 512
TE_SSM = 256
TE_GLU = 512
LB_GLU = 512
PREP_BLOCK = 1024

VMEM_LIMIT = 56 * 1024 * 1024


def _params(n_axes, vmem=VMEM_LIMIT):
    return pltpu.CompilerParams(dimension_semantics=("arbitrary",) * n_axes,
                                vmem_limit_bytes=vmem)


def _rms(x, w):
    ms = jnp.mean(x * x, axis=-1, keepdims=True)
    return x * lax.rsqrt(ms + RMS_EPS) * w


def _cast_kernel(x_ref, o_ref, *, transpose):
    x = x_ref[...]
    o_ref[...] = (x.T if transpose else x).astype(BF16)


def _cast_bf16(w, *, transpose):
    _, r, c = w.shape
    b = PREP_BLOCK
    out_shape = (c, r) if transpose else (r, c)
    out_map = (lambda i, j: (j, i)) if transpose else (lambda i, j: (i, j))
    return pl.pallas_call(
        functools.partial(_cast_kernel, transpose=transpose),
        grid=(r // b, c // b),
        in_specs=[pl.BlockSpec((None, b, b), lambda i, j: (0, i, j))],
        out_specs=pl.BlockSpec((b, b), out_map),
        out_shape=jax.ShapeDtypeStruct(out_shape, BF16),
        compiler_params=_params(2),
        name="cast_t" if transpose else "cast",
    )(w)


def _rmsnorm_kernel(x_ref, w_ref, o_ref):
    o_ref[...] = _rms(x_ref[...], w_ref[...]).astype(BF16)


def _rmsnorm(x, w, *, tm):
    rows = x.shape[1]
    return pl.pallas_call(
        _rmsnorm_kernel, grid=(rows // tm,),
        in_specs=[pl.BlockSpec((None, tm, D_MODEL), lambda i: (0, i, 0)),
                  pl.BlockSpec((1, D_MODEL), lambda i: (0, 0))],
        out_specs=pl.BlockSpec((tm, D_MODEL), lambda i: (i, 0)),
        out_shape=jax.ShapeDtypeStruct((rows, D_MODEL), BF16),
        compiler_params=_params(1), name="rmsnorm",
    )(x, w)


def _conv_proj_kernel(*refs, tm, te, n_pieces, special):
    if special:
        (xn_ref, wb_ref, wc_ref, wv_ref, wz_ref, cw_ref, cb_ref, init_ref, c0_ref, c1_ref,
         y_ref, cv_ref, w_scr, s_scr, p1_scr, p2_scr) = refs
    else:
        (xn_ref, wb_ref, wc_ref, wv_ref, wz_ref, cw_ref, cb_ref, init_ref,
         y_ref, tail_ref, w_scr, s_scr) = refs
    i = pl.program_id(1)

    @pl.when(i == 0)
    def _():
        for p, w_ref in enumerate((wb_ref, wc_ref, wv_ref, wz_ref)):
            w_scr[:, p * te:(p + 1) * te] = w_ref[...].astype(BF16)
        s_scr[0:8, :] = init_ref[...]

    pm = tm // n_pieces
    for r in range(n_pieces):
        lo = r * pm
        proj = jnp.dot(xn_ref[lo:lo + pm, :], w_scr[...], preferred_element_type=F32)
        bg = proj[:, 0 * te:1 * te]
        cg = proj[:, 1 * te:2 * te]
        vv = proj[:, 2 * te:3 * te]
        zz = proj[:, 3 * te:4 * te]
        cv = cg * vv
        s_scr[8 + lo:8 + lo + pm, :] = cv
        if special:
            cv_ref[lo:lo + pm, :] = cv
            p1_scr[...] = s_scr[7 + lo:7 + lo + pm, :]
            p2_scr[...] = s_scr[6 + lo:6 + lo + pm, :]
            for q in range(c0_ref.shape[0]):
                r0 = N_META + 32 * q
                p1_scr[r0:r0 + 1, :] = c1_ref[q:q + 1, :]
                p2_scr[r0:r0 + 1, :] = c0_ref[q:q + 1, :]
                p2_scr[r0 + 1:r0 + 2, :] = c1_ref[q:q + 1, :]
            p1 = p1_scr[...]
            p2 = p2_scr[...]
        else:
            p1 = s_scr[7 + lo:7 + lo + pm, :]
            p2 = s_scr[6 + lo:6 + lo + pm, :]
        conv = cb_ref[...] + cw_ref[0:1, :] * p2
        conv = conv + cw_ref[1:2, :] * p1
        conv = conv + cw_ref[2:3, :] * cv
        y_ref[lo:lo + pm, :] = (bg * conv * jax.nn.silu(zz)).astype(BF16)

    if not special:
        tail = s_scr[tm:tm + 8, :]
        s_scr[0:8, :] = tail
        tail_ref[...] = tail


def _conv_proj(xn, w_in, conv_w8, conv_b, init8, cache0=None, cache1=None):
    rows = xn.shape[0]
    tm, te = TM_CONV, TE_CONV
    n_j = D_INNER // te
    n_i = rows // tm
    special = cache0 is not None
    w_spec = lambda p: pl.BlockSpec((None, D_MODEL, te),
                                    lambda j, i, p=p: (0, 0, p * n_j + j))
    in_specs = [pl.BlockSpec((tm, D_MODEL), lambda j, i: (i, 0)),
                w_spec(0), w_spec(1), w_spec(2), w_spec(3),
                pl.BlockSpec((8, te), lambda j, i: (0, j)),
                pl.BlockSpec((1, te), lambda j, i: (0, j)),
                pl.BlockSpec((8, te), lambda j, i: (0, j))]
    args = [xn, w_in, w_in, w_in, w_in, conv_w8, conv_b, init8]
    scratch = [pltpu.VMEM((D_MODEL, 4 * te), BF16), pltpu.VMEM((tm + 8, te), F32)]
    y_shape = jax.ShapeDtypeStruct((rows, D_INNER), BF16)
    y_spec = pl.BlockSpec((tm, te), lambda j, i: (i, j))
    if special:
        assert n_i == 1
        nq = cache0.shape[0]
        in_specs += [pl.BlockSpec((nq, te), lambda j, i: (0, j)),
                     pl.BlockSpec((nq, te), lambda j, i: (0, j))]
        args += [cache0, cache1]
        out_shape = (y_shape, jax.ShapeDtypeStruct((rows, D_INNER), F32))
        out_specs = (y_spec, pl.BlockSpec((tm, te), lambda j, i: (i, j)))
        scratch += [pltpu.VMEM((tm, te), F32), pltpu.VMEM((tm, te), F32)]
    else:
        out_shape = (y_shape, jax.ShapeDtypeStruct((n_i, 8, D_INNER), F32))
        out_specs = (y_spec, pl.BlockSpec((None, 8, te), lambda j, i: (i, 0, j)))
    return pl.pallas_call(
        functools.partial(_conv_proj_kernel, tm=tm, te=te, n_pieces=1 if special else 2,
                          special=special),
        grid=(n_j, n_i), in_specs=in_specs, out_specs=out_specs, out_shape=out_shape,
        scratch_shapes=scratch, compiler_params=_params(2),
        name="conv_proj_special" if special else "conv_proj",
    )(*args)


def _out_proj_kernel(y_ref, w_ref, x_ref, nw_ref, h_ref, xs_ref, slab_scr):
    nc = y_ref.shape[0] // CHUNK
    n_slabs = D_MODEL // LANES
    h = x_ref[...] + jnp.dot(y_ref[...], w_ref[...], preferred_element_type=F32)
    for k in range(n_slabs):
        slab_scr[k] = h[:, LANES * k:LANES * (k + 1)]
    for s in range(CHUNK):
        hs = jnp.concatenate(
            [slab_scr[k, pl.ds(s, nc, stride=CHUNK), :] for k in range(n_slabs)], axis=1)
        h_ref[s] = hs
        xs_ref[s] = _rms(hs, nw_ref[...]).astype(BF16)


def _out_proj(y, w_out, x, norm_w, *, lanes):
    tm = TM_OUT
    nc = tm // CHUNK
    per_tile = lanes // nc
    n_i = x.shape[1] // tm
    tile_map = lambda i: (i // per_tile, 0, i % per_tile, 0)
    tile_shape = (n_i // per_tile, CHUNK, lanes, D_MODEL)
    return pl.pallas_call(
        _out_proj_kernel,
        grid=(n_i,),
        in_specs=[
            pl.BlockSpec((tm, D_INNER), lambda i: (i, 0)),
            pl.BlockSpec((D_INNER, D_MODEL), lambda i: (0, 0), pipeline_mode=pl.Buffered(1)),
            pl.BlockSpec((None, tm, D_MODEL), lambda i: (0, i, 0)),
            pl.BlockSpec((1, D_MODEL), lambda i: (0, 0)),
        ],
        out_specs=(pl.BlockSpec((None, CHUNK, nc, D_MODEL), tile_map),
                   pl.BlockSpec((None, CHUNK, nc, D_MODEL), tile_map)),
        out_shape=(jax.ShapeDtypeStruct(tile_shape, F32),
                   jax.ShapeDtypeStruct(tile_shape, BF16)),
        scratch_shapes=[pltpu.VMEM((D_MODEL // LANES, tm, LANES), F32)],
        compiler_params=_params(1, 60 * 1024 * 1024),
        name="out_proj",
    )(y, w_out, x, norm_w)


def _split_bf16(x):
    hi = x.astype(BF16)
    return hi, (x - hi.astype(F32)).astype(BF16)


def _ssm_kernel(xsp_ref, xs_ref, wu_ref, wz_ref, e1_ref, e2_ref, c1_ref, c2_ref, apt_ref, bb_ref,
                dv_ref, are_ref, aim_ref, sre_ref, sim_ref,
                gtsp_ref, sztsp_ref, gt_ref, szt_ref, hsre_ref, hsim_ref, hpre_ref, hpim_ref,
                w_scr, tk_scr, win_scr, cout_scr, zt_scr, xre_scr, xim_scr, pre_scr, pim_scr,
                cre_scr, cim_scr, *, te, n_tiles):
    i = pl.program_id(1)
    gb = te // SSM_GROUP
    rows = gb * SSM_STATE
    n_seq = sre_ref.shape[0]
    bdims = (((2,), (1,)), ((0,), (0,)))

    @pl.when(i == 0)
    def _():
        w_scr[0:te, :] = wu_ref[...]
        w_scr[te:2 * te, :] = wz_ref[...]

        c1 = c1_ref[...]
        c2 = c2_ref[...]
        cap = [c1 * e1_ref[:, t:t + 1, :] + c2 * e2_ref[:, t:t + 1, :] for t in range(CHUNK + 1)]
        ca = jnp.concatenate(cap[:CHUNK], axis=1)
        cout_scr[...] = jnp.concatenate(cap[1:], axis=1).astype(BF16)

        lane = lax.broadcasted_iota(jnp.int32, (SSM_GROUP, 256), 1)
        row = lax.broadcasted_iota(jnp.int32, (SSM_GROUP, 256), 0)
        diag = row == (lane & (SSM_GROUP - 1))
        tile_c = diag.astype(BF16)
        spread_s = (row == (lane >> 4)).astype(BF16)
        expand = lambda b, m, n_rows: jnp.dot(b, m, preferred_element_type=F32).reshape(
            gb, n_rows, 256)
        bhi, blo = _split_bf16(bb_ref[...].reshape(gb * 2 * SSM_STATE, SSM_GROUP))
        bhi_t = expand(bhi, tile_c, 2 * SSM_STATE)
        blo_t = expand(blo, tile_c, 2 * SSM_STATE)

        phi, plo = _split_bf16(apt_ref[...].reshape(gb * 2 * SSM_STATE, SSM_GROUP))
        ap_t = expand(phi, spread_s, 2 * SSM_STATE) + expand(plo, spread_s, 2 * SSM_STATE)
        b_t = bhi_t + blo_t
        pr, pi = ap_t[:, :SSM_STATE], ap_t[:, SSM_STATE:]
        br, bi = b_t[:, :SSM_STATE], b_t[:, SSM_STATE:]
        win_scr[:, 0:SSM_STATE, :] = (pr * br - pi * bi).astype(BF16)
        win_scr[:, SSM_STATE:, :] = (pr * bi + pi * br).astype(BF16)

        bhi_t, blo_t = bhi_t.astype(BF16), blo_t.astype(BF16)
        ahi, alo = _split_bf16(ca)
        kw = (lax.dot_general(ahi, bhi_t, bdims, preferred_element_type=F32)
              + lax.dot_general(ahi, blo_t, bdims, preferred_element_type=F32)
              + lax.dot_general(alo, bhi_t, bdims, preferred_element_type=F32))
        kw0 = (kw[:, 0:SSM_GROUP, :] + jnp.where(diag, dv_ref[...], 0.0)).astype(BF16)
        kw = kw.astype(BF16)
        tk_scr[...] = jnp.zeros_like(tk_scr)
        for s in range(CHUNK):
            lo = SSM_GROUP * s
            hi = lo + SSM_GROUP
            tk_scr[:, lo:hi, lo:hi] = kw0[:, :, lo:hi]
            if hi < 256:
                tk_scr[:, hi:256, lo:hi] = kw[:, SSM_GROUP:256 - lo, lo:hi]

    nt_dims = (((1,), (1,)), ((), ()))
    half = LANES // 2
    low_half = lax.broadcasted_iota(jnp.int32, (1, LANES), 1) < half

    @pl.when(i == 0)
    def _():
        uz = lax.dot_general(w_scr[...], xsp_ref[...], nt_dims,
                             preferred_element_type=F32)
        sztsp_ref[...] = jax.nn.silu(uz[te:, :]).astype(BF16)
        for s in range(CHUNK):
            blk = uz[:te, LANES * (s // 2):LANES * (s // 2 + 1)]
            if s % 2:
                blk = pltpu.roll(blk, half, 1)
            zt_scr[:, CHUNK * s:CHUNK * (s + 1), :] = (
                jnp.where(low_half, blk, 0.0).astype(BF16).reshape(gb, SSM_GROUP, LANES))

    @pl.when(i > 0)
    def _():
        uz = lax.dot_general(w_scr[...], xs_ref[...], nt_dims,
                             preferred_element_type=F32)
        szt_ref[...] = jax.nn.silu(uz[te:, :]).astype(BF16)
        ub = uz[:te, :].astype(BF16)
        for s in range(CHUNK):
            zt_scr[:, CHUNK * s:CHUNK * (s + 1), :] = (
                ub[:, LANES * s:LANES * (s + 1)].reshape(gb, SSM_GROUP, LANES))

    zt = zt_scr[...]
    yt = lax.dot_general(tk_scr[...], zt, bdims, preferred_element_type=F32)
    xt = lax.dot_general(win_scr[...], zt, bdims, preferred_element_type=F32)
    xre_scr[...] = xt[:, 0:SSM_STATE, :].reshape(rows, LANES).T
    xim_scr[...] = xt[:, SSM_STATE:, :].reshape(rows, LANES).T

    ar = are_ref[...]
    ai = aim_ref[...]

    def step(hr, hi, n):
        xr = xre_scr[pl.ds(n, 1), :]
        xi = xim_scr[pl.ds(n, 1), :]
        return ar * hr - ai * hi + xr, ar * hi + ai * hr + xi

    @pl.when(i == 0)
    def _():
        pre_scr[...] = jnp.zeros_like(pre_scr)
        pim_scr[...] = jnp.zeros_like(pim_scr)
        zero = jnp.zeros((1, rows), F32)
        hr, hi = step(zero, zero, 0)
        cre_scr[0:1, :] = hr
        cim_scr[0:1, :] = hi
        for q in range(n_seq):
            hr = sre_ref[q:q + 1, :]
            hi = sim_ref[q:q + 1, :]
            for e in range(2):
                n = 1 + 2 * q + e
                pre_scr[n:n + 1, :] = hr
                pim_scr[n:n + 1, :] = hi
                hr, hi = step(hr, hi, n)
            hsre_ref[q:q + 1, :] = hr
            hsim_ref[q:q + 1, :] = hi

    @pl.when(i > 0)
    def _():
        def body(n, carry):
            hr, hi = carry
            pre_scr[pl.ds(n, 1), :] = hr
            pim_scr[pl.ds(n, 1), :] = hi
            return step(hr, hi, n)

        hr, hi = lax.fori_loop(0, LANES, body, (cre_scr[0:1, :], cim_scr[0:1, :]), unroll=8)
        cre_scr[0:1, :] = hr
        cim_scr[0:1, :] = hi

    @pl.when(i == n_tiles - 1)
    def _():
        hpre_ref[...] = jnp.broadcast_to(cre_scr[0:1, :], hpre_ref.shape)
        hpim_ref[...] = jnp.broadcast_to(cim_scr[0:1, :], hpim_ref.shape)

    hp = jnp.concatenate([pre_scr[...].T.reshape(gb, SSM_STATE, LANES),
                          pim_scr[...].T.reshape(gb, SSM_STATE, LANES)], axis=1).astype(BF16)
    ycorr = lax.dot_general(cout_scr[...], hp, bdims, preferred_element_type=F32)
    g = jax.nn.gelu(yt + ycorr)
    piece = lambda t: g[:, SSM_GROUP * t:SSM_GROUP * (t + 1), :].reshape(te, LANES)

    @pl.when(i == 0)
    def _():
        for t in range(0, CHUNK, 2):
            both = jnp.where(low_half, piece(t), pltpu.roll(piece(t + 1), half, 1))
            gtsp_ref[:, LANES * (t // 2):LANES * (t // 2 + 1)] = both.astype(BF16)

    @pl.when(i > 0)
    def _():
        for t in range(CHUNK):
            gt_ref[:, LANES * t:LANES * (t + 1)] = piece(t).astype(BF16)


def _ssm(xs_sp, xs_p, wuz_t, e1, e2, c1, c2, apt, bb, dv, are, aim, sre, sim):
    n_ptiles = xs_p.shape[0]
    n_tiles = 1 + n_ptiles
    te = TE_SSM
    n_j = D_INNER // te
    gb = te // SSM_GROUP
    rows = gb * SSM_STATE
    n_seq = sre.shape[0]
    prev = lambda i: jnp.maximum(i - 1, 0)
    act_shape = jax.ShapeDtypeStruct((n_ptiles, D_INNER, TILE), BF16)
    act_spec = pl.BlockSpec((None, te, TILE), lambda j, i: (prev(i), j, 0))
    sp_shape = jax.ShapeDtypeStruct((1, D_INNER, TILE // 2), BF16)
    sp_spec = pl.BlockSpec((None, te, TILE // 2), lambda j, i: (0, j, 0))
    hs_shape = jax.ShapeDtypeStruct((n_seq, N_STATE_ROWS), F32)
    hs_spec = pl.BlockSpec((n_seq, rows), lambda j, i: (0, j))
    hp_shape = jax.ShapeDtypeStruct((8, N_STATE_ROWS), F32)
    hp_spec = pl.BlockSpec((8, rows), lambda j, i: (0, j))
    row_spec = pl.BlockSpec((1, rows), lambda j, i: (0, j))
    grp_spec = lambda r, c: pl.BlockSpec((gb, r, c), lambda j, i: (j, 0, 0))
    return pl.pallas_call(
        functools.partial(_ssm_kernel, te=te, n_tiles=n_tiles),
        grid=(n_j, n_tiles),
        in_specs=[
            pl.BlockSpec((TILE // 2, D_MODEL), lambda j, i: (0, 0), pipeline_mode=pl.Buffered(1)),
            pl.BlockSpec((None, TILE, D_MODEL), lambda j, i: (prev(i), 0, 0)),
            pl.BlockSpec((te, D_MODEL), lambda j, i: (j, 0)),
            pl.BlockSpec((te, D_MODEL), lambda j, i: (n_j + j, 0)),
            grp_spec(e1.shape[1], 2 * SSM_STATE), grp_spec(e2.shape[1], 2 * SSM_STATE),
            grp_spec(SSM_GROUP, 2 * SSM_STATE), grp_spec(SSM_GROUP, 2 * SSM_STATE),
            grp_spec(2 * SSM_STATE, SSM_GROUP), grp_spec(2 * SSM_STATE, SSM_GROUP),
            grp_spec(SSM_GROUP, 1),
            row_spec, row_spec, hs_spec, hs_spec,
        ],
        out_specs=(sp_spec, sp_spec, act_spec, act_spec, hs_spec, hs_spec, hp_spec, hp_spec),
        out_shape=(sp_shape, sp_shape, act_shape, act_shape, hs_shape, hs_shape,
                   hp_shape, hp_shape),
        scratch_shapes=[pltpu.VMEM((2 * te, D_MODEL), BF16),
                        pltpu.VMEM((gb, 256, 256), BF16),
                        pltpu.VMEM((gb, 2 * SSM_STATE, 256), BF16),
                        pltpu.VMEM((gb, 256, 2 * SSM_STATE), BF16),
                        pltpu.VMEM((gb, 256, LANES), BF16),
                        pltpu.VMEM((LANES, rows), F32), pltpu.VMEM((LANES, rows), F32),
                        pltpu.VMEM((LANES, rows), F32), pltpu.VMEM((LANES, rows), F32),
                        pltpu.VMEM((8, rows), F32), pltpu.VMEM((8, rows), F32)],
        compiler_params=_params(2, 60 * 1024 * 1024),
        name="s5_scan",
    )(xs_sp, xs_p, wuz_t, wuz_t, e1, e2, c1, c2, apt, bb, dv, are, aim, sre, sim)


def _glu_out_kernel(gt_ref, szt_ref, wg_ref, bg_ref, wo_ref, h_ref, nw_ref, o_ref, acc_scr, *,
                    te, lb):
    e = pl.program_id(2)

    @pl.when(e == 0)
    def _():
        acc_scr[...] = jnp.zeros_like(acc_scr)

    gate = jnp.dot(wg_ref[...], gt_ref[...], preferred_element_type=F32) + bg_ref[...]
    ge = gt_ref[pl.ds(pl.multiple_of(e * te, te), te), :].astype(F32)
    y3 = (ge * jax.nn.sigmoid(gate)) * szt_ref[...].astype(F32)
    acc_scr[...] += jnp.dot(wo_ref[...], y3.astype(BF16), preferred_element_type=F32)

    @pl.when(e == pl.num_programs(2) - 1)
    def _():
        o = acc_scr[...].T
        ln = h_ref.shape[1]
        for t in range(lb // ln):
            h = h_ref[t] + o[ln * t:ln * (t + 1), :]
            o_ref[:, D_MODEL * t:D_MODEL * (t + 1)] = _rms(h, nw_ref[...])


def _glu_out(gt, szt, wg_t, b_glu, wo_t, h_all, norm_w):
    n_tiles, _, lanes, _ = h_all.shape
    te = TE_GLU
    lb = LB_GLU
    n_l = CHUNK * lanes // lb
    n_e = D_INNER // te
    tpb = lb // lanes
    return pl.pallas_call(
        functools.partial(_glu_out_kernel, te=te, lb=lb),
        grid=(n_tiles, n_l, n_e),
        in_specs=[
            pl.BlockSpec((None, D_INNER, lb), lambda i, l, e: (i, 0, l)),
            pl.BlockSpec((None, te, lb), lambda i, l, e: (i, e, l)),
            pl.BlockSpec((te, D_INNER), lambda i, l, e: (e, 0)),
            pl.BlockSpec((te, 1), lambda i, l, e: (e, 0)),
            pl.BlockSpec((D_MODEL, te), lambda i, l, e: (0, e)),
            pl.BlockSpec((None, tpb, lanes, D_MODEL), lambda i, l, e: (i, l, 0, 0)),
            pl.BlockSpec((1, D_MODEL), lambda i, l, e: (0, 0)),
        ],
        out_specs=pl.BlockSpec((lanes, tpb * D_MODEL), lambda i, l, e: (i, l)),
        out_shape=jax.ShapeDtypeStruct((n_tiles * lanes, CHUNK * D_MODEL), F32),
        scratch_shapes=[pltpu.VMEM((D_MODEL, lb), F32)],
        compiler_params=_params(3),
        name="glu_out",
    )(gt, szt, wg_t, b_glu, wo_t, h_all, norm_w)


def _ssm_operators(a_re, a_im, b_re, b_im, c_re, c_im, d_vec, log_dt):
    a_re, a_im = a_re.astype(F32), a_im.astype(F32)
    dt = jnp.exp(log_dt.astype(F32))[:, None]
    th_re = jnp.concatenate([a_re * dt] * 2, axis=1)[:, None, :]
    th_im = jnp.concatenate([a_im * dt] * 2, axis=1)[:, None, :]
    n_tau = 24
    taus = jnp.arange(n_tau, dtype=F32)[None, :, None]
    used = taus <= CHUNK
    mag = jnp.exp(jnp.where(used, th_re * taus, 0.0))
    ang = jnp.where(used, th_im * taus, 0.0)
    ap_re = mag * jnp.cos(ang)
    ap_im = mag * jnp.sin(ang)
    sign = jnp.concatenate([jnp.ones((SSM_STATE,), F32), -jnp.ones((SSM_STATE,), F32)])
    e1 = ap_re * sign
    e2 = -ap_im
    c_re, c_im = c_re.astype(F32), c_im.astype(F32)
    c1 = jnp.concatenate([c_re, c_im], axis=2)
    c2 = jnp.concatenate([c_im, c_re], axis=2)

    ab_re, ab_im = ap_re[:, 1, :SSM_STATE], ap_im[:, 1, :SSM_STATE]
    nr, ni = ab_re - 1.0, ab_im
    den = a_re * a_re + a_im * a_im
    q_re = ((nr * a_re + ni * a_im) / den)[..., None]
    q_im = ((ni * a_re - nr * a_im) / den)[..., None]
    b_re, b_im = b_re.astype(F32), b_im.astype(F32)
    bb = jnp.concatenate([q_re * b_re - q_im * b_im, q_re * b_im + q_im * b_re], axis=1)

    rev = slice(CHUNK - 1, None, -1)
    apt = jnp.concatenate([ap_re[:, rev, :SSM_STATE].transpose(0, 2, 1),
                           ap_im[:, rev, :SSM_STATE].transpose(0, 2, 1)], axis=1)
    dv = d_vec.astype(F32).reshape(N_GROUPS, SSM_GROUP, 1)
    are = ap_re[:, CHUNK, :SSM_STATE].reshape(1, N_STATE_ROWS)
    aim = ap_im[:, CHUNK, :SSM_STATE].reshape(1, N_STATE_ROWS)
    return e1, e2, c1, c2, apt, bb, dv, are, aim


def kernel(x_prompt, x_sample, cache_conv, state_ssm_re, state_ssm_im, meta_tokens, norm_w,
           final_norm_w, conv_w_in, conv_w, conv_b, conv_w_out, ssm_w_in, ssm_a_re, ssm_a_im,
           ssm_b_re, ssm_b_im, ssm_c_re, ssm_c_im, ssm_d, ssm_log_dt, ssm_w_glu, ssm_b_glu,
           ssm_w_out):
    n_seq, seq_len = x_sample.shape[0], x_sample.shape[1]
    n_prompt_rows = x_prompt.shape[1]
    n_ptiles = n_prompt_rows // TILE
    n_sample_rows = n_seq * seq_len
    assert x_prompt.shape[0] == 1 and seq_len == 2 * CHUNK and N_META == CHUNK
    assert n_prompt_rows % TILE == 0 and N_META + n_sample_rows <= SPECIAL_ROWS

    w_out0 = _cast_bf16(conv_w_out.astype(F32), transpose=False)
    wuz_t = _cast_bf16(ssm_w_in.astype(F32), transpose=True)
    wg_t = _cast_bf16(ssm_w_glu.astype(F32), transpose=True)
    wo_t = _cast_bf16(ssm_w_out.astype(F32), transpose=True)
    w_in0 = conv_w_in.astype(F32)
    conv_w8 = jnp.zeros((8, D_INNER), F32).at[0:3].set(conv_w[0].astype(F32))
    conv_b2 = conv_b[0].astype(F32).reshape(1, D_INNER)
    b_glu = ssm_b_glu[0].astype(F32).reshape(D_INNER, 1)
    nw0 = norm_w[0].astype(F32).reshape(1, D_MODEL)
    nw1 = norm_w[1].astype(F32).reshape(1, D_MODEL)
    nwf = final_norm_w.astype(F32).reshape(1, D_MODEL)
    ssm_ops = _ssm_operators(
        ssm_a_re[0], ssm_a_im[0], ssm_b_re[0], ssm_b_im[0], ssm_c_re[0], ssm_c_im[0],
        ssm_d[0], ssm_log_dt[0])
    sre = state_ssm_re[0].astype(F32).reshape(n_seq, N_STATE_ROWS)
    sim = state_ssm_im[0].astype(F32).reshape(n_seq, N_STATE_ROWS)

    xp = x_prompt.astype(F32)
    x_sp = jnp.concatenate([
        meta_tokens.astype(F32), x_sample.astype(F32).reshape(n_sample_rows, D_MODEL),
        jnp.zeros((SPECIAL_ROWS - N_META - n_sample_rows, D_MODEL), F32)], axis=0)[None]

    zeros8 = jnp.zeros((8, D_INNER), F32)
    xn_sp = _rmsnorm(x_sp, nw0, tm=SPECIAL_ROWS)
    xn_p = _rmsnorm(xp, nw0, tm=TM_CONV)
    y_sp, cv_sp = _conv_proj(xn_sp, w_in0, conv_w8, conv_b2, zeros8,
                             cache_conv[0, :, 0, :].astype(F32), cache_conv[0, :, 1, :].astype(F32))
    init8 = zeros8.at[6:8].set(cv_sp[N_META - 2:N_META])
    y_p, tail_p = _conv_proj(xn_p, w_in0, conv_w8, conv_b2, init8)
    h1_p, xs_p = _out_proj(y_p, w_out0, xp, nw1, lanes=LANES)
    h1_sp, xs_sp = _out_proj(y_sp, w_out0, x_sp, nw1, lanes=SPECIAL_ROWS // CHUNK)

    gt_sp, szt_sp, gt_p, szt_p, hs_re, hs_im, hp_re, hp_im = _ssm(
        xs_sp.reshape(SPECIAL_ROWS, D_MODEL), xs_p.reshape(n_ptiles, TILE, D_MODEL),
        wuz_t, *ssm_ops, sre, sim)
    o_sp = _glu_out(gt_sp, szt_sp, wg_t, b_glu, wo_t, h1_sp, nwf)
    o_p = _glu_out(gt_p, szt_p, wg_t, b_glu, wo_t, h1_p, nwf)

    y_prompt = o_p.reshape(1, n_prompt_rows, D_MODEL)
    n_sc = n_sample_rows // CHUNK
    y_sample = o_sp[1:1 + n_sc].reshape(n_seq, seq_len, D_MODEL)
    new_conv_prompt = tail_p[-1, 6:8].reshape(1, 1, 2, D_INNER)
    cv_s = cv_sp[N_META:N_META + n_sample_rows].reshape(n_seq, seq_len, D_INNER)
    new_conv_sample = cv_s[:, seq_len - 2:].reshape(1, n_seq, 2, D_INNER)
    p_shape = (1, 1, N_GROUPS, SSM_STATE)
    s_shape = (1, n_seq, N_GROUPS, SSM_STATE)
    return (y_prompt, y_sample, new_conv_prompt, new_conv_sample,
            hp_re[0].reshape(p_shape), hp_im[0].reshape(p_shape),
            hs_re.reshape(s_shape), hs_im.reshape(s_shape))
```

```python
import functools

import jax
import jax.numpy as jnp
from jax import lax
from jax.experimental import pallas as pl
from jax.experimental.pallas import tpu as pltpu

F32 = jnp.float32
BF16 = jnp.bfloat16

D_MODEL = 2048
D_INNER = 4096
N_META = 16
SSM_GROUP = 16
N_GROUPS = D_INNER // SSM_GROUP
SSM_STATE = 64
N_STATE_ROWS = N_GROUPS * SSM_STATE
RMS_EPS = 1e-6

CHUNK = 16
LANES = 128
TILE = CHUNK * LANES
SPECIAL_ROWS = 1024

TM_CONV = 1024
TE_CONV = 256
TM_OUT = 512
TE_SSM = 256
TE_GLU = 512
LB_GLU = 512
PREP_BLOCK = 1024

VMEM_LIMIT = 56 * 1024 * 1024


def _params(n_axes, vmem=VMEM_LIMIT):
    return pltpu.CompilerParams(dimension_semantics=("arbitrary",) * n_axes,
                                vmem_limit_bytes=vmem)


def _rms(x, w):
    ms = jnp.mean(x * x, axis=-1, keepdims=True)
    return x * lax.rsqrt(ms + RMS_EPS) * w


def _cast_kernel(x_ref, o_ref, *, transpose):
    x = x_ref[...]
    o_ref[...] = (x.T if transpose else x).astype(BF16)


def _cast_bf16(w, *, transpose):
    _, r, c = w.shape
    b = PREP_BLOCK
    out_shape = (c, r) if transpose else (r, c)
    out_map = (lambda i, j: (j, i)) if transpose else (lambda i, j: (i, j))
    return pl.pallas_call(
        functools.partial(_cast_kernel, transpose=transpose),
        grid=(r // b, c // b),
        in_specs=[pl.BlockSpec((None, b, b), lambda i, j: (0, i, j))],
        out_specs=pl.BlockSpec((b, b), out_map),
        out_shape=jax.ShapeDtypeStruct(out_shape, BF16),
        compiler_params=_params(2),
        name="cast_t" if transpose else "cast",
    )(w)


def _rmsnorm_kernel(x_ref, w_ref, o_ref):
    o_ref[...] = _rms(x_ref[...], w_ref[...]).astype(BF16)


def _rmsnorm(x, w, *, tm):
    rows = x.shape[1]
    return pl.pallas_call(
        _rmsnorm_kernel, grid=(rows // tm,),
        in_specs=[pl.BlockSpec((None, tm, D_MODEL), lambda i: (0, i, 0)),
                  pl.BlockSpec((1, D_MODEL), lambda i: (0, 0))],
        out_specs=pl.BlockSpec((tm, D_MODEL), lambda i: (i, 0)),
        out_shape=jax.ShapeDtypeStruct((rows, D_MODEL), BF16),
        compiler_params=_params(1), name="rmsnorm",
    )(x, w)


def _conv_proj_kernel(*refs, tm, te, n_pieces, special):
    if special:
        (xn_ref, wb_ref, wc_ref, wv_ref, wz_ref, cw_ref, cb_ref, init_ref, c0_ref, c1_ref,
         y_ref, cv_ref, w_scr, s_scr, p1_scr, p2_scr) = refs
    else:
        (xn_ref, wb_ref, wc_ref, wv_ref, wz_ref, cw_ref, cb_ref, init_ref,
         y_ref, tail_ref, w_scr, s_scr) = refs
    i = pl.program_id(1)

    @pl.when(i == 0)
    def _():
        for p, w_ref in enumerate((wb_ref, wc_ref, wv_ref, wz_ref)):
            w_scr[:, p * te:(p + 1) * te] = w_ref[...].astype(BF16)
        s_scr[0:8, :] = init_ref[...]

    pm = tm // n_pieces
    for r in range(n_pieces):
        lo = r * pm
        proj = jnp.dot(xn_ref[lo:lo + pm, :], w_scr[...], preferred_element_type=F32)
        bg = proj[:, 0 * te:1 * te]
        cg = proj[:, 1 * te:2 * te]
        vv = proj[:, 2 * te:3 * te]
        zz = proj[:, 3 * te:4 * te]
        cv = cg * vv
        s_scr[8 + lo:8 + lo + pm, :] = cv
        if special:
            cv_ref[lo:lo + pm, :] = cv
            p1_scr[...] = s_scr[7 + lo:7 + lo + pm, :]
            p2_scr[...] = s_scr[6 + lo:6 + lo + pm, :]
            for q in range(c0_ref.shape[0]):
                r0 = N_META + 32 * q
                p1_scr[r0:r0 + 1, :] = c1_ref[q:q + 1, :]
                p2_scr[r0:r0 + 1, :] = c0_ref[q:q + 1, :]
                p2_scr[r0 + 1:r0 + 2, :] = c1_ref[q:q + 1, :]
            p1 = p1_scr[...]
            p2 = p2_scr[...]
        else:
            p1 = s_scr[7 + lo:7 + lo + pm, :]
            p2 = s_scr[6 + lo:6 + lo + pm, :]
        conv = cb_ref[...] + cw_ref[0:1, :] * p2
        conv = conv + cw_ref[1:2, :] * p1
        conv = conv + cw_ref[2:3, :] * cv
        y_ref[lo:lo + pm, :] = (bg * conv * jax.nn.silu(zz)).astype(BF16)

    if not special:
        tail = s_scr[tm:tm + 8, :]
        s_scr[0:8, :] = tail
        tail_ref[...] = tail


def _conv_proj(xn, w_in, conv_w8, conv_b, init8, cache0=None, cache1=None):
    rows = xn.shape[0]
    tm, te = TM_CONV, TE_CONV
    n_j = D_INNER // te
    n_i = rows // tm
    special = cache0 is not None
    w_spec = lambda p: pl.BlockSpec((None, D_MODEL, te),
                                    lambda j, i, p=p: (0, 0, p * n_j + j))
    in_specs = [pl.BlockSpec((tm, D_MODEL), lambda j, i: (i, 0)),
                w_spec(0), w_spec(1), w_spec(2), w_spec(3),
                pl.BlockSpec((8, te), lambda j, i: (0, j)),
                pl.BlockSpec((1, te), lambda j, i: (0, j)),
                pl.BlockSpec((8, te), lambda j, i: (0, j))]
    args = [xn, w_in, w_in, w_in, w_in, conv_w8, conv_b, init8]
    scratch = [pltpu.VMEM((D_MODEL, 4 * te), BF16), pltpu.VMEM((tm + 8, te), F32)]
    y_shape = jax.ShapeDtypeStruct((rows, D_INNER), BF16)
    y_spec = pl.BlockSpec((tm, te), lambda j, i: (i, j))
    if special:
        assert n_i == 1
        nq = cache0.shape[0]
        in_specs += [pl.BlockSpec((nq, te), lambda j, i: (0, j)),
                     pl.BlockSpec((nq, te), lambda j, i: (0, j))]
        args += [cache0, cache1]
        out_shape = (y_shape, jax.ShapeDtypeStruct((rows, D_INNER), F32))
        out_specs = (y_spec, pl.BlockSpec((tm, te), lambda j, i: (i, j)))
        scratch += [pltpu.VMEM((tm, te), F32), pltpu.VMEM((tm, te), F32)]
    else:
        out_shape = (y_shape, jax.ShapeDtypeStruct((n_i, 8, D_INNER), F32))
        out_specs = (y_spec, pl.BlockSpec((None, 8, te), lambda j, i: (i, 0, j)))
    return pl.pallas_call(
        functools.partial(_conv_proj_kernel, tm=tm, te=te, n_pieces=1 if special else 2,
                          special=special),
        grid=(n_j, n_i), in_specs=in_specs, out_specs=out_specs, out_shape=out_shape,
        scratch_shapes=scratch, compiler_params=_params(2),
        name="conv_proj_special" if special else "conv_proj",
    )(*args)


def _out_proj_kernel(y_ref, w_ref, x_ref, nw_ref, h_ref, xs_ref, slab_scr):
    nc = y_ref.shape[0] // CHUNK
    n_slabs = D_MODEL // LANES
    h = x_ref[...] + jnp.dot(y_ref[...], w_ref[...], preferred_element_type=F32)
    for k in range(n_slabs):
        slab_scr[k] = h[:, LANES * k:LANES * (k + 1)]
    for s in range(CHUNK):
        hs = jnp.concatenate(
            [slab_scr[k, pl.ds(s, nc, stride=CHUNK), :] for k in range(n_slabs)], axis=1)
        h_ref[s] = hs
        xs_ref[s] = _rms(hs, nw_ref[...]).astype(BF16)


def _out_proj(y, w_out, x, norm_w, *, lanes):
    tm = TM_OUT
    nc = tm // CHUNK
    per_tile = lanes // nc
    n_i = x.shape[1] // tm
    tile_map = lambda i: (i // per_tile, 0, i % per_tile, 0)
    tile_shape = (n_i // per_tile, CHUNK, lanes, D_MODEL)
    return pl.pallas_call(
        _out_proj_kernel,
        grid=(n_i,),
        in_specs=[
            pl.BlockSpec((tm, D_INNER), lambda i: (i, 0)),
            pl.BlockSpec((D_INNER, D_MODEL), lambda i: (0, 0), pipeline_mode=pl.Buffered(1)),
            pl.BlockSpec((None, tm, D_MODEL), lambda i: (0, i, 0)),
            pl.BlockSpec((1, D_MODEL), lambda i: (0, 0)),
        ],
        out_specs=(pl.BlockSpec((None, CHUNK, nc, D_MODEL), tile_map),
                   pl.BlockSpec((None, CHUNK, nc, D_MODEL), tile_map)),
        out_shape=(jax.ShapeDtypeStruct(tile_shape, F32),
                   jax.ShapeDtypeStruct(tile_shape, BF16)),
        scratch_shapes=[pltpu.VMEM((D_MODEL // LANES, tm, LANES), F32)],
        compiler_params=_params(1, 60 * 1024 * 1024),
        name="out_proj",
    )(y, w_out, x, norm_w)


def _split_bf16(x):
    hi = x.astype(BF16)
    return hi, (x - hi.astype(F32)).astype(BF16)


def _ssm_kernel(xsp_ref, xs_ref, wu_ref, wz_ref, e1_ref, e2_ref, c1_ref, c2_ref, apt_ref, bb_ref,
                dv_ref, are_ref, aim_ref, sre_ref, sim_ref,
                gtsp_ref, sztsp_ref, gt_ref, szt_ref, hsre_ref, hsim_ref, hpre_ref, hpim_ref,
                w_scr, tk_scr, win_scr, cout_scr, zt_scr, xre_scr, xim_scr, pre_scr, pim_scr,
                cre_scr, cim_scr, *, te, n_tiles):
    i = pl.program_id(1)
    gb = te // SSM_GROUP
    rows = gb * SSM_STATE
    n_seq = sre_ref.shape[0]
    bdims = (((2,), (1,)), ((0,), (0,)))

    @pl.when(i == 0)
    def _():
        w_scr[0:te, :] = wu_ref[...]
        w_scr[te:2 * te, :] = wz_ref[...]

        c1 = c1_ref[...]
        c2 = c2_ref[...]
        cap = [c1 * e1_ref[:, t:t + 1, :] + c2 * e2_ref[:, t:t + 1, :] for t in range(CHUNK + 1)]
        ca = jnp.concatenate(cap[:CHUNK], axis=1)
        cout_scr[...] = jnp.concatenate(cap[1:], axis=1).astype(BF16)

        lane = lax.broadcasted_iota(jnp.int32, (SSM_GROUP, 256), 1)
        row = lax.broadcasted_iota(jnp.int32, (SSM_GROUP, 256), 0)
        diag = row == (lane & (SSM_GROUP - 1))
        tile_c = diag.astype(BF16)
        spread_s = (row == (lane >> 4)).astype(BF16)
        expand = lambda b, m, n_rows: jnp.dot(b, m, preferred_element_type=F32).reshape(
            gb, n_rows, 256)
        bhi, blo = _split_bf16(bb_ref[...].reshape(gb * 2 * SSM_STATE, SSM_GROUP))
        bhi_t = expand(bhi, tile_c, 2 * SSM_STATE)
        blo_t = expand(blo, tile_c, 2 * SSM_STATE)

        phi, plo = _split_bf16(apt_ref[...].reshape(gb * 2 * SSM_STATE, SSM_GROUP))
        ap_t = expand(phi, spread_s, 2 * SSM_STATE) + expand(plo, spread_s, 2 * SSM_STATE)
        b_t = bhi_t + blo_t
        pr, pi = ap_t[:, :SSM_STATE], ap_t[:, SSM_STATE:]
        br, bi = b_t[:, :SSM_STATE], b_t[:, SSM_STATE:]
        win_scr[:, 0:SSM_STATE, :] = (pr * br - pi * bi).astype(BF16)
        win_scr[:, SSM_STATE:, :] = (pr * bi + pi * br).astype(BF16)

        bhi_t, blo_t = bhi_t.astype(BF16), blo_t.astype(BF16)
        ahi, alo = _split_bf16(ca)
        kw = (lax.dot_general(ahi, bhi_t, bdims, preferred_element_type=F32)
              + lax.dot_general(ahi, blo_t, bdims, preferred_element_type=F32)
              + lax.dot_general(alo, bhi_t, bdims, preferred_element_type=F32))
        kw0 = (kw[:, 0:SSM_GROUP, :] + jnp.where(diag, dv_ref[...], 0.0)).astype(BF16)
        kw = kw.astype(BF16)
        tk_scr[...] = jnp.zeros_like(tk_scr)
        for s in range(CHUNK):
            lo = SSM_GROUP * s
            hi = lo + SSM_GROUP
            tk_scr[:, lo:hi, lo:hi] = kw0[:, :, lo:hi]
            if hi < 256:
                tk_scr[:, hi:256, lo:hi] = kw[:, SSM_GROUP:256 - lo, lo:hi]

    nt_dims = (((1,), (1,)), ((), ()))
    half = LANES // 2
    low_half = lax.broadcasted_iota(jnp.int32, (1, LANES), 1) < half

    ar = are_ref[...]
    ai = aim_ref[...]

    def step(hr, hi, n):
        xr = xre_scr[pl.ds(n, 1), :]
        xi = xim_scr[pl.ds(n, 1), :]
        return ar * hr - ai * hi + xr, ar * hi + ai * hr + xi

    def chunk_inputs():
        zt = zt_scr[...]
        yt = lax.dot_general(tk_scr[...], zt, bdims, preferred_element_type=F32)
        xt = lax.dot_general(win_scr[...], zt, bdims, preferred_element_type=F32)
        xre_scr[...] = xt[:, 0:SSM_STATE, :].reshape(rows, LANES).T
        xim_scr[...] = xt[:, SSM_STATE:, :].reshape(rows, LANES).T
        return yt

    def gelu_pieces(yt):
        hp = jnp.concatenate([pre_scr[...].T.reshape(gb, SSM_STATE, LANES),
                              pim_scr[...].T.reshape(gb, SSM_STATE, LANES)], axis=1)
        ycorr = lax.dot_general(cout_scr[...], hp.astype(BF16), bdims,
                                preferred_element_type=F32)
        g = jax.nn.gelu(yt + ycorr)
        return [g[:, SSM_GROUP * t:SSM_GROUP * (t + 1), :].reshape(te, LANES)
                for t in range(CHUNK)]

    @pl.when(i == 0)
    def _():
        uz = lax.dot_general(w_scr[...], xsp_ref[...], nt_dims,
                             preferred_element_type=F32)
        sztsp_ref[...] = jax.nn.silu(uz[te:, :]).astype(BF16)
        for s in range(CHUNK):
            blk = uz[:te, LANES * (s // 2):LANES * (s // 2 + 1)]
            if s % 2:
                blk = pltpu.roll(blk, half, 1)
            zt_scr[:, CHUNK * s:CHUNK * (s + 1), :] = (
                jnp.where(low_half, blk, 0.0).astype(BF16).reshape(gb, SSM_GROUP, LANES))
        yt = chunk_inputs()

        pre_scr[...] = jnp.zeros_like(pre_scr)
        pim_scr[...] = jnp.zeros_like(pim_scr)
        zero = jnp.zeros((1, rows), F32)
        hr, hi = step(zero, zero, 0)
        cre_scr[0:1, :] = hr
        cim_scr[0:1, :] = hi
        for q in range(n_seq):
            hr = sre_ref[q:q + 1, :]
            hi = sim_ref[q:q + 1, :]
            for e in range(2):
                n = 1 + 2 * q + e
                pre_scr[n:n + 1, :] = hr
                pim_scr[n:n + 1, :] = hi
                hr, hi = step(hr, hi, n)
            hsre_ref[q:q + 1, :] = hr
            hsim_ref[q:q + 1, :] = hi

        pieces = gelu_pieces(yt)
        for t in range(0, CHUNK, 2):
            both = jnp.where(low_half, pieces[t], pltpu.roll(pieces[t + 1], half, 1))
            gtsp_ref[:, LANES * (t // 2):LANES * (t // 2 + 1)] = both.astype(BF16)

    @pl.when(i > 0)
    def _():
        uz = lax.dot_general(w_scr[...], xs_ref[...], nt_dims,
                             preferred_element_type=F32)
        szt_ref[...] = jax.nn.silu(uz[te:, :]).astype(BF16)
        ub = uz[:te, :].astype(BF16)
        for s in range(CHUNK):
            zt_scr[:, CHUNK * s:CHUNK * (s + 1), :] = (
                ub[:, LANES * s:LANES * (s + 1)].reshape(gb, SSM_GROUP, LANES))
        yt = chunk_inputs()

        hr, hi = cre_scr[0:1, :], cim_scr[0:1, :]
        for n in range(LANES):
            pre_scr[n:n + 1, :] = hr
            pim_scr[n:n + 1, :] = hi
            hr, hi = step(hr, hi, n)
        cre_scr[0:1, :] = hr
        cim_scr[0:1, :] = hi

        @pl.when(i == n_tiles - 1)
        def _():
            hpre_ref[...] = jnp.broadcast_to(hr, hpre_ref.shape)
            hpim_ref[...] = jnp.broadcast_to(hi, hpim_ref.shape)

        pieces = gelu_pieces(yt)
        for t in range(CHUNK):
            gt_ref[:, LANES * t:LANES * (t + 1)] = pieces[t].astype(BF16)


def _ssm(xs_sp, xs_p, wuz_t, e1, e2, c1, c2, apt, bb, dv, are, aim, sre, sim):
    n_ptiles = xs_p.shape[0]
    n_tiles = 1 + n_ptiles
    te = TE_SSM
    n_j = D_INNER // te
    gb = te // SSM_GROUP
    rows = gb * SSM_STATE
    n_seq = sre.shape[0]
    prev = lambda i: jnp.maximum(i - 1, 0)
    act_shape = jax.ShapeDtypeStruct((n_ptiles, D_INNER, TILE), BF16)
    act_spec = pl.BlockSpec((None, te, TILE), lambda j, i: (prev(i), j, 0))
    sp_shape = jax.ShapeDtypeStruct((1, D_INNER, TILE // 2), BF16)
    sp_spec = pl.BlockSpec((None, te, TILE // 2), lambda j, i: (0, j, 0))
    hs_shape = jax.ShapeDtypeStruct((n_seq, N_STATE_ROWS), F32)
    hs_spec = pl.BlockSpec((n_seq, rows), lambda j, i: (0, j))
    hp_shape = jax.ShapeDtypeStruct((8, N_STATE_ROWS), F32)
    hp_spec = pl.BlockSpec((8, rows), lambda j, i: (0, j))
    row_spec = pl.BlockSpec((1, rows), lambda j, i: (0, j))
    grp_spec = lambda r, c: pl.BlockSpec((gb, r, c), lambda j, i: (j, 0, 0))
    return pl.pallas_call(
        functools.partial(_ssm_kernel, te=te, n_tiles=n_tiles),
        grid=(n_j, n_tiles),
        in_specs=[
            pl.BlockSpec((TILE // 2, D_MODEL), lambda j, i: (0, 0), pipeline_mode=pl.Buffered(1)),
            pl.BlockSpec((None, TILE, D_MODEL), lambda j, i: (prev(i), 0, 0)),
            pl.BlockSpec((te, D_MODEL), lambda j, i: (j, 0)),
            pl.BlockSpec((te, D_MODEL), lambda j, i: (n_j + j, 0)),
            grp_spec(e1.shape[1], 2 * SSM_STATE), grp_spec(e2.shape[1], 2 * SSM_STATE),
            grp_spec(SSM_GROUP, 2 * SSM_STATE), grp_spec(SSM_GROUP, 2 * SSM_STATE),
            grp_spec(2 * SSM_STATE, SSM_GROUP), grp_spec(2 * SSM_STATE, SSM_GROUP),
            grp_spec(SSM_GROUP, 1),
            row_spec, row_spec, hs_spec, hs_spec,
        ],
        out_specs=(sp_spec, sp_spec, act_spec, act_spec, hs_spec, hs_spec, hp_spec, hp_spec),
        out_shape=(sp_shape, sp_shape, act_shape, act_shape, hs_shape, hs_shape,
                   hp_shape, hp_shape),
        scratch_shapes=[pltpu.VMEM((2 * te, D_MODEL), BF16),
                        pltpu.VMEM((gb, 256, 256), BF16),
                        pltpu.VMEM((gb, 2 * SSM_STATE, 256), BF16),
                        pltpu.VMEM((gb, 256, 2 * SSM_STATE), BF16),
                        pltpu.VMEM((gb, 256, LANES), BF16),
                        pltpu.VMEM((LANES, rows), F32), pltpu.VMEM((LANES, rows), F32),
                        pltpu.VMEM((LANES, rows), F32), pltpu.VMEM((LANES, rows), F32),
                        pltpu.VMEM((8, rows), F32), pltpu.VMEM((8, rows), F32)],
        compiler_params=_params(2, 60 * 1024 * 1024),
        name="s5_scan",
    )(xs_sp, xs_p, wuz_t, wuz_t, e1, e2, c1, c2, apt, bb, dv, are, aim, sre, sim)


def _glu_out_kernel(gt_ref, szt_ref, wg_ref, bg_ref, wo_ref, h_ref, nw_ref, o_ref, acc_scr, *,
                    te, lb):
    e = pl.program_id(2)

    @pl.when(e == 0)
    def _():
        acc_scr[...] = jnp.zeros_like(acc_scr)

    gate = jnp.dot(wg_ref[...], gt_ref[...], preferred_element_type=F32) + bg_ref[...]
    ge = gt_ref[pl.ds(pl.multiple_of(e * te, te), te), :].astype(F32)
    y3 = (ge * jax.nn.sigmoid(gate)) * szt_ref[...].astype(F32)
    acc_scr[...] += jnp.dot(wo_ref[...], y3.astype(BF16), preferred_element_type=F32)

    @pl.when(e == pl.num_programs(2) - 1)
    def _():
        o = acc_scr[...].T
        ln = h_ref.shape[1]
        for t in range(lb // ln):
            h = h_ref[t] + o[ln * t:ln * (t + 1), :]
            o_ref[:, D_MODEL * t:D_MODEL * (t + 1)] = _rms(h, nw_ref[...])


def _glu_out(gt, szt, wg_t, b_glu, wo_t, h_all, norm_w):
    n_tiles, _, lanes, _ = h_all.shape
    te = TE_GLU
    lb = LB_GLU
    n_l = CHUNK * lanes // lb
    n_e = D_INNER // te
    tpb = lb // lanes
    return pl.pallas_call(
        functools.partial(_glu_out_kernel, te=te, lb=lb),
        grid=(n_tiles, n_l, n_e),
        in_specs=[
            pl.BlockSpec((None, D_INNER, lb), lambda i, l, e: (i, 0, l)),
            pl.BlockSpec((None, te, lb), lambda i, l, e: (i, e, l)),
            pl.BlockSpec((te, D_INNER), lambda i, l, e: (e, 0)),
            pl.BlockSpec((te, 1), lambda i, l, e: (e, 0)),
            pl.BlockSpec((D_MODEL, te), lambda i, l, e: (0, e)),
            pl.BlockSpec((None, tpb, lanes, D_MODEL), lambda i, l, e: (i, l, 0, 0)),
            pl.BlockSpec((1, D_MODEL), lambda i, l, e: (0, 0)),
        ],
        out_specs=pl.BlockSpec((lanes, tpb * D_MODEL), lambda i, l, e: (i, l)),
        out_shape=jax.ShapeDtypeStruct((n_tiles * lanes, CHUNK * D_MODEL), F32),
        scratch_shapes=[pltpu.VMEM((D_MODEL, lb), F32)],
        compiler_params=_params(3),
        name="glu_out",
    )(gt, szt, wg_t, b_glu, wo_t, h_all, norm_w)


def _to_rows_kernel(o_ref, y_ref, slab_scr):
    nc = o_ref.shape[0]
    n_slabs = D_MODEL // LANES
    for t in range(CHUNK):
        for k in range(n_slabs):
            slab_scr[k, pl.ds(t, nc, stride=CHUNK), :] = (
                o_ref[:, D_MODEL * t + LANES * k:D_MODEL * t + LANES * (k + 1)])
    for k in range(n_slabs):
        y_ref[:, LANES * k:LANES * (k + 1)] = slab_scr[k]


def _to_rows(o):
    n_chunks = o.shape[0]
    nc = TM_OUT // CHUNK
    return pl.pallas_call(
        _to_rows_kernel,
        grid=(n_chunks // nc,),
        in_specs=[pl.BlockSpec((nc, CHUNK * D_MODEL), lambda i: (i, 0))],
        out_specs=pl.BlockSpec((None, TM_OUT, D_MODEL), lambda i: (0, i, 0)),
        out_shape=jax.ShapeDtypeStruct((1, n_chunks * CHUNK, D_MODEL), F32),
        scratch_shapes=[pltpu.VMEM((D_MODEL // LANES, TM_OUT, LANES), F32)],
        compiler_params=_params(1),
        name="to_rows",
    )(o)


def _ssm_operators(a_re, a_im, b_re, b_im, c_re, c_im, d_vec, log_dt):
    a_re, a_im = a_re.astype(F32), a_im.astype(F32)
    dt = jnp.exp(log_dt.astype(F32))[:, None]
    th_re = jnp.concatenate([a_re * dt] * 2, axis=1)[:, None, :]
    th_im = jnp.concatenate([a_im * dt] * 2, axis=1)[:, None, :]
    n_tau = 24
    taus = jnp.arange(n_tau, dtype=F32)[None, :, None]
    used = taus <= CHUNK
    mag = jnp.exp(jnp.where(used, th_re * taus, 0.0))
    ang = jnp.where(used, th_im * taus, 0.0)
    ap_re = mag * jnp.cos(ang)
    ap_im = mag * jnp.sin(ang)
    sign = jnp.concatenate([jnp.ones((SSM_STATE,), F32), -jnp.ones((SSM_STATE,), F32)])
    e1 = ap_re * sign
    e2 = -ap_im
    c_re, c_im = c_re.astype(F32), c_im.astype(F32)
    c1 = jnp.concatenate([c_re, c_im], axis=2)
    c2 = jnp.concatenate([c_im, c_re], axis=2)

    ab_re, ab_im = ap_re[:, 1, :SSM_STATE], ap_im[:, 1, :SSM_STATE]
    nr, ni = ab_re - 1.0, ab_im
    den = a_re * a_re + a_im * a_im
    q_re = ((nr * a_re + ni * a_im) / den)[..., None]
    q_im = ((ni * a_re - nr * a_im) / den)[..., None]
    b_re, b_im = b_re.astype(F32), b_im.astype(F32)
    bb = jnp.concatenate([q_re * b_re - q_im * b_im, q_re * b_im + q_im * b_re], axis=1)

    rev = slice(CHUNK - 1, None, -1)
    apt = jnp.concatenate([ap_re[:, rev, :SSM_STATE].transpose(0, 2, 1),
                           ap_im[:, rev, :SSM_STATE].transpose(0, 2, 1)], axis=1)
    dv = d_vec.astype(F32).reshape(N_GROUPS, SSM_GROUP, 1)
    are = ap_re[:, CHUNK, :SSM_STATE].reshape(1, N_STATE_ROWS)
    aim = ap_im[:, CHUNK, :SSM_STATE].reshape(1, N_STATE_ROWS)
    return e1, e2, c1, c2, apt, bb, dv, are, aim


def kernel(x_prompt, x_sample, cache_conv, state_ssm_re, state_ssm_im, meta_tokens, norm_w,
           final_norm_w, conv_w_in, conv_w, conv_b, conv_w_out, ssm_w_in, ssm_a_re, ssm_a_im,
           ssm_b_re, ssm_b_im, ssm_c_re, ssm_c_im, ssm_d, ssm_log_dt, ssm_w_glu, ssm_b_glu,
           ssm_w_out):
    n_seq, seq_len = x_sample.shape[0], x_sample.shape[1]
    n_prompt_rows = x_prompt.shape[1]
    n_ptiles = n_prompt_rows // TILE
    n_sample_rows = n_seq * seq_len
    assert x_prompt.shape[0] == 1 and seq_len == 2 * CHUNK and N_META == CHUNK
    assert n_prompt_rows % TILE == 0 and N_META + n_sample_rows <= SPECIAL_ROWS

    w_out0 = _cast_bf16(conv_w_out.astype(F32), transpose=False)
    wuz_t = _cast_bf16(ssm_w_in.astype(F32), transpose=True)
    wg_t = _cast_bf16(ssm_w_glu.astype(F32), transpose=True)
    wo_t = _cast_bf16(ssm_w_out.astype(F32), transpose=True)
    w_in0 = conv_w_in.astype(F32)
    conv_w8 = jnp.zeros((8, D_INNER), F32).at[0:3].set(conv_w[0].astype(F32))
    conv_b2 = conv_b[0].astype(F32).reshape(1, D_INNER)
    b_glu = ssm_b_glu[0].astype(F32).reshape(D_INNER, 1)
    nw0 = norm_w[0].astype(F32).reshape(1, D_MODEL)
    nw1 = norm_w[1].astype(F32).reshape(1, D_MODEL)
    nwf = final_norm_w.astype(F32).reshape(1, D_MODEL)
    ssm_ops = _ssm_operators(
        ssm_a_re[0], ssm_a_im[0], ssm_b_re[0], ssm_b_im[0], ssm_c_re[0], ssm_c_im[0],
        ssm_d[0], ssm_log_dt[0])
    sre = state_ssm_re[0].astype(F32).reshape(n_seq, N_STATE_ROWS)
    sim = state_ssm_im[0].astype(F32).reshape(n_seq, N_STATE_ROWS)

    xp = x_prompt.astype(F32)
    x_sp = jnp.concatenate([
        meta_tokens.astype(F32), x_sample.astype(F32).reshape(n_sample_rows, D_MODEL),
        jnp.zeros((SPECIAL_ROWS - N_META - n_sample_rows, D_MODEL), F32)], axis=0)[None]

    zeros8 = jnp.zeros((8, D_INNER), F32)
    xn_sp = _rmsnorm(x_sp, nw0, tm=SPECIAL_ROWS)
    xn_p = _rmsnorm(xp, nw0, tm=TM_CONV)
    y_sp, cv_sp = _conv_proj(xn_sp, w_in0, conv_w8, conv_b2, zeros8,
                             cache_conv[0, :, 0, :].astype(F32), cache_conv[0, :, 1, :].astype(F32))
    init8 = zeros8.at[6:8].set(cv_sp[N_META - 2:N_META])
    y_p, tail_p = _conv_proj(xn_p, w_in0, conv_w8, conv_b2, init8)
    h1_p, xs_p = _out_proj(y_p, w_out0, xp, nw1, lanes=LANES)
    h1_sp, xs_sp = _out_proj(y_sp, w_out0, x_sp, nw1, lanes=SPECIAL_ROWS // CHUNK)

    gt_sp, szt_sp, gt_p, szt_p, hs_re, hs_im, hp_re, hp_im = _ssm(
        xs_sp.reshape(SPECIAL_ROWS, D_MODEL), xs_p.reshape(n_ptiles, TILE, D_MODEL),
        wuz_t, *ssm_ops, sre, sim)
    o_sp = _glu_out(gt_sp, szt_sp, wg_t, b_glu, wo_t, h1_sp, nwf)
    o_p = _glu_out(gt_p, szt_p, wg_t, b_glu, wo_t, h1_p, nwf)

    y_prompt = _to_rows(o_p)
    n_sc = n_sample_rows // CHUNK
    y_sample = o_sp[1:1 + n_sc].reshape(n_seq, seq_len, D_MODEL)
    new_conv_prompt = tail_p[-1, 6:8].reshape(1, 1, 2, D_INNER)
    cv_s = cv_sp[N_META:N_META + n_sample_rows].reshape(n_seq, seq_len, D_INNER)
    new_conv_sample = cv_s[:, seq_len - 2:].reshape(1, n_seq, 2, D_INNER)
    p_shape = (1, 1, N_GROUPS, SSM_STATE)
    s_shape = (1, n_seq, N_GROUPS, SSM_STATE)
    return (y_prompt, y_sample, new_conv_prompt, new_conv_sample,
            hp_re[0].reshape(p_shape), hp_im[0].reshape(p_shape),
            hs_re.reshape(s_shape), hs_im.reshape(s_shape))
```

```python
import functools

import jax
import jax.numpy as jnp
from jax import lax
from jax.experimental import pallas as pl
from jax.experimental.pallas import tpu as pltpu

F32 = jnp.float32
BF16 = jnp.bfloat16

D_MODEL = 2048
D_INNER = 4096
N_META = 16
SSM_GROUP = 16
N_GROUPS = D_INNER // SSM_GROUP
SSM_STATE = 64
N_STATE_ROWS = N_GROUPS * SSM_STATE
RMS_EPS = 1e-6

CHUNK = 16
LANES = 128
TILE = CHUNK * LANES
SPECIAL_ROWS = 1024

TM_CONV = 1024
TE_CONV = 256
TM_OUT = 512
TE_SSM = 256
TE_GLU = 512
LB_GLU = 512
TM_ROWS = 1024
PREP_BLOCK = 1024

VMEM_LIMIT = 56 * 1024 * 1024


def _params(n_axes, vmem=VMEM_LIMIT):
    return pltpu.CompilerParams(dimension_semantics=("arbitrary",) * n_axes,
                                vmem_limit_bytes=vmem)


def _rms(x, w):
    ms = jnp.mean(x * x, axis=-1, keepdims=True)
    return x * lax.rsqrt(ms + RMS_EPS) * w


def _cast_kernel(x_ref, o_ref, *, transpose):
    x = x_ref[...]
    o_ref[...] = (x.T if transpose else x).astype(BF16)


def _cast_bf16(w, *, transpose):
    _, r, c = w.shape
    b = PREP_BLOCK
    out_shape = (c, r) if transpose else (r, c)
    out_map = (lambda i, j: (j, i)) if transpose else (lambda i, j: (i, j))
    return pl.pallas_call(
        functools.partial(_cast_kernel, transpose=transpose),
        grid=(r // b, c // b),
        in_specs=[pl.BlockSpec((None, b, b), lambda i, j: (0, i, j))],
        out_specs=pl.BlockSpec((b, b), out_map),
        out_shape=jax.ShapeDtypeStruct(out_shape, BF16),
        compiler_params=_params(2),
        name="cast_t" if transpose else "cast",
    )(w)


def _rmsnorm_kernel(x_ref, w_ref, o_ref):
    o_ref[...] = _rms(x_ref[...], w_ref[...]).astype(BF16)


def _rmsnorm(x, w, *, tm):
    rows = x.shape[1]
    return pl.pallas_call(
        _rmsnorm_kernel, grid=(rows // tm,),
        in_specs=[pl.BlockSpec((None, tm, D_MODEL), lambda i: (0, i, 0)),
                  pl.BlockSpec((1, D_MODEL), lambda i: (0, 0))],
        out_specs=pl.BlockSpec((tm, D_MODEL), lambda i: (i, 0)),
        out_shape=jax.ShapeDtypeStruct((rows, D_MODEL), BF16),
        compiler_params=_params(1), name="rmsnorm",
    )(x, w)


def _conv_proj_kernel(*refs, tm, te, n_pieces, special, real_rows):
    if special:
        (xn_ref, wb_ref, wc_ref, wv_ref, wz_ref, cw_ref, cb_ref, init_ref, c0_ref, c1_ref,
         y_ref, cv_ref, w_scr, s_scr, p1_scr, p2_scr) = refs
    else:
        (xn_ref, wb_ref, wc_ref, wv_ref, wz_ref, cw_ref, cb_ref, init_ref,
         y_ref, tail_ref, w_scr, s_scr) = refs
    i = pl.program_id(1)

    @pl.when(i == 0)
    def _():
        for p, w_ref in enumerate((wb_ref, wc_ref, wv_ref, wz_ref)):
            w_scr[:, p * te:(p + 1) * te] = w_ref[...].astype(BF16)
        s_scr[0:8, :] = init_ref[...]

    if special:
        pm = real_rows
        cv_ref[pm:tm, :] = jnp.zeros((tm - pm, te), F32)
        y_ref[pm:tm, :] = jnp.zeros((tm - pm, te), BF16)
    else:
        pm = tm // n_pieces
    for r in range(n_pieces):
        lo = r * pm
        proj = jnp.dot(xn_ref[lo:lo + pm, :], w_scr[...], preferred_element_type=F32)
        bg = proj[:, 0 * te:1 * te]
        cg = proj[:, 1 * te:2 * te]
        vv = proj[:, 2 * te:3 * te]
        zz = proj[:, 3 * te:4 * te]
        cv = cg * vv
        s_scr[8 + lo:8 + lo + pm, :] = cv
        if special:
            cv_ref[lo:lo + pm, :] = cv
            p1_scr[...] = s_scr[7 + lo:7 + lo + pm, :]
            p2_scr[...] = s_scr[6 + lo:6 + lo + pm, :]
            for q in range(c0_ref.shape[0]):
                r0 = N_META + 32 * q
                p1_scr[r0:r0 + 1, :] = c1_ref[q:q + 1, :]
                p2_scr[r0:r0 + 1, :] = c0_ref[q:q + 1, :]
                p2_scr[r0 + 1:r0 + 2, :] = c1_ref[q:q + 1, :]
            p1 = p1_scr[...]
            p2 = p2_scr[...]
        else:
            p1 = s_scr[7 + lo:7 + lo + pm, :]
            p2 = s_scr[6 + lo:6 + lo + pm, :]
        conv = cb_ref[...] + cw_ref[0:1, :] * p2
        conv = conv + cw_ref[1:2, :] * p1
        conv = conv + cw_ref[2:3, :] * cv
        y_ref[lo:lo + pm, :] = (bg * conv * jax.nn.silu(zz)).astype(BF16)

    if not special:
        tail = s_scr[tm:tm + 8, :]
        s_scr[0:8, :] = tail
        tail_ref[...] = tail


def _conv_proj(xn, w_in, conv_w8, conv_b, init8, cache0=None, cache1=None, real_rows=None):
    rows = xn.shape[0]
    tm, te = TM_CONV, TE_CONV
    n_j = D_INNER // te
    n_i = rows // tm
    special = cache0 is not None
    w_spec = lambda p: pl.BlockSpec((None, D_MODEL, te),
                                    lambda j, i, p=p: (0, 0, p * n_j + j))
    in_specs = [pl.BlockSpec((tm, D_MODEL), lambda j, i: (i, 0)),
                w_spec(0), w_spec(1), w_spec(2), w_spec(3),
                pl.BlockSpec((8, te), lambda j, i: (0, j)),
                pl.BlockSpec((1, te), lambda j, i: (0, j)),
                pl.BlockSpec((8, te), lambda j, i: (0, j))]
    args = [xn, w_in, w_in, w_in, w_in, conv_w8, conv_b, init8]
    scratch = [pltpu.VMEM((D_MODEL, 4 * te), BF16), pltpu.VMEM((tm + 8, te), F32)]
    y_shape = jax.ShapeDtypeStruct((rows, D_INNER), BF16)
    y_spec = pl.BlockSpec((tm, te), lambda j, i: (i, j))
    if special:
        assert n_i == 1
        nq = cache0.shape[0]
        in_specs += [pl.BlockSpec((nq, te), lambda j, i: (0, j)),
                     pl.BlockSpec((nq, te), lambda j, i: (0, j))]
        args += [cache0, cache1]
        out_shape = (y_shape, jax.ShapeDtypeStruct((rows, D_INNER), F32))
        out_specs = (y_spec, pl.BlockSpec((tm, te), lambda j, i: (i, j)))
        scratch += [pltpu.VMEM((real_rows, te), F32), pltpu.VMEM((real_rows, te), F32)]
    else:
        out_shape = (y_shape, jax.ShapeDtypeStruct((n_i, 8, D_INNER), F32))
        out_specs = (y_spec, pl.BlockSpec((None, 8, te), lambda j, i: (i, 0, j)))
    return pl.pallas_call(
        functools.partial(_conv_proj_kernel, tm=tm, te=te, n_pieces=1 if special else 2,
                          special=special, real_rows=real_rows),
        grid=(n_j, n_i), in_specs=in_specs, out_specs=out_specs, out_shape=out_shape,
        scratch_shapes=scratch, compiler_params=_params(2),
        name="conv_proj_special" if special else "conv_proj",
    )(*args)


def _out_proj_kernel(y_ref, w_ref, x_ref, nw_ref, h_ref, xs_ref, slab_scr, *, real_rows, n_steps):
    tm = y_ref.shape[0]
    nc = tm // CHUNK
    n_slabs = D_MODEL // LANES

    def block(r):
        if r == 0:
            h = jnp.zeros((tm, D_MODEL), F32)
        else:
            h = x_ref[0:r, :] + jnp.dot(y_ref[0:r, :], w_ref[...], preferred_element_type=F32)
            if r < tm:
                h = jnp.concatenate([h, jnp.zeros((tm - r, D_MODEL), F32)], axis=0)
        for k in range(n_slabs):
            slab_scr[k] = h[:, LANES * k:LANES * (k + 1)]
        for s in range(CHUNK):
            hs = jnp.concatenate(
                [slab_scr[k, pl.ds(s, nc, stride=CHUNK), :] for k in range(n_slabs)], axis=1)
            h_ref[s] = hs
            xs_ref[s] = _rms(hs, nw_ref[...]).astype(BF16)

    if real_rows is None:
        block(tm)
    else:
        i = pl.program_id(0)
        for b in range(n_steps):
            pl.when(i == b)(functools.partial(block, min(max(real_rows - b * tm, 0), tm)))


def _out_proj(y, w_out, x, norm_w, *, lanes, real_rows=None):
    tm = TM_OUT
    nc = tm // CHUNK
    per_tile = lanes // nc
    n_i = x.shape[1] // tm
    tile_map = lambda i: (i // per_tile, 0, i % per_tile, 0)
    tile_shape = (n_i // per_tile, CHUNK, lanes, D_MODEL)
    return pl.pallas_call(
        functools.partial(_out_proj_kernel, real_rows=real_rows, n_steps=n_i),
        grid=(n_i,),
        in_specs=[
            pl.BlockSpec((tm, D_INNER), lambda i: (i, 0)),
            pl.BlockSpec((D_INNER, D_MODEL), lambda i: (0, 0), pipeline_mode=pl.Buffered(1)),
            pl.BlockSpec((None, tm, D_MODEL), lambda i: (0, i, 0)),
            pl.BlockSpec((1, D_MODEL), lambda i: (0, 0)),
        ],
        out_specs=(pl.BlockSpec((None, CHUNK, nc, D_MODEL), tile_map),
                   pl.BlockSpec((None, CHUNK, nc, D_MODEL), tile_map)),
        out_shape=(jax.ShapeDtypeStruct(tile_shape, F32),
                   jax.ShapeDtypeStruct(tile_shape, BF16)),
        scratch_shapes=[pltpu.VMEM((D_MODEL // LANES, tm, LANES), F32)],
        compiler_params=_params(1, 60 * 1024 * 1024),
        name="out_proj",
    )(y, w_out, x, norm_w)


def _split_bf16(x):
    hi = x.astype(BF16)
    return hi, (x - hi.astype(F32)).astype(BF16)


def _ssm_kernel(xsp_ref, xs_ref, wu_ref, wz_ref, e1_ref, e2_ref, c1_ref, c2_ref, apt_ref, bb_ref,
                dv_ref, are_ref, aim_ref, sre_ref, sim_ref,
                gtsp_ref, sztsp_ref, gt_ref, szt_ref, hsre_ref, hsim_ref, hpre_ref, hpim_ref,
                w_scr, tk_scr, win_scr, cout_scr, zt_scr, xre_scr, xim_scr, pre_scr, pim_scr,
                cre_scr, cim_scr, *, te, n_tiles):
    i = pl.program_id(1)
    gb = te // SSM_GROUP
    rows = gb * SSM_STATE
    n_seq = sre_ref.shape[0]
    bdims = (((2,), (1,)), ((0,), (0,)))

    @pl.when(i == 0)
    def _():
        w_scr[0:te, :] = wu_ref[...]
        w_scr[te:2 * te, :] = wz_ref[...]

        c1 = c1_ref[...]
        c2 = c2_ref[...]
        cap = [c1 * e1_ref[:, t:t + 1, :] + c2 * e2_ref[:, t:t + 1, :] for t in range(CHUNK + 1)]
        ca = jnp.concatenate(cap[:CHUNK], axis=1)
        cout_scr[...] = jnp.concatenate(cap[1:], axis=1).astype(BF16)

        lane = lax.broadcasted_iota(jnp.int32, (SSM_GROUP, 256), 1)
        row = lax.broadcasted_iota(jnp.int32, (SSM_GROUP, 256), 0)
        diag = row == (lane & (SSM_GROUP - 1))
        tile_c = diag.astype(BF16)
        spread_s = (row == (lane >> 4)).astype(BF16)
        expand = lambda b, m, n_rows: jnp.dot(b, m, preferred_element_type=F32).reshape(
            gb, n_rows, 256)
        bhi, blo = _split_bf16(bb_ref[...].reshape(gb * 2 * SSM_STATE, SSM_GROUP))
        bhi_t = expand(bhi, tile_c, 2 * SSM_STATE)
        blo_t = expand(blo, tile_c, 2 * SSM_STATE)

        phi, plo = _split_bf16(apt_ref[...].reshape(gb * 2 * SSM_STATE, SSM_GROUP))
        ap_t = expand(phi, spread_s, 2 * SSM_STATE) + expand(plo, spread_s, 2 * SSM_STATE)
        b_t = bhi_t + blo_t
        pr, pi = ap_t[:, :SSM_STATE], ap_t[:, SSM_STATE:]
        br, bi = b_t[:, :SSM_STATE], b_t[:, SSM_STATE:]
        win_scr[:, 0:SSM_STATE, :] = (pr * br - pi * bi).astype(BF16)
        win_scr[:, SSM_STATE:, :] = (pr * bi + pi * br).astype(BF16)

        bhi_t, blo_t = bhi_t.astype(BF16), blo_t.astype(BF16)
        ahi, alo = _split_bf16(ca)
        kw = (lax.dot_general(ahi, bhi_t, bdims, preferred_element_type=F32)
              + lax.dot_general(ahi, blo_t, bdims, preferred_element_type=F32)
              + lax.dot_general(alo, bhi_t, bdims, preferred_element_type=F32))
        kw0 = (kw[:, 0:SSM_GROUP, :] + jnp.where(diag, dv_ref[...], 0.0)).astype(BF16)
        kw = kw.astype(BF16)
        tk_scr[...] = jnp.zeros_like(tk_scr)
        for s in range(CHUNK):
            lo = SSM_GROUP * s
            hi = lo + SSM_GROUP
            tk_scr[:, lo:hi, lo:hi] = kw0[:, :, lo:hi]
            if hi < 256:
                tk_scr[:, hi:256, lo:hi] = kw[:, SSM_GROUP:256 - lo, lo:hi]

    nt_dims = (((1,), (1,)), ((), ()))
    half = LANES // 2
    low_half = lax.broadcasted_iota(jnp.int32, (1, LANES), 1) < half

    ar = are_ref[...]
    ai = aim_ref[...]

    def step(hr, hi, n):
        xr = xre_scr[pl.ds(n, 1), :]
        xi = xim_scr[pl.ds(n, 1), :]
        return ar * hr - ai * hi + xr, ar * hi + ai * hr + xi

    def chunk_inputs():
        zt = zt_scr[...]
        yt = lax.dot_general(tk_scr[...], zt, bdims, preferred_element_type=F32)
        xt = lax.dot_general(win_scr[...], zt, bdims, preferred_element_type=F32)
        xre_scr[...] = xt[:, 0:SSM_STATE, :].reshape(rows, LANES).T
        xim_scr[...] = xt[:, SSM_STATE:, :].reshape(rows, LANES).T
        return yt

    def gelu_pieces(yt):
        hp = jnp.concatenate([pre_scr[...].T.reshape(gb, SSM_STATE, LANES),
                              pim_scr[...].T.reshape(gb, SSM_STATE, LANES)], axis=1)
        ycorr = lax.dot_general(cout_scr[...], hp.astype(BF16), bdims,
                                preferred_element_type=F32)
        g = jax.nn.gelu(yt + ycorr)
        return [g[:, SSM_GROUP * t:SSM_GROUP * (t + 1), :].reshape(te, LANES)
                for t in range(CHUNK)]

    @pl.when(i == 0)
    def _():
        uz = lax.dot_general(w_scr[...], xsp_ref[...], nt_dims,
                             preferred_element_type=F32)
        sztsp_ref[...] = jax.nn.silu(uz[te:, :]).astype(BF16)
        for s in range(CHUNK):
            blk = uz[:te, LANES * (s // 2):LANES * (s // 2 + 1)]
            if s % 2:
                blk = pltpu.roll(blk, half, 1)
            zt_scr[:, CHUNK * s:CHUNK * (s + 1), :] = (
                jnp.where(low_half, blk, 0.0).astype(BF16).reshape(gb, SSM_GROUP, LANES))
        yt = chunk_inputs()

        pre_scr[...] = jnp.zeros_like(pre_scr)
        pim_scr[...] = jnp.zeros_like(pim_scr)
        zero = jnp.zeros((1, rows), F32)
        hr, hi = step(zero, zero, 0)
        cre_scr[0:1, :] = hr
        cim_scr[0:1, :] = hi
        for q in range(n_seq):
            hr = sre_ref[q:q + 1, :]
            hi = sim_ref[q:q + 1, :]
            for e in range(2):
                n = 1 + 2 * q + e
                pre_scr[n:n + 1, :] = hr
                pim_scr[n:n + 1, :] = hi
                hr, hi = step(hr, hi, n)
            hsre_ref[q:q + 1, :] = hr
            hsim_ref[q:q + 1, :] = hi

        pieces = gelu_pieces(yt)
        for t in range(0, CHUNK, 2):
            both = jnp.where(low_half, pieces[t], pltpu.roll(pieces[t + 1], half, 1))
            gtsp_ref[:, LANES * (t // 2):LANES * (t // 2 + 1)] = both.astype(BF16)

    @pl.when(i > 0)
    def _():
        uz = lax.dot_general(w_scr[...], xs_ref[...], nt_dims,
                             preferred_element_type=F32)
        szt_ref[...] = jax.nn.silu(uz[te:, :]).astype(BF16)
        ub = uz[:te, :].astype(BF16)
        for s in range(CHUNK):
            zt_scr[:, CHUNK * s:CHUNK * (s + 1), :] = (
                ub[:, LANES * s:LANES * (s + 1)].reshape(gb, SSM_GROUP, LANES))
        yt = chunk_inputs()

        hr, hi = cre_scr[0:1, :], cim_scr[0:1, :]
        for n in range(LANES):
            pre_scr[n:n + 1, :] = hr
            pim_scr[n:n + 1, :] = hi
            hr, hi = step(hr, hi, n)
        cre_scr[0:1, :] = hr
        cim_scr[0:1, :] = hi

        @pl.when(i == n_tiles - 1)
        def _():
            hpre_ref[...] = jnp.broadcast_to(hr, hpre_ref.shape)
            hpim_ref[...] = jnp.broadcast_to(hi, hpim_ref.shape)

        pieces = gelu_pieces(yt)
        for t in range(CHUNK):
            gt_ref[:, LANES * t:LANES * (t + 1)] = pieces[t].astype(BF16)


def _ssm(xs_sp, xs_p, wuz_t, e1, e2, c1, c2, apt, bb, dv, are, aim, sre, sim):
    n_ptiles = xs_p.shape[0]
    n_tiles = 1 + n_ptiles
    te = TE_SSM
    n_j = D_INNER // te
    gb = te // SSM_GROUP
    rows = gb * SSM_STATE
    n_seq = sre.shape[0]
    prev = lambda i: jnp.maximum(i - 1, 0)
    act_shape = jax.ShapeDtypeStruct((n_ptiles, D_INNER, TILE), BF16)
    act_spec = pl.BlockSpec((None, te, TILE), lambda j, i: (prev(i), j, 0))
    sp_shape = jax.ShapeDtypeStruct((1, D_INNER, TILE // 2), BF16)
    sp_spec = pl.BlockSpec((None, te, TILE // 2), lambda j, i: (0, j, 0))
    hs_shape = jax.ShapeDtypeStruct((n_seq, N_STATE_ROWS), F32)
    hs_spec = pl.BlockSpec((n_seq, rows), lambda j, i: (0, j))
    hp_shape = jax.ShapeDtypeStruct((8, N_STATE_ROWS), F32)
    hp_spec = pl.BlockSpec((8, rows), lambda j, i: (0, j))
    row_spec = pl.BlockSpec((1, rows), lambda j, i: (0, j))
    grp_spec = lambda r, c: pl.BlockSpec((gb, r, c), lambda j, i: (j, 0, 0))
    return pl.pallas_call(
        functools.partial(_ssm_kernel, te=te, n_tiles=n_tiles),
        grid=(n_j, n_tiles),
        in_specs=[
            pl.BlockSpec((TILE // 2, D_MODEL), lambda j, i: (0, 0), pipeline_mode=pl.Buffered(1)),
            pl.BlockSpec((None, TILE, D_MODEL), lambda j, i: (prev(i), 0, 0)),
            pl.BlockSpec((te, D_MODEL), lambda j, i: (j, 0)),
            pl.BlockSpec((te, D_MODEL), lambda j, i: (n_j + j, 0)),
            grp_spec(e1.shape[1], 2 * SSM_STATE), grp_spec(e2.shape[1], 2 * SSM_STATE),
            grp_spec(SSM_GROUP, 2 * SSM_STATE), grp_spec(SSM_GROUP, 2 * SSM_STATE),
            grp_spec(2 * SSM_STATE, SSM_GROUP), grp_spec(2 * SSM_STATE, SSM_GROUP),
            grp_spec(SSM_GROUP, 1),
            row_spec, row_spec, hs_spec, hs_spec,
        ],
        out_specs=(sp_spec, sp_spec, act_spec, act_spec, hs_spec, hs_spec, hp_spec, hp_spec),
        out_shape=(sp_shape, sp_shape, act_shape, act_shape, hs_shape, hs_shape,
                   hp_shape, hp_shape),
        scratch_shapes=[pltpu.VMEM((2 * te, D_MODEL), BF16),
                        pltpu.VMEM((gb, 256, 256), BF16),
                        pltpu.VMEM((gb, 2 * SSM_STATE, 256), BF16),
                        pltpu.VMEM((gb, 256, 2 * SSM_STATE), BF16),
                        pltpu.VMEM((gb, 256, LANES), BF16),
                        pltpu.VMEM((LANES, rows), F32), pltpu.VMEM((LANES, rows), F32),
                        pltpu.VMEM((LANES, rows), F32), pltpu.VMEM((LANES, rows), F32),
                        pltpu.VMEM((8, rows), F32), pltpu.VMEM((8, rows), F32)],
        compiler_params=_params(2, 60 * 1024 * 1024),
        name="s5_scan",
    )(xs_sp, xs_p, wuz_t, wuz_t, e1, e2, c1, c2, apt, bb, dv, are, aim, sre, sim)


def _glu_out_kernel(gt_ref, szt_ref, wg_ref, bg_ref, wo_ref, h_ref, nw_ref, o_ref, acc_scr, *,
                    te, lb):
    e = pl.program_id(2)

    @pl.when(e == 0)
    def _():
        acc_scr[...] = jnp.zeros_like(acc_scr)

    gate = jnp.dot(wg_ref[...], gt_ref[...], preferred_element_type=F32) + bg_ref[...]
    ge = gt_ref[pl.ds(pl.multiple_of(e * te, te), te), :].astype(F32)
    y3 = (ge * jax.nn.sigmoid(gate)) * szt_ref[...].astype(F32)
    acc_scr[...] += jnp.dot(wo_ref[...], y3.astype(BF16), preferred_element_type=F32)

    @pl.when(e == pl.num_programs(2) - 1)
    def _():
        o = acc_scr[...].T
        ln = h_ref.shape[1]
        for t in range(lb // ln):
            h = h_ref[t] + o[ln * t:ln * (t + 1), :]
            o_ref[:, D_MODEL * t:D_MODEL * (t + 1)] = _rms(h, nw_ref[...])


def _glu_out(gt, szt, wg_t, b_glu, wo_t, h_all, norm_w):
    n_tiles, _, lanes, _ = h_all.shape
    te = TE_GLU
    lb = LB_GLU
    n_l = CHUNK * lanes // lb
    n_e = D_INNER // te
    tpb = lb // lanes
    return pl.pallas_call(
        functools.partial(_glu_out_kernel, te=te, lb=lb),
        grid=(n_tiles, n_l, n_e),
        in_specs=[
            pl.BlockSpec((None, D_INNER, lb), lambda i, l, e: (i, 0, l)),
            pl.BlockSpec((None, te, lb), lambda i, l, e: (i, e, l)),
            pl.BlockSpec((te, D_INNER), lambda i, l, e: (e, 0)),
            pl.BlockSpec((te, 1), lambda i, l, e: (e, 0)),
            pl.BlockSpec((D_MODEL, te), lambda i, l, e: (0, e)),
            pl.BlockSpec((None, tpb, lanes, D_MODEL), lambda i, l, e: (i, l, 0, 0)),
            pl.BlockSpec((1, D_MODEL), lambda i, l, e: (0, 0)),
        ],
        out_specs=pl.BlockSpec((lanes, tpb * D_MODEL), lambda i, l, e: (i, l)),
        out_shape=jax.ShapeDtypeStruct((n_tiles * lanes, CHUNK * D_MODEL), F32),
        scratch_shapes=[pltpu.VMEM((D_MODEL, lb), F32)],
        compiler_params=_params(3),
        name="glu_out",
    )(gt, szt, wg_t, b_glu, wo_t, h_all, norm_w)


def _to_rows_kernel(o_ref, y_ref, slab_scr):
    nc = o_ref.shape[0]
    n_slabs = D_MODEL // LANES
    for t in range(CHUNK):
        for k in range(n_slabs):
            slab_scr[k, pl.ds(t, nc, stride=CHUNK), :] = (
                o_ref[:, D_MODEL * t + LANES * k:D_MODEL * t + LANES * (k + 1)])
    for k in range(n_slabs):
        y_ref[:, LANES * k:LANES * (k + 1)] = slab_scr[k]


def _to_rows(o):
    n_chunks = o.shape[0]
    tm = TM_ROWS
    nc = tm // CHUNK
    return pl.pallas_call(
        _to_rows_kernel,
        grid=(n_chunks // nc,),
        in_specs=[pl.BlockSpec((nc, CHUNK * D_MODEL), lambda i: (i, 0))],
        out_specs=pl.BlockSpec((None, tm, D_MODEL), lambda i: (0, i, 0)),
        out_shape=jax.ShapeDtypeStruct((1, n_chunks * CHUNK, D_MODEL), F32),
        scratch_shapes=[pltpu.VMEM((D_MODEL // LANES, tm, LANES), F32)],
        compiler_params=_params(1),
        name="to_rows",
    )(o)


def _ssm_operators(a_re, a_im, b_re, b_im, c_re, c_im, d_vec, log_dt):
    a_re, a_im = a_re.astype(F32), a_im.astype(F32)
    dt = jnp.exp(log_dt.astype(F32))[:, None]
    th_re = jnp.concatenate([a_re * dt] * 2, axis=1)[:, None, :]
    th_im = jnp.concatenate([a_im * dt] * 2, axis=1)[:, None, :]
    n_tau = 24
    taus = jnp.arange(n_tau, dtype=F32)[None, :, None]
    used = taus <= CHUNK
    mag = jnp.exp(jnp.where(used, th_re * taus, 0.0))
    ang = jnp.where(used, th_im * taus, 0.0)
    ap_re = mag * jnp.cos(ang)
    ap_im = mag * jnp.sin(ang)
    sign = jnp.concatenate([jnp.ones((SSM_STATE,), F32), -jnp.ones((SSM_STATE,), F32)])
    e1 = ap_re * sign
    e2 = -ap_im
    c_re, c_im = c_re.astype(F32), c_im.astype(F32)
    c1 = jnp.concatenate([c_re, c_im], axis=2)
    c2 = jnp.concatenate([c_im, c_re], axis=2)

    ab_re, ab_im = ap_re[:, 1, :SSM_STATE], ap_im[:, 1, :SSM_STATE]
    nr, ni = ab_re - 1.0, ab_im
    den = a_re * a_re + a_im * a_im
    q_re = ((nr * a_re + ni * a_im) / den)[..., None]
    q_im = ((ni * a_re - nr * a_im) / den)[..., None]
    b_re, b_im = b_re.astype(F32), b_im.astype(F32)
    bb = jnp.concatenate([q_re * b_re - q_im * b_im, q_re * b_im + q_im * b_re], axis=1)

    rev = slice(CHUNK - 1, None, -1)
    apt = jnp.concatenate([ap_re[:, rev, :SSM_STATE].transpose(0, 2, 1),
                           ap_im[:, rev, :SSM_STATE].transpose(0, 2, 1)], axis=1)
    dv = d_vec.astype(F32).reshape(N_GROUPS, SSM_GROUP, 1)
    are = ap_re[:, CHUNK, :SSM_STATE].reshape(1, N_STATE_ROWS)
    aim = ap_im[:, CHUNK, :SSM_STATE].reshape(1, N_STATE_ROWS)
    return e1, e2, c1, c2, apt, bb, dv, are, aim


def kernel(x_prompt, x_sample, cache_conv, state_ssm_re, state_ssm_im, meta_tokens, norm_w,
           final_norm_w, conv_w_in, conv_w, conv_b, conv_w_out, ssm_w_in, ssm_a_re, ssm_a_im,
           ssm_b_re, ssm_b_im, ssm_c_re, ssm_c_im, ssm_d, ssm_log_dt, ssm_w_glu, ssm_b_glu,
           ssm_w_out):
    n_seq, seq_len = x_sample.shape[0], x_sample.shape[1]
    n_prompt_rows = x_prompt.shape[1]
    n_ptiles = n_prompt_rows // TILE
    n_sample_rows = n_seq * seq_len
    assert x_prompt.shape[0] == 1 and seq_len == 2 * CHUNK and N_META == CHUNK
    assert n_prompt_rows % TILE == 0 and N_META + n_sample_rows <= SPECIAL_ROWS

    w_out0 = _cast_bf16(conv_w_out.astype(F32), transpose=False)
    wuz_t = _cast_bf16(ssm_w_in.astype(F32), transpose=True)
    wg_t = _cast_bf16(ssm_w_glu.astype(F32), transpose=True)
    wo_t = _cast_bf16(ssm_w_out.astype(F32), transpose=True)
    w_in0 = conv_w_in.astype(F32)
    conv_w8 = jnp.zeros((8, D_INNER), F32).at[0:3].set(conv_w[0].astype(F32))
    conv_b2 = conv_b[0].astype(F32).reshape(1, D_INNER)
    b_glu = ssm_b_glu[0].astype(F32).reshape(D_INNER, 1)
    nw0 = norm_w[0].astype(F32).reshape(1, D_MODEL)
    nw1 = norm_w[1].astype(F32).reshape(1, D_MODEL)
    nwf = final_norm_w.astype(F32).reshape(1, D_MODEL)
    ssm_ops = _ssm_operators(
        ssm_a_re[0], ssm_a_im[0], ssm_b_re[0], ssm_b_im[0], ssm_c_re[0], ssm_c_im[0],
        ssm_d[0], ssm_log_dt[0])
    sre = state_ssm_re[0].astype(F32).reshape(n_seq, N_STATE_ROWS)
    sim = state_ssm_im[0].astype(F32).reshape(n_seq, N_STATE_ROWS)

    xp = x_prompt.astype(F32)
    x_sp = jnp.concatenate([
        meta_tokens.astype(F32), x_sample.astype(F32).reshape(n_sample_rows, D_MODEL),
        jnp.zeros((SPECIAL_ROWS - N_META - n_sample_rows, D_MODEL), F32)], axis=0)[None]

    zeros8 = jnp.zeros((8, D_INNER), F32)
    xn_sp = _rmsnorm(x_sp, nw0, tm=SPECIAL_ROWS)
    xn_p = _rmsnorm(xp, nw0, tm=TM_CONV)
    sp_real = N_META + n_sample_rows
    y_sp, cv_sp = _conv_proj(xn_sp, w_in0, conv_w8, conv_b2, zeros8,
                             cache_conv[0, :, 0, :].astype(F32), cache_conv[0, :, 1, :].astype(F32),
                             real_rows=sp_real)
    init8 = zeros8.at[6:8].set(cv_sp[N_META - 2:N_META])
    y_p, tail_p = _conv_proj(xn_p, w_in0, conv_w8, conv_b2, init8)
    h1_p, xs_p = _out_proj(y_p, w_out0, xp, nw1, lanes=LANES)
    h1_sp, xs_sp = _out_proj(y_sp, w_out0, x_sp, nw1, lanes=SPECIAL_ROWS // CHUNK,
                             real_rows=sp_real)

    gt_sp, szt_sp, gt_p, szt_p, hs_re, hs_im, hp_re, hp_im = _ssm(
        xs_sp.reshape(SPECIAL_ROWS, D_MODEL), xs_p.reshape(n_ptiles, TILE, D_MODEL),
        wuz_t, *ssm_ops, sre, sim)
    o_sp = _glu_out(gt_sp, szt_sp, wg_t, b_glu, wo_t, h1_sp, nwf)
    o_p = _glu_out(gt_p, szt_p, wg_t, b_glu, wo_t, h1_p, nwf)

    y_prompt = _to_rows(o_p)
    n_sc = n_sample_rows // CHUNK
    y_sample = o_sp[1:1 + n_sc].reshape(n_seq, seq_len, D_MODEL)
    new_conv_prompt = tail_p[-1, 6:8].reshape(1, 1, 2, D_INNER)
    cv_s = cv_sp[N_META:N_META + n_sample_rows].reshape(n_seq, seq_len, D_INNER)
    new_conv_sample = cv_s[:, seq_len - 2:].reshape(1, n_seq, 2, D_INNER)
    p_shape = (1, 1, N_GROUPS, SSM_STATE)
    s_shape = (1, n_seq, N_GROUPS, SSM_STATE)
    return (y_prompt, y_sample, new_conv_prompt, new_conv_sample,
            hp_re[0].reshape(p_shape), hp_im[0].reshape(p_shape),
            hs_re.reshape(s_shape), hs_im.reshape(s_shape))
```

```python
import functools

import jax
import jax.numpy as jnp
from jax import lax
from jax.experimental import pallas as pl
from jax.experimental.pallas import tpu as pltpu

F32 = jnp.float32
BF16 = jnp.bfloat16

D_MODEL = 2048
D_INNER = 4096
N_META = 16
SSM_GROUP = 16
N_GROUPS = D_INNER // SSM_GROUP
SSM_STATE = 64
N_STATE_ROWS = N_GROUPS * SSM_STATE
RMS_EPS = 1e-6

CHUNK = 16
LANES = 128
TILE = CHUNK * LANES
SPECIAL_ROWS = 1024

TM_CONV = 1024
TE_CONV = 512
TM_OUT = 512
TE_SSM = 256
TE_GLU = 512
LB_GLU = 512
TM_ROWS = 1024
PREP_BLOCK = 1024

VMEM_LIMIT = 56 * 1024 * 1024


def _params(n_axes, vmem=VMEM_LIMIT):
    return pltpu.CompilerParams(dimension_semantics=("arbitrary",) * n_axes,
                                vmem_limit_bytes=vmem)


def _rms(x, w):
    ms = jnp.mean(x * x, axis=-1, keepdims=True)
    return x * lax.rsqrt(ms + RMS_EPS) * w


def _cast_kernel(x_ref, o_ref, *, transpose):
    x = x_ref[...]
    o_ref[...] = (x.T if transpose else x).astype(BF16)


def _cast_bf16(w, *, transpose):
    _, r, c = w.shape
    b = PREP_BLOCK
    out_shape = (c, r) if transpose else (r, c)
    out_map = (lambda i, j: (j, i)) if transpose else (lambda i, j: (i, j))
    return pl.pallas_call(
        functools.partial(_cast_kernel, transpose=transpose),
        grid=(r // b, c // b),
        in_specs=[pl.BlockSpec((None, b, b), lambda i, j: (0, i, j))],
        out_specs=pl.BlockSpec((b, b), out_map),
        out_shape=jax.ShapeDtypeStruct(out_shape, BF16),
        compiler_params=_params(2),
        name="cast_t" if transpose else "cast",
    )(w)


def _rmsnorm_kernel(x_ref, w_ref, o_ref):
    o_ref[...] = _rms(x_ref[...], w_ref[...]).astype(BF16)


def _rmsnorm(x, w, *, tm):
    rows = x.shape[1]
    return pl.pallas_call(
        _rmsnorm_kernel, grid=(rows // tm,),
        in_specs=[pl.BlockSpec((None, tm, D_MODEL), lambda i: (0, i, 0)),
                  pl.BlockSpec((1, D_MODEL), lambda i: (0, 0))],
        out_specs=pl.BlockSpec((tm, D_MODEL), lambda i: (i, 0)),
        out_shape=jax.ShapeDtypeStruct((rows, D_MODEL), BF16),
        compiler_params=_params(1), name="rmsnorm",
    )(x, w)


def _conv_proj_kernel(*refs, tm, te, n_pieces, special, real_rows):
    if special:
        (xn_ref, wb_ref, wc_ref, wv_ref, wz_ref, cw_ref, cb_ref, init_ref, c0_ref, c1_ref,
         y_ref, cv_ref, w_scr, s_scr, p1_scr, p2_scr) = refs
    else:
        (xn_ref, wb_ref, wc_ref, wv_ref, wz_ref, cw_ref, cb_ref, init_ref,
         y_ref, tail_ref, w_scr, s_scr) = refs
    i = pl.program_id(1)

    @pl.when(i == 0)
    def _():
        for p, w_ref in enumerate((wb_ref, wc_ref, wv_ref, wz_ref)):
            w_scr[:, p * te:(p + 1) * te] = w_ref[...].astype(BF16)
        s_scr[0:8, :] = init_ref[...]

    if special:
        pm = real_rows
        cv_ref[pm:tm, :] = jnp.zeros((tm - pm, te), F32)
        y_ref[pm:tm, :] = jnp.zeros((tm - pm, te), BF16)
    else:
        pm = tm // n_pieces
    for r in range(n_pieces):
        lo = r * pm
        proj = jnp.dot(xn_ref[lo:lo + pm, :], w_scr[...], preferred_element_type=F32)
        bg = proj[:, 0 * te:1 * te]
        cg = proj[:, 1 * te:2 * te]
        vv = proj[:, 2 * te:3 * te]
        zz = proj[:, 3 * te:4 * te]
        cv = cg * vv
        s_scr[8 + lo:8 + lo + pm, :] = cv
        if special:
            cv_ref[lo:lo + pm, :] = cv
            p1_scr[...] = s_scr[7 + lo:7 + lo + pm, :]
            p2_scr[...] = s_scr[6 + lo:6 + lo + pm, :]
            for q in range(c0_ref.shape[0]):
                r0 = N_META + 32 * q
                p1_scr[r0:r0 + 1, :] = c1_ref[q:q + 1, :]
                p2_scr[r0:r0 + 1, :] = c0_ref[q:q + 1, :]
                p2_scr[r0 + 1:r0 + 2, :] = c1_ref[q:q + 1, :]
            p1 = p1_scr[...]
            p2 = p2_scr[...]
        else:
            p1 = s_scr[7 + lo:7 + lo + pm, :]
            p2 = s_scr[6 + lo:6 + lo + pm, :]
        conv = cb_ref[...] + cw_ref[0:1, :] * p2
        conv = conv + cw_ref[1:2, :] * p1
        conv = conv + cw_ref[2:3, :] * cv
        y_ref[lo:lo + pm, :] = (bg * conv * jax.nn.silu(zz)).astype(BF16)

    if not special:
        tail = s_scr[tm:tm + 8, :]
        s_scr[0:8, :] = tail
        tail_ref[...] = tail


def _conv_proj(xn, w_in, conv_w8, conv_b, init8, cache0=None, cache1=None, real_rows=None):
    rows = xn.shape[0]
    tm, te = TM_CONV, TE_CONV
    n_j = D_INNER // te
    n_i = rows // tm
    special = cache0 is not None
    w_spec = lambda p: pl.BlockSpec((None, D_MODEL, te),
                                    lambda j, i, p=p: (0, 0, p * n_j + j))
    in_specs = [pl.BlockSpec((tm, D_MODEL), lambda j, i: (i, 0)),
                w_spec(0), w_spec(1), w_spec(2), w_spec(3),
                pl.BlockSpec((8, te), lambda j, i: (0, j)),
                pl.BlockSpec((1, te), lambda j, i: (0, j)),
                pl.BlockSpec((8, te), lambda j, i: (0, j))]
    args = [xn, w_in, w_in, w_in, w_in, conv_w8, conv_b, init8]
    scratch = [pltpu.VMEM((D_MODEL, 4 * te), BF16), pltpu.VMEM((tm + 8, te), F32)]
    y_shape = jax.ShapeDtypeStruct((rows, D_INNER), BF16)
    y_spec = pl.BlockSpec((tm, te), lambda j, i: (i, j))
    if special:
        assert n_i == 1
        nq = cache0.shape[0]
        in_specs += [pl.BlockSpec((nq, te), lambda j, i: (0, j)),
                     pl.BlockSpec((nq, te), lambda j, i: (0, j))]
        args += [cache0, cache1]
        out_shape = (y_shape, jax.ShapeDtypeStruct((rows, D_INNER), F32))
        out_specs = (y_spec, pl.BlockSpec((tm, te), lambda j, i: (i, j)))
        scratch += [pltpu.VMEM((real_rows, te), F32), pltpu.VMEM((real_rows, te), F32)]
    else:
        out_shape = (y_shape, jax.ShapeDtypeStruct((n_i, 8, D_INNER), F32))
        out_specs = (y_spec, pl.BlockSpec((None, 8, te), lambda j, i: (i, 0, j)))
    return pl.pallas_call(
        functools.partial(_conv_proj_kernel, tm=tm, te=te, n_pieces=1 if special else 2,
                          special=special, real_rows=real_rows),
        grid=(n_j, n_i), in_specs=in_specs, out_specs=out_specs, out_shape=out_shape,
        scratch_shapes=scratch, compiler_params=_params(2, 60 * 1024 * 1024),
        name="conv_proj_special" if special else "conv_proj",
    )(*args)


def _out_proj_kernel(y_ref, w_ref, x_ref, nw_ref, h_ref, xs_ref, slab_scr, *, real_rows, n_steps):
    tm = y_ref.shape[0]
    nc = tm // CHUNK
    n_slabs = D_MODEL // LANES

    def block(r):
        if r == 0:
            h = jnp.zeros((tm, D_MODEL), F32)
        else:
            h = x_ref[0:r, :] + jnp.dot(y_ref[0:r, :], w_ref[...], preferred_element_type=F32)
            if r < tm:
                h = jnp.concatenate([h, jnp.zeros((tm - r, D_MODEL), F32)], axis=0)
        for k in range(n_slabs):
            slab_scr[k] = h[:, LANES * k:LANES * (k + 1)]
        for s in range(CHUNK):
            hs = jnp.concatenate(
                [slab_scr[k, pl.ds(s, nc, stride=CHUNK), :] for k in range(n_slabs)], axis=1)
            h_ref[s] = hs
            xs_ref[s] = _rms(hs, nw_ref[...]).astype(BF16)

    if real_rows is None:
        block(tm)
    else:
        i = pl.program_id(0)
        for b in range(n_steps):
            pl.when(i == b)(functools.partial(block, min(max(real_rows - b * tm, 0), tm)))


def _out_proj(y, w_out, x, norm_w, *, lanes, real_rows=None):
    tm = TM_OUT
    nc = tm // CHUNK
    per_tile = lanes // nc
    n_i = x.shape[1] // tm
    tile_map = lambda i: (i // per_tile, 0, i % per_tile, 0)
    tile_shape = (n_i // per_tile, CHUNK, lanes, D_MODEL)
    return pl.pallas_call(
        functools.partial(_out_proj_kernel, real_rows=real_rows, n_steps=n_i),
        grid=(n_i,),
        in_specs=[
            pl.BlockSpec((tm, D_INNER), lambda i: (i, 0)),
            pl.BlockSpec((D_INNER, D_MODEL), lambda i: (0, 0), pipeline_mode=pl.Buffered(1)),
            pl.BlockSpec((None, tm, D_MODEL), lambda i: (0, i, 0)),
            pl.BlockSpec((1, D_MODEL), lambda i: (0, 0)),
        ],
        out_specs=(pl.BlockSpec((None, CHUNK, nc, D_MODEL), tile_map),
                   pl.BlockSpec((None, CHUNK, nc, D_MODEL), tile_map)),
        out_shape=(jax.ShapeDtypeStruct(tile_shape, F32),
                   jax.ShapeDtypeStruct(tile_shape, BF16)),
        scratch_shapes=[pltpu.VMEM((D_MODEL // LANES, tm, LANES), F32)],
        compiler_params=_params(1, 60 * 1024 * 1024),
        name="out_proj",
    )(y, w_out, x, norm_w)


def _split_bf16(x):
    hi = x.astype(BF16)
    return hi, (x - hi.astype(F32)).astype(BF16)


def _ssm_kernel(xsp_ref, xs_ref, wu_ref, wz_ref, e1_ref, e2_ref, c1_ref, c2_ref, apt_ref, bb_ref,
                dv_ref, are_ref, aim_ref, sre_ref, sim_ref,
                gtsp_ref, sztsp_ref, gt_ref, szt_ref, hsre_ref, hsim_ref, hpre_ref, hpim_ref,
                w_scr, tk_scr, win_scr, cout_scr, zt_scr, xre_scr, xim_scr, pre_scr, pim_scr,
                cre_scr, cim_scr, *, te, n_tiles):
    i = pl.program_id(1)
    gb = te // SSM_GROUP
    rows = gb * SSM_STATE
    n_seq = sre_ref.shape[0]
    bdims = (((2,), (1,)), ((0,), (0,)))

    @pl.when(i == 0)
    def _():
        w_scr[0:te, :] = wu_ref[...]
        w_scr[te:2 * te, :] = wz_ref[...]

        c1 = c1_ref[...]
        c2 = c2_ref[...]
        cap = [c1 * e1_ref[:, t:t + 1, :] + c2 * e2_ref[:, t:t + 1, :] for t in range(CHUNK + 1)]
        ca = jnp.concatenate(cap[:CHUNK], axis=1)
        cout_scr[...] = jnp.concatenate(cap[1:], axis=1).astype(BF16)

        lane = lax.broadcasted_iota(jnp.int32, (SSM_GROUP, 256), 1)
        row = lax.broadcasted_iota(jnp.int32, (SSM_GROUP, 256), 0)
        diag = row == (lane & (SSM_GROUP - 1))
        tile_c = diag.astype(BF16)
        spread_s = (row == (lane >> 4)).astype(BF16)
        expand = lambda b, m, n_rows: jnp.dot(b, m, preferred_element_type=F32).reshape(
            gb, n_rows, 256)
        bhi, blo = _split_bf16(bb_ref[...].reshape(gb * 2 * SSM_STATE, SSM_GROUP))
        bhi_t = expand(bhi, tile_c, 2 * SSM_STATE)
        blo_t = expand(blo, tile_c, 2 * SSM_STATE)

        phi, plo = _split_bf16(apt_ref[...].reshape(gb * 2 * SSM_STATE, SSM_GROUP))
        ap_t = expand(phi, spread_s, 2 * SSM_STATE) + expand(plo, spread_s, 2 * SSM_STATE)
        b_t = bhi_t + blo_t
        pr, pi = ap_t[:, :SSM_STATE], ap_t[:, SSM_STATE:]
        br, bi = b_t[:, :SSM_STATE], b_t[:, SSM_STATE:]
        win_scr[:, 0:SSM_STATE, :] = (pr * br - pi * bi).astype(BF16)
        win_scr[:, SSM_STATE:, :] = (pr * bi + pi * br).astype(BF16)

        bhi_t, blo_t = bhi_t.astype(BF16), blo_t.astype(BF16)
        ahi, alo = _split_bf16(ca)
        kw = (lax.dot_general(ahi, bhi_t, bdims, preferred_element_type=F32)
              + lax.dot_general(ahi, blo_t, bdims, preferred_element_type=F32)
              + lax.dot_general(alo, bhi_t, bdims, preferred_element_type=F32))
        kw0 = (kw[:, 0:SSM_GROUP, :] + jnp.where(diag, dv_ref[...], 0.0)).astype(BF16)
        kw = kw.astype(BF16)
        tk_scr[...] = jnp.zeros_like(tk_scr)
        for s in range(CHUNK):
            lo = SSM_GROUP * s
            hi = lo + SSM_GROUP
            tk_scr[:, lo:hi, lo:hi] = kw0[:, :, lo:hi]
            if hi < 256:
                tk_scr[:, hi:256, lo:hi] = kw[:, SSM_GROUP:256 - lo, lo:hi]

    nt_dims = (((1,), (1,)), ((), ()))
    half = LANES // 2
    low_half = lax.broadcasted_iota(jnp.int32, (1, LANES), 1) < half

    ar = are_ref[...]
    ai = aim_ref[...]

    def step(hr, hi, n):
        xr = xre_scr[pl.ds(n, 1), :]
        xi = xim_scr[pl.ds(n, 1), :]
        return ar * hr - ai * hi + xr, ar * hi + ai * hr + xi

    def chunk_inputs():
        zt = zt_scr[...]
        yt = lax.dot_general(tk_scr[...], zt, bdims, preferred_element_type=F32)
        xt = lax.dot_general(win_scr[...], zt, bdims, preferred_element_type=F32)
        xre_scr[...] = xt[:, 0:SSM_STATE, :].reshape(rows, LANES).T
        xim_scr[...] = xt[:, SSM_STATE:, :].reshape(rows, LANES).T
        return yt

    def gelu_pieces(yt):
        hp = jnp.concatenate([pre_scr[...].T.reshape(gb, SSM_STATE, LANES),
                              pim_scr[...].T.reshape(gb, SSM_STATE, LANES)], axis=1)
        ycorr = lax.dot_general(cout_scr[...], hp.astype(BF16), bdims,
                                preferred_element_type=F32)
        g = jax.nn.gelu(yt + ycorr)
        return [g[:, SSM_GROUP * t:SSM_GROUP * (t + 1), :].reshape(te, LANES)
                for t in range(CHUNK)]

    @pl.when(i == 0)
    def _():
        uz = lax.dot_general(w_scr[...], xsp_ref[...], nt_dims,
                             preferred_element_type=F32)
        sztsp_ref[...] = jax.nn.silu(uz[te:, :]).astype(BF16)
        for s in range(CHUNK):
            blk = uz[:te, LANES * (s // 2):LANES * (s // 2 + 1)]
            if s % 2:
                blk = pltpu.roll(blk, half, 1)
            zt_scr[:, CHUNK * s:CHUNK * (s + 1), :] = (
                jnp.where(low_half, blk, 0.0).astype(BF16).reshape(gb, SSM_GROUP, LANES))
        yt = chunk_inputs()

        pre_scr[...] = jnp.zeros_like(pre_scr)
        pim_scr[...] = jnp.zeros_like(pim_scr)
        zero = jnp.zeros((1, rows), F32)
        hr, hi = step(zero, zero, 0)
        cre_scr[0:1, :] = hr
        cim_scr[0:1, :] = hi
        for q in range(n_seq):
            hr = sre_ref[q:q + 1, :]
            hi = sim_ref[q:q + 1, :]
            for e in range(2):
                n = 1 + 2 * q + e
                pre_scr[n:n + 1, :] = hr
                pim_scr[n:n + 1, :] = hi
                hr, hi = step(hr, hi, n)
            hsre_ref[q:q + 1, :] = hr
            hsim_ref[q:q + 1, :] = hi

        pieces = gelu_pieces(yt)
        for t in range(0, CHUNK, 2):
            both = jnp.where(low_half, pieces[t], pltpu.roll(pieces[t + 1], half, 1))
            gtsp_ref[:, LANES * (t // 2):LANES * (t // 2 + 1)] = both.astype(BF16)

    @pl.when(i > 0)
    def _():
        uz = lax.dot_general(w_scr[...], xs_ref[...], nt_dims,
                             preferred_element_type=F32)
        szt_ref[...] = jax.nn.silu(uz[te:, :]).astype(BF16)
        ub = uz[:te, :].astype(BF16)
        for s in range(CHUNK):
            zt_scr[:, CHUNK * s:CHUNK * (s + 1), :] = (
                ub[:, LANES * s:LANES * (s + 1)].reshape(gb, SSM_GROUP, LANES))
        yt = chunk_inputs()

        hr, hi = cre_scr[0:1, :], cim_scr[0:1, :]
        for n in range(LANES):
            pre_scr[n:n + 1, :] = hr
            pim_scr[n:n + 1, :] = hi
            hr, hi = step(hr, hi, n)
        cre_scr[0:1, :] = hr
        cim_scr[0:1, :] = hi

        @pl.when(i == n_tiles - 1)
        def _():
            hpre_ref[...] = jnp.broadcast_to(hr, hpre_ref.shape)
            hpim_ref[...] = jnp.broadcast_to(hi, hpim_ref.shape)

        pieces = gelu_pieces(yt)
        for t in range(CHUNK):
            gt_ref[:, LANES * t:LANES * (t + 1)] = pieces[t].astype(BF16)


def _ssm(xs_sp, xs_p, wuz_t, e1, e2, c1, c2, apt, bb, dv, are, aim, sre, sim):
    n_ptiles = xs_p.shape[0]
    n_tiles = 1 + n_ptiles
    te = TE_SSM
    n_j = D_INNER // te
    gb = te // SSM_GROUP
    rows = gb * SSM_STATE
    n_seq = sre.shape[0]
    prev = lambda i: jnp.maximum(i - 1, 0)
    act_shape = jax.ShapeDtypeStruct((n_ptiles, D_INNER, TILE), BF16)
    act_spec = pl.BlockSpec((None, te, TILE), lambda j, i: (prev(i), j, 0))
    sp_shape = jax.ShapeDtypeStruct((1, D_INNER, TILE // 2), BF16)
    sp_spec = pl.BlockSpec((None, te, TILE // 2), lambda j, i: (0, j, 0))
    hs_shape = jax.ShapeDtypeStruct((n_seq, N_STATE_ROWS), F32)
    hs_spec = pl.BlockSpec((n_seq, rows), lambda j, i: (0, j))
    hp_shape = jax.ShapeDtypeStruct((8, N_STATE_ROWS), F32)
    hp_spec = pl.BlockSpec((8, rows), lambda j, i: (0, j))
    row_spec = pl.BlockSpec((1, rows), lambda j, i: (0, j))
    grp_spec = lambda r, c: pl.BlockSpec((gb, r, c), lambda j, i: (j, 0, 0))
    return pl.pallas_call(
        functools.partial(_ssm_kernel, te=te, n_tiles=n_tiles),
        grid=(n_j, n_tiles),
        in_specs=[
            pl.BlockSpec((TILE // 2, D_MODEL), lambda j, i: (0, 0), pipeline_mode=pl.Buffered(1)),
            pl.BlockSpec((None, TILE, D_MODEL), lambda j, i: (prev(i), 0, 0)),
            pl.BlockSpec((te, D_MODEL), lambda j, i: (j, 0)),
            pl.BlockSpec((te, D_MODEL), lambda j, i: (n_j + j, 0)),
            grp_spec(e1.shape[1], 2 * SSM_STATE), grp_spec(e2.shape[1], 2 * SSM_STATE),
            grp_spec(SSM_GROUP, 2 * SSM_STATE), grp_spec(SSM_GROUP, 2 * SSM_STATE),
            grp_spec(2 * SSM_STATE, SSM_GROUP), grp_spec(2 * SSM_STATE, SSM_GROUP),
            grp_spec(SSM_GROUP, 1),
            row_spec, row_spec, hs_spec, hs_spec,
        ],
        out_specs=(sp_spec, sp_spec, act_spec, act_spec, hs_spec, hs_spec, hp_spec, hp_spec),
        out_shape=(sp_shape, sp_shape, act_shape, act_shape, hs_shape, hs_shape,
                   hp_shape, hp_shape),
        scratch_shapes=[pltpu.VMEM((2 * te, D_MODEL), BF16),
                        pltpu.VMEM((gb, 256, 256), BF16),
                        pltpu.VMEM((gb, 2 * SSM_STATE, 256), BF16),
                        pltpu.VMEM((gb, 256, 2 * SSM_STATE), BF16),
                        pltpu.VMEM((gb, 256, LANES), BF16),
                        pltpu.VMEM((LANES, rows), F32), pltpu.VMEM((LANES, rows), F32),
                        pltpu.VMEM((LANES, rows), F32), pltpu.VMEM((LANES, rows), F32),
                        pltpu.VMEM((8, rows), F32), pltpu.VMEM((8, rows), F32)],
        compiler_params=_params(2, 60 * 1024 * 1024),
        name="s5_scan",
    )(xs_sp, xs_p, wuz_t, wuz_t, e1, e2, c1, c2, apt, bb, dv, are, aim, sre, sim)


def _glu_out_kernel(gt_ref, szt_ref, wg_ref, bg_ref, wo_ref, h_ref, nw_ref, o_ref, acc_scr, *,
                    te, lb):
    e = pl.program_id(2)

    @pl.when(e == 0)
    def _():
        acc_scr[...] = jnp.zeros_like(acc_scr)

    gate = jnp.dot(wg_ref[...], gt_ref[...], preferred_element_type=F32) + bg_ref[...]
    ge = gt_ref[pl.ds(pl.multiple_of(e * te, te), te), :].astype(F32)
    y3 = (ge * jax.nn.sigmoid(gate)) * szt_ref[...].astype(F32)
    acc_scr[...] += jnp.dot(wo_ref[...], y3.astype(BF16), preferred_element_type=F32)

    @pl.when(e == pl.num_programs(2) - 1)
    def _():
        o = acc_scr[...].T
        ln = h_ref.shape[1]
        for t in range(lb // ln):
            h = h_ref[t] + o[ln * t:ln * (t + 1), :]
            o_ref[:, D_MODEL * t:D_MODEL * (t + 1)] = _rms(h, nw_ref[...])


def _glu_out(gt, szt, wg_t, b_glu, wo_t, h_all, norm_w):
    n_tiles, _, lanes, _ = h_all.shape
    te = TE_GLU
    lb = LB_GLU
    n_l = CHUNK * lanes // lb
    n_e = D_INNER // te
    tpb = lb // lanes
    return pl.pallas_call(
        functools.partial(_glu_out_kernel, te=te, lb=lb),
        grid=(n_tiles, n_l, n_e),
        in_specs=[
            pl.BlockSpec((None, D_INNER, lb), lambda i, l, e: (i, 0, l)),
            pl.BlockSpec((None, te, lb), lambda i, l, e: (i, e, l)),
            pl.BlockSpec((te, D_INNER), lambda i, l, e: (e, 0)),
            pl.BlockSpec((te, 1), lambda i, l, e: (e, 0)),
            pl.BlockSpec((D_MODEL, te), lambda i, l, e: (0, e)),
            pl.BlockSpec((None, tpb, lanes, D_MODEL), lambda i, l, e: (i, l, 0, 0)),
            pl.BlockSpec((1, D_MODEL), lambda i, l, e: (0, 0)),
        ],
        out_specs=pl.BlockSpec((lanes, tpb * D_MODEL), lambda i, l, e: (i, l)),
        out_shape=jax.ShapeDtypeStruct((n_tiles * lanes, CHUNK * D_MODEL), F32),
        scratch_shapes=[pltpu.VMEM((D_MODEL, lb), F32)],
        compiler_params=_params(3),
        name="glu_out",
    )(gt, szt, wg_t, b_glu, wo_t, h_all, norm_w)


def _to_rows_kernel(o_ref, y_ref, slab_scr):
    nc = o_ref.shape[0]
    n_slabs = D_MODEL // LANES
    for t in range(CHUNK):
        for k in range(n_slabs):
            slab_scr[k, pl.ds(t, nc, stride=CHUNK), :] = (
                o_ref[:, D_MODEL * t + LANES * k:D_MODEL * t + LANES * (k + 1)])
    for k in range(n_slabs):
        y_ref[:, LANES * k:LANES * (k + 1)] = slab_scr[k]


def _to_rows(o):
    n_chunks = o.shape[0]
    tm = TM_ROWS
    nc = tm // CHUNK
    return pl.pallas_call(
        _to_rows_kernel,
        grid=(n_chunks // nc,),
        in_specs=[pl.BlockSpec((nc, CHUNK * D_MODEL), lambda i: (i, 0))],
        out_specs=pl.BlockSpec((None, tm, D_MODEL), lambda i: (0, i, 0)),
        out_shape=jax.ShapeDtypeStruct((1, n_chunks * CHUNK, D_MODEL), F32),
        scratch_shapes=[pltpu.VMEM((D_MODEL // LANES, tm, LANES), F32)],
        compiler_params=_params(1),
        name="to_rows",
    )(o)


def _ssm_operators(a_re, a_im, b_re, b_im, c_re, c_im, d_vec, log_dt):
    a_re, a_im = a_re.astype(F32), a_im.astype(F32)
    dt = jnp.exp(log_dt.astype(F32))[:, None]
    th_re = jnp.concatenate([a_re * dt] * 2, axis=1)[:, None, :]
    th_im = jnp.concatenate([a_im * dt] * 2, axis=1)[:, None, :]
    n_tau = 24
    taus = jnp.arange(n_tau, dtype=F32)[None, :, None]
    used = taus <= CHUNK
    mag = jnp.exp(jnp.where(used, th_re * taus, 0.0))
    ang = jnp.where(used, th_im * taus, 0.0)
    ap_re = mag * jnp.cos(ang)
    ap_im = mag * jnp.sin(ang)
    sign = jnp.concatenate([jnp.ones((SSM_STATE,), F32), -jnp.ones((SSM_STATE,), F32)])
    e1 = ap_re * sign
    e2 = -ap_im
    c_re, c_im = c_re.astype(F32), c_im.astype(F32)
    c1 = jnp.concatenate([c_re, c_im], axis=2)
    c2 = jnp.concatenate([c_im, c_re], axis=2)

    ab_re, ab_im = ap_re[:, 1, :SSM_STATE], ap_im[:, 1, :SSM_STATE]
    nr, ni = ab_re - 1.0, ab_im
    den = a_re * a_re + a_im * a_im
    q_re = ((nr * a_re + ni * a_im) / den)[..., None]
    q_im = ((ni * a_re - nr * a_im) / den)[..., None]
    b_re, b_im = b_re.astype(F32), b_im.astype(F32)
    bb = jnp.concatenate([q_re * b_re - q_im * b_im, q_re * b_im + q_im * b_re], axis=1)

    rev = slice(CHUNK - 1, None, -1)
    apt = jnp.concatenate([ap_re[:, rev, :SSM_STATE].transpose(0, 2, 1),
                           ap_im[:, rev, :SSM_STATE].transpose(0, 2, 1)], axis=1)
    dv = d_vec.astype(F32).reshape(N_GROUPS, SSM_GROUP, 1)
    are = ap_re[:, CHUNK, :SSM_STATE].reshape(1, N_STATE_ROWS)
    aim = ap_im[:, CHUNK, :SSM_STATE].reshape(1, N_STATE_ROWS)
    return e1, e2, c1, c2, apt, bb, dv, are, aim


def kernel(x_prompt, x_sample, cache_conv, state_ssm_re, state_ssm_im, meta_tokens, norm_w,
           final_norm_w, conv_w_in, conv_w, conv_b, conv_w_out, ssm_w_in, ssm_a_re, ssm_a_im,
           ssm_b_re, ssm_b_im, ssm_c_re, ssm_c_im, ssm_d, ssm_log_dt, ssm_w_glu, ssm_b_glu,
           ssm_w_out):
    n_seq, seq_len = x_sample.shape[0], x_sample.shape[1]
    n_prompt_rows = x_prompt.shape[1]
    n_ptiles = n_prompt_rows // TILE
    n_sample_rows = n_seq * seq_len
    assert x_prompt.shape[0] == 1 and seq_len == 2 * CHUNK and N_META == CHUNK
    assert n_prompt_rows % TILE == 0 and N_META + n_sample_rows <= SPECIAL_ROWS

    w_out0 = _cast_bf16(conv_w_out.astype(F32), transpose=False)
    wuz_t = _cast_bf16(ssm_w_in.astype(F32), transpose=True)
    wg_t = _cast_bf16(ssm_w_glu.astype(F32), transpose=True)
    wo_t = _cast_bf16(ssm_w_out.astype(F32), transpose=True)
    w_in0 = conv_w_in.astype(F32)
    conv_w8 = jnp.zeros((8, D_INNER), F32).at[0:3].set(conv_w[0].astype(F32))
    conv_b2 = conv_b[0].astype(F32).reshape(1, D_INNER)
    b_glu = ssm_b_glu[0].astype(F32).reshape(D_INNER, 1)
    nw0 = norm_w[0].astype(F32).reshape(1, D_MODEL)
    nw1 = norm_w[1].astype(F32).reshape(1, D_MODEL)
    nwf = final_norm_w.astype(F32).reshape(1, D_MODEL)
    ssm_ops = _ssm_operators(
        ssm_a_re[0], ssm_a_im[0], ssm_b_re[0], ssm_b_im[0], ssm_c_re[0], ssm_c_im[0],
        ssm_d[0], ssm_log_dt[0])
    sre = state_ssm_re[0].astype(F32).reshape(n_seq, N_STATE_ROWS)
    sim = state_ssm_im[0].astype(F32).reshape(n_seq, N_STATE_ROWS)

    xp = x_prompt.astype(F32)
    x_sp = jnp.concatenate([
        meta_tokens.astype(F32), x_sample.astype(F32).reshape(n_sample_rows, D_MODEL),
        jnp.zeros((SPECIAL_ROWS - N_META - n_sample_rows, D_MODEL), F32)], axis=0)[None]

    zeros8 = jnp.zeros((8, D_INNER), F32)
    xn_sp = _rmsnorm(x_sp, nw0, tm=SPECIAL_ROWS)
    xn_p = _rmsnorm(xp, nw0, tm=TM_CONV)
    sp_real = N_META + n_sample_rows
    y_sp, cv_sp = _conv_proj(xn_sp, w_in0, conv_w8, conv_b2, zeros8,
                             cache_conv[0, :, 0, :].astype(F32), cache_conv[0, :, 1, :].astype(F32),
                             real_rows=sp_real)
    init8 = zeros8.at[6:8].set(cv_sp[N_META - 2:N_META])
    y_p, tail_p = _conv_proj(xn_p, w_in0, conv_w8, conv_b2, init8)
    h1_p, xs_p = _out_proj(y_p, w_out0, xp, nw1, lanes=LANES)
    h1_sp, xs_sp = _out_proj(y_sp, w_out0, x_sp, nw1, lanes=SPECIAL_ROWS // CHUNK,
                             real_rows=sp_real)

    gt_sp, szt_sp, gt_p, szt_p, hs_re, hs_im, hp_re, hp_im = _ssm(
        xs_sp.reshape(SPECIAL_ROWS, D_MODEL), xs_p.reshape(n_ptiles, TILE, D_MODEL),
        wuz_t, *ssm_ops, sre, sim)
    o_sp = _glu_out(gt_sp, szt_sp, wg_t, b_glu, wo_t, h1_sp, nwf)
    o_p = _glu_out(gt_p, szt_p, wg_t, b_glu, wo_t, h1_p, nwf)

    y_prompt = _to_rows(o_p)
    n_sc = n_sample_rows // CHUNK
    y_sample = o_sp[1:1 + n_sc].reshape(n_seq, seq_len, D_MODEL)
    new_conv_prompt = tail_p[-1, 6:8].reshape(1, 1, 2, D_INNER)
    cv_s = cv_sp[N_META:N_META + n_sample_rows].reshape(n_seq, seq_len, D_INNER)
    new_conv_sample = cv_s[:, seq_len - 2:].reshape(1, n_seq, 2, D_INNER)
    p_shape = (1, 1, N_GROUPS, SSM_STATE)
    s_shape = (1, n_seq, N_GROUPS, SSM_STATE)
    return (y_prompt, y_sample, new_conv_prompt, new_conv_sample,
            hp_re[0].reshape(p_shape), hp_im[0].reshape(p_shape),
            hs_re.reshape(s_shape), hs_im.reshape(s_shape))
```

```python
import functools

import jax
import jax.numpy as jnp
from jax import lax
from jax.experimental import pallas as pl
from jax.experimental.pallas import tpu as pltpu

F32 = jnp.float32
BF16 = jnp.bfloat16

D_MODEL = 2048
D_INNER = 4096
N_META = 16
SSM_GROUP = 16
N_GROUPS = D_INNER // SSM_GROUP
SSM_STATE = 64
N_STATE_ROWS = N_GROUPS * SSM_STATE
RMS_EPS = 1e-6

CHUNK = 16
LANES = 128
TILE = CHUNK * LANES
SPECIAL_ROWS = 1024

TM_CONV = 1024
TE_CONV = 512
TM_OUT = 512
TE_SSM = 256
TE_GLU = 512
LB_GLU = 512
TM_ROWS = 1024
PREP_BLOCK = 1024

VMEM_LIMIT = 56 * 1024 * 1024


def _params(n_axes, vmem=VMEM_LIMIT):
    return pltpu.CompilerParams(dimension_semantics=("arbitrary",) * n_axes,
                                vmem_limit_bytes=vmem)


def _rms(x, w):
    ms = jnp.mean(x * x, axis=-1, keepdims=True)
    return x * lax.rsqrt(ms + RMS_EPS) * w


def _cast_kernel(x_ref, o_ref, *, transpose):
    x = x_ref[...]
    o_ref[...] = (x.T if transpose else x).astype(BF16)


def _cast_bf16(w, *, transpose):
    _, r, c = w.shape
    b = PREP_BLOCK
    out_shape = (c, r) if transpose else (r, c)
    out_map = (lambda i, j: (j, i)) if transpose else (lambda i, j: (i, j))
    return pl.pallas_call(
        functools.partial(_cast_kernel, transpose=transpose),
        grid=(r // b, c // b),
        in_specs=[pl.BlockSpec((None, b, b), lambda i, j: (0, i, j))],
        out_specs=pl.BlockSpec((b, b), out_map),
        out_shape=jax.ShapeDtypeStruct(out_shape, BF16),
        compiler_params=_params(2),
        name="cast_t" if transpose else "cast",
    )(w)


def _rmsnorm_kernel(x_ref, w_ref, o_ref):
    o_ref[...] = _rms(x_ref[...], w_ref[...]).astype(BF16)


def _rmsnorm(x, w, *, tm):
    rows = x.shape[1]
    return pl.pallas_call(
        _rmsnorm_kernel, grid=(rows // tm,),
        in_specs=[pl.BlockSpec((None, tm, D_MODEL), lambda i: (0, i, 0)),
                  pl.BlockSpec((1, D_MODEL), lambda i: (0, 0))],
        out_specs=pl.BlockSpec((tm, D_MODEL), lambda i: (i, 0)),
        out_shape=jax.ShapeDtypeStruct((rows, D_MODEL), BF16),
        compiler_params=_params(1), name="rmsnorm",
    )(x, w)


def _conv_proj_kernel(*refs, tm, te, n_pieces, special, real_rows):
    if special:
        (xn_ref, wb_ref, wc_ref, wv_ref, wz_ref, cw_ref, cb_ref, init_ref, c0_ref, c1_ref,
         y_ref, cv_ref, w_scr, s_scr, p1_scr, p2_scr) = refs
    else:
        (xn_ref, wb_ref, wc_ref, wv_ref, wz_ref, cw_ref, cb_ref, init_ref,
         y_ref, tail_ref, w_scr, s_scr) = refs
    i = pl.program_id(1)

    @pl.when(i == 0)
    def _():
        for p, w_ref in enumerate((wb_ref, wc_ref, wv_ref, wz_ref)):
            w_scr[:, p * te:(p + 1) * te] = w_ref[...].astype(BF16)
        s_scr[0:8, :] = init_ref[...]

    if special:
        pm = real_rows
        cv_ref[pm:tm, :] = jnp.zeros((tm - pm, te), F32)
        y_ref[pm:tm, :] = jnp.zeros((tm - pm, te), BF16)
    else:
        pm = tm // n_pieces
    for r in range(n_pieces):
        lo = r * pm
        proj = jnp.dot(xn_ref[lo:lo + pm, :], w_scr[...], preferred_element_type=F32)
        bg = proj[:, 0 * te:1 * te]
        cg = proj[:, 1 * te:2 * te]
        vv = proj[:, 2 * te:3 * te]
        zz = proj[:, 3 * te:4 * te]
        cv = cg * vv
        s_scr[8 + lo:8 + lo + pm, :] = cv
        if special:
            cv_ref[lo:lo + pm, :] = cv
            p1_scr[...] = s_scr[7 + lo:7 + lo + pm, :]
            p2_scr[...] = s_scr[6 + lo:6 + lo + pm, :]
            for q in range(c0_ref.shape[0]):
                r0 = N_META + 32 * q
                p1_scr[r0:r0 + 1, :] = c1_ref[q:q + 1, :]
                p2_scr[r0:r0 + 1, :] = c0_ref[q:q + 1, :]
                p2_scr[r0 + 1:r0 + 2, :] = c1_ref[q:q + 1, :]
            p1 = p1_scr[...]
            p2 = p2_scr[...]
        else:
            p1 = s_scr[7 + lo:7 + lo + pm, :]
            p2 = s_scr[6 + lo:6 + lo + pm, :]
        conv = cb_ref[...] + cw_ref[0:1, :] * p2
        conv = conv + cw_ref[1:2, :] * p1
        conv = conv + cw_ref[2:3, :] * cv
        y_ref[lo:lo + pm, :] = (bg * conv * jax.nn.silu(zz)).astype(BF16)

    if not special:
        tail = s_scr[tm:tm + 8, :]
        s_scr[0:8, :] = tail
        tail_ref[...] = tail


def _conv_proj(xn, w_in, conv_w8, conv_b, init8, cache0=None, cache1=None, real_rows=None):
    rows = xn.shape[0]
    tm, te = TM_CONV, TE_CONV
    n_j = D_INNER // te
    n_i = rows // tm
    special = cache0 is not None
    w_spec = lambda p: pl.BlockSpec((None, D_MODEL, te),
                                    lambda j, i, p=p: (0, 0, p * n_j + j))
    in_specs = [pl.BlockSpec((tm, D_MODEL), lambda j, i: (i, 0)),
                w_spec(0), w_spec(1), w_spec(2), w_spec(3),
                pl.BlockSpec((8, te), lambda j, i: (0, j)),
                pl.BlockSpec((1, te), lambda j, i: (0, j)),
                pl.BlockSpec((8, te), lambda j, i: (0, j))]
    args = [xn, w_in, w_in, w_in, w_in, conv_w8, conv_b, init8]
    scratch = [pltpu.VMEM((D_MODEL, 4 * te), BF16), pltpu.VMEM((tm + 8, te), F32)]
    y_shape = jax.ShapeDtypeStruct((rows, D_INNER), BF16)
    y_spec = pl.BlockSpec((tm, te), lambda j, i: (i, j))
    if special:
        assert n_i == 1
        nq = cache0.shape[0]
        in_specs += [pl.BlockSpec((nq, te), lambda j, i: (0, j)),
                     pl.BlockSpec((nq, te), lambda j, i: (0, j))]
        args += [cache0, cache1]
        out_shape = (y_shape, jax.ShapeDtypeStruct((rows, D_INNER), F32))
        out_specs = (y_spec, pl.BlockSpec((tm, te), lambda j, i: (i, j)))
        scratch += [pltpu.VMEM((real_rows, te), F32), pltpu.VMEM((real_rows, te), F32)]
    else:
        out_shape = (y_shape, jax.ShapeDtypeStruct((n_i, 8, D_INNER), F32))
        out_specs = (y_spec, pl.BlockSpec((None, 8, te), lambda j, i: (i, 0, j)))
    return pl.pallas_call(
        functools.partial(_conv_proj_kernel, tm=tm, te=te, n_pieces=1 if special else 2,
                          special=special, real_rows=real_rows),
        grid=(n_j, n_i), in_specs=in_specs, out_specs=out_specs, out_shape=out_shape,
        scratch_shapes=scratch, compiler_params=_params(2, 60 * 1024 * 1024),
        name="conv_proj_special" if special else "conv_proj",
    )(*args)


def _out_proj_kernel(y_ref, w_ref, x_ref, nw_ref, h_ref, xs_ref, slab_scr, *, real_rows, n_steps):
    tm = y_ref.shape[0]
    nc = tm // CHUNK
    n_slabs = D_MODEL // LANES

    def block(r):
        if r == 0:
            h = jnp.zeros((tm, D_MODEL), F32)
        else:
            h = x_ref[0:r, :] + jnp.dot(y_ref[0:r, :], w_ref[...], preferred_element_type=F32)
            if r < tm:
                h = jnp.concatenate([h, jnp.zeros((tm - r, D_MODEL), F32)], axis=0)
        for k in range(n_slabs):
            slab_scr[k] = h[:, LANES * k:LANES * (k + 1)]
        for s in range(CHUNK):
            hs = jnp.concatenate(
                [slab_scr[k, pl.ds(s, nc, stride=CHUNK), :] for k in range(n_slabs)], axis=1)
            h_ref[s] = hs
            xs_ref[s] = _rms(hs, nw_ref[...]).astype(BF16)

    if real_rows is None:
        block(tm)
    else:
        i = pl.program_id(0)
        for b in range(n_steps):
            pl.when(i == b)(functools.partial(block, min(max(real_rows - b * tm, 0), tm)))


def _out_proj(y, w_out, x, norm_w, *, lanes, real_rows=None):
    tm = TM_OUT
    nc = tm // CHUNK
    per_tile = lanes // nc
    n_i = x.shape[1] // tm
    tile_map = lambda i: (i // per_tile, 0, i % per_tile, 0)
    tile_shape = (n_i // per_tile, CHUNK, lanes, D_MODEL)
    return pl.pallas_call(
        functools.partial(_out_proj_kernel, real_rows=real_rows, n_steps=n_i),
        grid=(n_i,),
        in_specs=[
            pl.BlockSpec((tm, D_INNER), lambda i: (i, 0)),
            pl.BlockSpec((D_INNER, D_MODEL), lambda i: (0, 0), pipeline_mode=pl.Buffered(1)),
            pl.BlockSpec((None, tm, D_MODEL), lambda i: (0, i, 0)),
            pl.BlockSpec((1, D_MODEL), lambda i: (0, 0)),
        ],
        out_specs=(pl.BlockSpec((None, CHUNK, nc, D_MODEL), tile_map),
                   pl.BlockSpec((None, CHUNK, nc, D_MODEL), tile_map)),
        out_shape=(jax.ShapeDtypeStruct(tile_shape, F32),
                   jax.ShapeDtypeStruct(tile_shape, BF16)),
        scratch_shapes=[pltpu.VMEM((D_MODEL // LANES, tm, LANES), F32)],
        compiler_params=_params(1, 60 * 1024 * 1024),
        name="out_proj",
    )(y, w_out, x, norm_w)


def _split_bf16(x):
    hi = x.astype(BF16)
    return hi, (x - hi.astype(F32)).astype(BF16)


def _ssm_kernel(xsp_ref, xs_ref, wu_ref, wz_ref, e1_ref, e2_ref, c1_ref, c2_ref, apt_ref, bb_ref,
                dv_ref, are_ref, aim_ref, sre_ref, sim_ref,
                gtsp_ref, sztsp_ref, gt_ref, szt_ref, hsre_ref, hsim_ref, hpre_ref, hpim_ref,
                w_scr, tk_scr, win_scr, cout_scr, zt_scr, xre_scr, xim_scr, pre_scr, pim_scr,
                cre_scr, cim_scr, *, te, n_tiles):
    i = pl.program_id(1)
    gb = te // SSM_GROUP
    rows = gb * SSM_STATE
    n_seq = sre_ref.shape[0]
    bdims = (((2,), (1,)), ((0,), (0,)))

    def build_operators():
        w_scr[0:te, :] = wu_ref[...]
        w_scr[te:2 * te, :] = wz_ref[...]

        c1 = c1_ref[...]
        c2 = c2_ref[...]
        cap = [c1 * e1_ref[:, t:t + 1, :] + c2 * e2_ref[:, t:t + 1, :] for t in range(CHUNK + 1)]
        ca = jnp.concatenate(cap[:CHUNK], axis=1)
        cout_scr[...] = jnp.concatenate(cap[1:], axis=1).astype(BF16)

        lane = lax.broadcasted_iota(jnp.int32, (SSM_GROUP, 256), 1)
        row = lax.broadcasted_iota(jnp.int32, (SSM_GROUP, 256), 0)
        diag = row == (lane & (SSM_GROUP - 1))
        tile_c = diag.astype(BF16)
        spread_s = (row == (lane >> 4)).astype(BF16)
        expand = lambda b, m, n_rows: jnp.dot(b, m, preferred_element_type=F32).reshape(
            gb, n_rows, 256)
        bhi, blo = _split_bf16(bb_ref[...].reshape(gb * 2 * SSM_STATE, SSM_GROUP))
        bhi_t = expand(bhi, tile_c, 2 * SSM_STATE)
        blo_t = expand(blo, tile_c, 2 * SSM_STATE)

        phi, plo = _split_bf16(apt_ref[...].reshape(gb * 2 * SSM_STATE, SSM_GROUP))
        ap_t = expand(phi, spread_s, 2 * SSM_STATE) + expand(plo, spread_s, 2 * SSM_STATE)
        b_t = bhi_t + blo_t
        pr, pi = ap_t[:, :SSM_STATE], ap_t[:, SSM_STATE:]
        br, bi = b_t[:, :SSM_STATE], b_t[:, SSM_STATE:]
        win_scr[:, 0:SSM_STATE, :] = (pr * br - pi * bi).astype(BF16)
        win_scr[:, SSM_STATE:, :] = (pr * bi + pi * br).astype(BF16)

        bhi_t, blo_t = bhi_t.astype(BF16), blo_t.astype(BF16)
        ahi, alo = _split_bf16(ca)
        kw = (lax.dot_general(ahi, bhi_t, bdims, preferred_element_type=F32)
              + lax.dot_general(ahi, blo_t, bdims, preferred_element_type=F32)
              + lax.dot_general(alo, bhi_t, bdims, preferred_element_type=F32))
        kw0 = (kw[:, 0:SSM_GROUP, :] + jnp.where(diag, dv_ref[...], 0.0)).astype(BF16)
        kw = kw.astype(BF16)
        tk_scr[...] = jnp.zeros_like(tk_scr)
        for s in range(CHUNK):
            lo = SSM_GROUP * s
            hi = lo + SSM_GROUP
            tk_scr[:, lo:hi, lo:hi] = kw0[:, :, lo:hi]
            if hi < 256:
                tk_scr[:, hi:256, lo:hi] = kw[:, SSM_GROUP:256 - lo, lo:hi]

    nt_dims = (((1,), (1,)), ((), ()))
    half = LANES // 2
    low_half = lax.broadcasted_iota(jnp.int32, (1, LANES), 1) < half

    ar = are_ref[...]
    ai = aim_ref[...]

    def step(hr, hi, n):
        xr = xre_scr[pl.ds(n, 1), :]
        xi = xim_scr[pl.ds(n, 1), :]
        return ar * hr - ai * hi + xr, ar * hi + ai * hr + xi

    def chunk_inputs():
        zt = zt_scr[...]
        yt = lax.dot_general(tk_scr[...], zt, bdims, preferred_element_type=F32)
        xt = lax.dot_general(win_scr[...], zt, bdims, preferred_element_type=F32)
        xre_scr[...] = xt[:, 0:SSM_STATE, :].reshape(rows, LANES).T
        xim_scr[...] = xt[:, SSM_STATE:, :].reshape(rows, LANES).T
        return yt

    def gelu_pieces(yt):
        hp = jnp.concatenate([pre_scr[...].T.reshape(gb, SSM_STATE, LANES),
                              pim_scr[...].T.reshape(gb, SSM_STATE, LANES)], axis=1)
        ycorr = lax.dot_general(cout_scr[...], hp.astype(BF16), bdims,
                                preferred_element_type=F32)
        g = jax.nn.gelu(yt + ycorr)
        return [g[:, SSM_GROUP * t:SSM_GROUP * (t + 1), :].reshape(te, LANES)
                for t in range(CHUNK)]

    @pl.when(i == 0)
    def _():
        build_operators()
        uz = lax.dot_general(w_scr[...], xsp_ref[...], nt_dims,
                             preferred_element_type=F32)
        sztsp_ref[...] = jax.nn.silu(uz[te:, :]).astype(BF16)
        for s in range(CHUNK):
            blk = uz[:te, LANES * (s // 2):LANES * (s // 2 + 1)]
            if s % 2:
                blk = pltpu.roll(blk, half, 1)
            zt_scr[:, CHUNK * s:CHUNK * (s + 1), :] = (
                jnp.where(low_half, blk, 0.0).astype(BF16).reshape(gb, SSM_GROUP, LANES))
        yt = chunk_inputs()

        pre_scr[...] = jnp.zeros_like(pre_scr)
        pim_scr[...] = jnp.zeros_like(pim_scr)
        zero = jnp.zeros((1, rows), F32)
        hr, hi = step(zero, zero, 0)
        cre_scr[0:1, :] = hr
        cim_scr[0:1, :] = hi
        for q in range(n_seq):
            hr = sre_ref[q:q + 1, :]
            hi = sim_ref[q:q + 1, :]
            for e in range(2):
                n = 1 + 2 * q + e
                pre_scr[n:n + 1, :] = hr
                pim_scr[n:n + 1, :] = hi
                hr, hi = step(hr, hi, n)
            hsre_ref[q:q + 1, :] = hr
            hsim_ref[q:q + 1, :] = hi

        pieces = gelu_pieces(yt)
        for t in range(0, CHUNK, 2):
            both = jnp.where(low_half, pieces[t], pltpu.roll(pieces[t + 1], half, 1))
            gtsp_ref[:, LANES * (t // 2):LANES * (t // 2 + 1)] = both.astype(BF16)

    @pl.when(i > 0)
    def _():
        uz = lax.dot_general(w_scr[...], xs_ref[...], nt_dims,
                             preferred_element_type=F32)
        szt_ref[...] = jax.nn.silu(uz[te:, :]).astype(BF16)
        ub = uz[:te, :].astype(BF16)
        for s in range(CHUNK):
            zt_scr[:, CHUNK * s:CHUNK * (s + 1), :] = (
                ub[:, LANES * s:LANES * (s + 1)].reshape(gb, SSM_GROUP, LANES))
        yt = chunk_inputs()

        hr, hi = cre_scr[0:1, :], cim_scr[0:1, :]
        for n in range(LANES):
            pre_scr[n:n + 1, :] = hr
            pim_scr[n:n + 1, :] = hi
            hr, hi = step(hr, hi, n)
        cre_scr[0:1, :] = hr
        cim_scr[0:1, :] = hi

        @pl.when(i == n_tiles - 1)
        def _():
            hpre_ref[...] = jnp.broadcast_to(hr, hpre_ref.shape)
            hpim_ref[...] = jnp.broadcast_to(hi, hpim_ref.shape)

        pieces = gelu_pieces(yt)
        for t in range(CHUNK):
            gt_ref[:, LANES * t:LANES * (t + 1)] = pieces[t].astype(BF16)


def _ssm(xs_sp, xs_p, wuz_t, e1, e2, c1, c2, apt, bb, dv, are, aim, sre, sim):
    n_ptiles = xs_p.shape[0]
    n_tiles = 1 + n_ptiles
    te = TE_SSM
    n_j = D_INNER // te
    gb = te // SSM_GROUP
    rows = gb * SSM_STATE
    n_seq = sre.shape[0]
    prev = lambda i: jnp.maximum(i - 1, 0)
    act_shape = jax.ShapeDtypeStruct((n_ptiles, D_INNER, TILE), BF16)
    act_spec = pl.BlockSpec((None, te, TILE), lambda j, i: (prev(i), j, 0))
    sp_shape = jax.ShapeDtypeStruct((1, D_INNER, TILE // 2), BF16)
    sp_spec = pl.BlockSpec((None, te, TILE // 2), lambda j, i: (0, j, 0))
    hs_shape = jax.ShapeDtypeStruct((n_seq, N_STATE_ROWS), F32)
    hs_spec = pl.BlockSpec((n_seq, rows), lambda j, i: (0, j))
    hp_shape = jax.ShapeDtypeStruct((8, N_STATE_ROWS), F32)
    hp_spec = pl.BlockSpec((8, rows), lambda j, i: (0, j))
    row_spec = pl.BlockSpec((1, rows), lambda j, i: (0, j))
    grp_spec = lambda r, c: pl.BlockSpec((gb, r, c), lambda j, i: (j, 0, 0))
    return pl.pallas_call(
        functools.partial(_ssm_kernel, te=te, n_tiles=n_tiles),
        grid=(n_j, n_tiles),
        in_specs=[
            pl.BlockSpec((TILE // 2, D_MODEL), lambda j, i: (0, 0), pipeline_mode=pl.Buffered(1)),
            pl.BlockSpec((None, TILE, D_MODEL), lambda j, i: (prev(i), 0, 0)),
            pl.BlockSpec((te, D_MODEL), lambda j, i: (j, 0)),
            pl.BlockSpec((te, D_MODEL), lambda j, i: (n_j + j, 0)),
            grp_spec(e1.shape[1], 2 * SSM_STATE), grp_spec(e2.shape[1], 2 * SSM_STATE),
            grp_spec(SSM_GROUP, 2 * SSM_STATE), grp_spec(SSM_GROUP, 2 * SSM_STATE),
            grp_spec(2 * SSM_STATE, SSM_GROUP), grp_spec(2 * SSM_STATE, SSM_GROUP),
            grp_spec(SSM_GROUP, 1),
            row_spec, row_spec, hs_spec, hs_spec,
        ],
        out_specs=(sp_spec, sp_spec, act_spec, act_spec, hs_spec, hs_spec, hp_spec, hp_spec),
        out_shape=(sp_shape, sp_shape, act_shape, act_shape, hs_shape, hs_shape,
                   hp_shape, hp_shape),
        scratch_shapes=[pltpu.VMEM((2 * te, D_MODEL), BF16),
                        pltpu.VMEM((gb, 256, 256), BF16),
                        pltpu.VMEM((gb, 2 * SSM_STATE, 256), BF16),
                        pltpu.VMEM((gb, 256, 2 * SSM_STATE), BF16),
                        pltpu.VMEM((gb, 256, LANES), BF16),
                        pltpu.VMEM((LANES, rows), F32), pltpu.VMEM((LANES, rows), F32),
                        pltpu.VMEM((LANES, rows), F32), pltpu.VMEM((LANES, rows), F32),
                        pltpu.VMEM((8, rows), F32), pltpu.VMEM((8, rows), F32)],
        compiler_params=_params(2, 60 * 1024 * 1024),
        name="s5_scan",
    )(xs_sp, xs_p, wuz_t, wuz_t, e1, e2, c1, c2, apt, bb, dv, are, aim, sre, sim)


def _glu_out_kernel(gt_ref, szt_ref, wg_ref, bg_ref, wo_ref, h_ref, nw_ref, o_ref, acc_scr, *,
                    te, lb):
    e = pl.program_id(2)

    @pl.when(e == 0)
    def _():
        acc_scr[...] = jnp.zeros_like(acc_scr)

    gate = jnp.dot(wg_ref[...], gt_ref[...], preferred_element_type=F32) + bg_ref[...]
    ge = gt_ref[pl.ds(pl.multiple_of(e * te, te), te), :].astype(F32)
    y3 = (ge * jax.nn.sigmoid(gate)) * szt_ref[...].astype(F32)
    acc_scr[...] += jnp.dot(wo_ref[...], y3.astype(BF16), preferred_element_type=F32)

    @pl.when(e == pl.num_programs(2) - 1)
    def _():
        o = acc_scr[...].T
        ln = h_ref.shape[1]
        for t in range(lb // ln):
            h = h_ref[t] + o[ln * t:ln * (t + 1), :]
            o_ref[:, D_MODEL * t:D_MODEL * (t + 1)] = _rms(h, nw_ref[...])


def _glu_out(gt, szt, wg_t, b_glu, wo_t, h_all, norm_w):
    n_tiles, _, lanes, _ = h_all.shape
    te = TE_GLU
    lb = LB_GLU
    n_l = CHUNK * lanes // lb
    n_e = D_INNER // te
    tpb = lb // lanes
    return pl.pallas_call(
        functools.partial(_glu_out_kernel, te=te, lb=lb),
        grid=(n_tiles, n_l, n_e),
        in_specs=[
            pl.BlockSpec((None, D_INNER, lb), lambda i, l, e: (i, 0, l)),
            pl.BlockSpec((None, te, lb), lambda i, l, e: (i, e, l)),
            pl.BlockSpec((te, D_INNER), lambda i, l, e: (e, 0)),
            pl.BlockSpec((te, 1), lambda i, l, e: (e, 0)),
            pl.BlockSpec((D_MODEL, te), lambda i, l, e: (0, e)),
            pl.BlockSpec((None, tpb, lanes, D_MODEL), lambda i, l, e: (i, l, 0, 0)),
            pl.BlockSpec((1, D_MODEL), lambda i, l, e: (0, 0)),
        ],
        out_specs=pl.BlockSpec((lanes, tpb * D_MODEL), lambda i, l, e: (i, l)),
        out_shape=jax.ShapeDtypeStruct((n_tiles * lanes, CHUNK * D_MODEL), F32),
        scratch_shapes=[pltpu.VMEM((D_MODEL, lb), F32)],
        compiler_params=_params(3),
        name="glu_out",
    )(gt, szt, wg_t, b_glu, wo_t, h_all, norm_w)


def _to_rows_kernel(o_ref, y_ref, slab_scr):
    nc = o_ref.shape[0]
    n_slabs = D_MODEL // LANES
    for t in range(CHUNK):
        for k in range(n_slabs):
            slab_scr[k, pl.ds(t, nc, stride=CHUNK), :] = (
                o_ref[:, D_MODEL * t + LANES * k:D_MODEL * t + LANES * (k + 1)])
    for k in range(n_slabs):
        y_ref[:, LANES * k:LANES * (k + 1)] = slab_scr[k]


def _to_rows(o):
    n_chunks = o.shape[0]
    tm = TM_ROWS
    nc = tm // CHUNK
    return pl.pallas_call(
        _to_rows_kernel,
        grid=(n_chunks // nc,),
        in_specs=[pl.BlockSpec((nc, CHUNK * D_MODEL), lambda i: (i, 0))],
        out_specs=pl.BlockSpec((None, tm, D_MODEL), lambda i: (0, i, 0)),
        out_shape=jax.ShapeDtypeStruct((1, n_chunks * CHUNK, D_MODEL), F32),
        scratch_shapes=[pltpu.VMEM((D_MODEL // LANES, tm, LANES), F32)],
        compiler_params=_params(1),
        name="to_rows",
    )(o)


def _ssm_operators(a_re, a_im, b_re, b_im, c_re, c_im, d_vec, log_dt):
    a_re, a_im = a_re.astype(F32), a_im.astype(F32)
    dt = jnp.exp(log_dt.astype(F32))[:, None]
    th_re = jnp.concatenate([a_re * dt] * 2, axis=1)[:, None, :]
    th_im = jnp.concatenate([a_im * dt] * 2, axis=1)[:, None, :]
    n_tau = 24
    taus = jnp.arange(n_tau, dtype=F32)[None, :, None]
    used = taus <= CHUNK
    mag = jnp.exp(jnp.where(used, th_re * taus, 0.0))
    ang = jnp.where(used, th_im * taus, 0.0)
    ap_re = mag * jnp.cos(ang)
    ap_im = mag * jnp.sin(ang)
    sign = jnp.concatenate([jnp.ones((SSM_STATE,), F32), -jnp.ones((SSM_STATE,), F32)])
    e1 = ap_re * sign
    e2 = -ap_im
    c_re, c_im = c_re.astype(F32), c_im.astype(F32)
    c1 = jnp.concatenate([c_re, c_im], axis=2)
    c2 = jnp.concatenate([c_im, c_re], axis=2)

    ab_re, ab_im = ap_re[:, 1, :SSM_STATE], ap_im[:, 1, :SSM_STATE]
    nr, ni = ab_re - 1.0, ab_im
    den = a_re * a_re + a_im * a_im
    q_re = ((nr * a_re + ni * a_im) / den)[..., None]
    q_im = ((ni * a_re - nr * a_im) / den)[..., None]
    b_re, b_im = b_re.astype(F32), b_im.astype(F32)
    bb = jnp.concatenate([q_re * b_re - q_im * b_im, q_re * b_im + q_im * b_re], axis=1)

    rev = slice(CHUNK - 1, None, -1)
    apt = jnp.concatenate([ap_re[:, rev, :SSM_STATE].transpose(0, 2, 1),
                           ap_im[:, rev, :SSM_STATE].transpose(0, 2, 1)], axis=1)
    dv = d_vec.astype(F32).reshape(N_GROUPS, SSM_GROUP, 1)
    are = ap_re[:, CHUNK, :SSM_STATE].reshape(1, N_STATE_ROWS)
    aim = ap_im[:, CHUNK, :SSM_STATE].reshape(1, N_STATE_ROWS)
    return e1, e2, c1, c2, apt, bb, dv, are, aim


def kernel(x_prompt, x_sample, cache_conv, state_ssm_re, state_ssm_im, meta_tokens, norm_w,
           final_norm_w, conv_w_in, conv_w, conv_b, conv_w_out, ssm_w_in, ssm_a_re, ssm_a_im,
           ssm_b_re, ssm_b_im, ssm_c_re, ssm_c_im, ssm_d, ssm_log_dt, ssm_w_glu, ssm_b_glu,
           ssm_w_out):
    n_seq, seq_len = x_sample.shape[0], x_sample.shape[1]
    n_prompt_rows = x_prompt.shape[1]
    n_ptiles = n_prompt_rows // TILE
    n_sample_rows = n_seq * seq_len
    assert x_prompt.shape[0] == 1 and seq_len == 2 * CHUNK and N_META == CHUNK
    assert n_prompt_rows % TILE == 0 and N_META + n_sample_rows <= SPECIAL_ROWS

    w_out0 = _cast_bf16(conv_w_out.astype(F32), transpose=False)
    wuz_t = _cast_bf16(ssm_w_in.astype(F32), transpose=True)
    wg_t = _cast_bf16(ssm_w_glu.astype(F32), transpose=True)
    wo_t = _cast_bf16(ssm_w_out.astype(F32), transpose=True)
    w_in0 = conv_w_in.astype(F32)
    conv_w8 = jnp.zeros((8, D_INNER), F32).at[0:3].set(conv_w[0].astype(F32))
    conv_b2 = conv_b[0].astype(F32).reshape(1, D_INNER)
    b_glu = ssm_b_glu[0].astype(F32).reshape(D_INNER, 1)
    nw0 = norm_w[0].astype(F32).reshape(1, D_MODEL)
    nw1 = norm_w[1].astype(F32).reshape(1, D_MODEL)
    nwf = final_norm_w.astype(F32).reshape(1, D_MODEL)
    ssm_ops = _ssm_operators(
        ssm_a_re[0], ssm_a_im[0], ssm_b_re[0], ssm_b_im[0], ssm_c_re[0], ssm_c_im[0],
        ssm_d[0], ssm_log_dt[0])
    sre = state_ssm_re[0].astype(F32).reshape(n_seq, N_STATE_ROWS)
    sim = state_ssm_im[0].astype(F32).reshape(n_seq, N_STATE_ROWS)

    xp = x_prompt.astype(F32)
    x_sp = jnp.concatenate([
        meta_tokens.astype(F32), x_sample.astype(F32).reshape(n_sample_rows, D_MODEL),
        jnp.zeros((SPECIAL_ROWS - N_META - n_sample_rows, D_MODEL), F32)], axis=0)[None]

    zeros8 = jnp.zeros((8, D_INNER), F32)
    xn_sp = _rmsnorm(x_sp, nw0, tm=SPECIAL_ROWS)
    xn_p = _rmsnorm(xp, nw0, tm=TM_CONV)
    sp_real = N_META + n_sample_rows
    y_sp, cv_sp = _conv_proj(xn_sp, w_in0, conv_w8, conv_b2, zeros8,
                             cache_conv[0, :, 0, :].astype(F32), cache_conv[0, :, 1, :].astype(F32),
                             real_rows=sp_real)
    init8 = zeros8.at[6:8].set(cv_sp[N_META - 2:N_META])
    y_p, tail_p = _conv_proj(xn_p, w_in0, conv_w8, conv_b2, init8)
    h1_p, xs_p = _out_proj(y_p, w_out0, xp, nw1, lanes=LANES)
    h1_sp, xs_sp = _out_proj(y_sp, w_out0, x_sp, nw1, lanes=SPECIAL_ROWS // CHUNK,
                             real_rows=sp_real)

    gt_sp, szt_sp, gt_p, szt_p, hs_re, hs_im, hp_re, hp_im = _ssm(
        xs_sp.reshape(SPECIAL_ROWS, D_MODEL), xs_p.reshape(n_ptiles, TILE, D_MODEL),
        wuz_t, *ssm_ops, sre, sim)
    o_sp = _glu_out(gt_sp, szt_sp, wg_t, b_glu, wo_t, h1_sp, nwf)
    o_p = _glu_out(gt_p, szt_p, wg_t, b_glu, wo_t, h1_p, nwf)

    y_prompt = _to_rows(o_p)
    n_sc = n_sample_rows // CHUNK
    y_sample = o_sp[1:1 + n_sc].reshape(n_seq, seq_len, D_MODEL)
    new_conv_prompt = tail_p[-1, 6:8].reshape(1, 1, 2, D_INNER)
    cv_s = cv_sp[N_META:N_META + n_sample_rows].reshape(n_seq, seq_len, D_INNER)
    new_conv_sample = cv_s[:, seq_len - 2:].reshape(1, n_seq, 2, D_INNER)
    p_shape = (1, 1, N_GROUPS, SSM_STATE)
    s_shape = (1, n_seq, N_GROUPS, SSM_STATE)
    return (y_prompt, y_sample, new_conv_prompt, new_conv_sample,
            hp_re[0].reshape(p_shape), hp_im[0].reshape(p_shape),
            hs_re.reshape(s_shape), hs_im.reshape(s_shape))
```

```python
import functools

import jax
import jax.numpy as jnp
from jax import lax
from jax.experimental import pallas as pl
from jax.experimental.pallas import tpu as pltpu

F32 = jnp.float32
BF16 = jnp.bfloat16

D_MODEL = 2048
D_INNER = 4096
N_META = 16
SSM_GROUP = 16
N_GROUPS = D_INNER // SSM_GROUP
SSM_STATE = 64
N_STATE_ROWS = N_GROUPS * SSM_STATE
RMS_EPS = 1e-6

CHUNK = 16
CHUNK_ROWS = CHUNK * SSM_GROUP
LANES = 128
TILE = CHUNK * LANES
SPECIAL_ROWS = 1024

TM_CONV = 1024
TE_CONV = 512
TM_OUT = 512
TE_SSM = 256
TE_GLU = 512
LB_GLU = 512
TM_ROWS = 1024
PREP_BLOCK = 1024

VMEM_LIMIT = 56 * 1024 * 1024
VMEM_LIMIT_LARGE = 60 * 1024 * 1024


def _params(n_axes, vmem=VMEM_LIMIT):
    return pltpu.CompilerParams(dimension_semantics=("arbitrary",) * n_axes,
                                vmem_limit_bytes=vmem)


def _rms(x, w):
    ms = jnp.mean(x * x, axis=-1, keepdims=True)
    return x * lax.rsqrt(ms + RMS_EPS) * w


def _cast_kernel(x_ref, o_ref, *, transpose):
    x = x_ref[...]
    o_ref[...] = (x.T if transpose else x).astype(BF16)


def _cast_bf16(w, *, transpose):
    _, r, c = w.shape
    b = PREP_BLOCK
    out_shape = (c, r) if transpose else (r, c)
    out_map = (lambda i, j: (j, i)) if transpose else (lambda i, j: (i, j))
    return pl.pallas_call(
        functools.partial(_cast_kernel, transpose=transpose),
        grid=(r // b, c // b),
        in_specs=[pl.BlockSpec((None, b, b), lambda i, j: (0, i, j))],
        out_specs=pl.BlockSpec((b, b), out_map),
        out_shape=jax.ShapeDtypeStruct(out_shape, BF16),
        compiler_params=_params(2),
        name="cast_t" if transpose else "cast",
    )(w)


def _rmsnorm_kernel(x_ref, w_ref, o_ref):
    o_ref[...] = _rms(x_ref[...], w_ref[...]).astype(BF16)


def _rmsnorm(x, w, *, tm):
    rows = x.shape[1]
    return pl.pallas_call(
        _rmsnorm_kernel, grid=(rows // tm,),
        in_specs=[pl.BlockSpec((None, tm, D_MODEL), lambda i: (0, i, 0)),
                  pl.BlockSpec((1, D_MODEL), lambda i: (0, 0))],
        out_specs=pl.BlockSpec((tm, D_MODEL), lambda i: (i, 0)),
        out_shape=jax.ShapeDtypeStruct((rows, D_MODEL), BF16),
        compiler_params=_params(1), name="rmsnorm",
    )(x, w)


def _conv_proj_kernel(*refs, tm, te, n_pieces, special, real_rows):
    if special:
        (xn_ref, wb_ref, wc_ref, wv_ref, wz_ref, cw_ref, cb_ref, init_ref, c0_ref, c1_ref,
         y_ref, cv_ref, w_scr, s_scr, p1_scr, p2_scr) = refs
    else:
        (xn_ref, wb_ref, wc_ref, wv_ref, wz_ref, cw_ref, cb_ref, init_ref,
         y_ref, tail_ref, w_scr, s_scr) = refs
    i = pl.program_id(1)

    @pl.when(i == 0)
    def _():
        for p, w_ref in enumerate((wb_ref, wc_ref, wv_ref, wz_ref)):
            w_scr[:, p * te:(p + 1) * te] = w_ref[...].astype(BF16)
        s_scr[0:8, :] = init_ref[...]

    if special:
        pm = real_rows
        cv_ref[pm:tm, :] = jnp.zeros((tm - pm, te), F32)
        y_ref[pm:tm, :] = jnp.zeros((tm - pm, te), BF16)
    else:
        pm = tm // n_pieces
    for r in range(n_pieces):
        lo = r * pm
        proj = jnp.dot(xn_ref[lo:lo + pm, :], w_scr[...], preferred_element_type=F32)
        bg = proj[:, 0 * te:1 * te]
        cg = proj[:, 1 * te:2 * te]
        vv = proj[:, 2 * te:3 * te]
        zz = proj[:, 3 * te:4 * te]
        cv = cg * vv
        s_scr[8 + lo:8 + lo + pm, :] = cv
        if special:
            cv_ref[lo:lo + pm, :] = cv
            p1_scr[...] = s_scr[7 + lo:7 + lo + pm, :]
            p2_scr[...] = s_scr[6 + lo:6 + lo + pm, :]
            for q in range(c0_ref.shape[0]):
                r0 = N_META + 32 * q
                p1_scr[r0:r0 + 1, :] = c1_ref[q:q + 1, :]
                p2_scr[r0:r0 + 1, :] = c0_ref[q:q + 1, :]
                p2_scr[r0 + 1:r0 + 2, :] = c1_ref[q:q + 1, :]
            p1 = p1_scr[...]
            p2 = p2_scr[...]
        else:
            p1 = s_scr[7 + lo:7 + lo + pm, :]
            p2 = s_scr[6 + lo:6 + lo + pm, :]
        conv = cb_ref[...] + cw_ref[0:1, :] * p2
        conv = conv + cw_ref[1:2, :] * p1
        conv = conv + cw_ref[2:3, :] * cv
        y_ref[lo:lo + pm, :] = (bg * conv * jax.nn.silu(zz)).astype(BF16)

    if not special:
        tail = s_scr[tm:tm + 8, :]
        s_scr[0:8, :] = tail
        tail_ref[...] = tail


def _conv_proj(xn, w_in, conv_w8, conv_b, init8, cache0=None, cache1=None, real_rows=None):
    rows = xn.shape[0]
    tm, te = TM_CONV, TE_CONV
    n_j = D_INNER // te
    n_i = rows // tm
    special = cache0 is not None
    w_spec = lambda p: pl.BlockSpec((None, D_MODEL, te),
                                    lambda j, i, p=p: (0, 0, p * n_j + j))
    in_specs = [pl.BlockSpec((tm, D_MODEL), lambda j, i: (i, 0)),
                w_spec(0), w_spec(1), w_spec(2), w_spec(3),
                pl.BlockSpec((8, te), lambda j, i: (0, j)),
                pl.BlockSpec((1, te), lambda j, i: (0, j)),
                pl.BlockSpec((8, te), lambda j, i: (0, j))]
    args = [xn, w_in, w_in, w_in, w_in, conv_w8, conv_b, init8]
    scratch = [pltpu.VMEM((D_MODEL, 4 * te), BF16), pltpu.VMEM((tm + 8, te), F32)]
    y_shape = jax.ShapeDtypeStruct((rows, D_INNER), BF16)
    y_spec = pl.BlockSpec((tm, te), lambda j, i: (i, j))
    if special:
        assert n_i == 1
        nq = cache0.shape[0]
        in_specs += [pl.BlockSpec((nq, te), lambda j, i: (0, j)),
                     pl.BlockSpec((nq, te), lambda j, i: (0, j))]
        args += [cache0, cache1]
        out_shape = (y_shape, jax.ShapeDtypeStruct((rows, D_INNER), F32))
        out_specs = (y_spec, pl.BlockSpec((tm, te), lambda j, i: (i, j)))
        scratch += [pltpu.VMEM((real_rows, te), F32), pltpu.VMEM((real_rows, te), F32)]
    else:
        out_shape = (y_shape, jax.ShapeDtypeStruct((n_i, 8, D_INNER), F32))
        out_specs = (y_spec, pl.BlockSpec((None, 8, te), lambda j, i: (i, 0, j)))
    return pl.pallas_call(
        functools.partial(_conv_proj_kernel, tm=tm, te=te, n_pieces=1 if special else 2,
                          special=special, real_rows=real_rows),
        grid=(n_j, n_i), in_specs=in_specs, out_specs=out_specs, out_shape=out_shape,
        scratch_shapes=scratch, compiler_params=_params(2, VMEM_LIMIT_LARGE),
        name="conv_proj_special" if special else "conv_proj",
    )(*args)


def _out_proj_kernel(y_ref, w_ref, x_ref, nw_ref, h_ref, xs_ref, slab_scr, *, real_rows, n_steps):
    tm = y_ref.shape[0]
    nc = tm // CHUNK
    n_slabs = D_MODEL // LANES

    def block(r):
        if r == 0:
            h = jnp.zeros((tm, D_MODEL), F32)
        else:
            h = x_ref[0:r, :] + jnp.dot(y_ref[0:r, :], w_ref[...], preferred_element_type=F32)
            if r < tm:
                h = jnp.concatenate([h, jnp.zeros((tm - r, D_MODEL), F32)], axis=0)
        for k in range(n_slabs):
            slab_scr[k] = h[:, LANES * k:LANES * (k + 1)]
        for s in range(CHUNK):
            hs = jnp.concatenate(
                [slab_scr[k, pl.ds(s, nc, stride=CHUNK), :] for k in range(n_slabs)], axis=1)
            h_ref[s] = hs
            xs_ref[s] = _rms(hs, nw_ref[...]).astype(BF16)

    if real_rows is None:
        block(tm)
    else:
        i = pl.program_id(0)
        for b in range(n_steps):
            pl.when(i == b)(functools.partial(block, min(max(real_rows - b * tm, 0), tm)))


def _out_proj(y, w_out, x, norm_w, *, lanes, real_rows=None):
    tm = TM_OUT
    nc = tm // CHUNK
    per_tile = lanes // nc
    n_i = x.shape[1] // tm
    tile_map = lambda i: (i // per_tile, 0, i % per_tile, 0)
    tile_shape = (n_i // per_tile, CHUNK, lanes, D_MODEL)
    return pl.pallas_call(
        functools.partial(_out_proj_kernel, real_rows=real_rows, n_steps=n_i),
        grid=(n_i,),
        in_specs=[
            pl.BlockSpec((tm, D_INNER), lambda i: (i, 0)),
            pl.BlockSpec((D_INNER, D_MODEL), lambda i: (0, 0), pipeline_mode=pl.Buffered(1)),
            pl.BlockSpec((None, tm, D_MODEL), lambda i: (0, i, 0)),
            pl.BlockSpec((1, D_MODEL), lambda i: (0, 0)),
        ],
        out_specs=(pl.BlockSpec((None, CHUNK, nc, D_MODEL), tile_map),
                   pl.BlockSpec((None, CHUNK, nc, D_MODEL), tile_map)),
        out_shape=(jax.ShapeDtypeStruct(tile_shape, F32),
                   jax.ShapeDtypeStruct(tile_shape, BF16)),
        scratch_shapes=[pltpu.VMEM((D_MODEL // LANES, tm, LANES), F32)],
        compiler_params=_params(1, VMEM_LIMIT_LARGE),
        name="out_proj",
    )(y, w_out, x, norm_w)


def _split_bf16(x):
    hi = x.astype(BF16)
    return hi, (x - hi.astype(F32)).astype(BF16)


def _ssm_kernel(xsp_ref, xs_ref, wu_ref, wz_ref, e1_ref, e2_ref, c1_ref, c2_ref, apt_ref, bb_ref,
                dv_ref, are_ref, aim_ref, sre_ref, sim_ref,
                gtsp_ref, sztsp_ref, gt_ref, szt_ref, hsre_ref, hsim_ref, hpre_ref, hpim_ref,
                w_scr, tk_scr, win_scr, cout_scr, zt_scr, xre_scr, xim_scr, pre_scr, pim_scr,
                cre_scr, cim_scr, *, te, n_tiles):
    i = pl.program_id(1)
    gb = te // SSM_GROUP
    rows = gb * SSM_STATE
    n_seq = sre_ref.shape[0]
    bdims = (((2,), (1,)), ((0,), (0,)))

    def build_operators():
        w_scr[0:te, :] = wu_ref[...]
        w_scr[te:2 * te, :] = wz_ref[...]

        c1 = c1_ref[...]
        c2 = c2_ref[...]
        cap = [c1 * e1_ref[:, t:t + 1, :] + c2 * e2_ref[:, t:t + 1, :] for t in range(CHUNK + 1)]
        ca = jnp.concatenate(cap[:CHUNK], axis=1)
        cout_scr[...] = jnp.concatenate(cap[1:], axis=1).astype(BF16)

        lane = lax.broadcasted_iota(jnp.int32, (SSM_GROUP, CHUNK_ROWS), 1)
        row = lax.broadcasted_iota(jnp.int32, (SSM_GROUP, CHUNK_ROWS), 0)
        diag = row == (lane & (SSM_GROUP - 1))
        tile_c = diag.astype(BF16)
        spread_s = (row == lane // SSM_GROUP).astype(BF16)
        expand = lambda b, m, n_rows: jnp.dot(b, m, preferred_element_type=F32).reshape(
            gb, n_rows, CHUNK_ROWS)
        bhi, blo = _split_bf16(bb_ref[...].reshape(gb * 2 * SSM_STATE, SSM_GROUP))
        bhi_t = expand(bhi, tile_c, 2 * SSM_STATE)
        blo_t = expand(blo, tile_c, 2 * SSM_STATE)

        phi, plo = _split_bf16(apt_ref[...].reshape(gb * 2 * SSM_STATE, SSM_GROUP))
        ap_t = expand(phi, spread_s, 2 * SSM_STATE) + expand(plo, spread_s, 2 * SSM_STATE)
        b_t = bhi_t + blo_t
        pr, pi = ap_t[:, :SSM_STATE], ap_t[:, SSM_STATE:]
        br, bi = b_t[:, :SSM_STATE], b_t[:, SSM_STATE:]
        win_scr[:, 0:SSM_STATE, :] = (pr * br - pi * bi).astype(BF16)
        win_scr[:, SSM_STATE:, :] = (pr * bi + pi * br).astype(BF16)

        bhi_t, blo_t = bhi_t.astype(BF16), blo_t.astype(BF16)
        ahi, alo = _split_bf16(ca)
        kw = (lax.dot_general(ahi, bhi_t, bdims, preferred_element_type=F32)
              + lax.dot_general(ahi, blo_t, bdims, preferred_element_type=F32)
              + lax.dot_general(alo, bhi_t, bdims, preferred_element_type=F32))
        kw0 = (kw[:, 0:SSM_GROUP, :] + jnp.where(diag, dv_ref[...], 0.0)).astype(BF16)
        kw = kw.astype(BF16)
        tk_scr[...] = jnp.zeros_like(tk_scr)
        for s in range(CHUNK):
            lo = SSM_GROUP * s
            hi = lo + SSM_GROUP
            tk_scr[:, lo:hi, lo:hi] = kw0[:, :, lo:hi]
            if hi < CHUNK_ROWS:
                tk_scr[:, hi:CHUNK_ROWS, lo:hi] = kw[:, SSM_GROUP:CHUNK_ROWS - lo, lo:hi]

    nt_dims = (((1,), (1,)), ((), ()))
    half = LANES // 2
    low_half = lax.broadcasted_iota(jnp.int32, (1, LANES), 1) < half

    ar = are_ref[...]
    ai = aim_ref[...]

    def step(hr, hi, n):
        xr = xre_scr[pl.ds(n, 1), :]
        xi = xim_scr[pl.ds(n, 1), :]
        return ar * hr - ai * hi + xr, ar * hi + ai * hr + xi

    def chunk_inputs():
        zt = zt_scr[...]
        yt = lax.dot_general(tk_scr[...], zt, bdims, preferred_element_type=F32)
        xt = lax.dot_general(win_scr[...], zt, bdims, preferred_element_type=F32)
        xre_scr[...] = xt[:, 0:SSM_STATE, :].reshape(rows, LANES).T
        xim_scr[...] = xt[:, SSM_STATE:, :].reshape(rows, LANES).T
        return yt

    def gelu_pieces(yt):
        hp = jnp.concatenate([pre_scr[...].T.reshape(gb, SSM_STATE, LANES),
                              pim_scr[...].T.reshape(gb, SSM_STATE, LANES)], axis=1)
        ycorr = lax.dot_general(cout_scr[...], hp.astype(BF16), bdims,
                                preferred_element_type=F32)
        g = jax.nn.gelu(yt + ycorr)
        return [g[:, SSM_GROUP * t:SSM_GROUP * (t + 1), :].reshape(te, LANES)
                for t in range(CHUNK)]

    @pl.when(i == 0)
    def _():
        build_operators()
        uz = lax.dot_general(w_scr[...], xsp_ref[...], nt_dims,
                             preferred_element_type=F32)
        sztsp_ref[...] = jax.nn.silu(uz[te:, :]).astype(BF16)
        for s in range(CHUNK):
            blk = uz[:te, LANES * (s // 2):LANES * (s // 2 + 1)]
            if s % 2:
                blk = pltpu.roll(blk, half, 1)
            zt_scr[:, CHUNK * s:CHUNK * (s + 1), :] = (
                jnp.where(low_half, blk, 0.0).astype(BF16).reshape(gb, SSM_GROUP, LANES))
        yt = chunk_inputs()

        pre_scr[...] = jnp.zeros_like(pre_scr)
        pim_scr[...] = jnp.zeros_like(pim_scr)
        zero = jnp.zeros((1, rows), F32)
        hr, hi = step(zero, zero, 0)
        cre_scr[0:1, :] = hr
        cim_scr[0:1, :] = hi
        for q in range(n_seq):
            hr = sre_ref[q:q + 1, :]
            hi = sim_ref[q:q + 1, :]
            for e in range(2):
                n = 1 + 2 * q + e
                pre_scr[n:n + 1, :] = hr
                pim_scr[n:n + 1, :] = hi
                hr, hi = step(hr, hi, n)
            hsre_ref[q:q + 1, :] = hr
            hsim_ref[q:q + 1, :] = hi

        pieces = gelu_pieces(yt)
        for t in range(0, CHUNK, 2):
            both = jnp.where(low_half, pieces[t], pltpu.roll(pieces[t + 1], half, 1))
            gtsp_ref[:, LANES * (t // 2):LANES * (t // 2 + 1)] = both.astype(BF16)

    @pl.when(i > 0)
    def _():
        uz = lax.dot_general(w_scr[...], xs_ref[...], nt_dims,
                             preferred_element_type=F32)
        szt_ref[...] = jax.nn.silu(uz[te:, :]).astype(BF16)
        ub = uz[:te, :].astype(BF16)
        for s in range(CHUNK):
            zt_scr[:, CHUNK * s:CHUNK * (s + 1), :] = (
                ub[:, LANES * s:LANES * (s + 1)].reshape(gb, SSM_GROUP, LANES))
        yt = chunk_inputs()

        hr, hi = cre_scr[0:1, :], cim_scr[0:1, :]
        for n in range(LANES):
            pre_scr[n:n + 1, :] = hr
            pim_scr[n:n + 1, :] = hi
            hr, hi = step(hr, hi, n)
        cre_scr[0:1, :] = hr
        cim_scr[0:1, :] = hi

        @pl.when(i == n_tiles - 1)
        def _():
            hpre_ref[...] = jnp.broadcast_to(hr, hpre_ref.shape)
            hpim_ref[...] = jnp.broadcast_to(hi, hpim_ref.shape)

        pieces = gelu_pieces(yt)
        for t in range(CHUNK):
            gt_ref[:, LANES * t:LANES * (t + 1)] = pieces[t].astype(BF16)


def _ssm(xs_sp, xs_p, wuz_t, e1, e2, c1, c2, apt, bb, dv, are, aim, sre, sim):
    n_ptiles = xs_p.shape[0]
    n_tiles = 1 + n_ptiles
    te = TE_SSM
    n_j = D_INNER // te
    gb = te // SSM_GROUP
    rows = gb * SSM_STATE
    n_seq = sre.shape[0]
    prev = lambda i: jnp.maximum(i - 1, 0)
    act_shape = jax.ShapeDtypeStruct((n_ptiles, D_INNER, TILE), BF16)
    act_spec = pl.BlockSpec((None, te, TILE), lambda j, i: (prev(i), j, 0))
    sp_shape = jax.ShapeDtypeStruct((1, D_INNER, TILE // 2), BF16)
    sp_spec = pl.BlockSpec((None, te, TILE // 2), lambda j, i: (0, j, 0))
    hs_shape = jax.ShapeDtypeStruct((n_seq, N_STATE_ROWS), F32)
    hs_spec = pl.BlockSpec((n_seq, rows), lambda j, i: (0, j))
    hp_shape = jax.ShapeDtypeStruct((8, N_STATE_ROWS), F32)
    hp_spec = pl.BlockSpec((8, rows), lambda j, i: (0, j))
    row_spec = pl.BlockSpec((1, rows), lambda j, i: (0, j))
    grp_spec = lambda r, c: pl.BlockSpec((gb, r, c), lambda j, i: (j, 0, 0))
    return pl.pallas_call(
        functools.partial(_ssm_kernel, te=te, n_tiles=n_tiles),
        grid=(n_j, n_tiles),
        in_specs=[
            pl.BlockSpec((TILE // 2, D_MODEL), lambda j, i: (0, 0), pipeline_mode=pl.Buffered(1)),
            pl.BlockSpec((None, TILE, D_MODEL), lambda j, i: (prev(i), 0, 0)),
            pl.BlockSpec((te, D_MODEL), lambda j, i: (j, 0)),
            pl.BlockSpec((te, D_MODEL), lambda j, i: (n_j + j, 0)),
            grp_spec(e1.shape[1], 2 * SSM_STATE), grp_spec(e2.shape[1], 2 * SSM_STATE),
            grp_spec(SSM_GROUP, 2 * SSM_STATE), grp_spec(SSM_GROUP, 2 * SSM_STATE),
            grp_spec(2 * SSM_STATE, SSM_GROUP), grp_spec(2 * SSM_STATE, SSM_GROUP),
            grp_spec(SSM_GROUP, 1),
            row_spec, row_spec, hs_spec, hs_spec,
        ],
        out_specs=(sp_spec, sp_spec, act_spec, act_spec, hs_spec, hs_spec, hp_spec, hp_spec),
        out_shape=(sp_shape, sp_shape, act_shape, act_shape, hs_shape, hs_shape,
                   hp_shape, hp_shape),
        scratch_shapes=[pltpu.VMEM((2 * te, D_MODEL), BF16),
                        pltpu.VMEM((gb, CHUNK_ROWS, CHUNK_ROWS), BF16),
                        pltpu.VMEM((gb, 2 * SSM_STATE, CHUNK_ROWS), BF16),
                        pltpu.VMEM((gb, CHUNK_ROWS, 2 * SSM_STATE), BF16),
                        pltpu.VMEM((gb, CHUNK_ROWS, LANES), BF16),
                        pltpu.VMEM((LANES, rows), F32), pltpu.VMEM((LANES, rows), F32),
                        pltpu.VMEM((LANES, rows), F32), pltpu.VMEM((LANES, rows), F32),
                        pltpu.VMEM((8, rows), F32), pltpu.VMEM((8, rows), F32)],
        compiler_params=_params(2, VMEM_LIMIT_LARGE),
        name="s5_scan",
    )(xs_sp, xs_p, wuz_t, wuz_t, e1, e2, c1, c2, apt, bb, dv, are, aim, sre, sim)


def _glu_out_kernel(gt_ref, szt_ref, wg_ref, bg_ref, wo_ref, h_ref, nw_ref, o_ref, acc_scr, *,
                    te, lb):
    e = pl.program_id(2)

    @pl.when(e == 0)
    def _():
        acc_scr[...] = jnp.zeros_like(acc_scr)

    gate = jnp.dot(wg_ref[...], gt_ref[...], preferred_element_type=F32) + bg_ref[...]
    ge = gt_ref[pl.ds(pl.multiple_of(e * te, te), te), :].astype(F32)
    y3 = (ge * jax.nn.sigmoid(gate)) * szt_ref[...].astype(F32)
    acc_scr[...] += jnp.dot(wo_ref[...], y3.astype(BF16), preferred_element_type=F32)

    @pl.when(e == pl.num_programs(2) - 1)
    def _():
        o = acc_scr[...].T
        ln = h_ref.shape[1]
        for t in range(lb // ln):
            h = h_ref[t] + o[ln * t:ln * (t + 1), :]
            o_ref[:, D_MODEL * t:D_MODEL * (t + 1)] = _rms(h, nw_ref[...])


def _glu_out(gt, szt, wg_t, b_glu, wo_t, h_all, norm_w):
    n_tiles, _, lanes, _ = h_all.shape
    te = TE_GLU
    lb = LB_GLU
    n_l = CHUNK * lanes // lb
    n_e = D_INNER // te
    tpb = lb // lanes
    return pl.pallas_call(
        functools.partial(_glu_out_kernel, te=te, lb=lb),
        grid=(n_tiles, n_l, n_e),
        in_specs=[
            pl.BlockSpec((None, D_INNER, lb), lambda i, l, e: (i, 0, l)),
            pl.BlockSpec((None, te, lb), lambda i, l, e: (i, e, l)),
            pl.BlockSpec((te, D_INNER), lambda i, l, e: (e, 0)),
            pl.BlockSpec((te, 1), lambda i, l, e: (e, 0)),
            pl.BlockSpec((D_MODEL, te), lambda i, l, e: (0, e)),
            pl.BlockSpec((None, tpb, lanes, D_MODEL), lambda i, l, e: (i, l, 0, 0)),
            pl.BlockSpec((1, D_MODEL), lambda i, l, e: (0, 0)),
        ],
        out_specs=pl.BlockSpec((lanes, tpb * D_MODEL), lambda i, l, e: (i, l)),
        out_shape=jax.ShapeDtypeStruct((n_tiles * lanes, CHUNK * D_MODEL), F32),
        scratch_shapes=[pltpu.VMEM((D_MODEL, lb), F32)],
        compiler_params=_params(3),
        name="glu_out",
    )(gt, szt, wg_t, b_glu, wo_t, h_all, norm_w)


def _to_rows_kernel(o_ref, y_ref, slab_scr):
    nc = o_ref.shape[0]
    n_slabs = D_MODEL // LANES
    for t in range(CHUNK):
        for k in range(n_slabs):
            slab_scr[k, pl.ds(t, nc, stride=CHUNK), :] = (
                o_ref[:, D_MODEL * t + LANES * k:D_MODEL * t + LANES * (k + 1)])
    for k in range(n_slabs):
        y_ref[:, LANES * k:LANES * (k + 1)] = slab_scr[k]


def _to_rows(o):
    n_chunks = o.shape[0]
    tm = TM_ROWS
    nc = tm // CHUNK
    return pl.pallas_call(
        _to_rows_kernel,
        grid=(n_chunks // nc,),
        in_specs=[pl.BlockSpec((nc, CHUNK * D_MODEL), lambda i: (i, 0))],
        out_specs=pl.BlockSpec((None, tm, D_MODEL), lambda i: (0, i, 0)),
        out_shape=jax.ShapeDtypeStruct((1, n_chunks * CHUNK, D_MODEL), F32),
        scratch_shapes=[pltpu.VMEM((D_MODEL // LANES, tm, LANES), F32)],
        compiler_params=_params(1),
        name="to_rows",
    )(o)


def _ssm_operators(a_re, a_im, b_re, b_im, c_re, c_im, d_vec, log_dt):
    a_re, a_im = a_re.astype(F32), a_im.astype(F32)
    dt = jnp.exp(log_dt.astype(F32))[:, None]
    th_re = jnp.concatenate([a_re * dt] * 2, axis=1)[:, None, :]
    th_im = jnp.concatenate([a_im * dt] * 2, axis=1)[:, None, :]
    n_tau = 24
    taus = jnp.arange(n_tau, dtype=F32)[None, :, None]
    used = taus <= CHUNK
    mag = jnp.exp(jnp.where(used, th_re * taus, 0.0))
    ang = jnp.where(used, th_im * taus, 0.0)
    ap_re = mag * jnp.cos(ang)
    ap_im = mag * jnp.sin(ang)
    sign = jnp.concatenate([jnp.ones((SSM_STATE,), F32), -jnp.ones((SSM_STATE,), F32)])
    e1 = ap_re * sign
    e2 = -ap_im
    c_re, c_im = c_re.astype(F32), c_im.astype(F32)
    c1 = jnp.concatenate([c_re, c_im], axis=2)
    c2 = jnp.concatenate([c_im, c_re], axis=2)

    ab_re, ab_im = ap_re[:, 1, :SSM_STATE], ap_im[:, 1, :SSM_STATE]
    nr, ni = ab_re - 1.0, ab_im
    den = a_re * a_re + a_im * a_im
    q_re = ((nr * a_re + ni * a_im) / den)[..., None]
    q_im = ((ni * a_re - nr * a_im) / den)[..., None]
    b_re, b_im = b_re.astype(F32), b_im.astype(F32)
    bb = jnp.concatenate([q_re * b_re - q_im * b_im, q_re * b_im + q_im * b_re], axis=1)

    rev = slice(CHUNK - 1, None, -1)
    apt = jnp.concatenate([ap_re[:, rev, :SSM_STATE].transpose(0, 2, 1),
                           ap_im[:, rev, :SSM_STATE].transpose(0, 2, 1)], axis=1)
    dv = d_vec.astype(F32).reshape(N_GROUPS, SSM_GROUP, 1)
    are = ap_re[:, CHUNK, :SSM_STATE].reshape(1, N_STATE_ROWS)
    aim = ap_im[:, CHUNK, :SSM_STATE].reshape(1, N_STATE_ROWS)
    return e1, e2, c1, c2, apt, bb, dv, are, aim


def kernel(x_prompt, x_sample, cache_conv, state_ssm_re, state_ssm_im, meta_tokens, norm_w,
           final_norm_w, conv_w_in, conv_w, conv_b, conv_w_out, ssm_w_in, ssm_a_re, ssm_a_im,
           ssm_b_re, ssm_b_im, ssm_c_re, ssm_c_im, ssm_d, ssm_log_dt, ssm_w_glu, ssm_b_glu,
           ssm_w_out):
    n_seq, seq_len = x_sample.shape[0], x_sample.shape[1]
    n_prompt_rows = x_prompt.shape[1]
    n_ptiles = n_prompt_rows // TILE
    n_sample_rows = n_seq * seq_len
    assert x_prompt.shape[0] == 1 and seq_len == 2 * CHUNK and N_META == CHUNK
    assert n_prompt_rows % TILE == 0 and N_META + n_sample_rows <= SPECIAL_ROWS

    w_out0 = _cast_bf16(conv_w_out.astype(F32), transpose=False)
    wuz_t = _cast_bf16(ssm_w_in.astype(F32), transpose=True)
    wg_t = _cast_bf16(ssm_w_glu.astype(F32), transpose=True)
    wo_t = _cast_bf16(ssm_w_out.astype(F32), transpose=True)
    w_in0 = conv_w_in.astype(F32)
    conv_w8 = jnp.zeros((8, D_INNER), F32).at[0:3].set(conv_w[0].astype(F32))
    conv_b2 = conv_b[0].astype(F32).reshape(1, D_INNER)
    b_glu = ssm_b_glu[0].astype(F32).reshape(D_INNER, 1)
    nw0 = norm_w[0].astype(F32).reshape(1, D_MODEL)
    nw1 = norm_w[1].astype(F32).reshape(1, D_MODEL)
    nwf = final_norm_w.astype(F32).reshape(1, D_MODEL)
    ssm_ops = _ssm_operators(
        ssm_a_re[0], ssm_a_im[0], ssm_b_re[0], ssm_b_im[0], ssm_c_re[0], ssm_c_im[0],
        ssm_d[0], ssm_log_dt[0])
    sre = state_ssm_re[0].astype(F32).reshape(n_seq, N_STATE_ROWS)
    sim = state_ssm_im[0].astype(F32).reshape(n_seq, N_STATE_ROWS)

    xp = x_prompt.astype(F32)
    x_sp = jnp.concatenate([
        meta_tokens.astype(F32), x_sample.astype(F32).reshape(n_sample_rows, D_MODEL),
        jnp.zeros((SPECIAL_ROWS - N_META - n_sample_rows, D_MODEL), F32)], axis=0)[None]

    zeros8 = jnp.zeros((8, D_INNER), F32)
    xn_sp = _rmsnorm(x_sp, nw0, tm=SPECIAL_ROWS)
    xn_p = _rmsnorm(xp, nw0, tm=TM_CONV)
    sp_real = N_META + n_sample_rows
    y_sp, cv_sp = _conv_proj(xn_sp, w_in0, conv_w8, conv_b2, zeros8,
                             cache_conv[0, :, 0, :].astype(F32), cache_conv[0, :, 1, :].astype(F32),
                             real_rows=sp_real)
    init8 = zeros8.at[6:8].set(cv_sp[N_META - 2:N_META])
    y_p, tail_p = _conv_proj(xn_p, w_in0, conv_w8, conv_b2, init8)
    h1_p, xs_p = _out_proj(y_p, w_out0, xp, nw1, lanes=LANES)
    h1_sp, xs_sp = _out_proj(y_sp, w_out0, x_sp, nw1, lanes=SPECIAL_ROWS // CHUNK,
                             real_rows=sp_real)

    gt_sp, szt_sp, gt_p, szt_p, hs_re, hs_im, hp_re, hp_im = _ssm(
        xs_sp.reshape(SPECIAL_ROWS, D_MODEL), xs_p.reshape(n_ptiles, TILE, D_MODEL),
        wuz_t, *ssm_ops, sre, sim)
    o_sp = _glu_out(gt_sp, szt_sp, wg_t, b_glu, wo_t, h1_sp, nwf)
    o_p = _glu_out(gt_p, szt_p, wg_t, b_glu, wo_t, h1_p, nwf)

    y_prompt = _to_rows(o_p)
    n_sc = n_sample_rows // CHUNK
    y_sample = o_sp[1:1 + n_sc].reshape(n_seq, seq_len, D_MODEL)
    new_conv_prompt = tail_p[-1, 6:8].reshape(1, 1, 2, D_INNER)
    cv_s = cv_sp[N_META:N_META + n_sample_rows].reshape(n_seq, seq_len, D_INNER)
    new_conv_sample = cv_s[:, seq_len - 2:].reshape(1, n_seq, 2, D_INNER)
    p_shape = (1, 1, N_GROUPS, SSM_STATE)
    s_shape = (1, n_seq, N_GROUPS, SSM_STATE)
    return (y_prompt, y_sample, new_conv_prompt, new_conv_sample,
            hp_re[0].reshape(p_shape), hp_im[0].reshape(p_shape),
            hs_re.reshape(s_shape), hs_im.reshape(s_shape))
```

```python
import functools

import jax
import jax.numpy as jnp
from jax import lax
from jax.experimental import pallas as pl
from jax.experimental.pallas import tpu as pltpu

F32 = jnp.float32
BF16 = jnp.bfloat16

D_MODEL = 2048
D_INNER = 4096
N_META = 16
SSM_GROUP = 16
N_GROUPS = D_INNER // SSM_GROUP
SSM_STATE = 64
N_STATE_ROWS = N_GROUPS * SSM_STATE
RMS_EPS = 1e-6

CHUNK = 16
CHUNK_ROWS = CHUNK * SSM_GROUP
LANES = 128
TILE = CHUNK * LANES
SPECIAL_ROWS = 1024

TM_CONV = 1024
TE_CONV = 512
TM_OUT = 512
TE_SSM = 256
TE_GLU = 512
LB_GLU = 512
TM_ROWS = 1024
PREP_BLOCK = 1024

VMEM_LIMIT = 56 * 1024 * 1024
VMEM_LIMIT_LARGE = 60 * 1024 * 1024


def _params(n_axes, vmem=VMEM_LIMIT):
    return pltpu.CompilerParams(dimension_semantics=("arbitrary",) * n_axes,
                                vmem_limit_bytes=vmem)


def _rms(x, w):
    ms = jnp.mean(x * x, axis=-1, keepdims=True)
    return x * lax.rsqrt(ms + RMS_EPS) * w


def _cast_kernel(x_ref, o_ref, *, transpose):
    x = x_ref[...]
    o_ref[...] = (x.T if transpose else x).astype(BF16)


def _cast_bf16(w, *, transpose):
    _, r, c = w.shape
    b = PREP_BLOCK
    out_shape = (c, r) if transpose else (r, c)
    out_map = (lambda i, j: (j, i)) if transpose else (lambda i, j: (i, j))
    return pl.pallas_call(
        functools.partial(_cast_kernel, transpose=transpose),
        grid=(r // b, c // b),
        in_specs=[pl.BlockSpec((None, b, b), lambda i, j: (0, i, j))],
        out_specs=pl.BlockSpec((b, b), out_map),
        out_shape=jax.ShapeDtypeStruct(out_shape, BF16),
        compiler_params=_params(2),
        name="cast_t" if transpose else "cast",
    )(w)


def _rmsnorm_kernel(x_ref, w_ref, o_ref):
    o_ref[...] = _rms(x_ref[...], w_ref[...]).astype(BF16)


def _rmsnorm(x, w, *, tm):
    rows = x.shape[1]
    return pl.pallas_call(
        _rmsnorm_kernel, grid=(rows // tm,),
        in_specs=[pl.BlockSpec((None, tm, D_MODEL), lambda i: (0, i, 0)),
                  pl.BlockSpec((1, D_MODEL), lambda i: (0, 0))],
        out_specs=pl.BlockSpec((tm, D_MODEL), lambda i: (i, 0)),
        out_shape=jax.ShapeDtypeStruct((rows, D_MODEL), BF16),
        compiler_params=_params(1), name="rmsnorm",
    )(x, w)


def _conv_proj_kernel(*refs, tm, te, n_pieces, special, real_rows):
    if special:
        (xn_ref, wb_ref, wc_ref, wv_ref, wz_ref, cw_ref, cb_ref, init_ref, c0_ref, c1_ref,
         y_ref, cv_ref, w_scr, s_scr, p1_scr, p2_scr) = refs
    else:
        (xn_ref, wb_ref, wc_ref, wv_ref, wz_ref, cw_ref, cb_ref, init_ref,
         y_ref, tail_ref, w_scr, s_scr) = refs
    i = pl.program_id(1)

    @pl.when(i == 0)
    def _():
        for p, w_ref in enumerate((wb_ref, wc_ref, wv_ref, wz_ref)):
            w_scr[:, p * te:(p + 1) * te] = w_ref[...].astype(BF16)
        s_scr[0:8, :] = init_ref[...]

    if special:
        pm = real_rows
        cv_ref[pm:tm, :] = jnp.zeros((tm - pm, te), F32)
        y_ref[pm:tm, :] = jnp.zeros((tm - pm, te), BF16)
    else:
        pm = tm // n_pieces
    for r in range(n_pieces):
        lo = r * pm
        proj = jnp.dot(xn_ref[lo:lo + pm, :], w_scr[...], preferred_element_type=F32)
        bg = proj[:, 0 * te:1 * te]
        cg = proj[:, 1 * te:2 * te]
        vv = proj[:, 2 * te:3 * te]
        zz = proj[:, 3 * te:4 * te]
        cv = cg * vv
        s_scr[8 + lo:8 + lo + pm, :] = cv
        if special:
            cv_ref[lo:lo + pm, :] = cv
            p1_scr[...] = s_scr[7 + lo:7 + lo + pm, :]
            p2_scr[...] = s_scr[6 + lo:6 + lo + pm, :]
            for q in range(c0_ref.shape[0]):
                r0 = N_META + 32 * q
                p1_scr[r0:r0 + 1, :] = c1_ref[q:q + 1, :]
                p2_scr[r0:r0 + 1, :] = c0_ref[q:q + 1, :]
                p2_scr[r0 + 1:r0 + 2, :] = c1_ref[q:q + 1, :]
            p1 = p1_scr[...]
            p2 = p2_scr[...]
        else:
            p1 = s_scr[7 + lo:7 + lo + pm, :]
            p2 = s_scr[6 + lo:6 + lo + pm, :]
        conv = cb_ref[...] + cw_ref[0:1, :] * p2
        conv = conv + cw_ref[1:2, :] * p1
        conv = conv + cw_ref[2:3, :] * cv
        y_ref[lo:lo + pm, :] = (bg * conv * jax.nn.silu(zz)).astype(BF16)

    if not special:
        tail = s_scr[tm:tm + 8, :]
        s_scr[0:8, :] = tail
        tail_ref[...] = tail


def _conv_proj(xn, w_in, conv_w8, conv_b, init8, cache0=None, cache1=None, real_rows=None):
    rows = xn.shape[0]
    tm, te = TM_CONV, TE_CONV
    n_j = D_INNER // te
    n_i = rows // tm
    special = cache0 is not None
    w_spec = lambda p: pl.BlockSpec((None, D_MODEL, te),
                                    lambda j, i, p=p: (0, 0, p * n_j + j))
    in_specs = [pl.BlockSpec((tm, D_MODEL), lambda j, i: (i, 0)),
                w_spec(0), w_spec(1), w_spec(2), w_spec(3),
                pl.BlockSpec((8, te), lambda j, i: (0, j)),
                pl.BlockSpec((1, te), lambda j, i: (0, j)),
                pl.BlockSpec((8, te), lambda j, i: (0, j))]
    args = [xn, w_in, w_in, w_in, w_in, conv_w8, conv_b, init8]
    scratch = [pltpu.VMEM((D_MODEL, 4 * te), BF16), pltpu.VMEM((tm + 8, te), F32)]
    y_shape = jax.ShapeDtypeStruct((rows, D_INNER), BF16)
    y_spec = pl.BlockSpec((tm, te), lambda j, i: (i, j))
    if special:
        assert n_i == 1
        nq = cache0.shape[0]
        in_specs += [pl.BlockSpec((nq, te), lambda j, i: (0, j)),
                     pl.BlockSpec((nq, te), lambda j, i: (0, j))]
        args += [cache0, cache1]
        out_shape = (y_shape, jax.ShapeDtypeStruct((rows, D_INNER), F32))
        out_specs = (y_spec, pl.BlockSpec((tm, te), lambda j, i: (i, j)))
        scratch += [pltpu.VMEM((real_rows, te), F32), pltpu.VMEM((real_rows, te), F32)]
    else:
        out_shape = (y_shape, jax.ShapeDtypeStruct((n_i, 8, D_INNER), F32))
        out_specs = (y_spec, pl.BlockSpec((None, 8, te), lambda j, i: (i, 0, j)))
    return pl.pallas_call(
        functools.partial(_conv_proj_kernel, tm=tm, te=te, n_pieces=1 if special else 2,
                          special=special, real_rows=real_rows),
        grid=(n_j, n_i), in_specs=in_specs, out_specs=out_specs, out_shape=out_shape,
        scratch_shapes=scratch, compiler_params=_params(2, VMEM_LIMIT_LARGE),
        name="conv_proj_special" if special else "conv_proj",
    )(*args)


def _out_proj_kernel(y_ref, w_ref, x_ref, nw_ref, h_ref, xs_ref, slab_scr, *, real_rows, n_steps):
    tm = y_ref.shape[0]
    nc = tm // CHUNK
    n_slabs = D_MODEL // LANES

    def block(r):
        if r == 0:
            h = jnp.zeros((tm, D_MODEL), F32)
        else:
            h = x_ref[0:r, :] + jnp.dot(y_ref[0:r, :], w_ref[...], preferred_element_type=F32)
            if r < tm:
                h = jnp.concatenate([h, jnp.zeros((tm - r, D_MODEL), F32)], axis=0)
        for k in range(n_slabs):
            slab_scr[k] = h[:, LANES * k:LANES * (k + 1)]
        for s in range(CHUNK):
            hs = jnp.concatenate(
                [slab_scr[k, pl.ds(s, nc, stride=CHUNK), :] for k in range(n_slabs)], axis=1)
            h_ref[s] = hs
            xs_ref[s] = _rms(hs, nw_ref[...]).astype(BF16)

    if real_rows is None:
        block(tm)
    else:
        i = pl.program_id(0)
        for b in range(n_steps):
            pl.when(i == b)(functools.partial(block, min(max(real_rows - b * tm, 0), tm)))


def _out_proj(y, w_out, x, norm_w, *, lanes, real_rows=None):
    tm = TM_OUT
    nc = tm // CHUNK
    per_tile = lanes // nc
    n_i = x.shape[1] // tm
    tile_map = lambda i: (i // per_tile, 0, i % per_tile, 0)
    tile_shape = (n_i // per_tile, CHUNK, lanes, D_MODEL)
    return pl.pallas_call(
        functools.partial(_out_proj_kernel, real_rows=real_rows, n_steps=n_i),
        grid=(n_i,),
        in_specs=[
            pl.BlockSpec((tm, D_INNER), lambda i: (i, 0)),
            pl.BlockSpec((D_INNER, D_MODEL), lambda i: (0, 0), pipeline_mode=pl.Buffered(1)),
            pl.BlockSpec((None, tm, D_MODEL), lambda i: (0, i, 0)),
            pl.BlockSpec((1, D_MODEL), lambda i: (0, 0)),
        ],
        out_specs=(pl.BlockSpec((None, CHUNK, nc, D_MODEL), tile_map),
                   pl.BlockSpec((None, CHUNK, nc, D_MODEL), tile_map)),
        out_shape=(jax.ShapeDtypeStruct(tile_shape, F32),
                   jax.ShapeDtypeStruct(tile_shape, BF16)),
        scratch_shapes=[pltpu.VMEM((D_MODEL // LANES, tm, LANES), F32)],
        compiler_params=_params(1, VMEM_LIMIT_LARGE),
        name="out_proj",
    )(y, w_out, x, norm_w)


def _split_bf16(x):
    hi = x.astype(BF16)
    return hi, (x - hi.astype(F32)).astype(BF16)


def _ssm_kernel(xsp_ref, xs_ref, wu_ref, wz_ref, e1_ref, e2_ref, c1_ref, c2_ref, apt_ref, bb_ref,
                dv_ref, are_ref, aim_ref, sre_ref, sim_ref,
                gtsp_ref, sztsp_ref, gt_ref, szt_ref, hsre_ref, hsim_ref, hpre_ref, hpim_ref,
                w_scr, tk_scr, win_scr, cout_scr, zt_scr, xre_scr, xim_scr, pre_scr, pim_scr,
                cre_scr, cim_scr, *, te, n_tiles):
    i = pl.program_id(1)
    gb = te // SSM_GROUP
    rows = gb * SSM_STATE
    n_seq = sre_ref.shape[0]
    bdims = (((2,), (1,)), ((0,), (0,)))

    def build_operators():
        w_scr[0:te, :] = wu_ref[...]
        w_scr[te:2 * te, :] = wz_ref[...]

        c1 = c1_ref[...]
        c2 = c2_ref[...]
        cap = [c1 * e1_ref[:, t:t + 1, :] + c2 * e2_ref[:, t:t + 1, :] for t in range(CHUNK + 1)]
        ca = jnp.concatenate(cap[:CHUNK], axis=1)
        cout_scr[...] = jnp.concatenate(cap[1:], axis=1).astype(BF16)

        lane = lax.broadcasted_iota(jnp.int32, (SSM_GROUP, CHUNK_ROWS), 1)
        row = lax.broadcasted_iota(jnp.int32, (SSM_GROUP, CHUNK_ROWS), 0)
        diag = row == (lane & (SSM_GROUP - 1))
        tile_c = diag.astype(BF16)
        spread_s = (row == lane // SSM_GROUP).astype(BF16)
        expand = lambda b, m, n_rows: jnp.dot(b, m, preferred_element_type=F32).reshape(
            gb, n_rows, CHUNK_ROWS)
        bhi, blo = _split_bf16(bb_ref[...].reshape(gb * 2 * SSM_STATE, SSM_GROUP))
        bhi_t = expand(bhi, tile_c, 2 * SSM_STATE)
        blo_t = expand(blo, tile_c, 2 * SSM_STATE)

        phi, plo = _split_bf16(apt_ref[...].reshape(gb * 2 * SSM_STATE, SSM_GROUP))
        ap_t = expand(phi, spread_s, 2 * SSM_STATE) + expand(plo, spread_s, 2 * SSM_STATE)
        b_t = bhi_t + blo_t
        pr, pi = ap_t[:, :SSM_STATE], ap_t[:, SSM_STATE:]
        br, bi = b_t[:, :SSM_STATE], b_t[:, SSM_STATE:]
        win_scr[:, 0:SSM_STATE, :] = (pr * br - pi * bi).astype(BF16)
        win_scr[:, SSM_STATE:, :] = (pr * bi + pi * br).astype(BF16)

        bhi_t, blo_t = bhi_t.astype(BF16), blo_t.astype(BF16)
        ahi, alo = _split_bf16(ca)
        kw = (lax.dot_general(ahi, bhi_t, bdims, preferred_element_type=F32)
              + lax.dot_general(ahi, blo_t, bdims, preferred_element_type=F32)
              + lax.dot_general(alo, bhi_t, bdims, preferred_element_type=F32))
        kw0 = (kw[:, 0:SSM_GROUP, :] + jnp.where(diag, dv_ref[...], 0.0)).astype(BF16)
        kw = kw.astype(BF16)
        tk_scr[...] = jnp.zeros_like(tk_scr)
        for s in range(CHUNK):
            lo = SSM_GROUP * s
            hi = lo + SSM_GROUP
            tk_scr[:, lo:hi, lo:hi] = kw0[:, :, lo:hi]
            if hi < CHUNK_ROWS:
                tk_scr[:, hi:CHUNK_ROWS, lo:hi] = kw[:, SSM_GROUP:CHUNK_ROWS - lo, lo:hi]

    nt_dims = (((1,), (1,)), ((), ()))
    half = LANES // 2
    low_half = lax.broadcasted_iota(jnp.int32, (1, LANES), 1) < half

    ar = are_ref[...]
    ai = aim_ref[...]

    def step(hr, hi, n):
        xr = xre_scr[pl.ds(n, 1), :]
        xi = xim_scr[pl.ds(n, 1), :]
        return ar * hr - ai * hi + xr, ar * hi + ai * hr + xi

    def chunk_inputs():
        zt = zt_scr[...]
        yt = lax.dot_general(tk_scr[...], zt, bdims, preferred_element_type=F32)
        xt = lax.dot_general(win_scr[...], zt, bdims, preferred_element_type=F32)
        xre_scr[...] = xt[:, 0:SSM_STATE, :].reshape(rows, LANES).T
        xim_scr[...] = xt[:, SSM_STATE:, :].reshape(rows, LANES).T
        return yt

    def gelu_pieces(yt):
        hp = jnp.concatenate([pre_scr[...].T.reshape(gb, SSM_STATE, LANES),
                              pim_scr[...].T.reshape(gb, SSM_STATE, LANES)], axis=1)
        ycorr = lax.dot_general(cout_scr[...], hp.astype(BF16), bdims,
                                preferred_element_type=F32)
        g = jax.nn.gelu(yt + ycorr)
        return [g[:, SSM_GROUP * t:SSM_GROUP * (t + 1), :].reshape(te, LANES)
                for t in range(CHUNK)]

    def pack_samples(out_ref, block_of, first_lane_of):
        quarter = LANES // 4
        lane_q = lax.broadcasted_iota(jnp.int32, (1, LANES), 1) // quarter
        for c in range(CHUNK // 4):
            out = jnp.zeros((te, LANES), F32)
            for k in range(4):
                t = 4 * c + k
                shift = (quarter * k - first_lane_of(t)) % LANES
                blk = block_of(t)
                out = jnp.where(lane_q == k, pltpu.roll(blk, shift, 1) if shift else blk, out)
            out_ref[:, LANES * c:LANES * (c + 1)] = out.astype(BF16)

    @pl.when(i == 0)
    def _():
        build_operators()
        uz = lax.dot_general(w_scr[...], xsp_ref[...], nt_dims,
                             preferred_element_type=F32)
        sz = jax.nn.silu(uz[te:, :])
        pack_samples(sztsp_ref, lambda t: sz[:, LANES * (t // 2):LANES * (t // 2 + 1)],
                     lambda t: half * (t % 2) + 1)
        for s in range(CHUNK):
            blk = uz[:te, LANES * (s // 2):LANES * (s // 2 + 1)]
            if s % 2:
                blk = pltpu.roll(blk, half, 1)
            zt_scr[:, CHUNK * s:CHUNK * (s + 1), :] = (
                jnp.where(low_half, blk, 0.0).astype(BF16).reshape(gb, SSM_GROUP, LANES))
        yt = chunk_inputs()

        pre_scr[...] = jnp.zeros_like(pre_scr)
        pim_scr[...] = jnp.zeros_like(pim_scr)
        zero = jnp.zeros((1, rows), F32)
        hr, hi = step(zero, zero, 0)
        cre_scr[0:1, :] = hr
        cim_scr[0:1, :] = hi
        for q in range(n_seq):
            hr = sre_ref[q:q + 1, :]
            hi = sim_ref[q:q + 1, :]
            for e in range(2):
                n = 1 + 2 * q + e
                pre_scr[n:n + 1, :] = hr
                pim_scr[n:n + 1, :] = hi
                hr, hi = step(hr, hi, n)
            hsre_ref[q:q + 1, :] = hr
            hsim_ref[q:q + 1, :] = hi

        pieces = gelu_pieces(yt)
        pack_samples(gtsp_ref, lambda t: pieces[t], lambda t: 1)

    @pl.when(i > 0)
    def _():
        uz = lax.dot_general(w_scr[...], xs_ref[...], nt_dims,
                             preferred_element_type=F32)
        szt_ref[...] = jax.nn.silu(uz[te:, :]).astype(BF16)
        ub = uz[:te, :].astype(BF16)
        for s in range(CHUNK):
            zt_scr[:, CHUNK * s:CHUNK * (s + 1), :] = (
                ub[:, LANES * s:LANES * (s + 1)].reshape(gb, SSM_GROUP, LANES))
        yt = chunk_inputs()

        hr, hi = cre_scr[0:1, :], cim_scr[0:1, :]
        for n in range(LANES):
            pre_scr[n:n + 1, :] = hr
            pim_scr[n:n + 1, :] = hi
            hr, hi = step(hr, hi, n)
        cre_scr[0:1, :] = hr
        cim_scr[0:1, :] = hi

        @pl.when(i == n_tiles - 1)
        def _():
            hpre_ref[...] = jnp.broadcast_to(hr, hpre_ref.shape)
            hpim_ref[...] = jnp.broadcast_to(hi, hpim_ref.shape)

        pieces = gelu_pieces(yt)
        for t in range(CHUNK):
            gt_ref[:, LANES * t:LANES * (t + 1)] = pieces[t].astype(BF16)


def _ssm(xs_sp, xs_p, wuz_t, e1, e2, c1, c2, apt, bb, dv, are, aim, sre, sim):
    n_ptiles = xs_p.shape[0]
    n_tiles = 1 + n_ptiles
    te = TE_SSM
    n_j = D_INNER // te
    gb = te // SSM_GROUP
    rows = gb * SSM_STATE
    n_seq = sre.shape[0]
    prev = lambda i: jnp.maximum(i - 1, 0)
    act_shape = jax.ShapeDtypeStruct((n_ptiles, D_INNER, TILE), BF16)
    act_spec = pl.BlockSpec((None, te, TILE), lambda j, i: (prev(i), j, 0))
    assert 2 * n_seq == LANES // 4
    sp_slots = CHUNK * 2 * n_seq
    sp_shape = jax.ShapeDtypeStruct((1, D_INNER, sp_slots), BF16)
    sp_spec = pl.BlockSpec((None, te, sp_slots), lambda j, i: (0, j, 0))
    hs_shape = jax.ShapeDtypeStruct((n_seq, N_STATE_ROWS), F32)
    hs_spec = pl.BlockSpec((n_seq, rows), lambda j, i: (0, j))
    hp_shape = jax.ShapeDtypeStruct((8, N_STATE_ROWS), F32)
    hp_spec = pl.BlockSpec((8, rows), lambda j, i: (0, j))
    row_spec = pl.BlockSpec((1, rows), lambda j, i: (0, j))
    grp_spec = lambda r, c: pl.BlockSpec((gb, r, c), lambda j, i: (j, 0, 0))
    return pl.pallas_call(
        functools.partial(_ssm_kernel, te=te, n_tiles=n_tiles),
        grid=(n_j, n_tiles),
        in_specs=[
            pl.BlockSpec((TILE // 2, D_MODEL), lambda j, i: (0, 0), pipeline_mode=pl.Buffered(1)),
            pl.BlockSpec((None, TILE, D_MODEL), lambda j, i: (prev(i), 0, 0)),
            pl.BlockSpec((te, D_MODEL), lambda j, i: (j, 0)),
            pl.BlockSpec((te, D_MODEL), lambda j, i: (n_j + j, 0)),
            grp_spec(e1.shape[1], 2 * SSM_STATE), grp_spec(e2.shape[1], 2 * SSM_STATE),
            grp_spec(SSM_GROUP, 2 * SSM_STATE), grp_spec(SSM_GROUP, 2 * SSM_STATE),
            grp_spec(2 * SSM_STATE, SSM_GROUP), grp_spec(2 * SSM_STATE, SSM_GROUP),
            grp_spec(SSM_GROUP, 1),
            row_spec, row_spec, hs_spec, hs_spec,
        ],
        out_specs=(sp_spec, sp_spec, act_spec, act_spec, hs_spec, hs_spec, hp_spec, hp_spec),
        out_shape=(sp_shape, sp_shape, act_shape, act_shape, hs_shape, hs_shape,
                   hp_shape, hp_shape),
        scratch_shapes=[pltpu.VMEM((2 * te, D_MODEL), BF16),
                        pltpu.VMEM((gb, CHUNK_ROWS, CHUNK_ROWS), BF16),
                        pltpu.VMEM((gb, 2 * SSM_STATE, CHUNK_ROWS), BF16),
                        pltpu.VMEM((gb, CHUNK_ROWS, 2 * SSM_STATE), BF16),
                        pltpu.VMEM((gb, CHUNK_ROWS, LANES), BF16),
                        pltpu.VMEM((LANES, rows), F32), pltpu.VMEM((LANES, rows), F32),
                        pltpu.VMEM((LANES, rows), F32), pltpu.VMEM((LANES, rows), F32),
                        pltpu.VMEM((8, rows), F32), pltpu.VMEM((8, rows), F32)],
        compiler_params=_params(2, VMEM_LIMIT_LARGE),
        name="s5_scan",
    )(xs_sp, xs_p, wuz_t, wuz_t, e1, e2, c1, c2, apt, bb, dv, are, aim, sre, sim)


def _glu_out_kernel(gt_ref, szt_ref, wg_ref, bg_ref, wo_ref, h_ref, nw_ref, o_ref, acc_scr, *,
                    te, lb):
    e = pl.program_id(2)

    @pl.when(e == 0)
    def _():
        acc_scr[...] = jnp.zeros_like(acc_scr)

    gate = jnp.dot(wg_ref[...], gt_ref[...], preferred_element_type=F32) + bg_ref[...]
    ge = gt_ref[pl.ds(pl.multiple_of(e * te, te), te), :].astype(F32)
    y3 = (ge * jax.nn.sigmoid(gate)) * szt_ref[...].astype(F32)
    acc_scr[...] += jnp.dot(wo_ref[...], y3.astype(BF16), preferred_element_type=F32)

    @pl.when(e == pl.num_programs(2) - 1)
    def _():
        o = acc_scr[...].T
        ln = h_ref.shape[1]
        for t in range(lb // ln):
            h = h_ref[t] + o[ln * t:ln * (t + 1), :]
            o_ref[:, D_MODEL * t:D_MODEL * (t + 1)] = _rms(h, nw_ref[...])


def _glu_out(gt, szt, wg_t, b_glu, wo_t, h_all, norm_w):
    n_tiles, _, lanes, _ = h_all.shape
    te = TE_GLU
    lb = LB_GLU
    n_l = CHUNK * lanes // lb
    n_e = D_INNER // te
    tpb = lb // lanes
    return pl.pallas_call(
        functools.partial(_glu_out_kernel, te=te, lb=lb),
        grid=(n_tiles, n_l, n_e),
        in_specs=[
            pl.BlockSpec((None, D_INNER, lb), lambda i, l, e: (i, 0, l)),
            pl.BlockSpec((None, te, lb), lambda i, l, e: (i, e, l)),
            pl.BlockSpec((te, D_INNER), lambda i, l, e: (e, 0)),
            pl.BlockSpec((te, 1), lambda i, l, e: (e, 0)),
            pl.BlockSpec((D_MODEL, te), lambda i, l, e: (0, e)),
            pl.BlockSpec((None, tpb, lanes, D_MODEL), lambda i, l, e: (i, l, 0, 0)),
            pl.BlockSpec((1, D_MODEL), lambda i, l, e: (0, 0)),
        ],
        out_specs=pl.BlockSpec((lanes, tpb * D_MODEL), lambda i, l, e: (i, l)),
        out_shape=jax.ShapeDtypeStruct((n_tiles * lanes, CHUNK * D_MODEL), F32),
        scratch_shapes=[pltpu.VMEM((D_MODEL, lb), F32)],
        compiler_params=_params(3),
        name="glu_out",
    )(gt, szt, wg_t, b_glu, wo_t, h_all, norm_w)


def _to_rows_kernel(o_ref, y_ref, slab_scr):
    nc = o_ref.shape[0]
    n_slabs = D_MODEL // LANES
    for t in range(CHUNK):
        for k in range(n_slabs):
            slab_scr[k, pl.ds(t, nc, stride=CHUNK), :] = (
                o_ref[:, D_MODEL * t + LANES * k:D_MODEL * t + LANES * (k + 1)])
    for k in range(n_slabs):
        y_ref[:, LANES * k:LANES * (k + 1)] = slab_scr[k]


def _to_rows(o):
    n_chunks = o.shape[0]
    tm = TM_ROWS
    nc = tm // CHUNK
    return pl.pallas_call(
        _to_rows_kernel,
        grid=(n_chunks // nc,),
        in_specs=[pl.BlockSpec((nc, CHUNK * D_MODEL), lambda i: (i, 0))],
        out_specs=pl.BlockSpec((None, tm, D_MODEL), lambda i: (0, i, 0)),
        out_shape=jax.ShapeDtypeStruct((1, n_chunks * CHUNK, D_MODEL), F32),
        scratch_shapes=[pltpu.VMEM((D_MODEL // LANES, tm, LANES), F32)],
        compiler_params=_params(1),
        name="to_rows",
    )(o)


def _ssm_operators(a_re, a_im, b_re, b_im, c_re, c_im, d_vec, log_dt):
    a_re, a_im = a_re.astype(F32), a_im.astype(F32)
    dt = jnp.exp(log_dt.astype(F32))[:, None]
    th_re = jnp.concatenate([a_re * dt] * 2, axis=1)[:, None, :]
    th_im = jnp.concatenate([a_im * dt] * 2, axis=1)[:, None, :]
    n_tau = 24
    taus = jnp.arange(n_tau, dtype=F32)[None, :, None]
    used = taus <= CHUNK
    mag = jnp.exp(jnp.where(used, th_re * taus, 0.0))
    ang = jnp.where(used, th_im * taus, 0.0)
    ap_re = mag * jnp.cos(ang)
    ap_im = mag * jnp.sin(ang)
    sign = jnp.concatenate([jnp.ones((SSM_STATE,), F32), -jnp.ones((SSM_STATE,), F32)])
    e1 = ap_re * sign
    e2 = -ap_im
    c_re, c_im = c_re.astype(F32), c_im.astype(F32)
    c1 = jnp.concatenate([c_re, c_im], axis=2)
    c2 = jnp.concatenate([c_im, c_re], axis=2)

    ab_re, ab_im = ap_re[:, 1, :SSM_STATE], ap_im[:, 1, :SSM_STATE]
    nr, ni = ab_re - 1.0, ab_im
    den = a_re * a_re + a_im * a_im
    q_re = ((nr * a_re + ni * a_im) / den)[..., None]
    q_im = ((ni * a_re - nr * a_im) / den)[..., None]
    b_re, b_im = b_re.astype(F32), b_im.astype(F32)
    bb = jnp.concatenate([q_re * b_re - q_im * b_im, q_re * b_im + q_im * b_re], axis=1)

    rev = slice(CHUNK - 1, None, -1)
    apt = jnp.concatenate([ap_re[:, rev, :SSM_STATE].transpose(0, 2, 1),
                           ap_im[:, rev, :SSM_STATE].transpose(0, 2, 1)], axis=1)
    dv = d_vec.astype(F32).reshape(N_GROUPS, SSM_GROUP, 1)
    are = ap_re[:, CHUNK, :SSM_STATE].reshape(1, N_STATE_ROWS)
    aim = ap_im[:, CHUNK, :SSM_STATE].reshape(1, N_STATE_ROWS)
    return e1, e2, c1, c2, apt, bb, dv, are, aim


def kernel(x_prompt, x_sample, cache_conv, state_ssm_re, state_ssm_im, meta_tokens, norm_w,
           final_norm_w, conv_w_in, conv_w, conv_b, conv_w_out, ssm_w_in, ssm_a_re, ssm_a_im,
           ssm_b_re, ssm_b_im, ssm_c_re, ssm_c_im, ssm_d, ssm_log_dt, ssm_w_glu, ssm_b_glu,
           ssm_w_out):
    n_seq, seq_len = x_sample.shape[0], x_sample.shape[1]
    n_prompt_rows = x_prompt.shape[1]
    n_ptiles = n_prompt_rows // TILE
    n_sample_rows = n_seq * seq_len
    assert x_prompt.shape[0] == 1 and seq_len == 2 * CHUNK and N_META == CHUNK
    assert n_prompt_rows % TILE == 0 and N_META + n_sample_rows <= SPECIAL_ROWS

    w_out0 = _cast_bf16(conv_w_out.astype(F32), transpose=False)
    wuz_t = _cast_bf16(ssm_w_in.astype(F32), transpose=True)
    wg_t = _cast_bf16(ssm_w_glu.astype(F32), transpose=True)
    wo_t = _cast_bf16(ssm_w_out.astype(F32), transpose=True)
    w_in0 = conv_w_in.astype(F32)
    conv_w8 = jnp.zeros((8, D_INNER), F32).at[0:3].set(conv_w[0].astype(F32))
    conv_b2 = conv_b[0].astype(F32).reshape(1, D_INNER)
    b_glu = ssm_b_glu[0].astype(F32).reshape(D_INNER, 1)
    nw0 = norm_w[0].astype(F32).reshape(1, D_MODEL)
    nw1 = norm_w[1].astype(F32).reshape(1, D_MODEL)
    nwf = final_norm_w.astype(F32).reshape(1, D_MODEL)
    ssm_ops = _ssm_operators(
        ssm_a_re[0], ssm_a_im[0], ssm_b_re[0], ssm_b_im[0], ssm_c_re[0], ssm_c_im[0],
        ssm_d[0], ssm_log_dt[0])
    sre = state_ssm_re[0].astype(F32).reshape(n_seq, N_STATE_ROWS)
    sim = state_ssm_im[0].astype(F32).reshape(n_seq, N_STATE_ROWS)

    xp = x_prompt.astype(F32)
    x_sp = jnp.concatenate([
        meta_tokens.astype(F32), x_sample.astype(F32).reshape(n_sample_rows, D_MODEL),
        jnp.zeros((SPECIAL_ROWS - N_META - n_sample_rows, D_MODEL), F32)], axis=0)[None]

    zeros8 = jnp.zeros((8, D_INNER), F32)
    xn_sp = _rmsnorm(x_sp, nw0, tm=SPECIAL_ROWS)
    xn_p = _rmsnorm(xp, nw0, tm=TM_CONV)
    sp_real = N_META + n_sample_rows
    y_sp, cv_sp = _conv_proj(xn_sp, w_in0, conv_w8, conv_b2, zeros8,
                             cache_conv[0, :, 0, :].astype(F32), cache_conv[0, :, 1, :].astype(F32),
                             real_rows=sp_real)
    init8 = zeros8.at[6:8].set(cv_sp[N_META - 2:N_META])
    y_p, tail_p = _conv_proj(xn_p, w_in0, conv_w8, conv_b2, init8)
    h1_p, xs_p = _out_proj(y_p, w_out0, xp, nw1, lanes=LANES)
    h1_sp, xs_sp = _out_proj(y_sp, w_out0, x_sp, nw1, lanes=SPECIAL_ROWS // CHUNK,
                             real_rows=sp_real)

    gt_sp, szt_sp, gt_p, szt_p, hs_re, hs_im, hp_re, hp_im = _ssm(
        xs_sp.reshape(SPECIAL_ROWS, D_MODEL), xs_p.reshape(n_ptiles, TILE, D_MODEL),
        wuz_t, *ssm_ops, sre, sim)
    n_sc = n_sample_rows // CHUNK
    o_sp = _glu_out(gt_sp, szt_sp, wg_t, b_glu, wo_t, h1_sp[:, :, 1:1 + n_sc], nwf)
    o_p = _glu_out(gt_p, szt_p, wg_t, b_glu, wo_t, h1_p, nwf)

    y_prompt = _to_rows(o_p)
    y_sample = o_sp.reshape(n_seq, seq_len, D_MODEL)
    new_conv_prompt = tail_p[-1, 6:8].reshape(1, 1, 2, D_INNER)
    cv_s = cv_sp[N_META:N_META + n_sample_rows].reshape(n_seq, seq_len, D_INNER)
    new_conv_sample = cv_s[:, seq_len - 2:].reshape(1, n_seq, 2, D_INNER)
    p_shape = (1, 1, N_GROUPS, SSM_STATE)
    s_shape = (1, n_seq, N_GROUPS, SSM_STATE)
    return (y_prompt, y_sample, new_conv_prompt, new_conv_sample,
            hp_re[0].reshape(p_shape), hp_im[0].reshape(p_shape),
            hs_re.reshape(s_shape), hs_im.reshape(s_shape))
```

```python
import functools

import jax
import jax.numpy as jnp
from jax import lax
from jax.experimental import pallas as pl
from jax.experimental.pallas import tpu as pltpu

F32 = jnp.float32
BF16 = jnp.bfloat16

D_MODEL = 2048
D_INNER = 4096
N_META = 16
SSM_GROUP = 16
N_GROUPS = D_INNER // SSM_GROUP
SSM_STATE = 64
N_STATE_ROWS = N_GROUPS * SSM_STATE
RMS_EPS = 1e-6

CHUNK = 16
CHUNK_ROWS = CHUNK * SSM_GROUP
LANES = 128
TILE = CHUNK * LANES
SPECIAL_ROWS = 1024

TM_CONV = 1024
TE_CONV = 512
TM_OUT = 512
TE_SSM = 256
TE_GLU = 1024
LB_GLU = 512
TM_ROWS = 1024
PREP_BLOCK = 1024

VMEM_LIMIT = 56 * 1024 * 1024
VMEM_LIMIT_LARGE = 60 * 1024 * 1024


def _params(n_axes, vmem=VMEM_LIMIT):
    return pltpu.CompilerParams(dimension_semantics=("arbitrary",) * n_axes,
                                vmem_limit_bytes=vmem)


def _rms(x, w):
    ms = jnp.mean(x * x, axis=-1, keepdims=True)
    return x * lax.rsqrt(ms + RMS_EPS) * w


def _cast_kernel(x_ref, o_ref, *, transpose):
    x = x_ref[...]
    o_ref[...] = (x.T if transpose else x).astype(BF16)


def _cast_bf16(w, *, transpose):
    _, r, c = w.shape
    b = PREP_BLOCK
    out_shape = (c, r) if transpose else (r, c)
    out_map = (lambda i, j: (j, i)) if transpose else (lambda i, j: (i, j))
    return pl.pallas_call(
        functools.partial(_cast_kernel, transpose=transpose),
        grid=(r // b, c // b),
        in_specs=[pl.BlockSpec((None, b, b), lambda i, j: (0, i, j))],
        out_specs=pl.BlockSpec((b, b), out_map),
        out_shape=jax.ShapeDtypeStruct(out_shape, BF16),
        compiler_params=_params(2),
        name="cast_t" if transpose else "cast",
    )(w)


def _rmsnorm_kernel(x_ref, w_ref, o_ref):
    o_ref[...] = _rms(x_ref[...], w_ref[...]).astype(BF16)


def _rmsnorm(x, w, *, tm):
    rows = x.shape[1]
    return pl.pallas_call(
        _rmsnorm_kernel, grid=(rows // tm,),
        in_specs=[pl.BlockSpec((None, tm, D_MODEL), lambda i: (0, i, 0)),
                  pl.BlockSpec((1, D_MODEL), lambda i: (0, 0))],
        out_specs=pl.BlockSpec((tm, D_MODEL), lambda i: (i, 0)),
        out_shape=jax.ShapeDtypeStruct((rows, D_MODEL), BF16),
        compiler_params=_params(1), name="rmsnorm",
    )(x, w)


def _conv_proj_kernel(*refs, tm, te, n_pieces, special, real_rows):
    if special:
        (xn_ref, wb_ref, wc_ref, wv_ref, wz_ref, cw_ref, cb_ref, init_ref, c0_ref, c1_ref,
         y_ref, cv_ref, w_scr, s_scr, p1_scr, p2_scr) = refs
    else:
        (xn_ref, wb_ref, wc_ref, wv_ref, wz_ref, cw_ref, cb_ref, init_ref,
         y_ref, tail_ref, w_scr, s_scr) = refs
    i = pl.program_id(1)

    @pl.when(i == 0)
    def _():
        for p, w_ref in enumerate((wb_ref, wc_ref, wv_ref, wz_ref)):
            w_scr[:, p * te:(p + 1) * te] = w_ref[...].astype(BF16)
        s_scr[0:8, :] = init_ref[...]

    if special:
        pm = real_rows
        cv_ref[pm:tm, :] = jnp.zeros((tm - pm, te), F32)
        y_ref[pm:tm, :] = jnp.zeros((tm - pm, te), BF16)
    else:
        pm = tm // n_pieces
    for r in range(n_pieces):
        lo = r * pm
        proj = jnp.dot(xn_ref[lo:lo + pm, :], w_scr[...], preferred_element_type=F32)
        bg = proj[:, 0 * te:1 * te]
        cg = proj[:, 1 * te:2 * te]
        vv = proj[:, 2 * te:3 * te]
        zz = proj[:, 3 * te:4 * te]
        cv = cg * vv
        s_scr[8 + lo:8 + lo + pm, :] = cv
        if special:
            cv_ref[lo:lo + pm, :] = cv
            p1_scr[...] = s_scr[7 + lo:7 + lo + pm, :]
            p2_scr[...] = s_scr[6 + lo:6 + lo + pm, :]
            for q in range(c0_ref.shape[0]):
                r0 = N_META + 32 * q
                p1_scr[r0:r0 + 1, :] = c1_ref[q:q + 1, :]
                p2_scr[r0:r0 + 1, :] = c0_ref[q:q + 1, :]
                p2_scr[r0 + 1:r0 + 2, :] = c1_ref[q:q + 1, :]
            p1 = p1_scr[...]
            p2 = p2_scr[...]
        else:
            p1 = s_scr[7 + lo:7 + lo + pm, :]
            p2 = s_scr[6 + lo:6 + lo + pm, :]
        conv = cb_ref[...] + cw_ref[0:1, :] * p2
        conv = conv + cw_ref[1:2, :] * p1
        conv = conv + cw_ref[2:3, :] * cv
        y_ref[lo:lo + pm, :] = (bg * conv * jax.nn.silu(zz)).astype(BF16)

    if not special:
        tail = s_scr[tm:tm + 8, :]
        s_scr[0:8, :] = tail
        tail_ref[...] = tail


def _conv_proj(xn, w_in, conv_w8, conv_b, init8, cache0=None, cache1=None, real_rows=None):
    rows = xn.shape[0]
    tm, te = TM_CONV, TE_CONV
    n_j = D_INNER // te
    n_i = rows // tm
    special = cache0 is not None
    w_spec = lambda p: pl.BlockSpec((None, D_MODEL, te),
                                    lambda j, i, p=p: (0, 0, p * n_j + j))
    in_specs = [pl.BlockSpec((tm, D_MODEL), lambda j, i: (i, 0)),
                w_spec(0), w_spec(1), w_spec(2), w_spec(3),
                pl.BlockSpec((8, te), lambda j, i: (0, j)),
                pl.BlockSpec((1, te), lambda j, i: (0, j)),
                pl.BlockSpec((8, te), lambda j, i: (0, j))]
    args = [xn, w_in, w_in, w_in, w_in, conv_w8, conv_b, init8]
    scratch = [pltpu.VMEM((D_MODEL, 4 * te), BF16), pltpu.VMEM((tm + 8, te), F32)]
    y_shape = jax.ShapeDtypeStruct((rows, D_INNER), BF16)
    y_spec = pl.BlockSpec((tm, te), lambda j, i: (i, j))
    if special:
        assert n_i == 1
        nq = cache0.shape[0]
        in_specs += [pl.BlockSpec((nq, te), lambda j, i: (0, j)),
                     pl.BlockSpec((nq, te), lambda j, i: (0, j))]
        args += [cache0, cache1]
        out_shape = (y_shape, jax.ShapeDtypeStruct((rows, D_INNER), F32))
        out_specs = (y_spec, pl.BlockSpec((tm, te), lambda j, i: (i, j)))
        scratch += [pltpu.VMEM((real_rows, te), F32), pltpu.VMEM((real_rows, te), F32)]
    else:
        out_shape = (y_shape, jax.ShapeDtypeStruct((n_i, 8, D_INNER), F32))
        out_specs = (y_spec, pl.BlockSpec((None, 8, te), lambda j, i: (i, 0, j)))
    return pl.pallas_call(
        functools.partial(_conv_proj_kernel, tm=tm, te=te, n_pieces=1 if special else 2,
                          special=special, real_rows=real_rows),
        grid=(n_j, n_i), in_specs=in_specs, out_specs=out_specs, out_shape=out_shape,
        scratch_shapes=scratch, compiler_params=_params(2, VMEM_LIMIT_LARGE),
        name="conv_proj_special" if special else "conv_proj",
    )(*args)


def _out_proj_kernel(y_ref, w_ref, x_ref, nw_ref, h_ref, xs_ref, slab_scr, *, real_rows, n_steps):
    tm = y_ref.shape[0]
    nc = tm // CHUNK
    n_slabs = D_MODEL // LANES

    def block(r):
        if r == 0:
            h = jnp.zeros((tm, D_MODEL), F32)
        else:
            h = x_ref[0:r, :] + jnp.dot(y_ref[0:r, :], w_ref[...], preferred_element_type=F32)
            if r < tm:
                h = jnp.concatenate([h, jnp.zeros((tm - r, D_MODEL), F32)], axis=0)
        for k in range(n_slabs):
            slab_scr[k] = h[:, LANES * k:LANES * (k + 1)]
        for s in range(CHUNK):
            hs = jnp.concatenate(
                [slab_scr[k, pl.ds(s, nc, stride=CHUNK), :] for k in range(n_slabs)], axis=1)
            h_ref[s] = hs
            xs_ref[s] = _rms(hs, nw_ref[...]).astype(BF16)

    if real_rows is None:
        block(tm)
    else:
        i = pl.program_id(0)
        for b in range(n_steps):
            pl.when(i == b)(functools.partial(block, min(max(real_rows - b * tm, 0), tm)))


def _out_proj(y, w_out, x, norm_w, *, lanes, real_rows=None):
    tm = TM_OUT
    nc = tm // CHUNK
    per_tile = lanes // nc
    n_i = x.shape[1] // tm
    tile_map = lambda i: (i // per_tile, 0, i % per_tile, 0)
    tile_shape = (n_i // per_tile, CHUNK, lanes, D_MODEL)
    return pl.pallas_call(
        functools.partial(_out_proj_kernel, real_rows=real_rows, n_steps=n_i),
        grid=(n_i,),
        in_specs=[
            pl.BlockSpec((tm, D_INNER), lambda i: (i, 0)),
            pl.BlockSpec((D_INNER, D_MODEL), lambda i: (0, 0), pipeline_mode=pl.Buffered(1)),
            pl.BlockSpec((None, tm, D_MODEL), lambda i: (0, i, 0)),
            pl.BlockSpec((1, D_MODEL), lambda i: (0, 0)),
        ],
        out_specs=(pl.BlockSpec((None, CHUNK, nc, D_MODEL), tile_map),
                   pl.BlockSpec((None, CHUNK, nc, D_MODEL), tile_map)),
        out_shape=(jax.ShapeDtypeStruct(tile_shape, F32),
                   jax.ShapeDtypeStruct(tile_shape, BF16)),
        scratch_shapes=[pltpu.VMEM((D_MODEL // LANES, tm, LANES), F32)],
        compiler_params=_params(1, VMEM_LIMIT_LARGE),
        name="out_proj",
    )(y, w_out, x, norm_w)


def _split_bf16(x):
    hi = x.astype(BF16)
    return hi, (x - hi.astype(F32)).astype(BF16)


def _ssm_kernel(xsp_ref, xs_ref, wu_ref, wz_ref, e1_ref, e2_ref, c1_ref, c2_ref, apt_ref, bb_ref,
                dv_ref, are_ref, aim_ref, sre_ref, sim_ref,
                gtsp_ref, sztsp_ref, gt_ref, szt_ref, hsre_ref, hsim_ref, hpre_ref, hpim_ref,
                w_scr, tk_scr, win_scr, cout_scr, zt_scr, xre_scr, xim_scr, pre_scr, pim_scr,
                cre_scr, cim_scr, *, te, n_tiles):
    i = pl.program_id(1)
    gb = te // SSM_GROUP
    rows = gb * SSM_STATE
    n_seq = sre_ref.shape[0]
    bdims = (((2,), (1,)), ((0,), (0,)))

    def build_operators():
        w_scr[0:te, :] = wu_ref[...]
        w_scr[te:2 * te, :] = wz_ref[...]

        c1 = c1_ref[...]
        c2 = c2_ref[...]
        cap = [c1 * e1_ref[:, t:t + 1, :] + c2 * e2_ref[:, t:t + 1, :] for t in range(CHUNK + 1)]
        ca = jnp.concatenate(cap[:CHUNK], axis=1)
        cout_scr[...] = jnp.concatenate(cap[1:], axis=1).astype(BF16)

        lane = lax.broadcasted_iota(jnp.int32, (SSM_GROUP, CHUNK_ROWS), 1)
        row = lax.broadcasted_iota(jnp.int32, (SSM_GROUP, CHUNK_ROWS), 0)
        diag = row == (lane & (SSM_GROUP - 1))
        tile_c = diag.astype(BF16)
        spread_s = (row == lane // SSM_GROUP).astype(BF16)
        expand = lambda b, m, n_rows: jnp.dot(b, m, preferred_element_type=F32).reshape(
            gb, n_rows, CHUNK_ROWS)
        bhi, blo = _split_bf16(bb_ref[...].reshape(gb * 2 * SSM_STATE, SSM_GROUP))
        bhi_t = expand(bhi, tile_c, 2 * SSM_STATE)
        blo_t = expand(blo, tile_c, 2 * SSM_STATE)

        phi, plo = _split_bf16(apt_ref[...].reshape(gb * 2 * SSM_STATE, SSM_GROUP))
        ap_t = expand(phi, spread_s, 2 * SSM_STATE) + expand(plo, spread_s, 2 * SSM_STATE)
        b_t = bhi_t + blo_t
        pr, pi = ap_t[:, :SSM_STATE], ap_t[:, SSM_STATE:]
        br, bi = b_t[:, :SSM_STATE], b_t[:, SSM_STATE:]
        win_scr[:, 0:SSM_STATE, :] = (pr * br - pi * bi).astype(BF16)
        win_scr[:, SSM_STATE:, :] = (pr * bi + pi * br).astype(BF16)

        bhi_t, blo_t = bhi_t.astype(BF16), blo_t.astype(BF16)
        ahi, alo = _split_bf16(ca)
        kw = (lax.dot_general(ahi, bhi_t, bdims, preferred_element_type=F32)
              + lax.dot_general(ahi, blo_t, bdims, preferred_element_type=F32)
              + lax.dot_general(alo, bhi_t, bdims, preferred_element_type=F32))
        kw0 = (kw[:, 0:SSM_GROUP, :] + jnp.where(diag, dv_ref[...], 0.0)).astype(BF16)
        kw = kw.astype(BF16)
        tk_scr[...] = jnp.zeros_like(tk_scr)
        for s in range(CHUNK):
            lo = SSM_GROUP * s
            hi = lo + SSM_GROUP
            tk_scr[:, lo:hi, lo:hi] = kw0[:, :, lo:hi]
            if hi < CHUNK_ROWS:
                tk_scr[:, hi:CHUNK_ROWS, lo:hi] = kw[:, SSM_GROUP:CHUNK_ROWS - lo, lo:hi]

    nt_dims = (((1,), (1,)), ((), ()))
    half = LANES // 2
    low_half = lax.broadcasted_iota(jnp.int32, (1, LANES), 1) < half

    ar = are_ref[...]
    ai = aim_ref[...]

    def step(hr, hi, n):
        xr = xre_scr[pl.ds(n, 1), :]
        xi = xim_scr[pl.ds(n, 1), :]
        return ar * hr - ai * hi + xr, ar * hi + ai * hr + xi

    def chunk_inputs():
        zt = zt_scr[...]
        yt = lax.dot_general(tk_scr[...], zt, bdims, preferred_element_type=F32)
        xt = lax.dot_general(win_scr[...], zt, bdims, preferred_element_type=F32)
        xre_scr[...] = xt[:, 0:SSM_STATE, :].reshape(rows, LANES).T
        xim_scr[...] = xt[:, SSM_STATE:, :].reshape(rows, LANES).T
        return yt

    def gelu_pieces(yt):
        hp = jnp.concatenate([pre_scr[...].T.reshape(gb, SSM_STATE, LANES),
                              pim_scr[...].T.reshape(gb, SSM_STATE, LANES)], axis=1)
        ycorr = lax.dot_general(cout_scr[...], hp.astype(BF16), bdims,
                                preferred_element_type=F32)
        g = jax.nn.gelu(yt + ycorr)
        return [g[:, SSM_GROUP * t:SSM_GROUP * (t + 1), :].reshape(te, LANES)
                for t in range(CHUNK)]

    def pack_samples(out_ref, block_of, first_lane_of):
        quarter = LANES // 4
        lane_q = lax.broadcasted_iota(jnp.int32, (1, LANES), 1) // quarter
        for c in range(CHUNK // 4):
            out = jnp.zeros((te, LANES), F32)
            for k in range(4):
                t = 4 * c + k
                shift = (quarter * k - first_lane_of(t)) % LANES
                blk = block_of(t)
                out = jnp.where(lane_q == k, pltpu.roll(blk, shift, 1) if shift else blk, out)
            out_ref[:, LANES * c:LANES * (c + 1)] = out.astype(BF16)

    @pl.when(i == 0)
    def _():
        build_operators()
        uz = lax.dot_general(w_scr[...], xsp_ref[...], nt_dims,
                             preferred_element_type=F32)
        sz = jax.nn.silu(uz[te:, :])
        pack_samples(sztsp_ref, lambda t: sz[:, LANES * (t // 2):LANES * (t // 2 + 1)],
                     lambda t: half * (t % 2) + 1)
        for s in range(CHUNK):
            blk = uz[:te, LANES * (s // 2):LANES * (s // 2 + 1)]
            if s % 2:
                blk = pltpu.roll(blk, half, 1)
            zt_scr[:, CHUNK * s:CHUNK * (s + 1), :] = (
                jnp.where(low_half, blk, 0.0).astype(BF16).reshape(gb, SSM_GROUP, LANES))
        yt = chunk_inputs()

        pre_scr[...] = jnp.zeros_like(pre_scr)
        pim_scr[...] = jnp.zeros_like(pim_scr)
        zero = jnp.zeros((1, rows), F32)
        hr, hi = step(zero, zero, 0)
        cre_scr[0:1, :] = hr
        cim_scr[0:1, :] = hi
        for q in range(n_seq):
            hr = sre_ref[q:q + 1, :]
            hi = sim_ref[q:q + 1, :]
            for e in range(2):
                n = 1 + 2 * q + e
                pre_scr[n:n + 1, :] = hr
                pim_scr[n:n + 1, :] = hi
                hr, hi = step(hr, hi, n)
            hsre_ref[q:q + 1, :] = hr
            hsim_ref[q:q + 1, :] = hi

        pieces = gelu_pieces(yt)
        pack_samples(gtsp_ref, lambda t: pieces[t], lambda t: 1)

    @pl.when(i > 0)
    def _():
        uz = lax.dot_general(w_scr[...], xs_ref[...], nt_dims,
                             preferred_element_type=F32)
        szt_ref[...] = jax.nn.silu(uz[te:, :]).astype(BF16)
        ub = uz[:te, :].astype(BF16)
        for s in range(CHUNK):
            zt_scr[:, CHUNK * s:CHUNK * (s + 1), :] = (
                ub[:, LANES * s:LANES * (s + 1)].reshape(gb, SSM_GROUP, LANES))
        yt = chunk_inputs()

        hr, hi = cre_scr[0:1, :], cim_scr[0:1, :]
        for n in range(LANES):
            pre_scr[n:n + 1, :] = hr
            pim_scr[n:n + 1, :] = hi
            hr, hi = step(hr, hi, n)
        cre_scr[0:1, :] = hr
        cim_scr[0:1, :] = hi

        @pl.when(i == n_tiles - 1)
        def _():
            hpre_ref[...] = jnp.broadcast_to(hr, hpre_ref.shape)
            hpim_ref[...] = jnp.broadcast_to(hi, hpim_ref.shape)

        pieces = gelu_pieces(yt)
        for t in range(CHUNK):
            gt_ref[:, LANES * t:LANES * (t + 1)] = pieces[t].astype(BF16)


def _ssm(xs_sp, xs_p, wuz_t, e1, e2, c1, c2, apt, bb, dv, are, aim, sre, sim):
    n_ptiles = xs_p.shape[0]
    n_tiles = 1 + n_ptiles
    te = TE_SSM
    n_j = D_INNER // te
    gb = te // SSM_GROUP
    rows = gb * SSM_STATE
    n_seq = sre.shape[0]
    prev = lambda i: jnp.maximum(i - 1, 0)
    act_shape = jax.ShapeDtypeStruct((n_ptiles, D_INNER, TILE), BF16)
    act_spec = pl.BlockSpec((None, te, TILE), lambda j, i: (prev(i), j, 0))
    assert 2 * n_seq == LANES // 4
    sp_slots = CHUNK * 2 * n_seq
    sp_shape = jax.ShapeDtypeStruct((1, D_INNER, sp_slots), BF16)
    sp_spec = pl.BlockSpec((None, te, sp_slots), lambda j, i: (0, j, 0))
    hs_shape = jax.ShapeDtypeStruct((n_seq, N_STATE_ROWS), F32)
    hs_spec = pl.BlockSpec((n_seq, rows), lambda j, i: (0, j))
    hp_shape = jax.ShapeDtypeStruct((8, N_STATE_ROWS), F32)
    hp_spec = pl.BlockSpec((8, rows), lambda j, i: (0, j))
    row_spec = pl.BlockSpec((1, rows), lambda j, i: (0, j))
    grp_spec = lambda r, c: pl.BlockSpec((gb, r, c), lambda j, i: (j, 0, 0))
    return pl.pallas_call(
        functools.partial(_ssm_kernel, te=te, n_tiles=n_tiles),
        grid=(n_j, n_tiles),
        in_specs=[
            pl.BlockSpec((TILE // 2, D_MODEL), lambda j, i: (0, 0), pipeline_mode=pl.Buffered(1)),
            pl.BlockSpec((None, TILE, D_MODEL), lambda j, i: (prev(i), 0, 0)),
            pl.BlockSpec((te, D_MODEL), lambda j, i: (j, 0)),
            pl.BlockSpec((te, D_MODEL), lambda j, i: (n_j + j, 0)),
            grp_spec(e1.shape[1], 2 * SSM_STATE), grp_spec(e2.shape[1], 2 * SSM_STATE),
            grp_spec(SSM_GROUP, 2 * SSM_STATE), grp_spec(SSM_GROUP, 2 * SSM_STATE),
            grp_spec(2 * SSM_STATE, SSM_GROUP), grp_spec(2 * SSM_STATE, SSM_GROUP),
            grp_spec(SSM_GROUP, 1),
            row_spec, row_spec, hs_spec, hs_spec,
        ],
        out_specs=(sp_spec, sp_spec, act_spec, act_spec, hs_spec, hs_spec, hp_spec, hp_spec),
        out_shape=(sp_shape, sp_shape, act_shape, act_shape, hs_shape, hs_shape,
                   hp_shape, hp_shape),
        scratch_shapes=[pltpu.VMEM((2 * te, D_MODEL), BF16),
                        pltpu.VMEM((gb, CHUNK_ROWS, CHUNK_ROWS), BF16),
                        pltpu.VMEM((gb, 2 * SSM_STATE, CHUNK_ROWS), BF16),
                        pltpu.VMEM((gb, CHUNK_ROWS, 2 * SSM_STATE), BF16),
                        pltpu.VMEM((gb, CHUNK_ROWS, LANES), BF16),
                        pltpu.VMEM((LANES, rows), F32), pltpu.VMEM((LANES, rows), F32),
                        pltpu.VMEM((LANES, rows), F32), pltpu.VMEM((LANES, rows), F32),
                        pltpu.VMEM((8, rows), F32), pltpu.VMEM((8, rows), F32)],
        compiler_params=_params(2, VMEM_LIMIT_LARGE),
        name="s5_scan",
    )(xs_sp, xs_p, wuz_t, wuz_t, e1, e2, c1, c2, apt, bb, dv, are, aim, sre, sim)


def _glu_out_kernel(gt_ref, szt_ref, wg_ref, bg_ref, wo_ref, h_ref, nw_ref, o_ref, acc_scr, *,
                    te, lb):
    e = pl.program_id(2)

    @pl.when(e == 0)
    def _():
        acc_scr[...] = jnp.zeros_like(acc_scr)

    gate = jnp.dot(wg_ref[...], gt_ref[...], preferred_element_type=F32) + bg_ref[...]
    ge = gt_ref[pl.ds(pl.multiple_of(e * te, te), te), :].astype(F32)
    y3 = (ge * jax.nn.sigmoid(gate)) * szt_ref[...].astype(F32)
    acc_scr[...] += jnp.dot(wo_ref[...], y3.astype(BF16), preferred_element_type=F32)

    @pl.when(e == pl.num_programs(2) - 1)
    def _():
        o = acc_scr[...].T
        ln = h_ref.shape[1]
        for t in range(lb // ln):
            h = h_ref[t] + o[ln * t:ln * (t + 1), :]
            o_ref[:, D_MODEL * t:D_MODEL * (t + 1)] = _rms(h, nw_ref[...])


def _glu_out(gt, szt, wg_t, b_glu, wo_t, h_all, norm_w):
    n_tiles, _, lanes, _ = h_all.shape
    te = TE_GLU
    lb = LB_GLU
    n_l = CHUNK * lanes // lb
    n_e = D_INNER // te
    tpb = lb // lanes
    return pl.pallas_call(
        functools.partial(_glu_out_kernel, te=te, lb=lb),
        grid=(n_tiles, n_l, n_e),
        in_specs=[
            pl.BlockSpec((None, D_INNER, lb), lambda i, l, e: (i, 0, l),
                         pipeline_mode=pl.Buffered(1)),
            pl.BlockSpec((None, te, lb), lambda i, l, e: (i, e, l)),
            pl.BlockSpec((te, D_INNER), lambda i, l, e: (e, 0)),
            pl.BlockSpec((te, 1), lambda i, l, e: (e, 0)),
            pl.BlockSpec((D_MODEL, te), lambda i, l, e: (0, e)),
            pl.BlockSpec((None, tpb, lanes, D_MODEL), lambda i, l, e: (i, l, 0, 0)),
            pl.BlockSpec((1, D_MODEL), lambda i, l, e: (0, 0)),
        ],
        out_specs=pl.BlockSpec((lanes, tpb * D_MODEL), lambda i, l, e: (i, l)),
        out_shape=jax.ShapeDtypeStruct((n_tiles * lanes, CHUNK * D_MODEL), F32),
        scratch_shapes=[pltpu.VMEM((D_MODEL, lb), F32)],
        compiler_params=_params(3, VMEM_LIMIT_LARGE),
        name="glu_out",
    )(gt, szt, wg_t, b_glu, wo_t, h_all, norm_w)


def _to_rows_kernel(o_ref, y_ref, slab_scr):
    nc = o_ref.shape[0]
    n_slabs = D_MODEL // LANES
    for t in range(CHUNK):
        for k in range(n_slabs):
            slab_scr[k, pl.ds(t, nc, stride=CHUNK), :] = (
                o_ref[:, D_MODEL * t + LANES * k:D_MODEL * t + LANES * (k + 1)])
    for k in range(n_slabs):
        y_ref[:, LANES * k:LANES * (k + 1)] = slab_scr[k]


def _to_rows(o):
    n_chunks = o.shape[0]
    tm = TM_ROWS
    nc = tm // CHUNK
    return pl.pallas_call(
        _to_rows_kernel,
        grid=(n_chunks // nc,),
        in_specs=[pl.BlockSpec((nc, CHUNK * D_MODEL), lambda i: (i, 0))],
        out_specs=pl.BlockSpec((None, tm, D_MODEL), lambda i: (0, i, 0)),
        out_shape=jax.ShapeDtypeStruct((1, n_chunks * CHUNK, D_MODEL), F32),
        scratch_shapes=[pltpu.VMEM((D_MODEL // LANES, tm, LANES), F32)],
        compiler_params=_params(1),
        name="to_rows",
    )(o)


def _ssm_operators(a_re, a_im, b_re, b_im, c_re, c_im, d_vec, log_dt):
    a_re, a_im = a_re.astype(F32), a_im.astype(F32)
    dt = jnp.exp(log_dt.astype(F32))[:, None]
    th_re = jnp.concatenate([a_re * dt] * 2, axis=1)[:, None, :]
    th_im = jnp.concatenate([a_im * dt] * 2, axis=1)[:, None, :]
    n_tau = 24
    taus = jnp.arange(n_tau, dtype=F32)[None, :, None]
    used = taus <= CHUNK
    mag = jnp.exp(jnp.where(used, th_re * taus, 0.0))
    ang = jnp.where(used, th_im * taus, 0.0)
    ap_re = mag * jnp.cos(ang)
    ap_im = mag * jnp.sin(ang)
    sign = jnp.concatenate([jnp.ones((SSM_STATE,), F32), -jnp.ones((SSM_STATE,), F32)])
    e1 = ap_re * sign
    e2 = -ap_im
    c_re, c_im = c_re.astype(F32), c_im.astype(F32)
    c1 = jnp.concatenate([c_re, c_im], axis=2)
    c2 = jnp.concatenate([c_im, c_re], axis=2)

    ab_re, ab_im = ap_re[:, 1, :SSM_STATE], ap_im[:, 1, :SSM_STATE]
    nr, ni = ab_re - 1.0, ab_im
    den = a_re * a_re + a_im * a_im
    q_re = ((nr * a_re + ni * a_im) / den)[..., None]
    q_im = ((ni * a_re - nr * a_im) / den)[..., None]
    b_re, b_im = b_re.astype(F32), b_im.astype(F32)
    bb = jnp.concatenate([q_re * b_re - q_im * b_im, q_re * b_im + q_im * b_re], axis=1)

    rev = slice(CHUNK - 1, None, -1)
    apt = jnp.concatenate([ap_re[:, rev, :SSM_STATE].transpose(0, 2, 1),
                           ap_im[:, rev, :SSM_STATE].transpose(0, 2, 1)], axis=1)
    dv = d_vec.astype(F32).reshape(N_GROUPS, SSM_GROUP, 1)
    are = ap_re[:, CHUNK, :SSM_STATE].reshape(1, N_STATE_ROWS)
    aim = ap_im[:, CHUNK, :SSM_STATE].reshape(1, N_STATE_ROWS)
    return e1, e2, c1, c2, apt, bb, dv, are, aim


def kernel(x_prompt, x_sample, cache_conv, state_ssm_re, state_ssm_im, meta_tokens, norm_w,
           final_norm_w, conv_w_in, conv_w, conv_b, conv_w_out, ssm_w_in, ssm_a_re, ssm_a_im,
           ssm_b_re, ssm_b_im, ssm_c_re, ssm_c_im, ssm_d, ssm_log_dt, ssm_w_glu, ssm_b_glu,
           ssm_w_out):
    n_seq, seq_len = x_sample.shape[0], x_sample.shape[1]
    n_prompt_rows = x_prompt.shape[1]
    n_ptiles = n_prompt_rows // TILE
    n_sample_rows = n_seq * seq_len
    assert x_prompt.shape[0] == 1 and seq_len == 2 * CHUNK and N_META == CHUNK
    assert n_prompt_rows % TILE == 0 and N_META + n_sample_rows <= SPECIAL_ROWS

    w_out0 = _cast_bf16(conv_w_out.astype(F32), transpose=False)
    wuz_t = _cast_bf16(ssm_w_in.astype(F32), transpose=True)
    wg_t = _cast_bf16(ssm_w_glu.astype(F32), transpose=True)
    wo_t = _cast_bf16(ssm_w_out.astype(F32), transpose=True)
    w_in0 = conv_w_in.astype(F32)
    conv_w8 = jnp.zeros((8, D_INNER), F32).at[0:3].set(conv_w[0].astype(F32))
    conv_b2 = conv_b[0].astype(F32).reshape(1, D_INNER)
    b_glu = ssm_b_glu[0].astype(F32).reshape(D_INNER, 1)
    nw0 = norm_w[0].astype(F32).reshape(1, D_MODEL)
    nw1 = norm_w[1].astype(F32).reshape(1, D_MODEL)
    nwf = final_norm_w.astype(F32).reshape(1, D_MODEL)
    ssm_ops = _ssm_operators(
        ssm_a_re[0], ssm_a_im[0], ssm_b_re[0], ssm_b_im[0], ssm_c_re[0], ssm_c_im[0],
        ssm_d[0], ssm_log_dt[0])
    sre = state_ssm_re[0].astype(F32).reshape(n_seq, N_STATE_ROWS)
    sim = state_ssm_im[0].astype(F32).reshape(n_seq, N_STATE_ROWS)

    xp = x_prompt.astype(F32)
    x_sp = jnp.concatenate([
        meta_tokens.astype(F32), x_sample.astype(F32).reshape(n_sample_rows, D_MODEL),
        jnp.zeros((SPECIAL_ROWS - N_META - n_sample_rows, D_MODEL), F32)], axis=0)[None]

    zeros8 = jnp.zeros((8, D_INNER), F32)
    xn_sp = _rmsnorm(x_sp, nw0, tm=SPECIAL_ROWS)
    xn_p = _rmsnorm(xp, nw0, tm=TM_CONV)
    sp_real = N_META + n_sample_rows
    y_sp, cv_sp = _conv_proj(xn_sp, w_in0, conv_w8, conv_b2, zeros8,
                             cache_conv[0, :, 0, :].astype(F32), cache_conv[0, :, 1, :].astype(F32),
                             real_rows=sp_real)
    init8 = zeros8.at[6:8].set(cv_sp[N_META - 2:N_META])
    y_p, tail_p = _conv_proj(xn_p, w_in0, conv_w8, conv_b2, init8)
    h1_p, xs_p = _out_proj(y_p, w_out0, xp, nw1, lanes=LANES)
    h1_sp, xs_sp = _out_proj(y_sp, w_out0, x_sp, nw1, lanes=SPECIAL_ROWS // CHUNK,
                             real_rows=sp_real)

    gt_sp, szt_sp, gt_p, szt_p, hs_re, hs_im, hp_re, hp_im = _ssm(
        xs_sp.reshape(SPECIAL_ROWS, D_MODEL), xs_p.reshape(n_ptiles, TILE, D_MODEL),
        wuz_t, *ssm_ops, sre, sim)
    n_sc = n_sample_rows // CHUNK
    o_sp = _glu_out(gt_sp, szt_sp, wg_t, b_glu, wo_t, h1_sp[:, :, 1:1 + n_sc], nwf)
    o_p = _glu_out(gt_p, szt_p, wg_t, b_glu, wo_t, h1_p, nwf)

    y_prompt = _to_rows(o_p)
    y_sample = o_sp.reshape(n_seq, seq_len, D_MODEL)
    new_conv_prompt = tail_p[-1, 6:8].reshape(1, 1, 2, D_INNER)
    cv_s = cv_sp[N_META:N_META + n_sample_rows].reshape(n_seq, seq_len, D_INNER)
    new_conv_sample = cv_s[:, seq_len - 2:].reshape(1, n_seq, 2, D_INNER)
    p_shape = (1, 1, N_GROUPS, SSM_STATE)
    s_shape = (1, n_seq, N_GROUPS, SSM_STATE)
    return (y_prompt, y_sample, new_conv_prompt, new_conv_sample,
            hp_re[0].reshape(p_shape), hp_im[0].reshape(p_shape),
            hs_re.reshape(s_shape), hs_im.reshape(s_shape))
```

```python
import functools

import jax
import jax.numpy as jnp
from jax import lax
from jax.experimental import pallas as pl
from jax.experimental.pallas import tpu as pltpu

F32 = jnp.float32
BF16 = jnp.bfloat16

D_MODEL = 2048
D_INNER = 4096
N_META = 16
SSM_GROUP = 16
N_GROUPS = D_INNER // SSM_GROUP
SSM_STATE = 64
N_STATE_ROWS = N_GROUPS * SSM_STATE
RMS_EPS = 1e-6

CHUNK = 16
CHUNK_ROWS = CHUNK * SSM_GROUP
LANES = 128
TILE = CHUNK * LANES
SPECIAL_ROWS = 1024

TM_CONV = 1024
TE_CONV = 512
TM_OUT = 512
TE_SSM = 256
TE_GLU = 512
LB_GLU = 512
TM_ROWS = 1024
PREP_BLOCK = 1024
CAST_ROWS = 512

VMEM_LIMIT = 56 * 1024 * 1024
VMEM_LIMIT_LARGE = 60 * 1024 * 1024


def _params(n_axes, vmem=VMEM_LIMIT):
    return pltpu.CompilerParams(dimension_semantics=("arbitrary",) * n_axes,
                                vmem_limit_bytes=vmem)


def _rms(x, w):
    ms = jnp.mean(x * x, axis=-1, keepdims=True)
    return x * lax.rsqrt(ms + RMS_EPS) * w


def _cast_kernel(x_ref, o_ref):
    o_ref[...] = x_ref[...].astype(BF16)


def _cast_bf16(w):
    _, r, c = w.shape
    b = PREP_BLOCK
    return pl.pallas_call(
        _cast_kernel,
        grid=(r // b, c // b),
        in_specs=[pl.BlockSpec((None, b, b), lambda i, j: (0, i, j))],
        out_specs=pl.BlockSpec((b, b), lambda i, j: (i, j)),
        out_shape=jax.ShapeDtypeStruct((r, c), BF16),
        compiler_params=_params(2),
        name="cast",
    )(w)


def _hosted_cast(w, n_steps, step_of):
    _, r, c = w.shape
    bc = (r * c) // (n_steps * CAST_ROWS)
    nc = c // bc
    assert bc % LANES == 0 and (r // CAST_ROWS) * nc == n_steps
    in_spec = pl.BlockSpec((None, CAST_ROWS, bc),
                           lambda *g: (0, step_of(*g) // nc, step_of(*g) % nc))
    out_spec = pl.BlockSpec((bc, CAST_ROWS), lambda *g: (step_of(*g) % nc, step_of(*g) // nc))
    return in_spec, out_spec, jax.ShapeDtypeStruct((c, r), BF16)


def _hosted_cast_step(src_ref, dst_ref):
    dst_ref[...] = src_ref[...].T.astype(BF16)


def _rmsnorm_kernel(x_ref, w_ref, o_ref):
    o_ref[...] = _rms(x_ref[...], w_ref[...]).astype(BF16)


def _rmsnorm(x, w, *, tm):
    rows = x.shape[1]
    return pl.pallas_call(
        _rmsnorm_kernel, grid=(rows // tm,),
        in_specs=[pl.BlockSpec((None, tm, D_MODEL), lambda i: (0, i, 0)),
                  pl.BlockSpec((1, D_MODEL), lambda i: (0, 0))],
        out_specs=pl.BlockSpec((tm, D_MODEL), lambda i: (i, 0)),
        out_shape=jax.ShapeDtypeStruct((rows, D_MODEL), BF16),
        compiler_params=_params(1), name="rmsnorm",
    )(x, w)


def _conv_proj_kernel(*refs, tm, te, n_pieces, special, real_rows):
    if special:
        (xn_ref, wb_ref, wc_ref, wv_ref, wz_ref, cw_ref, cb_ref, init_ref, c0_ref, c1_ref,
         y_ref, cv_ref, w_scr, s_scr, p1_scr, p2_scr) = refs
    else:
        (xn_ref, wb_ref, wc_ref, wv_ref, wz_ref, cw_ref, cb_ref, init_ref, cast_src_ref,
         y_ref, tail_ref, cast_dst_ref, w_scr, s_scr) = refs
    i = pl.program_id(1)

    @pl.when(i == 0)
    def _():
        for p, w_ref in enumerate((wb_ref, wc_ref, wv_ref, wz_ref)):
            w_scr[:, p * te:(p + 1) * te] = w_ref[...].astype(BF16)
        s_scr[0:8, :] = init_ref[...]

    if special:
        pm = real_rows
        cv_ref[pm:tm, :] = jnp.zeros((tm - pm, te), F32)
        y_ref[pm:tm, :] = jnp.zeros((tm - pm, te), BF16)
    else:
        pm = tm // n_pieces
    for r in range(n_pieces):
        lo = r * pm
        if not special and r == n_pieces - 1:
            _hosted_cast_step(cast_src_ref, cast_dst_ref)
        proj = jnp.dot(xn_ref[lo:lo + pm, :], w_scr[...], preferred_element_type=F32)
        bg = proj[:, 0 * te:1 * te]
        cg = proj[:, 1 * te:2 * te]
        vv = proj[:, 2 * te:3 * te]
        zz = proj[:, 3 * te:4 * te]
        cv = cg * vv
        s_scr[8 + lo:8 + lo + pm, :] = cv
        if special:
            cv_ref[lo:lo + pm, :] = cv
            p1_scr[...] = s_scr[7 + lo:7 + lo + pm, :]
            p2_scr[...] = s_scr[6 + lo:6 + lo + pm, :]
            for q in range(c0_ref.shape[0]):
                r0 = N_META + 32 * q
                p1_scr[r0:r0 + 1, :] = c1_ref[q:q + 1, :]
                p2_scr[r0:r0 + 1, :] = c0_ref[q:q + 1, :]
                p2_scr[r0 + 1:r0 + 2, :] = c1_ref[q:q + 1, :]
            p1 = p1_scr[...]
            p2 = p2_scr[...]
        else:
            p1 = s_scr[7 + lo:7 + lo + pm, :]
            p2 = s_scr[6 + lo:6 + lo + pm, :]
        conv = cb_ref[...] + cw_ref[0:1, :] * p2
        conv = conv + cw_ref[1:2, :] * p1
        conv = conv + cw_ref[2:3, :] * cv
        y_ref[lo:lo + pm, :] = (bg * conv * jax.nn.silu(zz)).astype(BF16)

    if not special:
        tail = s_scr[tm:tm + 8, :]
        s_scr[0:8, :] = tail
        tail_ref[...] = tail


def _conv_proj(xn, w_in, conv_w8, conv_b, init8, cache0=None, cache1=None, real_rows=None,
               cast_w=None):
    rows = xn.shape[0]
    tm, te = TM_CONV, TE_CONV
    n_j = D_INNER // te
    n_i = rows // tm
    special = cache0 is not None
    w_spec = lambda p: pl.BlockSpec((None, D_MODEL, te),
                                    lambda j, i, p=p: (0, 0, p * n_j + j))
    in_specs = [pl.BlockSpec((tm, D_MODEL), lambda j, i: (i, 0)),
                w_spec(0), w_spec(1), w_spec(2), w_spec(3),
                pl.BlockSpec((8, te), lambda j, i: (0, j)),
                pl.BlockSpec((1, te), lambda j, i: (0, j)),
                pl.BlockSpec((8, te), lambda j, i: (0, j))]
    args = [xn, w_in, w_in, w_in, w_in, conv_w8, conv_b, init8]
    scratch = [pltpu.VMEM((D_MODEL, 4 * te), BF16), pltpu.VMEM((tm + 8, te), F32)]
    y_shape = jax.ShapeDtypeStruct((rows, D_INNER), BF16)
    y_spec = pl.BlockSpec((tm, te), lambda j, i: (i, j))
    if special:
        assert n_i == 1
        nq = cache0.shape[0]
        in_specs += [pl.BlockSpec((nq, te), lambda j, i: (0, j)),
                     pl.BlockSpec((nq, te), lambda j, i: (0, j))]
        args += [cache0, cache1]
        out_shape = (y_shape, jax.ShapeDtypeStruct((rows, D_INNER), F32))
        out_specs = (y_spec, pl.BlockSpec((tm, te), lambda j, i: (i, j)))
        scratch += [pltpu.VMEM((real_rows, te), F32), pltpu.VMEM((real_rows, te), F32)]
    else:
        cast_in, cast_out, cast_shape = _hosted_cast(cast_w, n_j * n_i, lambda j, i: j * n_i + i)
        in_specs += [cast_in]
        args += [cast_w]
        out_shape = (y_shape, jax.ShapeDtypeStruct((n_i, 8, D_INNER), F32), cast_shape)
        out_specs = (y_spec, pl.BlockSpec((None, 8, te), lambda j, i: (i, 0, j)), cast_out)
    return pl.pallas_call(
        functools.partial(_conv_proj_kernel, tm=tm, te=te, n_pieces=1 if special else 2,
                          special=special, real_rows=real_rows),
        grid=(n_j, n_i), in_specs=in_specs, out_specs=out_specs, out_shape=out_shape,
        scratch_shapes=scratch, compiler_params=_params(2, VMEM_LIMIT_LARGE),
        name="conv_proj_special" if special else "conv_proj",
    )(*args)


def _out_proj_kernel(y_ref, w_ref, x_ref, nw_ref, h_ref, xs_ref, slab_scr, *, real_rows, n_steps):
    tm = y_ref.shape[0]
    nc = tm // CHUNK
    n_slabs = D_MODEL // LANES

    def block(r):
        if r == 0:
            h = jnp.zeros((tm, D_MODEL), F32)
        else:
            h = x_ref[0:r, :] + jnp.dot(y_ref[0:r, :], w_ref[...], preferred_element_type=F32)
            if r < tm:
                h = jnp.concatenate([h, jnp.zeros((tm - r, D_MODEL), F32)], axis=0)
        for k in range(n_slabs):
            slab_scr[k] = h[:, LANES * k:LANES * (k + 1)]
        for s in range(CHUNK):
            hs = jnp.concatenate(
                [slab_scr[k, pl.ds(s, nc, stride=CHUNK), :] for k in range(n_slabs)], axis=1)
            h_ref[s] = hs
            xs_ref[s] = _rms(hs, nw_ref[...]).astype(BF16)

    if real_rows is None:
        block(tm)
    else:
        i = pl.program_id(0)
        for b in range(n_steps):
            pl.when(i == b)(functools.partial(block, min(max(real_rows - b * tm, 0), tm)))


def _out_proj(y, w_out, x, norm_w, *, lanes, real_rows=None):
    tm = TM_OUT
    nc = tm // CHUNK
    per_tile = lanes // nc
    n_i = x.shape[1] // tm
    tile_map = lambda i: (i // per_tile, 0, i % per_tile, 0)
    tile_shape = (n_i // per_tile, CHUNK, lanes, D_MODEL)
    return pl.pallas_call(
        functools.partial(_out_proj_kernel, real_rows=real_rows, n_steps=n_i),
        grid=(n_i,),
        in_specs=[
            pl.BlockSpec((tm, D_INNER), lambda i: (i, 0)),
            pl.BlockSpec((D_INNER, D_MODEL), lambda i: (0, 0), pipeline_mode=pl.Buffered(1)),
            pl.BlockSpec((None, tm, D_MODEL), lambda i: (0, i, 0)),
            pl.BlockSpec((1, D_MODEL), lambda i: (0, 0)),
        ],
        out_specs=(pl.BlockSpec((None, CHUNK, nc, D_MODEL), tile_map),
                   pl.BlockSpec((None, CHUNK, nc, D_MODEL), tile_map)),
        out_shape=(jax.ShapeDtypeStruct(tile_shape, F32),
                   jax.ShapeDtypeStruct(tile_shape, BF16)),
        scratch_shapes=[pltpu.VMEM((D_MODEL // LANES, tm, LANES), F32)],
        compiler_params=_params(1, VMEM_LIMIT_LARGE),
        name="out_proj",
    )(y, w_out, x, norm_w)


def _split_bf16(x):
    hi = x.astype(BF16)
    return hi, (x - hi.astype(F32)).astype(BF16)


def _ssm_kernel(xsp_ref, xs_ref, wu_ref, wz_ref, e1_ref, e2_ref, c1_ref, c2_ref, apt_ref, bb_ref,
                dv_ref, are_ref, aim_ref, sre_ref, sim_ref, wg_ref, wo_ref,
                gtsp_ref, sztsp_ref, gt_ref, szt_ref, hsre_ref, hsim_ref, hpre_ref, hpim_ref,
                wgt_ref, wot_ref, w_scr, tk_scr, win_scr, cout_scr, zt_scr, xre_scr, xim_scr, pre_scr, pim_scr,
                cre_scr, cim_scr, *, te, n_tiles):
    i = pl.program_id(1)
    gb = te // SSM_GROUP
    rows = gb * SSM_STATE
    n_seq = sre_ref.shape[0]
    bdims = (((2,), (1,)), ((0,), (0,)))

    def build_operators():
        w_scr[0:te, :] = wu_ref[...]
        w_scr[te:2 * te, :] = wz_ref[...]

        c1 = c1_ref[...]
        c2 = c2_ref[...]
        cap = [c1 * e1_ref[:, t:t + 1, :] + c2 * e2_ref[:, t:t + 1, :] for t in range(CHUNK + 1)]
        ca = jnp.concatenate(cap[:CHUNK], axis=1)
        cout_scr[...] = jnp.concatenate(cap[1:], axis=1).astype(BF16)

        lane = lax.broadcasted_iota(jnp.int32, (SSM_GROUP, CHUNK_ROWS), 1)
        row = lax.broadcasted_iota(jnp.int32, (SSM_GROUP, CHUNK_ROWS), 0)
        diag = row == (lane & (SSM_GROUP - 1))
        tile_c = diag.astype(BF16)
        spread_s = (row == lane // SSM_GROUP).astype(BF16)
        expand = lambda b, m, n_rows: jnp.dot(b, m, preferred_element_type=F32).reshape(
            gb, n_rows, CHUNK_ROWS)
        bhi, blo = _split_bf16(bb_ref[...].reshape(gb * 2 * SSM_STATE, SSM_GROUP))
        bhi_t = expand(bhi, tile_c, 2 * SSM_STATE)
        blo_t = expand(blo, tile_c, 2 * SSM_STATE)

        phi, plo = _split_bf16(apt_ref[...].reshape(gb * 2 * SSM_STATE, SSM_GROUP))
        ap_t = expand(phi, spread_s, 2 * SSM_STATE) + expand(plo, spread_s, 2 * SSM_STATE)
        b_t = bhi_t + blo_t
        pr, pi = ap_t[:, :SSM_STATE], ap_t[:, SSM_STATE:]
        br, bi = b_t[:, :SSM_STATE], b_t[:, SSM_STATE:]
        win_scr[:, 0:SSM_STATE, :] = (pr * br - pi * bi).astype(BF16)
        win_scr[:, SSM_STATE:, :] = (pr * bi + pi * br).astype(BF16)

        bhi_t, blo_t = bhi_t.astype(BF16), blo_t.astype(BF16)
        ahi, alo = _split_bf16(ca)
        kw = (lax.dot_general(ahi, bhi_t, bdims, preferred_element_type=F32)
              + lax.dot_general(ahi, blo_t, bdims, preferred_element_type=F32)
              + lax.dot_general(alo, bhi_t, bdims, preferred_element_type=F32))
        kw0 = (kw[:, 0:SSM_GROUP, :] + jnp.where(diag, dv_ref[...], 0.0)).astype(BF16)
        kw = kw.astype(BF16)
        tk_scr[...] = jnp.zeros_like(tk_scr)
        for s in range(CHUNK):
            lo = SSM_GROUP * s
            hi = lo + SSM_GROUP
            tk_scr[:, lo:hi, lo:hi] = kw0[:, :, lo:hi]
            if hi < CHUNK_ROWS:
                tk_scr[:, hi:CHUNK_ROWS, lo:hi] = kw[:, SSM_GROUP:CHUNK_ROWS - lo, lo:hi]

    nt_dims = (((1,), (1,)), ((), ()))
    half = LANES // 2
    low_half = lax.broadcasted_iota(jnp.int32, (1, LANES), 1) < half

    ar = are_ref[...]
    ai = aim_ref[...]

    def step(hr, hi, n):
        xr = xre_scr[pl.ds(n, 1), :]
        xi = xim_scr[pl.ds(n, 1), :]
        return ar * hr - ai * hi + xr, ar * hi + ai * hr + xi

    def chunk_inputs():
        zt = zt_scr[...]
        yt = lax.dot_general(tk_scr[...], zt, bdims, preferred_element_type=F32)
        xt = lax.dot_general(win_scr[...], zt, bdims, preferred_element_type=F32)
        xre_scr[...] = xt[:, 0:SSM_STATE, :].reshape(rows, LANES).T
        xim_scr[...] = xt[:, SSM_STATE:, :].reshape(rows, LANES).T
        return yt

    def gelu_pieces(yt):
        hp = jnp.concatenate([pre_scr[...].T.reshape(gb, SSM_STATE, LANES),
                              pim_scr[...].T.reshape(gb, SSM_STATE, LANES)], axis=1)
        ycorr = lax.dot_general(cout_scr[...], hp.astype(BF16), bdims,
                                preferred_element_type=F32)
        g = jax.nn.gelu(yt + ycorr)
        return [g[:, SSM_GROUP * t:SSM_GROUP * (t + 1), :].reshape(te, LANES)
                for t in range(CHUNK)]

    def pack_samples(out_ref, block_of, first_lane_of):
        quarter = LANES // 4
        lane_q = lax.broadcasted_iota(jnp.int32, (1, LANES), 1) // quarter
        for c in range(CHUNK // 4):
            out = jnp.zeros((te, LANES), F32)
            for k in range(4):
                t = 4 * c + k
                shift = (quarter * k - first_lane_of(t)) % LANES
                blk = block_of(t)
                out = jnp.where(lane_q == k, pltpu.roll(blk, shift, 1) if shift else blk, out)
            out_ref[:, LANES * c:LANES * (c + 1)] = out.astype(BF16)

    @pl.when(i == 0)
    def _():
        build_operators()
        uz = lax.dot_general(w_scr[...], xsp_ref[...], nt_dims,
                             preferred_element_type=F32)
        sz = jax.nn.silu(uz[te:, :])
        pack_samples(sztsp_ref, lambda t: sz[:, LANES * (t // 2):LANES * (t // 2 + 1)],
                     lambda t: half * (t % 2) + 1)
        for s in range(CHUNK):
            blk = uz[:te, LANES * (s // 2):LANES * (s // 2 + 1)]
            if s % 2:
                blk = pltpu.roll(blk, half, 1)
            zt_scr[:, CHUNK * s:CHUNK * (s + 1), :] = (
                jnp.where(low_half, blk, 0.0).astype(BF16).reshape(gb, SSM_GROUP, LANES))
        yt = chunk_inputs()

        pre_scr[...] = jnp.zeros_like(pre_scr)
        pim_scr[...] = jnp.zeros_like(pim_scr)
        zero = jnp.zeros((1, rows), F32)
        hr, hi = step(zero, zero, 0)
        cre_scr[0:1, :] = hr
        cim_scr[0:1, :] = hi
        for q in range(n_seq):
            hr = sre_ref[q:q + 1, :]
            hi = sim_ref[q:q + 1, :]
            for e in range(2):
                n = 1 + 2 * q + e
                pre_scr[n:n + 1, :] = hr
                pim_scr[n:n + 1, :] = hi
                hr, hi = step(hr, hi, n)
            hsre_ref[q:q + 1, :] = hr
            hsim_ref[q:q + 1, :] = hi

        pieces = gelu_pieces(yt)
        pack_samples(gtsp_ref, lambda t: pieces[t], lambda t: 1)

    @pl.when(i > 0)
    def _():
        uz = lax.dot_general(w_scr[...], xs_ref[...], nt_dims,
                             preferred_element_type=F32)
        szt_ref[...] = jax.nn.silu(uz[te:, :]).astype(BF16)
        ub = uz[:te, :].astype(BF16)
        for s in range(CHUNK):
            zt_scr[:, CHUNK * s:CHUNK * (s + 1), :] = (
                ub[:, LANES * s:LANES * (s + 1)].reshape(gb, SSM_GROUP, LANES))
        yt = chunk_inputs()
        _hosted_cast_step(wg_ref, wgt_ref)
        _hosted_cast_step(wo_ref, wot_ref)

        hr, hi = cre_scr[0:1, :], cim_scr[0:1, :]
        for n in range(LANES):
            pre_scr[n:n + 1, :] = hr
            pim_scr[n:n + 1, :] = hi
            hr, hi = step(hr, hi, n)
        cre_scr[0:1, :] = hr
        cim_scr[0:1, :] = hi

        @pl.when(i == n_tiles - 1)
        def _():
            hpre_ref[...] = jnp.broadcast_to(hr, hpre_ref.shape)
            hpim_ref[...] = jnp.broadcast_to(hi, hpim_ref.shape)

        pieces = gelu_pieces(yt)
        for t in range(CHUNK):
            gt_ref[:, LANES * t:LANES * (t + 1)] = pieces[t].astype(BF16)


def _ssm(xs_sp, xs_p, wuz_t, e1, e2, c1, c2, apt, bb, dv, are, aim, sre, sim, w_glu, w_out):
    n_ptiles = xs_p.shape[0]
    n_tiles = 1 + n_ptiles
    te = TE_SSM
    n_j = D_INNER // te
    gb = te // SSM_GROUP
    rows = gb * SSM_STATE
    n_seq = sre.shape[0]
    prev = lambda i: jnp.maximum(i - 1, 0)
    act_shape = jax.ShapeDtypeStruct((n_ptiles, D_INNER, TILE), BF16)
    act_spec = pl.BlockSpec((None, te, TILE), lambda j, i: (prev(i), j, 0))
    assert 2 * n_seq == LANES // 4
    sp_slots = CHUNK * 2 * n_seq
    sp_shape = jax.ShapeDtypeStruct((1, D_INNER, sp_slots), BF16)
    sp_spec = pl.BlockSpec((None, te, sp_slots), lambda j, i: (0, j, 0))
    hs_shape = jax.ShapeDtypeStruct((n_seq, N_STATE_ROWS), F32)
    hs_spec = pl.BlockSpec((n_seq, rows), lambda j, i: (0, j))
    hp_shape = jax.ShapeDtypeStruct((8, N_STATE_ROWS), F32)
    hp_spec = pl.BlockSpec((8, rows), lambda j, i: (0, j))
    row_spec = pl.BlockSpec((1, rows), lambda j, i: (0, j))
    grp_spec = lambda r, c: pl.BlockSpec((gb, r, c), lambda j, i: (j, 0, 0))
    prompt_step = lambda j, i: j * n_ptiles + prev(i)
    wg_in, wg_out, wg_shape = _hosted_cast(w_glu, n_j * n_ptiles, prompt_step)
    wo_in, wo_out, wo_shape = _hosted_cast(w_out, n_j * n_ptiles, prompt_step)
    return pl.pallas_call(
        functools.partial(_ssm_kernel, te=te, n_tiles=n_tiles),
        grid=(n_j, n_tiles),
        in_specs=[
            pl.BlockSpec((TILE // 2, D_MODEL), lambda j, i: (0, 0), pipeline_mode=pl.Buffered(1)),
            pl.BlockSpec((None, TILE, D_MODEL), lambda j, i: (prev(i), 0, 0)),
            pl.BlockSpec((te, D_MODEL), lambda j, i: (j, 0)),
            pl.BlockSpec((te, D_MODEL), lambda j, i: (n_j + j, 0)),
            grp_spec(e1.shape[1], 2 * SSM_STATE), grp_spec(e2.shape[1], 2 * SSM_STATE),
            grp_spec(SSM_GROUP, 2 * SSM_STATE), grp_spec(SSM_GROUP, 2 * SSM_STATE),
            grp_spec(2 * SSM_STATE, SSM_GROUP), grp_spec(2 * SSM_STATE, SSM_GROUP),
            grp_spec(SSM_GROUP, 1),
            row_spec, row_spec, hs_spec, hs_spec, wg_in, wo_in,
        ],
        out_specs=(sp_spec, sp_spec, act_spec, act_spec, hs_spec, hs_spec, hp_spec, hp_spec,
                   wg_out, wo_out),
        out_shape=(sp_shape, sp_shape, act_shape, act_shape, hs_shape, hs_shape,
                   hp_shape, hp_shape, wg_shape, wo_shape),
        scratch_shapes=[pltpu.VMEM((2 * te, D_MODEL), BF16),
                        pltpu.VMEM((gb, CHUNK_ROWS, CHUNK_ROWS), BF16),
                        pltpu.VMEM((gb, 2 * SSM_STATE, CHUNK_ROWS), BF16),
                        pltpu.VMEM((gb, CHUNK_ROWS, 2 * SSM_STATE), BF16),
                        pltpu.VMEM((gb, CHUNK_ROWS, LANES), BF16),
                        pltpu.VMEM((LANES, rows), F32), pltpu.VMEM((LANES, rows), F32),
                        pltpu.VMEM((LANES, rows), F32), pltpu.VMEM((LANES, rows), F32),
                        pltpu.VMEM((8, rows), F32), pltpu.VMEM((8, rows), F32)],
        compiler_params=_params(2, VMEM_LIMIT_LARGE),
        name="s5_scan",
    )(xs_sp, xs_p, wuz_t, wuz_t, e1, e2, c1, c2, apt, bb, dv, are, aim, sre, sim, w_glu, w_out)


def _glu_out_kernel(gt_ref, szt_ref, wg_ref, bg_ref, wo_ref, h_ref, nw_ref, o_ref, acc_scr, *,
                    te, lb):
    e = pl.program_id(2)

    @pl.when(e == 0)
    def _():
        acc_scr[...] = jnp.zeros_like(acc_scr)

    gate = jnp.dot(wg_ref[...], gt_ref[...], preferred_element_type=F32) + bg_ref[...]
    ge = gt_ref[pl.ds(pl.multiple_of(e * te, te), te), :].astype(F32)
    y3 = (ge * jax.nn.sigmoid(gate)) * szt_ref[...].astype(F32)
    acc_scr[...] += jnp.dot(wo_ref[...], y3.astype(BF16), preferred_element_type=F32)

    @pl.when(e == pl.num_programs(2) - 1)
    def _():
        o = acc_scr[...].T
        ln = h_ref.shape[1]
        for t in range(lb // ln):
            h = h_ref[t] + o[ln * t:ln * (t + 1), :]
            o_ref[:, D_MODEL * t:D_MODEL * (t + 1)] = _rms(h, nw_ref[...])


def _glu_out(gt, szt, wg_t, b_glu, wo_t, h_all, norm_w):
    n_tiles, _, lanes, _ = h_all.shape
    te = TE_GLU
    lb = LB_GLU
    n_l = CHUNK * lanes // lb
    n_e = D_INNER // te
    tpb = lb // lanes
    return pl.pallas_call(
        functools.partial(_glu_out_kernel, te=te, lb=lb),
        grid=(n_tiles, n_l, n_e),
        in_specs=[
            pl.BlockSpec((None, D_INNER, lb), lambda i, l, e: (i, 0, l)),
            pl.BlockSpec((None, te, lb), lambda i, l, e: (i, e, l)),
            pl.BlockSpec((te, D_INNER), lambda i, l, e: (e, 0)),
            pl.BlockSpec((te, 1), lambda i, l, e: (e, 0)),
            pl.BlockSpec((D_MODEL, te), lambda i, l, e: (0, e)),
            pl.BlockSpec((None, tpb, lanes, D_MODEL), lambda i, l, e: (i, l, 0, 0)),
            pl.BlockSpec((1, D_MODEL), lambda i, l, e: (0, 0)),
        ],
        out_specs=pl.BlockSpec((lanes, tpb * D_MODEL), lambda i, l, e: (i, l)),
        out_shape=jax.ShapeDtypeStruct((n_tiles * lanes, CHUNK * D_MODEL), F32),
        scratch_shapes=[pltpu.VMEM((D_MODEL, lb), F32)],
        compiler_params=_params(3),
        name="glu_out",
    )(gt, szt, wg_t, b_glu, wo_t, h_all, norm_w)


def _to_rows_kernel(o_ref, y_ref, slab_scr):
    nc = o_ref.shape[0]
    n_slabs = D_MODEL // LANES
    for t in range(CHUNK):
        for k in range(n_slabs):
            slab_scr[k, pl.ds(t, nc, stride=CHUNK), :] = (
                o_ref[:, D_MODEL * t + LANES * k:D_MODEL * t + LANES * (k + 1)])
    for k in range(n_slabs):
        y_ref[:, LANES * k:LANES * (k + 1)] = slab_scr[k]


def _to_rows(o):
    n_chunks = o.shape[0]
    tm = TM_ROWS
    nc = tm // CHUNK
    return pl.pallas_call(
        _to_rows_kernel,
        grid=(n_chunks // nc,),
        in_specs=[pl.BlockSpec((nc, CHUNK * D_MODEL), lambda i: (i, 0))],
        out_specs=pl.BlockSpec((None, tm, D_MODEL), lambda i: (0, i, 0)),
        out_shape=jax.ShapeDtypeStruct((1, n_chunks * CHUNK, D_MODEL), F32),
        scratch_shapes=[pltpu.VMEM((D_MODEL // LANES, tm, LANES), F32)],
        compiler_params=_params(1),
        name="to_rows",
    )(o)


def _ssm_operators(a_re, a_im, b_re, b_im, c_re, c_im, d_vec, log_dt):
    a_re, a_im = a_re.astype(F32), a_im.astype(F32)
    dt = jnp.exp(log_dt.astype(F32))[:, None]
    th_re = jnp.concatenate([a_re * dt] * 2, axis=1)[:, None, :]
    th_im = jnp.concatenate([a_im * dt] * 2, axis=1)[:, None, :]
    n_tau = 24
    taus = jnp.arange(n_tau, dtype=F32)[None, :, None]
    used = taus <= CHUNK
    mag = jnp.exp(jnp.where(used, th_re * taus, 0.0))
    ang = jnp.where(used, th_im * taus, 0.0)
    ap_re = mag * jnp.cos(ang)
    ap_im = mag * jnp.sin(ang)
    sign = jnp.concatenate([jnp.ones((SSM_STATE,), F32), -jnp.ones((SSM_STATE,), F32)])
    e1 = ap_re * sign
    e2 = -ap_im
    c_re, c_im = c_re.astype(F32), c_im.astype(F32)
    c1 = jnp.concatenate([c_re, c_im], axis=2)
    c2 = jnp.concatenate([c_im, c_re], axis=2)

    ab_re, ab_im = ap_re[:, 1, :SSM_STATE], ap_im[:, 1, :SSM_STATE]
    nr, ni = ab_re - 1.0, ab_im
    den = a_re * a_re + a_im * a_im
    q_re = ((nr * a_re + ni * a_im) / den)[..., None]
    q_im = ((ni * a_re - nr * a_im) / den)[..., None]
    b_re, b_im = b_re.astype(F32), b_im.astype(F32)
    bb = jnp.concatenate([q_re * b_re - q_im * b_im, q_re * b_im + q_im * b_re], axis=1)

    rev = slice(CHUNK - 1, None, -1)
    apt = jnp.concatenate([ap_re[:, rev, :SSM_STATE].transpose(0, 2, 1),
                           ap_im[:, rev, :SSM_STATE].transpose(0, 2, 1)], axis=1)
    dv = d_vec.astype(F32).reshape(N_GROUPS, SSM_GROUP, 1)
    are = ap_re[:, CHUNK, :SSM_STATE].reshape(1, N_STATE_ROWS)
    aim = ap_im[:, CHUNK, :SSM_STATE].reshape(1, N_STATE_ROWS)
    return e1, e2, c1, c2, apt, bb, dv, are, aim


def kernel(x_prompt, x_sample, cache_conv, state_ssm_re, state_ssm_im, meta_tokens, norm_w,
           final_norm_w, conv_w_in, conv_w, conv_b, conv_w_out, ssm_w_in, ssm_a_re, ssm_a_im,
           ssm_b_re, ssm_b_im, ssm_c_re, ssm_c_im, ssm_d, ssm_log_dt, ssm_w_glu, ssm_b_glu,
           ssm_w_out):
    n_seq, seq_len = x_sample.shape[0], x_sample.shape[1]
    n_prompt_rows = x_prompt.shape[1]
    n_ptiles = n_prompt_rows // TILE
    n_sample_rows = n_seq * seq_len
    assert x_prompt.shape[0] == 1 and seq_len == 2 * CHUNK and N_META == CHUNK
    assert n_prompt_rows % TILE == 0 and N_META + n_sample_rows <= SPECIAL_ROWS

    w_out0 = _cast_bf16(conv_w_out.astype(F32))
    w_in0 = conv_w_in.astype(F32)
    conv_w8 = jnp.zeros((8, D_INNER), F32).at[0:3].set(conv_w[0].astype(F32))
    conv_b2 = conv_b[0].astype(F32).reshape(1, D_INNER)
    b_glu = ssm_b_glu[0].astype(F32).reshape(D_INNER, 1)
    nw0 = norm_w[0].astype(F32).reshape(1, D_MODEL)
    nw1 = norm_w[1].astype(F32).reshape(1, D_MODEL)
    nwf = final_norm_w.astype(F32).reshape(1, D_MODEL)
    ssm_ops = _ssm_operators(
        ssm_a_re[0], ssm_a_im[0], ssm_b_re[0], ssm_b_im[0], ssm_c_re[0], ssm_c_im[0],
        ssm_d[0], ssm_log_dt[0])
    sre = state_ssm_re[0].astype(F32).reshape(n_seq, N_STATE_ROWS)
    sim = state_ssm_im[0].astype(F32).reshape(n_seq, N_STATE_ROWS)

    xp = x_prompt.astype(F32)
    x_sp = jnp.concatenate([
        meta_tokens.astype(F32), x_sample.astype(F32).reshape(n_sample_rows, D_MODEL),
        jnp.zeros((SPECIAL_ROWS - N_META - n_sample_rows, D_MODEL), F32)], axis=0)[None]

    zeros8 = jnp.zeros((8, D_INNER), F32)
    xn_sp = _rmsnorm(x_sp, nw0, tm=SPECIAL_ROWS)
    xn_p = _rmsnorm(xp, nw0, tm=TM_CONV)
    sp_real = N_META + n_sample_rows
    y_sp, cv_sp = _conv_proj(xn_sp, w_in0, conv_w8, conv_b2, zeros8,
                             cache_conv[0, :, 0, :].astype(F32), cache_conv[0, :, 1, :].astype(F32),
                             real_rows=sp_real)
    init8 = zeros8.at[6:8].set(cv_sp[N_META - 2:N_META])
    y_p, tail_p, wuz_t = _conv_proj(xn_p, w_in0, conv_w8, conv_b2, init8,
                                    cast_w=ssm_w_in.astype(F32))
    h1_p, xs_p = _out_proj(y_p, w_out0, xp, nw1, lanes=LANES)
    h1_sp, xs_sp = _out_proj(y_sp, w_out0, x_sp, nw1, lanes=SPECIAL_ROWS // CHUNK,
                             real_rows=sp_real)

    gt_sp, szt_sp, gt_p, szt_p, hs_re, hs_im, hp_re, hp_im, wg_t, wo_t = _ssm(
        xs_sp.reshape(SPECIAL_ROWS, D_MODEL), xs_p.reshape(n_ptiles, TILE, D_MODEL),
        wuz_t, *ssm_ops, sre, sim, ssm_w_glu.astype(F32), ssm_w_out.astype(F32))
    n_sc = n_sample_rows // CHUNK
    o_sp = _glu_out(gt_sp, szt_sp, wg_t, b_glu, wo_t, h1_sp[:, :, 1:1 + n_sc], nwf)
    o_p = _glu_out(gt_p, szt_p, wg_t, b_glu, wo_t, h1_p, nwf)

    y_prompt = _to_rows(o_p)
    y_sample = o_sp.reshape(n_seq, seq_len, D_MODEL)
    new_conv_prompt = tail_p[-1, 6:8].reshape(1, 1, 2, D_INNER)
    cv_s = cv_sp[N_META:N_META + n_sample_rows].reshape(n_seq, seq_len, D_INNER)
    new_conv_sample = cv_s[:, seq_len - 2:].reshape(1, n_seq, 2, D_INNER)
    p_shape = (1, 1, N_GROUPS, SSM_STATE)
    s_shape = (1, n_seq, N_GROUPS, SSM_STATE)
    return (y_prompt, y_sample, new_conv_prompt, new_conv_sample,
            hp_re[0].reshape(p_shape), hp_im[0].reshape(p_shape),
            hs_re.reshape(s_shape), hs_im.reshape(s_shape))
```

```python
import functools

import jax
import jax.numpy as jnp
from jax import lax
from jax.experimental import pallas as pl
from jax.experimental.pallas import tpu as pltpu

F32 = jnp.float32
BF16 = jnp.bfloat16

D_MODEL = 2048
D_INNER = 4096
N_META = 16
SSM_GROUP = 16
N_GROUPS = D_INNER // SSM_GROUP
SSM_STATE = 64
N_STATE_ROWS = N_GROUPS * SSM_STATE
RMS_EPS = 1e-6

CHUNK = 16
CHUNK_ROWS = CHUNK * SSM_GROUP
LANES = 128
TILE = CHUNK * LANES
SPECIAL_ROWS = 1024

TM_CONV = 1024
TE_CONV = 512
TM_OUT = 512
TE_SSM = 256
TE_GLU = 512
LB_GLU = 512
TM_ROWS = 1024
CAST_ROWS = 512

VMEM_LIMIT = 56 * 1024 * 1024
VMEM_LIMIT_LARGE = 60 * 1024 * 1024


def _params(n_axes, vmem=VMEM_LIMIT):
    return pltpu.CompilerParams(dimension_semantics=("arbitrary",) * n_axes,
                                vmem_limit_bytes=vmem)


def _rms(x, w):
    ms = jnp.mean(x * x, axis=-1, keepdims=True)
    return x * lax.rsqrt(ms + RMS_EPS) * w


def _hosted_cast(w, n_steps, step_of, *, transpose):
    _, r, c = w.shape
    bc = (r * c) // (n_steps * CAST_ROWS)
    nc = c // bc
    assert bc % LANES == 0 and (r // CAST_ROWS) * nc == n_steps
    in_spec = pl.BlockSpec((None, CAST_ROWS, bc),
                           lambda *g: (0, step_of(*g) // nc, step_of(*g) % nc))
    if transpose:
        out_spec = pl.BlockSpec((bc, CAST_ROWS),
                                lambda *g: (step_of(*g) % nc, step_of(*g) // nc))
        return in_spec, out_spec, jax.ShapeDtypeStruct((c, r), BF16)
    out_spec = pl.BlockSpec((CAST_ROWS, bc), lambda *g: (step_of(*g) // nc, step_of(*g) % nc))
    return in_spec, out_spec, jax.ShapeDtypeStruct((r, c), BF16)


def _hosted_cast_step(src_ref, dst_ref, *, transpose):
    x = src_ref[...]
    dst_ref[...] = (x.T if transpose else x).astype(BF16)


def _rmsnorm_kernel(x_ref, w_ref, o_ref):
    o_ref[...] = _rms(x_ref[...], w_ref[...]).astype(BF16)


def _rmsnorm(x, w, *, tm):
    rows = x.shape[1]
    return pl.pallas_call(
        _rmsnorm_kernel, grid=(rows // tm,),
        in_specs=[pl.BlockSpec((None, tm, D_MODEL), lambda i: (0, i, 0)),
                  pl.BlockSpec((1, D_MODEL), lambda i: (0, 0))],
        out_specs=pl.BlockSpec((tm, D_MODEL), lambda i: (i, 0)),
        out_shape=jax.ShapeDtypeStruct((rows, D_MODEL), BF16),
        compiler_params=_params(1), name="rmsnorm",
    )(x, w)


def _conv_proj_kernel(*refs, tm, te, n_pieces, special, real_rows):
    if special:
        (xn_ref, wb_ref, wc_ref, wv_ref, wz_ref, cw_ref, cb_ref, init_ref, c0_ref, c1_ref,
         y_ref, cv_ref, w_scr, s_scr, p1_scr, p2_scr) = refs
    else:
        (xn_ref, wb_ref, wc_ref, wv_ref, wz_ref, cw_ref, cb_ref, init_ref, wp_ref, wt_ref,
         y_ref, tail_ref, wp_bf_ref, wt_bf_ref, w_scr, s_scr) = refs
    i = pl.program_id(1)

    @pl.when(i == 0)
    def _():
        for p, w_ref in enumerate((wb_ref, wc_ref, wv_ref, wz_ref)):
            w_scr[:, p * te:(p + 1) * te] = w_ref[...].astype(BF16)
        s_scr[0:8, :] = init_ref[...]

    if special:
        pm = real_rows
        cv_ref[pm:tm, :] = jnp.zeros((tm - pm, te), F32)
        y_ref[pm:tm, :] = jnp.zeros((tm - pm, te), BF16)
    else:
        pm = tm // n_pieces
    for r in range(n_pieces):
        lo = r * pm
        if not special and r == n_pieces - 1:
            _hosted_cast_step(wp_ref, wp_bf_ref, transpose=False)
            _hosted_cast_step(wt_ref, wt_bf_ref, transpose=True)
        proj = jnp.dot(xn_ref[lo:lo + pm, :], w_scr[...], preferred_element_type=F32)
        bg = proj[:, 0 * te:1 * te]
        cg = proj[:, 1 * te:2 * te]
        vv = proj[:, 2 * te:3 * te]
        zz = proj[:, 3 * te:4 * te]
        cv = cg * vv
        s_scr[8 + lo:8 + lo + pm, :] = cv
        if special:
            cv_ref[lo:lo + pm, :] = cv
            p1_scr[...] = s_scr[7 + lo:7 + lo + pm, :]
            p2_scr[...] = s_scr[6 + lo:6 + lo + pm, :]
            for q in range(c0_ref.shape[0]):
                r0 = N_META + 32 * q
                p1_scr[r0:r0 + 1, :] = c1_ref[q:q + 1, :]
                p2_scr[r0:r0 + 1, :] = c0_ref[q:q + 1, :]
                p2_scr[r0 + 1:r0 + 2, :] = c1_ref[q:q + 1, :]
            p1 = p1_scr[...]
            p2 = p2_scr[...]
        else:
            p1 = s_scr[7 + lo:7 + lo + pm, :]
            p2 = s_scr[6 + lo:6 + lo + pm, :]
        conv = cb_ref[...] + cw_ref[0:1, :] * p2
        conv = conv + cw_ref[1:2, :] * p1
        conv = conv + cw_ref[2:3, :] * cv
        y_ref[lo:lo + pm, :] = (bg * conv * jax.nn.silu(zz)).astype(BF16)

    if not special:
        tail = s_scr[tm:tm + 8, :]
        s_scr[0:8, :] = tail
        tail_ref[...] = tail


def _conv_proj(xn, w_in, conv_w8, conv_b, init8, cache0=None, cache1=None, real_rows=None,
               cast_w=None, cast_w_t=None):
    rows = xn.shape[0]
    tm, te = TM_CONV, TE_CONV
    n_j = D_INNER // te
    n_i = rows // tm
    special = cache0 is not None
    w_spec = lambda p: pl.BlockSpec((None, D_MODEL, te),
                                    lambda j, i, p=p: (0, 0, p * n_j + j))
    in_specs = [pl.BlockSpec((tm, D_MODEL), lambda j, i: (i, 0)),
                w_spec(0), w_spec(1), w_spec(2), w_spec(3),
                pl.BlockSpec((8, te), lambda j, i: (0, j)),
                pl.BlockSpec((1, te), lambda j, i: (0, j)),
                pl.BlockSpec((8, te), lambda j, i: (0, j))]
    args = [xn, w_in, w_in, w_in, w_in, conv_w8, conv_b, init8]
    scratch = [pltpu.VMEM((D_MODEL, 4 * te), BF16), pltpu.VMEM((tm + 8, te), F32)]
    y_shape = jax.ShapeDtypeStruct((rows, D_INNER), BF16)
    y_spec = pl.BlockSpec((tm, te), lambda j, i: (i, j))
    if special:
        assert n_i == 1
        nq = cache0.shape[0]
        in_specs += [pl.BlockSpec((nq, te), lambda j, i: (0, j)),
                     pl.BlockSpec((nq, te), lambda j, i: (0, j))]
        args += [cache0, cache1]
        out_shape = (y_shape, jax.ShapeDtypeStruct((rows, D_INNER), F32))
        out_specs = (y_spec, pl.BlockSpec((tm, te), lambda j, i: (i, j)))
        scratch += [pltpu.VMEM((real_rows, te), F32), pltpu.VMEM((real_rows, te), F32)]
    else:
        step_of = lambda j, i: j * n_i + i
        wp_in, wp_out, wp_shape = _hosted_cast(cast_w, n_j * n_i, step_of, transpose=False)
        wt_in, wt_out, wt_shape = _hosted_cast(cast_w_t, n_j * n_i, step_of, transpose=True)
        in_specs += [wp_in, wt_in]
        args += [cast_w, cast_w_t]
        out_shape = (y_shape, jax.ShapeDtypeStruct((n_i, 8, D_INNER), F32), wp_shape, wt_shape)
        out_specs = (y_spec, pl.BlockSpec((None, 8, te), lambda j, i: (i, 0, j)), wp_out, wt_out)
    return pl.pallas_call(
        functools.partial(_conv_proj_kernel, tm=tm, te=te, n_pieces=1 if special else 4,
                          special=special, real_rows=real_rows),
        grid=(n_j, n_i), in_specs=in_specs, out_specs=out_specs, out_shape=out_shape,
        scratch_shapes=scratch, compiler_params=_params(2, VMEM_LIMIT_LARGE),
        name="conv_proj_special" if special else "conv_proj",
    )(*args)


def _out_proj_kernel(y_ref, w_ref, x_ref, nw_ref, h_ref, xs_ref, slab_scr, *, real_rows, n_steps):
    tm = y_ref.shape[0]
    nc = tm // CHUNK
    n_slabs = D_MODEL // LANES

    def block(r):
        if r == 0:
            h = jnp.zeros((tm, D_MODEL), F32)
        else:
            h = x_ref[0:r, :] + jnp.dot(y_ref[0:r, :], w_ref[...], preferred_element_type=F32)
            if r < tm:
                h = jnp.concatenate([h, jnp.zeros((tm - r, D_MODEL), F32)], axis=0)
        for k in range(n_slabs):
            slab_scr[k] = h[:, LANES * k:LANES * (k + 1)]
        for s in range(CHUNK):
            hs = jnp.concatenate(
                [slab_scr[k, pl.ds(s, nc, stride=CHUNK), :] for k in range(n_slabs)], axis=1)
            h_ref[s] = hs
            xs_ref[s] = _rms(hs, nw_ref[...]).astype(BF16)

    if real_rows is None:
        block(tm)
    else:
        i = pl.program_id(0)
        for b in range(n_steps):
            pl.when(i == b)(functools.partial(block, min(max(real_rows - b * tm, 0), tm)))


def _out_proj(y, w_out, x, norm_w, *, lanes, real_rows=None):
    tm = TM_OUT
    nc = tm // CHUNK
    per_tile = lanes // nc
    n_i = x.shape[1] // tm
    tile_map = lambda i: (i // per_tile, 0, i % per_tile, 0)
    tile_shape = (n_i // per_tile, CHUNK, lanes, D_MODEL)
    return pl.pallas_call(
        functools.partial(_out_proj_kernel, real_rows=real_rows, n_steps=n_i),
        grid=(n_i,),
        in_specs=[
            pl.BlockSpec((tm, D_INNER), lambda i: (i, 0)),
            pl.BlockSpec((D_INNER, D_MODEL), lambda i: (0, 0), pipeline_mode=pl.Buffered(1)),
            pl.BlockSpec((None, tm, D_MODEL), lambda i: (0, i, 0)),
            pl.BlockSpec((1, D_MODEL), lambda i: (0, 0)),
        ],
        out_specs=(pl.BlockSpec((None, CHUNK, nc, D_MODEL), tile_map),
                   pl.BlockSpec((None, CHUNK, nc, D_MODEL), tile_map)),
        out_shape=(jax.ShapeDtypeStruct(tile_shape, F32),
                   jax.ShapeDtypeStruct(tile_shape, BF16)),
        scratch_shapes=[pltpu.VMEM((D_MODEL // LANES, tm, LANES), F32)],
        compiler_params=_params(1, VMEM_LIMIT_LARGE),
        name="out_proj",
    )(y, w_out, x, norm_w)


def _split_bf16(x):
    hi = x.astype(BF16)
    return hi, (x - hi.astype(F32)).astype(BF16)


def _ssm_kernel(xsp_ref, xs_ref, wu_ref, wz_ref, e1_ref, e2_ref, c1_ref, c2_ref, apt_ref, bb_ref,
                dv_ref, are_ref, aim_ref, sre_ref, sim_ref, wg_ref, wo_ref,
                gtsp_ref, sztsp_ref, gt_ref, szt_ref, hsre_ref, hsim_ref, hpre_ref, hpim_ref,
                wgt_ref, wot_ref, w_scr, tk_scr, win_scr, cout_scr, zt_scr, xre_scr, xim_scr, pre_scr, pim_scr,
                cre_scr, cim_scr, *, te, n_tiles):
    i = pl.program_id(1)
    gb = te // SSM_GROUP
    rows = gb * SSM_STATE
    n_seq = sre_ref.shape[0]
    bdims = (((2,), (1,)), ((0,), (0,)))

    def build_operators():
        w_scr[0:te, :] = wu_ref[...]
        w_scr[te:2 * te, :] = wz_ref[...]

        c1 = c1_ref[...]
        c2 = c2_ref[...]
        cap = [c1 * e1_ref[:, t:t + 1, :] + c2 * e2_ref[:, t:t + 1, :] for t in range(CHUNK + 1)]
        ca = jnp.concatenate(cap[:CHUNK], axis=1)
        cout_scr[...] = jnp.concatenate(cap[1:], axis=1).astype(BF16)

        lane = lax.broadcasted_iota(jnp.int32, (SSM_GROUP, CHUNK_ROWS), 1)
        row = lax.broadcasted_iota(jnp.int32, (SSM_GROUP, CHUNK_ROWS), 0)
        diag = row == (lane & (SSM_GROUP - 1))
        tile_c = diag.astype(BF16)
        spread_s = (row == lane // SSM_GROUP).astype(BF16)
        expand = lambda b, m, n_rows: jnp.dot(b, m, preferred_element_type=F32).reshape(
            gb, n_rows, CHUNK_ROWS)
        bhi, blo = _split_bf16(bb_ref[...].reshape(gb * 2 * SSM_STATE, SSM_GROUP))
        bhi_t = expand(bhi, tile_c, 2 * SSM_STATE)
        blo_t = expand(blo, tile_c, 2 * SSM_STATE)

        phi, plo = _split_bf16(apt_ref[...].reshape(gb * 2 * SSM_STATE, SSM_GROUP))
        ap_t = expand(phi, spread_s, 2 * SSM_STATE) + expand(plo, spread_s, 2 * SSM_STATE)
        b_t = bhi_t + blo_t
        pr, pi = ap_t[:, :SSM_STATE], ap_t[:, SSM_STATE:]
        br, bi = b_t[:, :SSM_STATE], b_t[:, SSM_STATE:]
        win_scr[:, 0:SSM_STATE, :] = (pr * br - pi * bi).astype(BF16)
        win_scr[:, SSM_STATE:, :] = (pr * bi + pi * br).astype(BF16)

        bhi_t, blo_t = bhi_t.astype(BF16), blo_t.astype(BF16)
        ahi, alo = _split_bf16(ca)
        kw = (lax.dot_general(ahi, bhi_t, bdims, preferred_element_type=F32)
              + lax.dot_general(ahi, blo_t, bdims, preferred_element_type=F32)
              + lax.dot_general(alo, bhi_t, bdims, preferred_element_type=F32))
        kw0 = (kw[:, 0:SSM_GROUP, :] + jnp.where(diag, dv_ref[...], 0.0)).astype(BF16)
        kw = kw.astype(BF16)
        tk_scr[...] = jnp.zeros_like(tk_scr)
        for s in range(CHUNK):
            lo = SSM_GROUP * s
            hi = lo + SSM_GROUP
            tk_scr[:, lo:hi, lo:hi] = kw0[:, :, lo:hi]
            if hi < CHUNK_ROWS:
                tk_scr[:, hi:CHUNK_ROWS, lo:hi] = kw[:, SSM_GROUP:CHUNK_ROWS - lo, lo:hi]

    nt_dims = (((1,), (1,)), ((), ()))
    half = LANES // 2
    low_half = lax.broadcasted_iota(jnp.int32, (1, LANES), 1) < half

    ar = are_ref[...]
    ai = aim_ref[...]

    def step(hr, hi, n):
        xr = xre_scr[pl.ds(n, 1), :]
        xi = xim_scr[pl.ds(n, 1), :]
        return ar * hr - ai * hi + xr, ar * hi + ai * hr + xi

    def chunk_inputs():
        zt = zt_scr[...]
        yt = lax.dot_general(tk_scr[...], zt, bdims, preferred_element_type=F32)
        xt = lax.dot_general(win_scr[...], zt, bdims, preferred_element_type=F32)
        xre_scr[...] = xt[:, 0:SSM_STATE, :].reshape(rows, LANES).T
        xim_scr[...] = xt[:, SSM_STATE:, :].reshape(rows, LANES).T
        return yt

    def gelu_pieces(yt):
        hp = jnp.concatenate([pre_scr[...].T.reshape(gb, SSM_STATE, LANES),
                              pim_scr[...].T.reshape(gb, SSM_STATE, LANES)], axis=1)
        ycorr = lax.dot_general(cout_scr[...], hp.astype(BF16), bdims,
                                preferred_element_type=F32)
        g = jax.nn.gelu(yt + ycorr)
        return [g[:, SSM_GROUP * t:SSM_GROUP * (t + 1), :].reshape(te, LANES)
                for t in range(CHUNK)]

    def pack_samples(out_ref, block_of, first_lane_of):
        quarter = LANES // 4
        lane_q = lax.broadcasted_iota(jnp.int32, (1, LANES), 1) // quarter
        for c in range(CHUNK // 4):
            out = jnp.zeros((te, LANES), F32)
            for k in range(4):
                t = 4 * c + k
                shift = (quarter * k - first_lane_of(t)) % LANES
                blk = block_of(t)
                out = jnp.where(lane_q == k, pltpu.roll(blk, shift, 1) if shift else blk, out)
            out_ref[:, LANES * c:LANES * (c + 1)] = out.astype(BF16)

    @pl.when(i == 0)
    def _():
        build_operators()
        uz = lax.dot_general(w_scr[...], xsp_ref[...], nt_dims,
                             preferred_element_type=F32)
        sz = jax.nn.silu(uz[te:, :])
        pack_samples(sztsp_ref, lambda t: sz[:, LANES * (t // 2):LANES * (t // 2 + 1)],
                     lambda t: half * (t % 2) + 1)
        for s in range(CHUNK):
            blk = uz[:te, LANES * (s // 2):LANES * (s // 2 + 1)]
            if s % 2:
                blk = pltpu.roll(blk, half, 1)
            zt_scr[:, CHUNK * s:CHUNK * (s + 1), :] = (
                jnp.where(low_half, blk, 0.0).astype(BF16).reshape(gb, SSM_GROUP, LANES))
        yt = chunk_inputs()

        pre_scr[...] = jnp.zeros_like(pre_scr)
        pim_scr[...] = jnp.zeros_like(pim_scr)
        zero = jnp.zeros((1, rows), F32)
        hr, hi = step(zero, zero, 0)
        cre_scr[0:1, :] = hr
        cim_scr[0:1, :] = hi
        for q in range(n_seq):
            hr = sre_ref[q:q + 1, :]
            hi = sim_ref[q:q + 1, :]
            for e in range(2):
                n = 1 + 2 * q + e
                pre_scr[n:n + 1, :] = hr
                pim_scr[n:n + 1, :] = hi
                hr, hi = step(hr, hi, n)
            hsre_ref[q:q + 1, :] = hr
            hsim_ref[q:q + 1, :] = hi

        pieces = gelu_pieces(yt)
        pack_samples(gtsp_ref, lambda t: pieces[t], lambda t: 1)

    @pl.when(i > 0)
    def _():
        uz = lax.dot_general(w_scr[...], xs_ref[...], nt_dims,
                             preferred_element_type=F32)
        szt_ref[...] = jax.nn.silu(uz[te:, :]).astype(BF16)
        ub = uz[:te, :].astype(BF16)
        for s in range(CHUNK):
            zt_scr[:, CHUNK * s:CHUNK * (s + 1), :] = (
                ub[:, LANES * s:LANES * (s + 1)].reshape(gb, SSM_GROUP, LANES))
        yt = chunk_inputs()
        _hosted_cast_step(wg_ref, wgt_ref, transpose=True)
        _hosted_cast_step(wo_ref, wot_ref, transpose=True)

        hr, hi = cre_scr[0:1, :], cim_scr[0:1, :]
        for n in range(LANES):
            pre_scr[n:n + 1, :] = hr
            pim_scr[n:n + 1, :] = hi
            hr, hi = step(hr, hi, n)
        cre_scr[0:1, :] = hr
        cim_scr[0:1, :] = hi

        @pl.when(i == n_tiles - 1)
        def _():
            hpre_ref[...] = jnp.broadcast_to(hr, hpre_ref.shape)
            hpim_ref[...] = jnp.broadcast_to(hi, hpim_ref.shape)

        pieces = gelu_pieces(yt)
        for t in range(CHUNK):
            gt_ref[:, LANES * t:LANES * (t + 1)] = pieces[t].astype(BF16)


def _ssm(xs_sp, xs_p, wuz_t, e1, e2, c1, c2, apt, bb, dv, are, aim, sre, sim, w_glu, w_out):
    n_ptiles = xs_p.shape[0]
    n_tiles = 1 + n_ptiles
    te = TE_SSM
    n_j = D_INNER // te
    gb = te // SSM_GROUP
    rows = gb * SSM_STATE
    n_seq = sre.shape[0]
    prev = lambda i: jnp.maximum(i - 1, 0)
    act_shape = jax.ShapeDtypeStruct((n_ptiles, D_INNER, TILE), BF16)
    act_spec = pl.BlockSpec((None, te, TILE), lambda j, i: (prev(i), j, 0))
    assert 2 * n_seq == LANES // 4
    sp_slots = CHUNK * 2 * n_seq
    sp_shape = jax.ShapeDtypeStruct((1, D_INNER, sp_slots), BF16)
    sp_spec = pl.BlockSpec((None, te, sp_slots), lambda j, i: (0, j, 0))
    hs_shape = jax.ShapeDtypeStruct((n_seq, N_STATE_ROWS), F32)
    hs_spec = pl.BlockSpec((n_seq, rows), lambda j, i: (0, j))
    hp_shape = jax.ShapeDtypeStruct((8, N_STATE_ROWS), F32)
    hp_spec = pl.BlockSpec((8, rows), lambda j, i: (0, j))
    row_spec = pl.BlockSpec((1, rows), lambda j, i: (0, j))
    grp_spec = lambda r, c: pl.BlockSpec((gb, r, c), lambda j, i: (j, 0, 0))
    prompt_step = lambda j, i: j * n_ptiles + prev(i)
    wg_in, wg_out, wg_shape = _hosted_cast(w_glu, n_j * n_ptiles, prompt_step, transpose=True)
    wo_in, wo_out, wo_shape = _hosted_cast(w_out, n_j * n_ptiles, prompt_step, transpose=True)
    return pl.pallas_call(
        functools.partial(_ssm_kernel, te=te, n_tiles=n_tiles),
        grid=(n_j, n_tiles),
        in_specs=[
            pl.BlockSpec((TILE // 2, D_MODEL), lambda j, i: (0, 0), pipeline_mode=pl.Buffered(1)),
            pl.BlockSpec((None, TILE, D_MODEL), lambda j, i: (prev(i), 0, 0)),
            pl.BlockSpec((te, D_MODEL), lambda j, i: (j, 0)),
            pl.BlockSpec((te, D_MODEL), lambda j, i: (n_j + j, 0)),
            grp_spec(e1.shape[1], 2 * SSM_STATE), grp_spec(e2.shape[1], 2 * SSM_STATE),
            grp_spec(SSM_GROUP, 2 * SSM_STATE), grp_spec(SSM_GROUP, 2 * SSM_STATE),
            grp_spec(2 * SSM_STATE, SSM_GROUP), grp_spec(2 * SSM_STATE, SSM_GROUP),
            grp_spec(SSM_GROUP, 1),
            row_spec, row_spec, hs_spec, hs_spec, wg_in, wo_in,
        ],
        out_specs=(sp_spec, sp_spec, act_spec, act_spec, hs_spec, hs_spec, hp_spec, hp_spec,
                   wg_out, wo_out),
        out_shape=(sp_shape, sp_shape, act_shape, act_shape, hs_shape, hs_shape,
                   hp_shape, hp_shape, wg_shape, wo_shape),
        scratch_shapes=[pltpu.VMEM((2 * te, D_MODEL), BF16),
                        pltpu.VMEM((gb, CHUNK_ROWS, CHUNK_ROWS), BF16),
                        pltpu.VMEM((gb, 2 * SSM_STATE, CHUNK_ROWS), BF16),
                        pltpu.VMEM((gb, CHUNK_ROWS, 2 * SSM_STATE), BF16),
                        pltpu.VMEM((gb, CHUNK_ROWS, LANES), BF16),
                        pltpu.VMEM((LANES, rows), F32), pltpu.VMEM((LANES, rows), F32),
                        pltpu.VMEM((LANES, rows), F32), pltpu.VMEM((LANES, rows), F32),
                        pltpu.VMEM((8, rows), F32), pltpu.VMEM((8, rows), F32)],
        compiler_params=_params(2, VMEM_LIMIT_LARGE),
        name="s5_scan",
    )(xs_sp, xs_p, wuz_t, wuz_t, e1, e2, c1, c2, apt, bb, dv, are, aim, sre, sim, w_glu, w_out)


def _glu_out_kernel(gt_ref, szt_ref, wg_ref, bg_ref, wo_ref, h_ref, nw_ref, o_ref, acc_scr, *,
                    te, lb):
    e = pl.program_id(2)

    @pl.when(e == 0)
    def _():
        acc_scr[...] = jnp.zeros_like(acc_scr)

    gate = jnp.dot(wg_ref[...], gt_ref[...], preferred_element_type=F32) + bg_ref[...]
    ge = gt_ref[pl.ds(pl.multiple_of(e * te, te), te), :].astype(F32)
    y3 = (ge * jax.nn.sigmoid(gate)) * szt_ref[...].astype(F32)
    acc_scr[...] += jnp.dot(wo_ref[...], y3.astype(BF16), preferred_element_type=F32)

    @pl.when(e == pl.num_programs(2) - 1)
    def _():
        o = acc_scr[...].T
        ln = h_ref.shape[1]
        for t in range(lb // ln):
            h = h_ref[t] + o[ln * t:ln * (t + 1), :]
            o_ref[:, D_MODEL * t:D_MODEL * (t + 1)] = _rms(h, nw_ref[...])


def _glu_out(gt, szt, wg_t, b_glu, wo_t, h_all, norm_w):
    n_tiles, _, lanes, _ = h_all.shape
    te = TE_GLU
    lb = LB_GLU
    n_l = CHUNK * lanes // lb
    n_e = D_INNER // te
    tpb = lb // lanes
    return pl.pallas_call(
        functools.partial(_glu_out_kernel, te=te, lb=lb),
        grid=(n_tiles, n_l, n_e),
        in_specs=[
            pl.BlockSpec((None, D_INNER, lb), lambda i, l, e: (i, 0, l)),
            pl.BlockSpec((None, te, lb), lambda i, l, e: (i, e, l)),
            pl.BlockSpec((te, D_INNER), lambda i, l, e: (e, 0)),
            pl.BlockSpec((te, 1), lambda i, l, e: (e, 0)),
            pl.BlockSpec((D_MODEL, te), lambda i, l, e: (0, e)),
            pl.BlockSpec((None, tpb, lanes, D_MODEL), lambda i, l, e: (i, l, 0, 0)),
            pl.BlockSpec((1, D_MODEL), lambda i, l, e: (0, 0)),
        ],
        out_specs=pl.BlockSpec((lanes, tpb * D_MODEL), lambda i, l, e: (i, l)),
        out_shape=jax.ShapeDtypeStruct((n_tiles * lanes, CHUNK * D_MODEL), F32),
        scratch_shapes=[pltpu.VMEM((D_MODEL, lb), F32)],
        compiler_params=_params(3),
        name="glu_out",
    )(gt, szt, wg_t, b_glu, wo_t, h_all, norm_w)


def _to_rows_kernel(o_ref, y_ref, slab_scr):
    nc = o_ref.shape[0]
    n_slabs = D_MODEL // LANES
    for t in range(CHUNK):
        for k in range(n_slabs):
            slab_scr[k, pl.ds(t, nc, stride=CHUNK), :] = (
                o_ref[:, D_MODEL * t + LANES * k:D_MODEL * t + LANES * (k + 1)])
    for k in range(n_slabs):
        y_ref[:, LANES * k:LANES * (k + 1)] = slab_scr[k]


def _to_rows(o):
    n_chunks = o.shape[0]
    tm = TM_ROWS
    nc = tm // CHUNK
    return pl.pallas_call(
        _to_rows_kernel,
        grid=(n_chunks // nc,),
        in_specs=[pl.BlockSpec((nc, CHUNK * D_MODEL), lambda i: (i, 0))],
        out_specs=pl.BlockSpec((None, tm, D_MODEL), lambda i: (0, i, 0)),
        out_shape=jax.ShapeDtypeStruct((1, n_chunks * CHUNK, D_MODEL), F32),
        scratch_shapes=[pltpu.VMEM((D_MODEL // LANES, tm, LANES), F32)],
        compiler_params=_params(1),
        name="to_rows",
    )(o)


def _ssm_operators(a_re, a_im, b_re, b_im, c_re, c_im, d_vec, log_dt):
    a_re, a_im = a_re.astype(F32), a_im.astype(F32)
    dt = jnp.exp(log_dt.astype(F32))[:, None]
    th_re = jnp.concatenate([a_re * dt] * 2, axis=1)[:, None, :]
    th_im = jnp.concatenate([a_im * dt] * 2, axis=1)[:, None, :]
    n_tau = 24
    taus = jnp.arange(n_tau, dtype=F32)[None, :, None]
    used = taus <= CHUNK
    mag = jnp.exp(jnp.where(used, th_re * taus, 0.0))
    ang = jnp.where(used, th_im * taus, 0.0)
    ap_re = mag * jnp.cos(ang)
    ap_im = mag * jnp.sin(ang)
    sign = jnp.concatenate([jnp.ones((SSM_STATE,), F32), -jnp.ones((SSM_STATE,), F32)])
    e1 = ap_re * sign
    e2 = -ap_im
    c_re, c_im = c_re.astype(F32), c_im.astype(F32)
    c1 = jnp.concatenate([c_re, c_im], axis=2)
    c2 = jnp.concatenate([c_im, c_re], axis=2)

    ab_re, ab_im = ap_re[:, 1, :SSM_STATE], ap_im[:, 1, :SSM_STATE]
    nr, ni = ab_re - 1.0, ab_im
    den = a_re * a_re + a_im * a_im
    q_re = ((nr * a_re + ni * a_im) / den)[..., None]
    q_im = ((ni * a_re - nr * a_im) / den)[..., None]
    b_re, b_im = b_re.astype(F32), b_im.astype(F32)
    bb = jnp.concatenate([q_re * b_re - q_im * b_im, q_re * b_im + q_im * b_re], axis=1)

    rev = slice(CHUNK - 1, None, -1)
    apt = jnp.concatenate([ap_re[:, rev, :SSM_STATE].transpose(0, 2, 1),
                           ap_im[:, rev, :SSM_STATE].transpose(0, 2, 1)], axis=1)
    dv = d_vec.astype(F32).reshape(N_GROUPS, SSM_GROUP, 1)
    are = ap_re[:, CHUNK, :SSM_STATE].reshape(1, N_STATE_ROWS)
    aim = ap_im[:, CHUNK, :SSM_STATE].reshape(1, N_STATE_ROWS)
    return e1, e2, c1, c2, apt, bb, dv, are, aim


def kernel(x_prompt, x_sample, cache_conv, state_ssm_re, state_ssm_im, meta_tokens, norm_w,
           final_norm_w, conv_w_in, conv_w, conv_b, conv_w_out, ssm_w_in, ssm_a_re, ssm_a_im,
           ssm_b_re, ssm_b_im, ssm_c_re, ssm_c_im, ssm_d, ssm_log_dt, ssm_w_glu, ssm_b_glu,
           ssm_w_out):
    n_seq, seq_len = x_sample.shape[0], x_sample.shape[1]
    n_prompt_rows = x_prompt.shape[1]
    n_ptiles = n_prompt_rows // TILE
    n_sample_rows = n_seq * seq_len
    assert x_prompt.shape[0] == 1 and seq_len == 2 * CHUNK and N_META == CHUNK
    assert n_prompt_rows % TILE == 0 and N_META + n_sample_rows <= SPECIAL_ROWS

    w_in0 = conv_w_in.astype(F32)
    conv_w8 = jnp.zeros((8, D_INNER), F32).at[0:3].set(conv_w[0].astype(F32))
    conv_b2 = conv_b[0].astype(F32).reshape(1, D_INNER)
    b_glu = ssm_b_glu[0].astype(F32).reshape(D_INNER, 1)
    nw0 = norm_w[0].astype(F32).reshape(1, D_MODEL)
    nw1 = norm_w[1].astype(F32).reshape(1, D_MODEL)
    nwf = final_norm_w.astype(F32).reshape(1, D_MODEL)
    ssm_ops = _ssm_operators(
        ssm_a_re[0], ssm_a_im[0], ssm_b_re[0], ssm_b_im[0], ssm_c_re[0], ssm_c_im[0],
        ssm_d[0], ssm_log_dt[0])
    sre = state_ssm_re[0].astype(F32).reshape(n_seq, N_STATE_ROWS)
    sim = state_ssm_im[0].astype(F32).reshape(n_seq, N_STATE_ROWS)

    xp = x_prompt.astype(F32)
    x_sp = jnp.concatenate([
        meta_tokens.astype(F32), x_sample.astype(F32).reshape(n_sample_rows, D_MODEL),
        jnp.zeros((SPECIAL_ROWS - N_META - n_sample_rows, D_MODEL), F32)], axis=0)[None]

    zeros8 = jnp.zeros((8, D_INNER), F32)
    xn_sp = _rmsnorm(x_sp, nw0, tm=SPECIAL_ROWS)
    xn_p = _rmsnorm(xp, nw0, tm=TM_CONV)
    sp_real = N_META + n_sample_rows
    y_sp, cv_sp = _conv_proj(xn_sp, w_in0, conv_w8, conv_b2, zeros8,
                             cache_conv[0, :, 0, :].astype(F32), cache_conv[0, :, 1, :].astype(F32),
                             real_rows=sp_real)
    init8 = zeros8.at[6:8].set(cv_sp[N_META - 2:N_META])
    y_p, tail_p, w_out0, wuz_t = _conv_proj(
        xn_p, w_in0, conv_w8, conv_b2, init8,
        cast_w=conv_w_out.astype(F32), cast_w_t=ssm_w_in.astype(F32))
    h1_p, xs_p = _out_proj(y_p, w_out0, xp, nw1, lanes=LANES)
    h1_sp, xs_sp = _out_proj(y_sp, w_out0, x_sp, nw1, lanes=SPECIAL_ROWS // CHUNK,
                             real_rows=sp_real)

    gt_sp, szt_sp, gt_p, szt_p, hs_re, hs_im, hp_re, hp_im, wg_t, wo_t = _ssm(
        xs_sp.reshape(SPECIAL_ROWS, D_MODEL), xs_p.reshape(n_ptiles, TILE, D_MODEL),
        wuz_t, *ssm_ops, sre, sim, ssm_w_glu.astype(F32), ssm_w_out.astype(F32))
    n_sc = n_sample_rows // CHUNK
    o_sp = _glu_out(gt_sp, szt_sp, wg_t, b_glu, wo_t, h1_sp[:, :, 1:1 + n_sc], nwf)
    o_p = _glu_out(gt_p, szt_p, wg_t, b_glu, wo_t, h1_p, nwf)

    y_prompt = _to_rows(o_p)
    y_sample = o_sp.reshape(n_seq, seq_len, D_MODEL)
    new_conv_prompt = tail_p[-1, 6:8].reshape(1, 1, 2, D_INNER)
    cv_s = cv_sp[N_META:N_META + n_sample_rows].reshape(n_seq, seq_len, D_INNER)
    new_conv_sample = cv_s[:, seq_len - 2:].reshape(1, n_seq, 2, D_INNER)
    p_shape = (1, 1, N_GROUPS, SSM_STATE)
    s_shape = (1, n_seq, N_GROUPS, SSM_STATE)
    return (y_prompt, y_sample, new_conv_prompt, new_conv_sample,
            hp_re[0].reshape(p_shape), hp_im[0].reshape(p_shape),
            hs_re.reshape(s_shape), hs_im.reshape(s_shape))
```

```python
import functools

import jax
import jax.numpy as jnp
from jax import lax
from jax.experimental import pallas as pl
from jax.experimental.pallas import tpu as pltpu

F32 = jnp.float32
BF16 = jnp.bfloat16

D_MODEL = 2048
D_INNER = 4096
N_META = 16
SSM_GROUP = 16
N_GROUPS = D_INNER // SSM_GROUP
SSM_STATE = 64
N_STATE_ROWS = N_GROUPS * SSM_STATE
RMS_EPS = 1e-6

CHUNK = 16
CHUNK_ROWS = CHUNK * SSM_GROUP
LANES = 128
TILE = CHUNK * LANES
SPECIAL_ROWS = 1024

TM_CONV = 1024
TE_CONV = 512
TM_OUT = 512
TE_SSM = 256
TE_GLU = 512
LB_GLU = 512
TM_ROWS = 1024
CAST_ROWS = 512

VMEM_LIMIT = 56 * 1024 * 1024
VMEM_LIMIT_LARGE = 60 * 1024 * 1024


def _params(n_axes, vmem=VMEM_LIMIT):
    return pltpu.CompilerParams(dimension_semantics=("arbitrary",) * n_axes,
                                vmem_limit_bytes=vmem)


def _rms(x, w):
    ms = jnp.mean(x * x, axis=-1, keepdims=True)
    return x * lax.rsqrt(ms + RMS_EPS) * w


def _hosted_cast(w, n_steps, step_of, *, transpose):
    _, r, c = w.shape
    bc = (r * c) // (n_steps * CAST_ROWS)
    nc = c // bc
    assert bc % LANES == 0 and (r // CAST_ROWS) * nc == n_steps
    in_spec = pl.BlockSpec((None, CAST_ROWS, bc),
                           lambda *g: (0, step_of(*g) // nc, step_of(*g) % nc))
    if transpose:
        out_spec = pl.BlockSpec((bc, CAST_ROWS),
                                lambda *g: (step_of(*g) % nc, step_of(*g) // nc))
        return in_spec, out_spec, jax.ShapeDtypeStruct((c, r), BF16)
    out_spec = pl.BlockSpec((CAST_ROWS, bc), lambda *g: (step_of(*g) // nc, step_of(*g) % nc))
    return in_spec, out_spec, jax.ShapeDtypeStruct((r, c), BF16)


def _hosted_cast_step(src_ref, dst_ref, *, transpose):
    x = src_ref[...]
    dst_ref[...] = (x.T if transpose else x).astype(BF16)


def _rmsnorm_kernel(x_ref, w_ref, o_ref):
    o_ref[...] = _rms(x_ref[...], w_ref[...]).astype(BF16)


def _rmsnorm(x, w, *, tm):
    rows = x.shape[1]
    return pl.pallas_call(
        _rmsnorm_kernel, grid=(rows // tm,),
        in_specs=[pl.BlockSpec((None, tm, D_MODEL), lambda i: (0, i, 0)),
                  pl.BlockSpec((1, D_MODEL), lambda i: (0, 0))],
        out_specs=pl.BlockSpec((tm, D_MODEL), lambda i: (i, 0)),
        out_shape=jax.ShapeDtypeStruct((rows, D_MODEL), BF16),
        compiler_params=_params(1), name="rmsnorm",
    )(x, w)


def _conv_proj_kernel(*refs, tm, te, n_pieces, special, real_rows):
    if special:
        (xn_ref, wb_ref, wc_ref, wv_ref, wz_ref, cw_ref, cb_ref, init_ref, c0_ref, c1_ref,
         y_ref, cv_ref, w_scr, s_scr, p1_scr, p2_scr) = refs
    else:
        (xn_ref, wb_ref, wc_ref, wv_ref, wz_ref, cw_ref, cb_ref, init_ref, wp_ref, wt_ref,
         y_ref, tail_ref, wp_bf_ref, wt_bf_ref, w_scr, s_scr) = refs
    i = pl.program_id(1)

    @pl.when(i == 0)
    def _():
        for p, w_ref in enumerate((wb_ref, wc_ref, wv_ref, wz_ref)):
            w_scr[:, p * te:(p + 1) * te] = w_ref[...].astype(BF16)
        s_scr[0:8, :] = init_ref[...]

    if special:
        pm = real_rows
        cv_ref[pm:tm, :] = jnp.zeros((tm - pm, te), F32)
        y_ref[pm:tm, :] = jnp.zeros((tm - pm, te), BF16)
    else:
        pm = tm // n_pieces
    for r in range(n_pieces):
        lo = r * pm
        if not special and r == n_pieces - 1:
            _hosted_cast_step(wp_ref, wp_bf_ref, transpose=False)
            _hosted_cast_step(wt_ref, wt_bf_ref, transpose=True)
        proj = jnp.dot(xn_ref[lo:lo + pm, :], w_scr[...], preferred_element_type=F32)
        bg = proj[:, 0 * te:1 * te]
        cg = proj[:, 1 * te:2 * te]
        vv = proj[:, 2 * te:3 * te]
        zz = proj[:, 3 * te:4 * te]
        cv = cg * vv
        s_scr[8 + lo:8 + lo + pm, :] = cv
        if special:
            cv_ref[lo:lo + pm, :] = cv
            p1_scr[...] = s_scr[7 + lo:7 + lo + pm, :]
            p2_scr[...] = s_scr[6 + lo:6 + lo + pm, :]
            for q in range(c0_ref.shape[0]):
                r0 = N_META + 32 * q
                p1_scr[r0:r0 + 1, :] = c1_ref[q:q + 1, :]
                p2_scr[r0:r0 + 1, :] = c0_ref[q:q + 1, :]
                p2_scr[r0 + 1:r0 + 2, :] = c1_ref[q:q + 1, :]
            p1 = p1_scr[...]
            p2 = p2_scr[...]
        else:
            p1 = s_scr[7 + lo:7 + lo + pm, :]
            p2 = s_scr[6 + lo:6 + lo + pm, :]
        conv = cb_ref[...] + cw_ref[0:1, :] * p2
        conv = conv + cw_ref[1:2, :] * p1
        conv = conv + cw_ref[2:3, :] * cv
        y_ref[lo:lo + pm, :] = (bg * conv * jax.nn.silu(zz)).astype(BF16)

    if not special:
        tail = s_scr[tm:tm + 8, :]
        s_scr[0:8, :] = tail
        tail_ref[...] = tail


def _conv_proj(xn, w_in, conv_w8, conv_b, init8, cache0=None, cache1=None, real_rows=None,
               cast_w=None, cast_w_t=None):
    rows = xn.shape[0]
    tm, te = TM_CONV, TE_CONV
    n_j = D_INNER // te
    n_i = rows // tm
    special = cache0 is not None
    w_spec = lambda p: pl.BlockSpec(
        (None, D_MODEL, te),
        lambda j, i, p=p: (0, 0, p * n_j + jnp.minimum(j + (i > p), n_j - 1)))
    in_specs = [pl.BlockSpec((tm, D_MODEL), lambda j, i: (i, 0)),
                w_spec(0), w_spec(1), w_spec(2), w_spec(3),
                pl.BlockSpec((8, te), lambda j, i: (0, j)),
                pl.BlockSpec((1, te), lambda j, i: (0, j)),
                pl.BlockSpec((8, te), lambda j, i: (0, j))]
    args = [xn, w_in, w_in, w_in, w_in, conv_w8, conv_b, init8]
    scratch = [pltpu.VMEM((D_MODEL, 4 * te), BF16), pltpu.VMEM((tm + 8, te), F32)]
    y_shape = jax.ShapeDtypeStruct((rows, D_INNER), BF16)
    y_spec = pl.BlockSpec((tm, te), lambda j, i: (i, j))
    if special:
        assert n_i == 1
        nq = cache0.shape[0]
        in_specs += [pl.BlockSpec((nq, te), lambda j, i: (0, j)),
                     pl.BlockSpec((nq, te), lambda j, i: (0, j))]
        args += [cache0, cache1]
        out_shape = (y_shape, jax.ShapeDtypeStruct((rows, D_INNER), F32))
        out_specs = (y_spec, pl.BlockSpec((tm, te), lambda j, i: (i, j)))
        scratch += [pltpu.VMEM((real_rows, te), F32), pltpu.VMEM((real_rows, te), F32)]
    else:
        step_of = lambda j, i: j * n_i + i
        wp_in, wp_out, wp_shape = _hosted_cast(cast_w, n_j * n_i, step_of, transpose=False)
        wt_in, wt_out, wt_shape = _hosted_cast(cast_w_t, n_j * n_i, step_of, transpose=True)
        in_specs += [wp_in, wt_in]
        args += [cast_w, cast_w_t]
        out_shape = (y_shape, jax.ShapeDtypeStruct((n_i, 8, D_INNER), F32), wp_shape, wt_shape)
        out_specs = (y_spec, pl.BlockSpec((None, 8, te), lambda j, i: (i, 0, j)), wp_out, wt_out)
    return pl.pallas_call(
        functools.partial(_conv_proj_kernel, tm=tm, te=te, n_pieces=1 if special else 4,
                          special=special, real_rows=real_rows),
        grid=(n_j, n_i), in_specs=in_specs, out_specs=out_specs, out_shape=out_shape,
        scratch_shapes=scratch, compiler_params=_params(2, VMEM_LIMIT_LARGE),
        name="conv_proj_special" if special else "conv_proj",
    )(*args)


def _out_proj_kernel(y_ref, w_ref, x_ref, nw_ref, h_ref, xs_ref, slab_scr, *, real_rows, n_steps):
    tm = y_ref.shape[0]
    nc = tm // CHUNK
    n_slabs = D_MODEL // LANES

    def block(r):
        if r == 0:
            h = jnp.zeros((tm, D_MODEL), F32)
        else:
            h = x_ref[0:r, :] + jnp.dot(y_ref[0:r, :], w_ref[...], preferred_element_type=F32)
            if r < tm:
                h = jnp.concatenate([h, jnp.zeros((tm - r, D_MODEL), F32)], axis=0)
        for k in range(n_slabs):
            slab_scr[k] = h[:, LANES * k:LANES * (k + 1)]
        for s in range(CHUNK):
            hs = jnp.concatenate(
                [slab_scr[k, pl.ds(s, nc, stride=CHUNK), :] for k in range(n_slabs)], axis=1)
            h_ref[s] = hs
            xs_ref[s] = _rms(hs, nw_ref[...]).astype(BF16)

    if real_rows is None:
        block(tm)
    else:
        i = pl.program_id(0)
        for b in range(n_steps):
            pl.when(i == b)(functools.partial(block, min(max(real_rows - b * tm, 0), tm)))


def _out_proj(y, w_out, x, norm_w, *, lanes, real_rows=None):
    tm = TM_OUT
    nc = tm // CHUNK
    per_tile = lanes // nc
    n_i = x.shape[1] // tm
    tile_map = lambda i: (i // per_tile, 0, i % per_tile, 0)
    tile_shape = (n_i // per_tile, CHUNK, lanes, D_MODEL)
    return pl.pallas_call(
        functools.partial(_out_proj_kernel, real_rows=real_rows, n_steps=n_i),
        grid=(n_i,),
        in_specs=[
            pl.BlockSpec((tm, D_INNER), lambda i: (i, 0)),
            pl.BlockSpec((D_INNER, D_MODEL), lambda i: (0, 0), pipeline_mode=pl.Buffered(1)),
            pl.BlockSpec((None, tm, D_MODEL), lambda i: (0, i, 0)),
            pl.BlockSpec((1, D_MODEL), lambda i: (0, 0)),
        ],
        out_specs=(pl.BlockSpec((None, CHUNK, nc, D_MODEL), tile_map),
                   pl.BlockSpec((None, CHUNK, nc, D_MODEL), tile_map)),
        out_shape=(jax.ShapeDtypeStruct(tile_shape, F32),
                   jax.ShapeDtypeStruct(tile_shape, BF16)),
        scratch_shapes=[pltpu.VMEM((D_MODEL // LANES, tm, LANES), F32)],
        compiler_params=_params(1, VMEM_LIMIT_LARGE),
        name="out_proj",
    )(y, w_out, x, norm_w)


def _split_bf16(x):
    hi = x.astype(BF16)
    return hi, (x - hi.astype(F32)).astype(BF16)


def _ssm_kernel(xsp_ref, xs_ref, wu_ref, wz_ref, e1_ref, e2_ref, c1_ref, c2_ref, apt_ref, bb_ref,
                dv_ref, are_ref, aim_ref, sre_ref, sim_ref, wg_ref, wo_ref,
                gtsp_ref, sztsp_ref, gt_ref, szt_ref, hsre_ref, hsim_ref, hpre_ref, hpim_ref,
                wgt_ref, wot_ref, w_scr, tk_scr, win_scr, cout_scr, zt_scr, xre_scr, xim_scr, pre_scr, pim_scr,
                cre_scr, cim_scr, *, te, n_tiles):
    i = pl.program_id(1)
    gb = te // SSM_GROUP
    rows = gb * SSM_STATE
    n_seq = sre_ref.shape[0]
    bdims = (((2,), (1,)), ((0,), (0,)))

    def build_operators():
        w_scr[0:te, :] = wu_ref[...]
        w_scr[te:2 * te, :] = wz_ref[...]

        c1 = c1_ref[...]
        c2 = c2_ref[...]
        cap = [c1 * e1_ref[:, t:t + 1, :] + c2 * e2_ref[:, t:t + 1, :] for t in range(CHUNK + 1)]
        ca = jnp.concatenate(cap[:CHUNK], axis=1)
        cout_scr[...] = jnp.concatenate(cap[1:], axis=1).astype(BF16)

        lane = lax.broadcasted_iota(jnp.int32, (SSM_GROUP, CHUNK_ROWS), 1)
        row = lax.broadcasted_iota(jnp.int32, (SSM_GROUP, CHUNK_ROWS), 0)
        diag = row == (lane & (SSM_GROUP - 1))
        tile_c = diag.astype(BF16)
        spread_s = (row == lane // SSM_GROUP).astype(BF16)
        expand = lambda b, m, n_rows: jnp.dot(b, m, preferred_element_type=F32).reshape(
            gb, n_rows, CHUNK_ROWS)
        bhi, blo = _split_bf16(bb_ref[...].reshape(gb * 2 * SSM_STATE, SSM_GROUP))
        bhi_t = expand(bhi, tile_c, 2 * SSM_STATE)
        blo_t = expand(blo, tile_c, 2 * SSM_STATE)

        phi, plo = _split_bf16(apt_ref[...].reshape(gb * 2 * SSM_STATE, SSM_GROUP))
        ap_t = expand(phi, spread_s, 2 * SSM_STATE) + expand(plo, spread_s, 2 * SSM_STATE)
        b_t = bhi_t + blo_t
        pr, pi = ap_t[:, :SSM_STATE], ap_t[:, SSM_STATE:]
        br, bi = b_t[:, :SSM_STATE], b_t[:, SSM_STATE:]
        win_scr[:, 0:SSM_STATE, :] = (pr * br - pi * bi).astype(BF16)
        win_scr[:, SSM_STATE:, :] = (pr * bi + pi * br).astype(BF16)

        bhi_t, blo_t = bhi_t.astype(BF16), blo_t.astype(BF16)
        ahi, alo = _split_bf16(ca)
        kw = (lax.dot_general(ahi, bhi_t, bdims, preferred_element_type=F32)
              + lax.dot_general(ahi, blo_t, bdims, preferred_element_type=F32)
              + lax.dot_general(alo, bhi_t, bdims, preferred_element_type=F32))
        kw0 = (kw[:, 0:SSM_GROUP, :] + jnp.where(diag, dv_ref[...], 0.0)).astype(BF16)
        kw = kw.astype(BF16)
        tk_scr[...] = jnp.zeros_like(tk_scr)
        for s in range(CHUNK):
            lo = SSM_GROUP * s
            hi = lo + SSM_GROUP
            tk_scr[:, lo:hi, lo:hi] = kw0[:, :, lo:hi]
            if hi < CHUNK_ROWS:
                tk_scr[:, hi:CHUNK_ROWS, lo:hi] = kw[:, SSM_GROUP:CHUNK_ROWS - lo, lo:hi]

    nt_dims = (((1,), (1,)), ((), ()))
    half = LANES // 2
    low_half = lax.broadcasted_iota(jnp.int32, (1, LANES), 1) < half

    ar = are_ref[...]
    ai = aim_ref[...]

    def step(hr, hi, n):
        xr = xre_scr[pl.ds(n, 1), :]
        xi = xim_scr[pl.ds(n, 1), :]
        return ar * hr - ai * hi + xr, ar * hi + ai * hr + xi

    def chunk_inputs():
        zt = zt_scr[...]
        yt = lax.dot_general(tk_scr[...], zt, bdims, preferred_element_type=F32)
        xt = lax.dot_general(win_scr[...], zt, bdims, preferred_element_type=F32)
        xre_scr[...] = xt[:, 0:SSM_STATE, :].reshape(rows, LANES).T
        xim_scr[...] = xt[:, SSM_STATE:, :].reshape(rows, LANES).T
        return yt

    def gelu_pieces(yt):
        hp = jnp.concatenate([pre_scr[...].T.reshape(gb, SSM_STATE, LANES),
                              pim_scr[...].T.reshape(gb, SSM_STATE, LANES)], axis=1)
        ycorr = lax.dot_general(cout_scr[...], hp.astype(BF16), bdims,
                                preferred_element_type=F32)
        g = jax.nn.gelu(yt + ycorr)
        return [g[:, SSM_GROUP * t:SSM_GROUP * (t + 1), :].reshape(te, LANES)
                for t in range(CHUNK)]

    def pack_samples(out_ref, block_of, first_lane_of):
        quarter = LANES // 4
        lane_q = lax.broadcasted_iota(jnp.int32, (1, LANES), 1) // quarter
        for c in range(CHUNK // 4):
            out = jnp.zeros((te, LANES), F32)
            for k in range(4):
                t = 4 * c + k
                shift = (quarter * k - first_lane_of(t)) % LANES
                blk = block_of(t)
                out = jnp.where(lane_q == k, pltpu.roll(blk, shift, 1) if shift else blk, out)
            out_ref[:, LANES * c:LANES * (c + 1)] = out.astype(BF16)

    @pl.when(i == 0)
    def _():
        build_operators()
        uz = lax.dot_general(w_scr[...], xsp_ref[...], nt_dims,
                             preferred_element_type=F32)
        sz = jax.nn.silu(uz[te:, :])
        pack_samples(sztsp_ref, lambda t: sz[:, LANES * (t // 2):LANES * (t // 2 + 1)],
                     lambda t: half * (t % 2) + 1)
        for s in range(CHUNK):
            blk = uz[:te, LANES * (s // 2):LANES * (s // 2 + 1)]
            if s % 2:
                blk = pltpu.roll(blk, half, 1)
            zt_scr[:, CHUNK * s:CHUNK * (s + 1), :] = (
                jnp.where(low_half, blk, 0.0).astype(BF16).reshape(gb, SSM_GROUP, LANES))
        yt = chunk_inputs()

        pre_scr[...] = jnp.zeros_like(pre_scr)
        pim_scr[...] = jnp.zeros_like(pim_scr)
        zero = jnp.zeros((1, rows), F32)
        hr, hi = step(zero, zero, 0)
        cre_scr[0:1, :] = hr
        cim_scr[0:1, :] = hi
        for q in range(n_seq):
            hr = sre_ref[q:q + 1, :]
            hi = sim_ref[q:q + 1, :]
            for e in range(2):
                n = 1 + 2 * q + e
                pre_scr[n:n + 1, :] = hr
                pim_scr[n:n + 1, :] = hi
                hr, hi = step(hr, hi, n)
            hsre_ref[q:q + 1, :] = hr
            hsim_ref[q:q + 1, :] = hi

        pieces = gelu_pieces(yt)
        pack_samples(gtsp_ref, lambda t: pieces[t], lambda t: 1)

    @pl.when(i > 0)
    def _():
        uz = lax.dot_general(w_scr[...], xs_ref[...], nt_dims,
                             preferred_element_type=F32)
        szt_ref[...] = jax.nn.silu(uz[te:, :]).astype(BF16)
        ub = uz[:te, :].astype(BF16)
        for s in range(CHUNK):
            zt_scr[:, CHUNK * s:CHUNK * (s + 1), :] = (
                ub[:, LANES * s:LANES * (s + 1)].reshape(gb, SSM_GROUP, LANES))
        yt = chunk_inputs()
        _hosted_cast_step(wg_ref, wgt_ref, transpose=True)
        _hosted_cast_step(wo_ref, wot_ref, transpose=True)

        hr, hi = cre_scr[0:1, :], cim_scr[0:1, :]
        for n in range(LANES):
            pre_scr[n:n + 1, :] = hr
            pim_scr[n:n + 1, :] = hi
            hr, hi = step(hr, hi, n)
        cre_scr[0:1, :] = hr
        cim_scr[0:1, :] = hi

        @pl.when(i == n_tiles - 1)
        def _():
            hpre_ref[...] = jnp.broadcast_to(hr, hpre_ref.shape)
            hpim_ref[...] = jnp.broadcast_to(hi, hpim_ref.shape)

        pieces = gelu_pieces(yt)
        for t in range(CHUNK):
            gt_ref[:, LANES * t:LANES * (t + 1)] = pieces[t].astype(BF16)


def _ssm(xs_sp, xs_p, wuz_t, e1, e2, c1, c2, apt, bb, dv, are, aim, sre, sim, w_glu, w_out):
    n_ptiles = xs_p.shape[0]
    n_tiles = 1 + n_ptiles
    te = TE_SSM
    n_j = D_INNER // te
    gb = te // SSM_GROUP
    rows = gb * SSM_STATE
    n_seq = sre.shape[0]
    prev = lambda i: jnp.maximum(i - 1, 0)
    act_shape = jax.ShapeDtypeStruct((n_ptiles, D_INNER, TILE), BF16)
    act_spec = pl.BlockSpec((None, te, TILE), lambda j, i: (prev(i), j, 0))
    assert 2 * n_seq == LANES // 4
    sp_slots = CHUNK * 2 * n_seq
    sp_shape = jax.ShapeDtypeStruct((1, D_INNER, sp_slots), BF16)
    sp_spec = pl.BlockSpec((None, te, sp_slots), lambda j, i: (0, j, 0))
    hs_shape = jax.ShapeDtypeStruct((n_seq, N_STATE_ROWS), F32)
    hs_spec = pl.BlockSpec((n_seq, rows), lambda j, i: (0, j))
    hp_shape = jax.ShapeDtypeStruct((8, N_STATE_ROWS), F32)
    hp_spec = pl.BlockSpec((8, rows), lambda j, i: (0, j))
    row_spec = pl.BlockSpec((1, rows), lambda j, i: (0, j))
    grp_spec = lambda r, c: pl.BlockSpec((gb, r, c), lambda j, i: (j, 0, 0))
    prompt_step = lambda j, i: j * n_ptiles + prev(i)
    wg_in, wg_out, wg_shape = _hosted_cast(w_glu, n_j * n_ptiles, prompt_step, transpose=True)
    wo_in, wo_out, wo_shape = _hosted_cast(w_out, n_j * n_ptiles, prompt_step, transpose=True)
    return pl.pallas_call(
        functools.partial(_ssm_kernel, te=te, n_tiles=n_tiles),
        grid=(n_j, n_tiles),
        in_specs=[
            pl.BlockSpec((TILE // 2, D_MODEL), lambda j, i: (0, 0), pipeline_mode=pl.Buffered(1)),
            pl.BlockSpec((None, TILE, D_MODEL), lambda j, i: (prev(i), 0, 0)),
            pl.BlockSpec((te, D_MODEL), lambda j, i: (j, 0)),
            pl.BlockSpec((te, D_MODEL), lambda j, i: (n_j + j, 0)),
            grp_spec(e1.shape[1], 2 * SSM_STATE), grp_spec(e2.shape[1], 2 * SSM_STATE),
            grp_spec(SSM_GROUP, 2 * SSM_STATE), grp_spec(SSM_GROUP, 2 * SSM_STATE),
            grp_spec(2 * SSM_STATE, SSM_GROUP), grp_spec(2 * SSM_STATE, SSM_GROUP),
            grp_spec(SSM_GROUP, 1),
            row_spec, row_spec, hs_spec, hs_spec, wg_in, wo_in,
        ],
        out_specs=(sp_spec, sp_spec, act_spec, act_spec, hs_spec, hs_spec, hp_spec, hp_spec,
                   wg_out, wo_out),
        out_shape=(sp_shape, sp_shape, act_shape, act_shape, hs_shape, hs_shape,
                   hp_shape, hp_shape, wg_shape, wo_shape),
        scratch_shapes=[pltpu.VMEM((2 * te, D_MODEL), BF16),
                        pltpu.VMEM((gb, CHUNK_ROWS, CHUNK_ROWS), BF16),
                        pltpu.VMEM((gb, 2 * SSM_STATE, CHUNK_ROWS), BF16),
                        pltpu.VMEM((gb, CHUNK_ROWS, 2 * SSM_STATE), BF16),
                        pltpu.VMEM((gb, CHUNK_ROWS, LANES), BF16),
                        pltpu.VMEM((LANES, rows), F32), pltpu.VMEM((LANES, rows), F32),
                        pltpu.VMEM((LANES, rows), F32), pltpu.VMEM((LANES, rows), F32),
                        pltpu.VMEM((8, rows), F32), pltpu.VMEM((8, rows), F32)],
        compiler_params=_params(2, VMEM_LIMIT_LARGE),
        name="s5_scan",
    )(xs_sp, xs_p, wuz_t, wuz_t, e1, e2, c1, c2, apt, bb, dv, are, aim, sre, sim, w_glu, w_out)


def _glu_out_kernel(gt_ref, szt_ref, wg_ref, bg_ref, wo_ref, h_ref, nw_ref, o_ref, acc_scr, *,
                    te, lb):
    e = pl.program_id(2)

    @pl.when(e == 0)
    def _():
        acc_scr[...] = jnp.zeros_like(acc_scr)

    gate = jnp.dot(wg_ref[...], gt_ref[...], preferred_element_type=F32) + bg_ref[...]
    ge = gt_ref[pl.ds(pl.multiple_of(e * te, te), te), :].astype(F32)
    y3 = (ge * jax.nn.sigmoid(gate)) * szt_ref[...].astype(F32)
    acc_scr[...] += jnp.dot(wo_ref[...], y3.astype(BF16), preferred_element_type=F32)

    @pl.when(e == pl.num_programs(2) - 1)
    def _():
        o = acc_scr[...].T
        ln = h_ref.shape[1]
        for t in range(lb // ln):
            h = h_ref[t] + o[ln * t:ln * (t + 1), :]
            o_ref[:, D_MODEL * t:D_MODEL * (t + 1)] = _rms(h, nw_ref[...])


def _glu_out(gt, szt, wg_t, b_glu, wo_t, h_all, norm_w):
    n_tiles, _, lanes, _ = h_all.shape
    te = TE_GLU
    lb = LB_GLU
    n_l = CHUNK * lanes // lb
    n_e = D_INNER // te
    tpb = lb // lanes
    return pl.pallas_call(
        functools.partial(_glu_out_kernel, te=te, lb=lb),
        grid=(n_tiles, n_l, n_e),
        in_specs=[
            pl.BlockSpec((None, D_INNER, lb), lambda i, l, e: (i, 0, l)),
            pl.BlockSpec((None, te, lb), lambda i, l, e: (i, e, l)),
            pl.BlockSpec((te, D_INNER), lambda i, l, e: (e, 0)),
            pl.BlockSpec((te, 1), lambda i, l, e: (e, 0)),
            pl.BlockSpec((D_MODEL, te), lambda i, l, e: (0, e)),
            pl.BlockSpec((None, tpb, lanes, D_MODEL), lambda i, l, e: (i, l, 0, 0)),
            pl.BlockSpec((1, D_MODEL), lambda i, l, e: (0, 0)),
        ],
        out_specs=pl.BlockSpec((lanes, tpb * D_MODEL), lambda i, l, e: (i, l)),
        out_shape=jax.ShapeDtypeStruct((n_tiles * lanes, CHUNK * D_MODEL), F32),
        scratch_shapes=[pltpu.VMEM((D_MODEL, lb), F32)],
        compiler_params=_params(3),
        name="glu_out",
    )(gt, szt, wg_t, b_glu, wo_t, h_all, norm_w)


def _to_rows_kernel(o_ref, y_ref, slab_scr):
    nc = o_ref.shape[0]
    n_slabs = D_MODEL // LANES
    for t in range(CHUNK):
        for k in range(n_slabs):
            slab_scr[k, pl.ds(t, nc, stride=CHUNK), :] = (
                o_ref[:, D_MODEL * t + LANES * k:D_MODEL * t + LANES * (k + 1)])
    for k in range(n_slabs):
        y_ref[:, LANES * k:LANES * (k + 1)] = slab_scr[k]


def _to_rows(o):
    n_chunks = o.shape[0]
    tm = TM_ROWS
    nc = tm // CHUNK
    return pl.pallas_call(
        _to_rows_kernel,
        grid=(n_chunks // nc,),
        in_specs=[pl.BlockSpec((nc, CHUNK * D_MODEL), lambda i: (i, 0))],
        out_specs=pl.BlockSpec((None, tm, D_MODEL), lambda i: (0, i, 0)),
        out_shape=jax.ShapeDtypeStruct((1, n_chunks * CHUNK, D_MODEL), F32),
        scratch_shapes=[pltpu.VMEM((D_MODEL // LANES, tm, LANES), F32)],
        compiler_params=_params(1),
        name="to_rows",
    )(o)


def _ssm_operators(a_re, a_im, b_re, b_im, c_re, c_im, d_vec, log_dt):
    a_re, a_im = a_re.astype(F32), a_im.astype(F32)
    dt = jnp.exp(log_dt.astype(F32))[:, None]
    th_re = jnp.concatenate([a_re * dt] * 2, axis=1)[:, None, :]
    th_im = jnp.concatenate([a_im * dt] * 2, axis=1)[:, None, :]
    n_tau = 24
    taus = jnp.arange(n_tau, dtype=F32)[None, :, None]
    used = taus <= CHUNK
    mag = jnp.exp(jnp.where(used, th_re * taus, 0.0))
    ang = jnp.where(used, th_im * taus, 0.0)
    ap_re = mag * jnp.cos(ang)
    ap_im = mag * jnp.sin(ang)
    sign = jnp.concatenate([jnp.ones((SSM_STATE,), F32), -jnp.ones((SSM_STATE,), F32)])
    e1 = ap_re * sign
    e2 = -ap_im
    c_re, c_im = c_re.astype(F32), c_im.astype(F32)
    c1 = jnp.concatenate([c_re, c_im], axis=2)
    c2 = jnp.concatenate([c_im, c_re], axis=2)

    ab_re, ab_im = ap_re[:, 1, :SSM_STATE], ap_im[:, 1, :SSM_STATE]
    nr, ni = ab_re - 1.0, ab_im
    den = a_re * a_re + a_im * a_im
    q_re = ((nr * a_re + ni * a_im) / den)[..., None]
    q_im = ((ni * a_re - nr * a_im) / den)[..., None]
    b_re, b_im = b_re.astype(F32), b_im.astype(F32)
    bb = jnp.concatenate([q_re * b_re - q_im * b_im, q_re * b_im + q_im * b_re], axis=1)

    rev = slice(CHUNK - 1, None, -1)
    apt = jnp.concatenate([ap_re[:, rev, :SSM_STATE].transpose(0, 2, 1),
                           ap_im[:, rev, :SSM_STATE].transpose(0, 2, 1)], axis=1)
    dv = d_vec.astype(F32).reshape(N_GROUPS, SSM_GROUP, 1)
    are = ap_re[:, CHUNK, :SSM_STATE].reshape(1, N_STATE_ROWS)
    aim = ap_im[:, CHUNK, :SSM_STATE].reshape(1, N_STATE_ROWS)
    return e1, e2, c1, c2, apt, bb, dv, are, aim


def kernel(x_prompt, x_sample, cache_conv, state_ssm_re, state_ssm_im, meta_tokens, norm_w,
           final_norm_w, conv_w_in, conv_w, conv_b, conv_w_out, ssm_w_in, ssm_a_re, ssm_a_im,
           ssm_b_re, ssm_b_im, ssm_c_re, ssm_c_im, ssm_d, ssm_log_dt, ssm_w_glu, ssm_b_glu,
           ssm_w_out):
    n_seq, seq_len = x_sample.shape[0], x_sample.shape[1]
    n_prompt_rows = x_prompt.shape[1]
    n_ptiles = n_prompt_rows // TILE
    n_sample_rows = n_seq * seq_len
    assert x_prompt.shape[0] == 1 and seq_len == 2 * CHUNK and N_META == CHUNK
    assert n_prompt_rows % TILE == 0 and N_META + n_sample_rows <= SPECIAL_ROWS

    w_in0 = conv_w_in.astype(F32)
    conv_w8 = jnp.zeros((8, D_INNER), F32).at[0:3].set(conv_w[0].astype(F32))
    conv_b2 = conv_b[0].astype(F32).reshape(1, D_INNER)
    b_glu = ssm_b_glu[0].astype(F32).reshape(D_INNER, 1)
    nw0 = norm_w[0].astype(F32).reshape(1, D_MODEL)
    nw1 = norm_w[1].astype(F32).reshape(1, D_MODEL)
    nwf = final_norm_w.astype(F32).reshape(1, D_MODEL)
    ssm_ops = _ssm_operators(
        ssm_a_re[0], ssm_a_im[0], ssm_b_re[0], ssm_b_im[0], ssm_c_re[0], ssm_c_im[0],
        ssm_d[0], ssm_log_dt[0])
    sre = state_ssm_re[0].astype(F32).reshape(n_seq, N_STATE_ROWS)
    sim = state_ssm_im[0].astype(F32).reshape(n_seq, N_STATE_ROWS)

    xp = x_prompt.astype(F32)
    x_sp = jnp.concatenate([
        meta_tokens.astype(F32), x_sample.astype(F32).reshape(n_sample_rows, D_MODEL),
        jnp.zeros((SPECIAL_ROWS - N_META - n_sample_rows, D_MODEL), F32)], axis=0)[None]

    zeros8 = jnp.zeros((8, D_INNER), F32)
    xn_sp = _rmsnorm(x_sp, nw0, tm=SPECIAL_ROWS)
    xn_p = _rmsnorm(xp, nw0, tm=TM_CONV)
    sp_real = N_META + n_sample_rows
    y_sp, cv_sp = _conv_proj(xn_sp, w_in0, conv_w8, conv_b2, zeros8,
                             cache_conv[0, :, 0, :].astype(F32), cache_conv[0, :, 1, :].astype(F32),
                             real_rows=sp_real)
    init8 = zeros8.at[6:8].set(cv_sp[N_META - 2:N_META])
    y_p, tail_p, w_out0, wuz_t = _conv_proj(
        xn_p, w_in0, conv_w8, conv_b2, init8,
        cast_w=conv_w_out.astype(F32), cast_w_t=ssm_w_in.astype(F32))
    h1_p, xs_p = _out_proj(y_p, w_out0, xp, nw1, lanes=LANES)
    h1_sp, xs_sp = _out_proj(y_sp, w_out0, x_sp, nw1, lanes=SPECIAL_ROWS // CHUNK,
                             real_rows=sp_real)

    gt_sp, szt_sp, gt_p, szt_p, hs_re, hs_im, hp_re, hp_im, wg_t, wo_t = _ssm(
        xs_sp.reshape(SPECIAL_ROWS, D_MODEL), xs_p.reshape(n_ptiles, TILE, D_MODEL),
        wuz_t, *ssm_ops, sre, sim, ssm_w_glu.astype(F32), ssm_w_out.astype(F32))
    n_sc = n_sample_rows // CHUNK
    o_sp = _glu_out(gt_sp, szt_sp, wg_t, b_glu, wo_t, h1_sp[:, :, 1:1 + n_sc], nwf)
    o_p = _glu_out(gt_p, szt_p, wg_t, b_glu, wo_t, h1_p, nwf)

    y_prompt = _to_rows(o_p)
    y_sample = o_sp.reshape(n_seq, seq_len, D_MODEL)
    new_conv_prompt = tail_p[-1, 6:8].reshape(1, 1, 2, D_INNER)
    cv_s = cv_sp[N_META:N_META + n_sample_rows].reshape(n_seq, seq_len, D_INNER)
    new_conv_sample = cv_s[:, seq_len - 2:].reshape(1, n_seq, 2, D_INNER)
    p_shape = (1, 1, N_GROUPS, SSM_STATE)
    s_shape = (1, n_seq, N_GROUPS, SSM_STATE)
    return (y_prompt, y_sample, new_conv_prompt, new_conv_sample,
            hp_re[0].reshape(p_shape), hp_im[0].reshape(p_shape),
            hs_re.reshape(s_shape), hs_im.reshape(s_shape))
```

```python
import functools

import jax
import jax.numpy as jnp
from jax import lax
from jax.experimental import pallas as pl
from jax.experimental.pallas import tpu as pltpu

F32 = jnp.float32
BF16 = jnp.bfloat16

D_MODEL = 2048
D_INNER = 4096
N_META = 16
SSM_GROUP = 16
N_GROUPS = D_INNER // SSM_GROUP
SSM_STATE = 64
N_STATE_ROWS = N_GROUPS * SSM_STATE
RMS_EPS = 1e-6

CHUNK = 16
CHUNK_ROWS = CHUNK * SSM_GROUP
LANES = 128
TILE = CHUNK * LANES
SPECIAL_ROWS = 1024

TM_CONV = 1024
TE_CONV = 512
TM_OUT = 512
TE_SSM = 256
TE_GLU = 512
LB_GLU = 512
TM_ROWS = 1024
CAST_ROWS = 512

VMEM_LIMIT = 56 * 1024 * 1024
VMEM_LIMIT_LARGE = 62 * 1024 * 1024


def _params(n_axes, vmem=VMEM_LIMIT):
    return pltpu.CompilerParams(dimension_semantics=("arbitrary",) * n_axes,
                                vmem_limit_bytes=vmem)


def _rms(x, w):
    ms = jnp.mean(x * x, axis=-1, keepdims=True)
    return x * lax.rsqrt(ms + RMS_EPS) * w


def _hosted_cast(w, n_steps, step_of, *, transpose):
    _, r, c = w.shape
    bc = (r * c) // (n_steps * CAST_ROWS)
    nc = c // bc
    assert bc % LANES == 0 and (r // CAST_ROWS) * nc == n_steps
    in_spec = pl.BlockSpec((None, CAST_ROWS, bc),
                           lambda *g: (0, step_of(*g) // nc, step_of(*g) % nc))
    if transpose:
        out_spec = pl.BlockSpec((bc, CAST_ROWS),
                                lambda *g: (step_of(*g) % nc, step_of(*g) // nc))
        return in_spec, out_spec, jax.ShapeDtypeStruct((c, r), BF16)
    out_spec = pl.BlockSpec((CAST_ROWS, bc), lambda *g: (step_of(*g) // nc, step_of(*g) % nc))
    return in_spec, out_spec, jax.ShapeDtypeStruct((r, c), BF16)


def _hosted_cast_step(src_ref, dst_ref, *, transpose):
    x = src_ref[...]
    dst_ref[...] = (x.T if transpose else x).astype(BF16)


def _rmsnorm_kernel(x_ref, w_ref, o_ref):
    o_ref[...] = _rms(x_ref[...], w_ref[...]).astype(BF16)


def _rmsnorm(x, w, *, tm):
    rows = x.shape[1]
    return pl.pallas_call(
        _rmsnorm_kernel, grid=(rows // tm,),
        in_specs=[pl.BlockSpec((None, tm, D_MODEL), lambda i: (0, i, 0)),
                  pl.BlockSpec((1, D_MODEL), lambda i: (0, 0))],
        out_specs=pl.BlockSpec((tm, D_MODEL), lambda i: (i, 0)),
        out_shape=jax.ShapeDtypeStruct((rows, D_MODEL), BF16),
        compiler_params=_params(1), name="rmsnorm",
    )(x, w)


def _conv_proj_kernel(*refs, tm, te, n_pieces, special, real_rows):
    if special:
        (xn_ref, wb_ref, wc_ref, wv_ref, wz_ref, cw_ref, cb_ref, init_ref, c0_ref, c1_ref,
         y_ref, cv_ref, w_scr, s_scr, p1_scr, p2_scr) = refs
    else:
        (xn_ref, wb_ref, wc_ref, wv_ref, wz_ref, cw_ref, cb_ref, init_ref, wp_ref, wt_ref,
         y_ref, tail_ref, wp_bf_ref, wt_bf_ref, w_scr, s_scr) = refs
    i = pl.program_id(1)

    @pl.when(i == 0)
    def _():
        for p, w_ref in enumerate((wb_ref, wc_ref, wv_ref, wz_ref)):
            w_scr[:, p * te:(p + 1) * te] = w_ref[...].astype(BF16)
        s_scr[0:8, :] = init_ref[...]

    if special:
        pm = real_rows
        cv_ref[pm:tm, :] = jnp.zeros((tm - pm, te), F32)
        y_ref[pm:tm, :] = jnp.zeros((tm - pm, te), BF16)
    else:
        pm = tm // n_pieces
    for r in range(n_pieces):
        lo = r * pm
        if not special and r == n_pieces - 1:
            _hosted_cast_step(wp_ref, wp_bf_ref, transpose=False)
            _hosted_cast_step(wt_ref, wt_bf_ref, transpose=True)
        proj = jnp.dot(xn_ref[lo:lo + pm, :], w_scr[...], preferred_element_type=F32)
        bg = proj[:, 0 * te:1 * te]
        cg = proj[:, 1 * te:2 * te]
        vv = proj[:, 2 * te:3 * te]
        zz = proj[:, 3 * te:4 * te]
        cv = cg * vv
        s_scr[8 + lo:8 + lo + pm, :] = cv
        if special:
            cv_ref[lo:lo + pm, :] = cv
            p1_scr[...] = s_scr[7 + lo:7 + lo + pm, :]
            p2_scr[...] = s_scr[6 + lo:6 + lo + pm, :]
            for q in range(c0_ref.shape[0]):
                r0 = N_META + 32 * q
                p1_scr[r0:r0 + 1, :] = c1_ref[q:q + 1, :]
                p2_scr[r0:r0 + 1, :] = c0_ref[q:q + 1, :]
                p2_scr[r0 + 1:r0 + 2, :] = c1_ref[q:q + 1, :]
            p1 = p1_scr[...]
            p2 = p2_scr[...]
        else:
            p1 = s_scr[7 + lo:7 + lo + pm, :]
            p2 = s_scr[6 + lo:6 + lo + pm, :]
        conv = cb_ref[...] + cw_ref[0:1, :] * p2
        conv = conv + cw_ref[1:2, :] * p1
        conv = conv + cw_ref[2:3, :] * cv
        y_ref[lo:lo + pm, :] = (bg * conv * jax.nn.silu(zz)).astype(BF16)

    if not special:
        tail = s_scr[tm:tm + 8, :]
        s_scr[0:8, :] = tail
        tail_ref[...] = tail


def _conv_proj(xn, w_in, conv_w8, conv_b, init8, cache0=None, cache1=None, real_rows=None,
               cast_w=None, cast_w_t=None):
    rows = xn.shape[0]
    tm, te = TM_CONV, TE_CONV
    n_j = D_INNER // te
    n_i = rows // tm
    special = cache0 is not None
    w_spec = lambda p: pl.BlockSpec((None, D_MODEL, te),
                                    lambda j, i, p=p: (0, 0, p * n_j + j))
    in_specs = [pl.BlockSpec((tm, D_MODEL), lambda j, i: (i, 0)),
                w_spec(0), w_spec(1), w_spec(2), w_spec(3),
                pl.BlockSpec((8, te), lambda j, i: (0, j)),
                pl.BlockSpec((1, te), lambda j, i: (0, j)),
                pl.BlockSpec((8, te), lambda j, i: (0, j))]
    args = [xn, w_in, w_in, w_in, w_in, conv_w8, conv_b, init8]
    scratch = [pltpu.VMEM((D_MODEL, 4 * te), BF16), pltpu.VMEM((tm + 8, te), F32)]
    y_shape = jax.ShapeDtypeStruct((rows, D_INNER), BF16)
    y_spec = pl.BlockSpec((tm, te), lambda j, i: (i, j))
    if special:
        assert n_i == 1
        nq = cache0.shape[0]
        in_specs += [pl.BlockSpec((nq, te), lambda j, i: (0, j)),
                     pl.BlockSpec((nq, te), lambda j, i: (0, j))]
        args += [cache0, cache1]
        out_shape = (y_shape, jax.ShapeDtypeStruct((rows, D_INNER), F32))
        out_specs = (y_spec, pl.BlockSpec((tm, te), lambda j, i: (i, j)))
        scratch += [pltpu.VMEM((real_rows, te), F32), pltpu.VMEM((real_rows, te), F32)]
    else:
        step_of = lambda j, i: j * n_i + i
        wp_in, wp_out, wp_shape = _hosted_cast(cast_w, n_j * n_i, step_of, transpose=False)
        wt_in, wt_out, wt_shape = _hosted_cast(cast_w_t, n_j * n_i, step_of, transpose=True)
        in_specs += [wp_in, wt_in]
        args += [cast_w, cast_w_t]
        out_shape = (y_shape, jax.ShapeDtypeStruct((n_i, 8, D_INNER), F32), wp_shape, wt_shape)
        out_specs = (y_spec, pl.BlockSpec((None, 8, te), lambda j, i: (i, 0, j)), wp_out, wt_out)
    return pl.pallas_call(
        functools.partial(_conv_proj_kernel, tm=tm, te=te, n_pieces=1 if special else 2,
                          special=special, real_rows=real_rows),
        grid=(n_j, n_i), in_specs=in_specs, out_specs=out_specs, out_shape=out_shape,
        scratch_shapes=scratch, compiler_params=_params(2, VMEM_LIMIT_LARGE),
        name="conv_proj_special" if special else "conv_proj",
    )(*args)


def _out_proj_kernel(y_ref, w_ref, x_ref, nw_ref, h_ref, xs_ref, slab_scr, *, real_rows, n_steps):
    tm = y_ref.shape[0]
    nc = tm // CHUNK
    n_slabs = D_MODEL // LANES

    def block(r):
        if r == 0:
            h = jnp.zeros((tm, D_MODEL), F32)
        else:
            h = x_ref[0:r, :] + jnp.dot(y_ref[0:r, :], w_ref[...], preferred_element_type=F32)
            if r < tm:
                h = jnp.concatenate([h, jnp.zeros((tm - r, D_MODEL), F32)], axis=0)
        for k in range(n_slabs):
            slab_scr[k] = h[:, LANES * k:LANES * (k + 1)]
        for s in range(CHUNK):
            hs = jnp.concatenate(
                [slab_scr[k, pl.ds(s, nc, stride=CHUNK), :] for k in range(n_slabs)], axis=1)
            h_ref[s] = hs
            xs_ref[s] = _rms(hs, nw_ref[...]).astype(BF16)

    if real_rows is None:
        block(tm)
    else:
        i = pl.program_id(0)
        for b in range(n_steps):
            pl.when(i == b)(functools.partial(block, min(max(real_rows - b * tm, 0), tm)))


def _out_proj(y, w_out, x, norm_w, *, lanes, real_rows=None):
    tm = TM_OUT
    nc = tm // CHUNK
    per_tile = lanes // nc
    n_i = x.shape[1] // tm
    tile_map = lambda i: (i // per_tile, 0, i % per_tile, 0)
    tile_shape = (n_i // per_tile, CHUNK, lanes, D_MODEL)
    return pl.pallas_call(
        functools.partial(_out_proj_kernel, real_rows=real_rows, n_steps=n_i),
        grid=(n_i,),
        in_specs=[
            pl.BlockSpec((tm, D_INNER), lambda i: (i, 0)),
            pl.BlockSpec((D_INNER, D_MODEL), lambda i: (0, 0), pipeline_mode=pl.Buffered(1)),
            pl.BlockSpec((None, tm, D_MODEL), lambda i: (0, i, 0)),
            pl.BlockSpec((1, D_MODEL), lambda i: (0, 0)),
        ],
        out_specs=(pl.BlockSpec((None, CHUNK, nc, D_MODEL), tile_map),
                   pl.BlockSpec((None, CHUNK, nc, D_MODEL), tile_map)),
        out_shape=(jax.ShapeDtypeStruct(tile_shape, F32),
                   jax.ShapeDtypeStruct(tile_shape, BF16)),
        scratch_shapes=[pltpu.VMEM((D_MODEL // LANES, tm, LANES), F32)],
        compiler_params=_params(1, VMEM_LIMIT_LARGE),
        name="out_proj",
    )(y, w_out, x, norm_w)


def _split_bf16(x):
    hi = x.astype(BF16)
    return hi, (x - hi.astype(F32)).astype(BF16)


def _ssm_kernel(xsp_ref, xs_ref, wu_ref, wz_ref, e1_ref, e2_ref, c1_ref, c2_ref, apt_ref, bb_ref,
                dv_ref, are_ref, aim_ref, sre_ref, sim_ref, wg_ref, wo_ref,
                gtsp_ref, sztsp_ref, gt_ref, szt_ref, hsre_ref, hsim_ref, hpre_ref, hpim_ref,
                wgt_ref, wot_ref, w_scr, tk_scr, win_scr, cout_scr, zt_scr, xre_scr, xim_scr, pre_scr, pim_scr,
                cre_scr, cim_scr, *, te, n_tiles):
    i = pl.program_id(1)
    gb = te // SSM_GROUP
    rows = gb * SSM_STATE
    n_seq = sre_ref.shape[0]
    bdims = (((2,), (1,)), ((0,), (0,)))

    def build_operators():
        w_scr[0:te, :] = wu_ref[...]
        w_scr[te:2 * te, :] = wz_ref[...]

        c1 = c1_ref[...]
        c2 = c2_ref[...]
        cap = [c1 * e1_ref[:, t:t + 1, :] + c2 * e2_ref[:, t:t + 1, :] for t in range(CHUNK + 1)]
        ca = jnp.concatenate(cap[:CHUNK], axis=1)
        cout_scr[...] = jnp.concatenate(cap[1:], axis=1).astype(BF16)

        lane = lax.broadcasted_iota(jnp.int32, (SSM_GROUP, CHUNK_ROWS), 1)
        row = lax.broadcasted_iota(jnp.int32, (SSM_GROUP, CHUNK_ROWS), 0)
        diag = row == (lane & (SSM_GROUP - 1))
        tile_c = diag.astype(BF16)
        spread_s = (row == lane // SSM_GROUP).astype(BF16)
        expand = lambda b, m, n_rows: jnp.dot(b, m, preferred_element_type=F32).reshape(
            gb, n_rows, CHUNK_ROWS)
        bhi, blo = _split_bf16(bb_ref[...].reshape(gb * 2 * SSM_STATE, SSM_GROUP))
        bhi_t = expand(bhi, tile_c, 2 * SSM_STATE)
        blo_t = expand(blo, tile_c, 2 * SSM_STATE)

        phi, plo = _split_bf16(apt_ref[...].reshape(gb * 2 * SSM_STATE, SSM_GROUP))
        ap_t = expand(phi, spread_s, 2 * SSM_STATE) + expand(plo, spread_s, 2 * SSM_STATE)
        b_t = bhi_t + blo_t
        pr, pi = ap_t[:, :SSM_STATE], ap_t[:, SSM_STATE:]
        br, bi = b_t[:, :SSM_STATE], b_t[:, SSM_STATE:]
        win_scr[:, 0:SSM_STATE, :] = (pr * br - pi * bi).astype(BF16)
        win_scr[:, SSM_STATE:, :] = (pr * bi + pi * br).astype(BF16)

        bhi_t, blo_t = bhi_t.astype(BF16), blo_t.astype(BF16)
        ahi, alo = _split_bf16(ca)
        kw = (lax.dot_general(ahi, bhi_t, bdims, preferred_element_type=F32)
              + lax.dot_general(ahi, blo_t, bdims, preferred_element_type=F32)
              + lax.dot_general(alo, bhi_t, bdims, preferred_element_type=F32))
        kw0 = (kw[:, 0:SSM_GROUP, :] + jnp.where(diag, dv_ref[...], 0.0)).astype(BF16)
        kw = kw.astype(BF16)
        tk_scr[...] = jnp.zeros_like(tk_scr)
        for s in range(CHUNK):
            lo = SSM_GROUP * s
            hi = lo + SSM_GROUP
            tk_scr[:, lo:hi, lo:hi] = kw0[:, :, lo:hi]
            if hi < CHUNK_ROWS:
                tk_scr[:, hi:CHUNK_ROWS, lo:hi] = kw[:, SSM_GROUP:CHUNK_ROWS - lo, lo:hi]

    nt_dims = (((1,), (1,)), ((), ()))
    half = LANES // 2
    low_half = lax.broadcasted_iota(jnp.int32, (1, LANES), 1) < half

    ar = are_ref[...]
    ai = aim_ref[...]

    def step(hr, hi, n):
        xr = xre_scr[pl.ds(n, 1), :]
        xi = xim_scr[pl.ds(n, 1), :]
        return ar * hr - ai * hi + xr, ar * hi + ai * hr + xi

    def chunk_inputs():
        zt = zt_scr[...]
        yt = lax.dot_general(tk_scr[...], zt, bdims, preferred_element_type=F32)
        xt = lax.dot_general(win_scr[...], zt, bdims, preferred_element_type=F32)
        xre_scr[...] = xt[:, 0:SSM_STATE, :].reshape(rows, LANES).T
        xim_scr[...] = xt[:, SSM_STATE:, :].reshape(rows, LANES).T
        return yt

    def gelu_pieces(yt):
        hp = jnp.concatenate([pre_scr[...].T.reshape(gb, SSM_STATE, LANES),
                              pim_scr[...].T.reshape(gb, SSM_STATE, LANES)], axis=1)
        ycorr = lax.dot_general(cout_scr[...], hp.astype(BF16), bdims,
                                preferred_element_type=F32)
        g = jax.nn.gelu(yt + ycorr)
        return [g[:, SSM_GROUP * t:SSM_GROUP * (t + 1), :].reshape(te, LANES)
                for t in range(CHUNK)]

    def pack_samples(out_ref, block_of, first_lane_of):
        quarter = LANES // 4
        lane_q = lax.broadcasted_iota(jnp.int32, (1, LANES), 1) // quarter
        for c in range(CHUNK // 4):
            out = jnp.zeros((te, LANES), F32)
            for k in range(4):
                t = 4 * c + k
                shift = (quarter * k - first_lane_of(t)) % LANES
                blk = block_of(t)
                out = jnp.where(lane_q == k, pltpu.roll(blk, shift, 1) if shift else blk, out)
            out_ref[:, LANES * c:LANES * (c + 1)] = out.astype(BF16)

    @pl.when(i == 0)
    def _():
        build_operators()
        uz = lax.dot_general(w_scr[...], xsp_ref[...], nt_dims,
                             preferred_element_type=F32)
        sz = jax.nn.silu(uz[te:, :])
        pack_samples(sztsp_ref, lambda t: sz[:, LANES * (t // 2):LANES * (t // 2 + 1)],
                     lambda t: half * (t % 2) + 1)
        for s in range(CHUNK):
            blk = uz[:te, LANES * (s // 2):LANES * (s // 2 + 1)]
            if s % 2:
                blk = pltpu.roll(blk, half, 1)
            zt_scr[:, CHUNK * s:CHUNK * (s + 1), :] = (
                jnp.where(low_half, blk, 0.0).astype(BF16).reshape(gb, SSM_GROUP, LANES))
        yt = chunk_inputs()

        pre_scr[...] = jnp.zeros_like(pre_scr)
        pim_scr[...] = jnp.zeros_like(pim_scr)
        zero = jnp.zeros((1, rows), F32)
        hr, hi = step(zero, zero, 0)
        cre_scr[0:1, :] = hr
        cim_scr[0:1, :] = hi
        for q in range(n_seq):
            hr = sre_ref[q:q + 1, :]
            hi = sim_ref[q:q + 1, :]
            for e in range(2):
                n = 1 + 2 * q + e
                pre_scr[n:n + 1, :] = hr
                pim_scr[n:n + 1, :] = hi
                hr, hi = step(hr, hi, n)
            hsre_ref[q:q + 1, :] = hr
            hsim_ref[q:q + 1, :] = hi

        pieces = gelu_pieces(yt)
        pack_samples(gtsp_ref, lambda t: pieces[t], lambda t: 1)

    @pl.when(i > 0)
    def _():
        uz = lax.dot_general(w_scr[...], xs_ref[...], nt_dims,
                             preferred_element_type=F32)
        szt_ref[...] = jax.nn.silu(uz[te:, :]).astype(BF16)
        ub = uz[:te, :].astype(BF16)
        for s in range(CHUNK):
            zt_scr[:, CHUNK * s:CHUNK * (s + 1), :] = (
                ub[:, LANES * s:LANES * (s + 1)].reshape(gb, SSM_GROUP, LANES))
        yt = chunk_inputs()
        _hosted_cast_step(wg_ref, wgt_ref, transpose=True)
        _hosted_cast_step(wo_ref, wot_ref, transpose=True)

        hr, hi = cre_scr[0:1, :], cim_scr[0:1, :]
        for n in range(LANES):
            pre_scr[n:n + 1, :] = hr
            pim_scr[n:n + 1, :] = hi
            hr, hi = step(hr, hi, n)
        cre_scr[0:1, :] = hr
        cim_scr[0:1, :] = hi

        @pl.when(i == n_tiles - 1)
        def _():
            hpre_ref[...] = jnp.broadcast_to(hr, hpre_ref.shape)
            hpim_ref[...] = jnp.broadcast_to(hi, hpim_ref.shape)

        pieces = gelu_pieces(yt)
        for t in range(CHUNK):
            gt_ref[:, LANES * t:LANES * (t + 1)] = pieces[t].astype(BF16)


def _ssm(xs_sp, xs_p, wuz_t, e1, e2, c1, c2, apt, bb, dv, are, aim, sre, sim, w_glu, w_out):
    n_ptiles = xs_p.shape[0]
    n_tiles = 1 + n_ptiles
    te = TE_SSM
    n_j = D_INNER // te
    gb = te // SSM_GROUP
    rows = gb * SSM_STATE
    n_seq = sre.shape[0]
    prev = lambda i: jnp.maximum(i - 1, 0)
    act_shape = jax.ShapeDtypeStruct((n_ptiles, D_INNER, TILE), BF16)
    act_spec = pl.BlockSpec((None, te, TILE), lambda j, i: (prev(i), j, 0))
    assert 2 * n_seq == LANES // 4
    sp_slots = CHUNK * 2 * n_seq
    sp_shape = jax.ShapeDtypeStruct((1, D_INNER, sp_slots), BF16)
    sp_spec = pl.BlockSpec((None, te, sp_slots), lambda j, i: (0, j, 0))
    hs_shape = jax.ShapeDtypeStruct((n_seq, N_STATE_ROWS), F32)
    hs_spec = pl.BlockSpec((n_seq, rows), lambda j, i: (0, j))
    hp_shape = jax.ShapeDtypeStruct((8, N_STATE_ROWS), F32)
    hp_spec = pl.BlockSpec((8, rows), lambda j, i: (0, j))
    row_spec = pl.BlockSpec((1, rows), lambda j, i: (0, j))
    grp_spec = lambda r, c: pl.BlockSpec((gb, r, c), lambda j, i: (j, 0, 0))
    prompt_step = lambda j, i: j * n_ptiles + prev(i)
    wg_in, wg_out, wg_shape = _hosted_cast(w_glu, n_j * n_ptiles, prompt_step, transpose=True)
    wo_in, wo_out, wo_shape = _hosted_cast(w_out, n_j * n_ptiles, prompt_step, transpose=True)
    return pl.pallas_call(
        functools.partial(_ssm_kernel, te=te, n_tiles=n_tiles),
        grid=(n_j, n_tiles),
        in_specs=[
            pl.BlockSpec((TILE // 2, D_MODEL), lambda j, i: (0, 0), pipeline_mode=pl.Buffered(1)),
            pl.BlockSpec((None, TILE, D_MODEL), lambda j, i: (prev(i), 0, 0)),
            pl.BlockSpec((te, D_MODEL), lambda j, i: (j, 0)),
            pl.BlockSpec((te, D_MODEL), lambda j, i: (n_j + j, 0)),
            grp_spec(e1.shape[1], 2 * SSM_STATE), grp_spec(e2.shape[1], 2 * SSM_STATE),
            grp_spec(SSM_GROUP, 2 * SSM_STATE), grp_spec(SSM_GROUP, 2 * SSM_STATE),
            grp_spec(2 * SSM_STATE, SSM_GROUP), grp_spec(2 * SSM_STATE, SSM_GROUP),
            grp_spec(SSM_GROUP, 1),
            row_spec, row_spec, hs_spec, hs_spec, wg_in, wo_in,
        ],
        out_specs=(sp_spec, sp_spec, act_spec, act_spec, hs_spec, hs_spec, hp_spec, hp_spec,
                   wg_out, wo_out),
        out_shape=(sp_shape, sp_shape, act_shape, act_shape, hs_shape, hs_shape,
                   hp_shape, hp_shape, wg_shape, wo_shape),
        scratch_shapes=[pltpu.VMEM((2 * te, D_MODEL), BF16),
                        pltpu.VMEM((gb, CHUNK_ROWS, CHUNK_ROWS), BF16),
                        pltpu.VMEM((gb, 2 * SSM_STATE, CHUNK_ROWS), BF16),
                        pltpu.VMEM((gb, CHUNK_ROWS, 2 * SSM_STATE), BF16),
                        pltpu.VMEM((gb, CHUNK_ROWS, LANES), BF16),
                        pltpu.VMEM((LANES, rows), F32), pltpu.VMEM((LANES, rows), F32),
                        pltpu.VMEM((LANES, rows), F32), pltpu.VMEM((LANES, rows), F32),
                        pltpu.VMEM((8, rows), F32), pltpu.VMEM((8, rows), F32)],
        compiler_params=_params(2, VMEM_LIMIT_LARGE),
        name="s5_scan",
    )(xs_sp, xs_p, wuz_t, wuz_t, e1, e2, c1, c2, apt, bb, dv, are, aim, sre, sim, w_glu, w_out)


def _glu_out_kernel(gt_ref, szt_ref, wg_ref, bg_ref, wo_ref, h_ref, nw_ref, o_ref, acc_scr, *,
                    te, lb):
    e = pl.program_id(2)

    @pl.when(e == 0)
    def _():
        acc_scr[...] = jnp.zeros_like(acc_scr)

    gate = jnp.dot(wg_ref[...], gt_ref[...], preferred_element_type=F32) + bg_ref[...]
    ge = gt_ref[pl.ds(pl.multiple_of(e * te, te), te), :].astype(F32)
    y3 = (ge * jax.nn.sigmoid(gate)) * szt_ref[...].astype(F32)
    acc_scr[...] += jnp.dot(wo_ref[...], y3.astype(BF16), preferred_element_type=F32)

    @pl.when(e == pl.num_programs(2) - 1)
    def _():
        o = acc_scr[...].T
        ln = h_ref.shape[1]
        for t in range(lb // ln):
            h = h_ref[t] + o[ln * t:ln * (t + 1), :]
            o_ref[:, D_MODEL * t:D_MODEL * (t + 1)] = _rms(h, nw_ref[...])


def _glu_out(gt, szt, wg_t, b_glu, wo_t, h_all, norm_w):
    n_tiles, _, lanes, _ = h_all.shape
    te = TE_GLU
    lb = LB_GLU
    n_l = CHUNK * lanes // lb
    n_e = D_INNER // te
    tpb = lb // lanes
    return pl.pallas_call(
        functools.partial(_glu_out_kernel, te=te, lb=lb),
        grid=(n_tiles, n_l, n_e),
        in_specs=[
            pl.BlockSpec((None, D_INNER, lb), lambda i, l, e: (i, 0, l)),
            pl.BlockSpec((None, te, lb), lambda i, l, e: (i, e, l)),
            pl.BlockSpec((te, D_INNER), lambda i, l, e: (e, 0)),
            pl.BlockSpec((te, 1), lambda i, l, e: (e, 0)),
            pl.BlockSpec((D_MODEL, te), lambda i, l, e: (0, e)),
            pl.BlockSpec((None, tpb, lanes, D_MODEL), lambda i, l, e: (i, l, 0, 0)),
            pl.BlockSpec((1, D_MODEL), lambda i, l, e: (0, 0)),
        ],
        out_specs=pl.BlockSpec((lanes, tpb * D_MODEL), lambda i, l, e: (i, l)),
        out_shape=jax.ShapeDtypeStruct((n_tiles * lanes, CHUNK * D_MODEL), F32),
        scratch_shapes=[pltpu.VMEM((D_MODEL, lb), F32)],
        compiler_params=_params(3),
        name="glu_out",
    )(gt, szt, wg_t, b_glu, wo_t, h_all, norm_w)


def _to_rows_kernel(o_ref, y_ref, slab_scr):
    nc = o_ref.shape[0]
    n_slabs = D_MODEL // LANES
    for t in range(CHUNK):
        for k in range(n_slabs):
            slab_scr[k, pl.ds(t, nc, stride=CHUNK), :] = (
                o_ref[:, D_MODEL * t + LANES * k:D_MODEL * t + LANES * (k + 1)])
    for k in range(n_slabs):
        y_ref[:, LANES * k:LANES * (k + 1)] = slab_scr[k]


def _to_rows(o):
    n_chunks = o.shape[0]
    tm = TM_ROWS
    nc = tm // CHUNK
    return pl.pallas_call(
        _to_rows_kernel,
        grid=(n_chunks // nc,),
        in_specs=[pl.BlockSpec((nc, CHUNK * D_MODEL), lambda i: (i, 0))],
        out_specs=pl.BlockSpec((None, tm, D_MODEL), lambda i: (0, i, 0)),
        out_shape=jax.ShapeDtypeStruct((1, n_chunks * CHUNK, D_MODEL), F32),
        scratch_shapes=[pltpu.VMEM((D_MODEL // LANES, tm, LANES), F32)],
        compiler_params=_params(1),
        name="to_rows",
    )(o)


def _ssm_operators(a_re, a_im, b_re, b_im, c_re, c_im, d_vec, log_dt):
    a_re, a_im = a_re.astype(F32), a_im.astype(F32)
    dt = jnp.exp(log_dt.astype(F32))[:, None]
    th_re = jnp.concatenate([a_re * dt] * 2, axis=1)[:, None, :]
    th_im = jnp.concatenate([a_im * dt] * 2, axis=1)[:, None, :]
    n_tau = 24
    taus = jnp.arange(n_tau, dtype=F32)[None, :, None]
    used = taus <= CHUNK
    mag = jnp.exp(jnp.where(used, th_re * taus, 0.0))
    ang = jnp.where(used, th_im * taus, 0.0)
    ap_re = mag * jnp.cos(ang)
    ap_im = mag * jnp.sin(ang)
    sign = jnp.concatenate([jnp.ones((SSM_STATE,), F32), -jnp.ones((SSM_STATE,), F32)])
    e1 = ap_re * sign
    e2 = -ap_im
    c_re, c_im = c_re.astype(F32), c_im.astype(F32)
    c1 = jnp.concatenate([c_re, c_im], axis=2)
    c2 = jnp.concatenate([c_im, c_re], axis=2)

    ab_re, ab_im = ap_re[:, 1, :SSM_STATE], ap_im[:, 1, :SSM_STATE]
    nr, ni = ab_re - 1.0, ab_im
    den = a_re * a_re + a_im * a_im
    q_re = ((nr * a_re + ni * a_im) / den)[..., None]
    q_im = ((ni * a_re - nr * a_im) / den)[..., None]
    b_re, b_im = b_re.astype(F32), b_im.astype(F32)
    bb = jnp.concatenate([q_re * b_re - q_im * b_im, q_re * b_im + q_im * b_re], axis=1)

    rev = slice(CHUNK - 1, None, -1)
    apt = jnp.concatenate([ap_re[:, rev, :SSM_STATE].transpose(0, 2, 1),
                           ap_im[:, rev, :SSM_STATE].transpose(0, 2, 1)], axis=1)
    dv = d_vec.astype(F32).reshape(N_GROUPS, SSM_GROUP, 1)
    are = ap_re[:, CHUNK, :SSM_STATE].reshape(1, N_STATE_ROWS)
    aim = ap_im[:, CHUNK, :SSM_STATE].reshape(1, N_STATE_ROWS)
    return e1, e2, c1, c2, apt, bb, dv, are, aim


def kernel(x_prompt, x_sample, cache_conv, state_ssm_re, state_ssm_im, meta_tokens, norm_w,
           final_norm_w, conv_w_in, conv_w, conv_b, conv_w_out, ssm_w_in, ssm_a_re, ssm_a_im,
           ssm_b_re, ssm_b_im, ssm_c_re, ssm_c_im, ssm_d, ssm_log_dt, ssm_w_glu, ssm_b_glu,
           ssm_w_out):
    n_seq, seq_len = x_sample.shape[0], x_sample.shape[1]
    n_prompt_rows = x_prompt.shape[1]
    n_ptiles = n_prompt_rows // TILE
    n_sample_rows = n_seq * seq_len
    assert x_prompt.shape[0] == 1 and seq_len == 2 * CHUNK and N_META == CHUNK
    assert n_prompt_rows % TILE == 0 and N_META + n_sample_rows <= SPECIAL_ROWS

    w_in0 = conv_w_in.astype(F32)
    conv_w8 = jnp.zeros((8, D_INNER), F32).at[0:3].set(conv_w[0].astype(F32))
    conv_b2 = conv_b[0].astype(F32).reshape(1, D_INNER)
    b_glu = ssm_b_glu[0].astype(F32).reshape(D_INNER, 1)
    nw0 = norm_w[0].astype(F32).reshape(1, D_MODEL)
    nw1 = norm_w[1].astype(F32).reshape(1, D_MODEL)
    nwf = final_norm_w.astype(F32).reshape(1, D_MODEL)
    ssm_ops = _ssm_operators(
        ssm_a_re[0], ssm_a_im[0], ssm_b_re[0], ssm_b_im[0], ssm_c_re[0], ssm_c_im[0],
        ssm_d[0], ssm_log_dt[0])
    sre = state_ssm_re[0].astype(F32).reshape(n_seq, N_STATE_ROWS)
    sim = state_ssm_im[0].astype(F32).reshape(n_seq, N_STATE_ROWS)

    xp = x_prompt.astype(F32)
    x_sp = jnp.concatenate([
        meta_tokens.astype(F32), x_sample.astype(F32).reshape(n_sample_rows, D_MODEL),
        jnp.zeros((SPECIAL_ROWS - N_META - n_sample_rows, D_MODEL), F32)], axis=0)[None]

    zeros8 = jnp.zeros((8, D_INNER), F32)
    xn_sp = _rmsnorm(x_sp, nw0, tm=SPECIAL_ROWS)
    xn_p = _rmsnorm(xp, nw0, tm=TM_CONV)
    sp_real = N_META + n_sample_rows
    y_sp, cv_sp = _conv_proj(xn_sp, w_in0, conv_w8, conv_b2, zeros8,
                             cache_conv[0, :, 0, :].astype(F32), cache_conv[0, :, 1, :].astype(F32),
                             real_rows=sp_real)
    init8 = zeros8.at[6:8].set(cv_sp[N_META - 2:N_META])
    y_p, tail_p, w_out0, wuz_t = _conv_proj(
        xn_p, w_in0, conv_w8, conv_b2, init8,
        cast_w=conv_w_out.astype(F32), cast_w_t=ssm_w_in.astype(F32))
    h1_p, xs_p = _out_proj(y_p, w_out0, xp, nw1, lanes=LANES)
    h1_sp, xs_sp = _out_proj(y_sp, w_out0, x_sp, nw1, lanes=SPECIAL_ROWS // CHUNK,
                             real_rows=sp_real)

    gt_sp, szt_sp, gt_p, szt_p, hs_re, hs_im, hp_re, hp_im, wg_t, wo_t = _ssm(
        xs_sp.reshape(SPECIAL_ROWS, D_MODEL), xs_p.reshape(n_ptiles, TILE, D_MODEL),
        wuz_t, *ssm_ops, sre, sim, ssm_w_glu.astype(F32), ssm_w_out.astype(F32))
    n_sc = n_sample_rows // CHUNK
    o_sp = _glu_out(gt_sp, szt_sp, wg_t, b_glu, wo_t, h1_sp[:, :, 1:1 + n_sc], nwf)
    o_p = _glu_out(gt_p, szt_p, wg_t, b_glu, wo_t, h1_p, nwf)

    y_prompt = _to_rows(o_p)
    y_sample = o_sp.reshape(n_seq, seq_len, D_MODEL)
    new_conv_prompt = tail_p[-1, 6:8].reshape(1, 1, 2, D_INNER)
    cv_s = cv_sp[N_META:N_META + n_sample_rows].reshape(n_seq, seq_len, D_INNER)
    new_conv_sample = cv_s[:, seq_len - 2:].reshape(1, n_seq, 2, D_INNER)
    p_shape = (1, 1, N_GROUPS, SSM_STATE)
    s_shape = (1, n_seq, N_GROUPS, SSM_STATE)
    return (y_prompt, y_sample, new_conv_prompt, new_conv_sample,
            hp_re[0].reshape(p_shape), hp_im[0].reshape(p_shape),
            hs_re.reshape(s_shape), hs_im.reshape(s_shape))
```

```python
import functools

import jax
import jax.numpy as jnp
from jax import lax
from jax.experimental import pallas as pl
from jax.experimental.pallas import tpu as pltpu

F32 = jnp.float32
BF16 = jnp.bfloat16

D_MODEL = 2048
D_INNER = 4096
N_META = 16
SSM_GROUP = 16
N_GROUPS = D_INNER // SSM_GROUP
SSM_STATE = 64
N_STATE_ROWS = N_GROUPS * SSM_STATE
RMS_EPS = 1e-6

CHUNK = 16
CHUNK_ROWS = CHUNK * SSM_GROUP
LANES = 128
TILE = CHUNK * LANES
SPECIAL_ROWS = 1024

TM_CONV = 1024
TE_CONV = 512
TM_OUT = 512
TE_SSM = 256
TE_GLU = 512
LB_GLU = 512
TM_ROWS = 1024
CAST_ROWS = 512

VMEM_LIMIT = 56 * 1024 * 1024
VMEM_LIMIT_LARGE = 62 * 1024 * 1024


def _params(n_axes, vmem=VMEM_LIMIT):
    return pltpu.CompilerParams(dimension_semantics=("arbitrary",) * n_axes,
                                vmem_limit_bytes=vmem)


def _rms(x, w):
    ms = jnp.mean(x * x, axis=-1, keepdims=True)
    return x * lax.rsqrt(ms + RMS_EPS) * w


def _hosted_cast(w, n_steps, step_of, *, transpose):
    _, r, c = w.shape
    bc = (r * c) // (n_steps * CAST_ROWS)
    nc = c // bc
    assert bc % LANES == 0 and (r // CAST_ROWS) * nc == n_steps
    in_spec = pl.BlockSpec((None, CAST_ROWS, bc),
                           lambda *g: (0, step_of(*g) // nc, step_of(*g) % nc))
    if transpose:
        out_spec = pl.BlockSpec((bc, CAST_ROWS),
                                lambda *g: (step_of(*g) % nc, step_of(*g) // nc))
        return in_spec, out_spec, jax.ShapeDtypeStruct((c, r), BF16)
    out_spec = pl.BlockSpec((CAST_ROWS, bc), lambda *g: (step_of(*g) // nc, step_of(*g) % nc))
    return in_spec, out_spec, jax.ShapeDtypeStruct((r, c), BF16)


def _hosted_cast_step(src_ref, dst_ref, *, transpose):
    x = src_ref[...]
    dst_ref[...] = (x.T if transpose else x).astype(BF16)


def _rmsnorm_kernel(x_ref, w_ref, o_ref):
    o_ref[...] = _rms(x_ref[...], w_ref[...]).astype(BF16)


def _rmsnorm(x, w, *, tm):
    rows = x.shape[1]
    return pl.pallas_call(
        _rmsnorm_kernel, grid=(rows // tm,),
        in_specs=[pl.BlockSpec((None, tm, D_MODEL), lambda i: (0, i, 0)),
                  pl.BlockSpec((1, D_MODEL), lambda i: (0, 0))],
        out_specs=pl.BlockSpec((tm, D_MODEL), lambda i: (i, 0)),
        out_shape=jax.ShapeDtypeStruct((rows, D_MODEL), BF16),
        compiler_params=_params(1), name="rmsnorm",
    )(x, w)


def _special_rows_kernel(meta_ref, xs_ref, w_ref, x_ref, xn_ref):
    n_meta = meta_ref.shape[0]
    n_real = n_meta + xs_ref.shape[0]
    n_pad = x_ref.shape[0] - n_real
    for ref, lo, hi in ((meta_ref, 0, n_meta), (xs_ref, n_meta, n_real)):
        x = ref[...]
        x_ref[lo:hi, :] = x
        xn_ref[lo:hi, :] = _rms(x, w_ref[...]).astype(BF16)
    x_ref[n_real:, :] = jnp.zeros((n_pad, D_MODEL), F32)
    xn_ref[n_real:, :] = jnp.zeros((n_pad, D_MODEL), BF16)


def _special_rows(meta, xs, w):
    whole = lambda a: pl.BlockSpec(a.shape, lambda i: (0,) * a.ndim)
    return pl.pallas_call(
        _special_rows_kernel, grid=(1,),
        in_specs=[whole(meta), whole(xs), whole(w)],
        out_specs=(pl.BlockSpec((None, SPECIAL_ROWS, D_MODEL), lambda i: (0, 0, 0)),
                   pl.BlockSpec((SPECIAL_ROWS, D_MODEL), lambda i: (0, 0))),
        out_shape=(jax.ShapeDtypeStruct((1, SPECIAL_ROWS, D_MODEL), F32),
                   jax.ShapeDtypeStruct((SPECIAL_ROWS, D_MODEL), BF16)),
        compiler_params=_params(1), name="special_rows",
    )(meta, xs, w)


def _conv_proj_kernel(*refs, tm, te, n_pieces, special, real_rows):
    if special:
        (xn_ref, wb_ref, wc_ref, wv_ref, wz_ref, cw_ref, cb_ref, init_ref, c0_ref, c1_ref,
         y_ref, cv_ref, w_scr, s_scr, p1_scr, p2_scr) = refs
    else:
        (xn_ref, wb_ref, wc_ref, wv_ref, wz_ref, cw_ref, cb_ref, init_ref, wp_ref, wt_ref,
         y_ref, tail_ref, wp_bf_ref, wt_bf_ref, w_scr, s_scr) = refs
    i = pl.program_id(1)

    @pl.when(i == 0)
    def _():
        for p, w_ref in enumerate((wb_ref, wc_ref, wv_ref, wz_ref)):
            w_scr[:, p * te:(p + 1) * te] = w_ref[...].astype(BF16)
        s_scr[0:8, :] = init_ref[...]

    if special:
        pm = real_rows
        cv_ref[pm:tm, :] = jnp.zeros((tm - pm, te), F32)
        y_ref[pm:tm, :] = jnp.zeros((tm - pm, te), BF16)
    else:
        pm = tm // n_pieces
    for r in range(n_pieces):
        lo = r * pm
        if not special and r == n_pieces - 1:
            _hosted_cast_step(wp_ref, wp_bf_ref, transpose=False)
            _hosted_cast_step(wt_ref, wt_bf_ref, transpose=True)
        proj = jnp.dot(xn_ref[lo:lo + pm, :], w_scr[...], preferred_element_type=F32)
        bg = proj[:, 0 * te:1 * te]
        cg = proj[:, 1 * te:2 * te]
        vv = proj[:, 2 * te:3 * te]
        zz = proj[:, 3 * te:4 * te]
        cv = cg * vv
        s_scr[8 + lo:8 + lo + pm, :] = cv
        if special:
            cv_ref[lo:lo + pm, :] = cv
            p1_scr[...] = s_scr[7 + lo:7 + lo + pm, :]
            p2_scr[...] = s_scr[6 + lo:6 + lo + pm, :]
            for q in range(c0_ref.shape[0]):
                r0 = N_META + 32 * q
                p1_scr[r0:r0 + 1, :] = c1_ref[q:q + 1, :]
                p2_scr[r0:r0 + 1, :] = c0_ref[q:q + 1, :]
                p2_scr[r0 + 1:r0 + 2, :] = c1_ref[q:q + 1, :]
            p1 = p1_scr[...]
            p2 = p2_scr[...]
        else:
            p1 = s_scr[7 + lo:7 + lo + pm, :]
            p2 = s_scr[6 + lo:6 + lo + pm, :]
        conv = cb_ref[...] + cw_ref[0:1, :] * p2
        conv = conv + cw_ref[1:2, :] * p1
        conv = conv + cw_ref[2:3, :] * cv
        y_ref[lo:lo + pm, :] = (bg * conv * jax.nn.silu(zz)).astype(BF16)

    if not special:
        tail = s_scr[tm:tm + 8, :]
        s_scr[0:8, :] = tail
        tail_ref[...] = tail


def _conv_proj(xn, w_in, conv_w8, conv_b, init8, cache0=None, cache1=None, real_rows=None,
               cast_w=None, cast_w_t=None):
    rows = xn.shape[0]
    tm, te = TM_CONV, TE_CONV
    n_j = D_INNER // te
    n_i = rows // tm
    special = cache0 is not None
    w_spec = lambda p: pl.BlockSpec((None, D_MODEL, te),
                                    lambda j, i, p=p: (0, 0, p * n_j + j))
    in_specs = [pl.BlockSpec((tm, D_MODEL), lambda j, i: (i, 0)),
                w_spec(0), w_spec(1), w_spec(2), w_spec(3),
                pl.BlockSpec((8, te), lambda j, i: (0, j)),
                pl.BlockSpec((1, te), lambda j, i: (0, j)),
                pl.BlockSpec((8, te), lambda j, i: (0, j))]
    args = [xn, w_in, w_in, w_in, w_in, conv_w8, conv_b, init8]
    scratch = [pltpu.VMEM((D_MODEL, 4 * te), BF16), pltpu.VMEM((tm + 8, te), F32)]
    y_shape = jax.ShapeDtypeStruct((rows, D_INNER), BF16)
    y_spec = pl.BlockSpec((tm, te), lambda j, i: (i, j))
    if special:
        assert n_i == 1
        nq = cache0.shape[0]
        in_specs += [pl.BlockSpec((nq, te), lambda j, i: (0, j)),
                     pl.BlockSpec((nq, te), lambda j, i: (0, j))]
        args += [cache0, cache1]
        out_shape = (y_shape, jax.ShapeDtypeStruct((rows, D_INNER), F32))
        out_specs = (y_spec, pl.BlockSpec((tm, te), lambda j, i: (i, j)))
        scratch += [pltpu.VMEM((real_rows, te), F32), pltpu.VMEM((real_rows, te), F32)]
    else:
        step_of = lambda j, i: j * n_i + i
        wp_in, wp_out, wp_shape = _hosted_cast(cast_w, n_j * n_i, step_of, transpose=False)
        wt_in, wt_out, wt_shape = _hosted_cast(cast_w_t, n_j * n_i, step_of, transpose=True)
        in_specs += [wp_in, wt_in]
        args += [cast_w, cast_w_t]
        out_shape = (y_shape, jax.ShapeDtypeStruct((n_i, 8, D_INNER), F32), wp_shape, wt_shape)
        out_specs = (y_spec, pl.BlockSpec((None, 8, te), lambda j, i: (i, 0, j)), wp_out, wt_out)
    return pl.pallas_call(
        functools.partial(_conv_proj_kernel, tm=tm, te=te, n_pieces=1 if special else 2,
                          special=special, real_rows=real_rows),
        grid=(n_j, n_i), in_specs=in_specs, out_specs=out_specs, out_shape=out_shape,
        scratch_shapes=scratch, compiler_params=_params(2, VMEM_LIMIT_LARGE),
        name="conv_proj_special" if special else "conv_proj",
    )(*args)


def _out_proj_kernel(y_ref, w_ref, x_ref, nw_ref, h_ref, xs_ref, slab_scr, *, real_rows, n_steps):
    tm = y_ref.shape[0]
    nc = tm // CHUNK
    n_slabs = D_MODEL // LANES

    def block(r):
        if r == 0:
            h = jnp.zeros((tm, D_MODEL), F32)
        else:
            h = x_ref[0:r, :] + jnp.dot(y_ref[0:r, :], w_ref[...], preferred_element_type=F32)
            if r < tm:
                h = jnp.concatenate([h, jnp.zeros((tm - r, D_MODEL), F32)], axis=0)
        for k in range(n_slabs):
            slab_scr[k] = h[:, LANES * k:LANES * (k + 1)]
        for s in range(CHUNK):
            hs = jnp.concatenate(
                [slab_scr[k, pl.ds(s, nc, stride=CHUNK), :] for k in range(n_slabs)], axis=1)
            h_ref[s] = hs
            xs_ref[s] = _rms(hs, nw_ref[...]).astype(BF16)

    if real_rows is None:
        block(tm)
    else:
        i = pl.program_id(0)
        for b in range(n_steps):
            pl.when(i == b)(functools.partial(block, min(max(real_rows - b * tm, 0), tm)))


def _out_proj(y, w_out, x, norm_w, *, lanes, real_rows=None):
    tm = TM_OUT
    nc = tm // CHUNK
    per_tile = lanes // nc
    n_i = x.shape[1] // tm
    tile_map = lambda i: (i // per_tile, 0, i % per_tile, 0)
    tile_shape = (n_i // per_tile, CHUNK, lanes, D_MODEL)
    return pl.pallas_call(
        functools.partial(_out_proj_kernel, real_rows=real_rows, n_steps=n_i),
        grid=(n_i,),
        in_specs=[
            pl.BlockSpec((tm, D_INNER), lambda i: (i, 0)),
            pl.BlockSpec((D_INNER, D_MODEL), lambda i: (0, 0), pipeline_mode=pl.Buffered(1)),
            pl.BlockSpec((None, tm, D_MODEL), lambda i: (0, i, 0)),
            pl.BlockSpec((1, D_MODEL), lambda i: (0, 0)),
        ],
        out_specs=(pl.BlockSpec((None, CHUNK, nc, D_MODEL), tile_map),
                   pl.BlockSpec((None, CHUNK, nc, D_MODEL), tile_map)),
        out_shape=(jax.ShapeDtypeStruct(tile_shape, F32),
                   jax.ShapeDtypeStruct(tile_shape, BF16)),
        scratch_shapes=[pltpu.VMEM((D_MODEL // LANES, tm, LANES), F32)],
        compiler_params=_params(1, VMEM_LIMIT_LARGE),
        name="out_proj",
    )(y, w_out, x, norm_w)


def _split_bf16(x):
    hi = x.astype(BF16)
    return hi, (x - hi.astype(F32)).astype(BF16)


def _ssm_kernel(xsp_ref, xs_ref, wu_ref, wz_ref, e1_ref, e2_ref, c1_ref, c2_ref, apt_ref, bb_ref,
                dv_ref, are_ref, aim_ref, sre_ref, sim_ref, wg_ref, wo_ref,
                gtsp_ref, sztsp_ref, gt_ref, szt_ref, hsre_ref, hsim_ref, hpre_ref, hpim_ref,
                wgt_ref, wot_ref, w_scr, tk_scr, win_scr, cout_scr, zt_scr, xre_scr, xim_scr, pre_scr, pim_scr,
                cre_scr, cim_scr, *, te, n_tiles):
    i = pl.program_id(1)
    gb = te // SSM_GROUP
    rows = gb * SSM_STATE
    n_seq = sre_ref.shape[0]
    bdims = (((2,), (1,)), ((0,), (0,)))

    def build_operators():
        w_scr[0:te, :] = wu_ref[...]
        w_scr[te:2 * te, :] = wz_ref[...]

        c1 = c1_ref[...]
        c2 = c2_ref[...]
        cap = [c1 * e1_ref[:, t:t + 1, :] + c2 * e2_ref[:, t:t + 1, :] for t in range(CHUNK + 1)]
        ca = jnp.concatenate(cap[:CHUNK], axis=1)
        cout_scr[...] = jnp.concatenate(cap[1:], axis=1).astype(BF16)

        lane = lax.broadcasted_iota(jnp.int32, (SSM_GROUP, CHUNK_ROWS), 1)
        row = lax.broadcasted_iota(jnp.int32, (SSM_GROUP, CHUNK_ROWS), 0)
        diag = row == (lane & (SSM_GROUP - 1))
        tile_c = diag.astype(BF16)
        spread_s = (row == lane // SSM_GROUP).astype(BF16)
        expand = lambda b, m, n_rows: jnp.dot(b, m, preferred_element_type=F32).reshape(
            gb, n_rows, CHUNK_ROWS)
        bhi, blo = _split_bf16(bb_ref[...].reshape(gb * 2 * SSM_STATE, SSM_GROUP))
        bhi_t = expand(bhi, tile_c, 2 * SSM_STATE)
        blo_t = expand(blo, tile_c, 2 * SSM_STATE)

        phi, plo = _split_bf16(apt_ref[...].reshape(gb * 2 * SSM_STATE, SSM_GROUP))
        ap_t = expand(phi, spread_s, 2 * SSM_STATE) + expand(plo, spread_s, 2 * SSM_STATE)
        b_t = bhi_t + blo_t
        pr, pi = ap_t[:, :SSM_STATE], ap_t[:, SSM_STATE:]
        br, bi = b_t[:, :SSM_STATE], b_t[:, SSM_STATE:]
        win_scr[:, 0:SSM_STATE, :] = (pr * br - pi * bi).astype(BF16)
        win_scr[:, SSM_STATE:, :] = (pr * bi + pi * br).astype(BF16)

        bhi_t, blo_t = bhi_t.astype(BF16), blo_t.astype(BF16)
        ahi, alo = _split_bf16(ca)
        kw = (lax.dot_general(ahi, bhi_t, bdims, preferred_element_type=F32)
              + lax.dot_general(ahi, blo_t, bdims, preferred_element_type=F32)
              + lax.dot_general(alo, bhi_t, bdims, preferred_element_type=F32))
        kw0 = (kw[:, 0:SSM_GROUP, :] + jnp.where(diag, dv_ref[...], 0.0)).astype(BF16)
        kw = kw.astype(BF16)
        tk_scr[...] = jnp.zeros_like(tk_scr)
        for s in range(CHUNK):
            lo = SSM_GROUP * s
            hi = lo + SSM_GROUP
            tk_scr[:, lo:hi, lo:hi] = kw0[:, :, lo:hi]
            if hi < CHUNK_ROWS:
                tk_scr[:, hi:CHUNK_ROWS, lo:hi] = kw[:, SSM_GROUP:CHUNK_ROWS - lo, lo:hi]

    nt_dims = (((1,), (1,)), ((), ()))
    half = LANES // 2
    low_half = lax.broadcasted_iota(jnp.int32, (1, LANES), 1) < half

    ar = are_ref[...]
    ai = aim_ref[...]

    def step(hr, hi, n):
        xr = xre_scr[pl.ds(n, 1), :]
        xi = xim_scr[pl.ds(n, 1), :]
        return ar * hr - ai * hi + xr, ar * hi + ai * hr + xi

    def chunk_inputs():
        zt = zt_scr[...]
        yt = lax.dot_general(tk_scr[...], zt, bdims, preferred_element_type=F32)
        xt = lax.dot_general(win_scr[...], zt, bdims, preferred_element_type=F32)
        xre_scr[...] = xt[:, 0:SSM_STATE, :].reshape(rows, LANES).T
        xim_scr[...] = xt[:, SSM_STATE:, :].reshape(rows, LANES).T
        return yt

    def gelu_pieces(yt):
        hp = jnp.concatenate([pre_scr[...].T.reshape(gb, SSM_STATE, LANES),
                              pim_scr[...].T.reshape(gb, SSM_STATE, LANES)], axis=1)
        ycorr = lax.dot_general(cout_scr[...], hp.astype(BF16), bdims,
                                preferred_element_type=F32)
        g = jax.nn.gelu(yt + ycorr)
        return [g[:, SSM_GROUP * t:SSM_GROUP * (t + 1), :].reshape(te, LANES)
                for t in range(CHUNK)]

    def pack_samples(out_ref, block_of, first_lane_of):
        quarter = LANES // 4
        lane_q = lax.broadcasted_iota(jnp.int32, (1, LANES), 1) // quarter
        for c in range(CHUNK // 4):
            out = jnp.zeros((te, LANES), F32)
            for k in range(4):
                t = 4 * c + k
                shift = (quarter * k - first_lane_of(t)) % LANES
                blk = block_of(t)
                out = jnp.where(lane_q == k, pltpu.roll(blk, shift, 1) if shift else blk, out)
            out_ref[:, LANES * c:LANES * (c + 1)] = out.astype(BF16)

    @pl.when(i == 0)
    def _():
        build_operators()
        uz = lax.dot_general(w_scr[...], xsp_ref[...], nt_dims,
                             preferred_element_type=F32)
        sz = jax.nn.silu(uz[te:, :])
        pack_samples(sztsp_ref, lambda t: sz[:, LANES * (t // 2):LANES * (t // 2 + 1)],
                     lambda t: half * (t % 2) + 1)
        for s in range(CHUNK):
            blk = uz[:te, LANES * (s // 2):LANES * (s // 2 + 1)]
            if s % 2:
                blk = pltpu.roll(blk, half, 1)
            zt_scr[:, CHUNK * s:CHUNK * (s + 1), :] = (
                jnp.where(low_half, blk, 0.0).astype(BF16).reshape(gb, SSM_GROUP, LANES))
        yt = chunk_inputs()

        pre_scr[...] = jnp.zeros_like(pre_scr)
        pim_scr[...] = jnp.zeros_like(pim_scr)
        zero = jnp.zeros((1, rows), F32)
        hr, hi = step(zero, zero, 0)
        cre_scr[0:1, :] = hr
        cim_scr[0:1, :] = hi
        for q in range(n_seq):
            hr = sre_ref[q:q + 1, :]
            hi = sim_ref[q:q + 1, :]
            for e in range(2):
                n = 1 + 2 * q + e
                pre_scr[n:n + 1, :] = hr
                pim_scr[n:n + 1, :] = hi
                hr, hi = step(hr, hi, n)
            hsre_ref[q:q + 1, :] = hr
            hsim_ref[q:q + 1, :] = hi

        pieces = gelu_pieces(yt)
        pack_samples(gtsp_ref, lambda t: pieces[t], lambda t: 1)

    @pl.when(i > 0)
    def _():
        uz = lax.dot_general(w_scr[...], xs_ref[...], nt_dims,
                             preferred_element_type=F32)
        szt_ref[...] = jax.nn.silu(uz[te:, :]).astype(BF16)
        ub = uz[:te, :].astype(BF16)
        for s in range(CHUNK):
            zt_scr[:, CHUNK * s:CHUNK * (s + 1), :] = (
                ub[:, LANES * s:LANES * (s + 1)].reshape(gb, SSM_GROUP, LANES))
        yt = chunk_inputs()
        _hosted_cast_step(wg_ref, wgt_ref, transpose=True)
        _hosted_cast_step(wo_ref, wot_ref, transpose=True)

        hr, hi = cre_scr[0:1, :], cim_scr[0:1, :]
        for n in range(LANES):
            pre_scr[n:n + 1, :] = hr
            pim_scr[n:n + 1, :] = hi
            hr, hi = step(hr, hi, n)
        cre_scr[0:1, :] = hr
        cim_scr[0:1, :] = hi

        @pl.when(i == n_tiles - 1)
        def _():
            hpre_ref[...] = jnp.broadcast_to(hr, hpre_ref.shape)
            hpim_ref[...] = jnp.broadcast_to(hi, hpim_ref.shape)

        pieces = gelu_pieces(yt)
        for t in range(CHUNK):
            gt_ref[:, LANES * t:LANES * (t + 1)] = pieces[t].astype(BF16)


def _ssm(xs_sp, xs_p, wuz_t, e1, e2, c1, c2, apt, bb, dv, are, aim, sre, sim, w_glu, w_out):
    n_ptiles = xs_p.shape[0]
    n_tiles = 1 + n_ptiles
    te = TE_SSM
    n_j = D_INNER // te
    gb = te // SSM_GROUP
    rows = gb * SSM_STATE
    n_seq = sre.shape[0]
    prev = lambda i: jnp.maximum(i - 1, 0)
    act_shape = jax.ShapeDtypeStruct((n_ptiles, D_INNER, TILE), BF16)
    act_spec = pl.BlockSpec((None, te, TILE), lambda j, i: (prev(i), j, 0))
    assert 2 * n_seq == LANES // 4
    sp_slots = CHUNK * 2 * n_seq
    sp_shape = jax.ShapeDtypeStruct((1, D_INNER, sp_slots), BF16)
    sp_spec = pl.BlockSpec((None, te, sp_slots), lambda j, i: (0, j, 0))
    hs_shape = jax.ShapeDtypeStruct((n_seq, N_STATE_ROWS), F32)
    hs_spec = pl.BlockSpec((n_seq, rows), lambda j, i: (0, j))
    hp_shape = jax.ShapeDtypeStruct((8, N_STATE_ROWS), F32)
    hp_spec = pl.BlockSpec((8, rows), lambda j, i: (0, j))
    row_spec = pl.BlockSpec((1, rows), lambda j, i: (0, j))
    grp_spec = lambda r, c: pl.BlockSpec((gb, r, c), lambda j, i: (j, 0, 0))
    prompt_step = lambda j, i: j * n_ptiles + prev(i)
    wg_in, wg_out, wg_shape = _hosted_cast(w_glu, n_j * n_ptiles, prompt_step, transpose=True)
    wo_in, wo_out, wo_shape = _hosted_cast(w_out, n_j * n_ptiles, prompt_step, transpose=True)
    return pl.pallas_call(
        functools.partial(_ssm_kernel, te=te, n_tiles=n_tiles),
        grid=(n_j, n_tiles),
        in_specs=[
            pl.BlockSpec((TILE // 2, D_MODEL), lambda j, i: (0, 0), pipeline_mode=pl.Buffered(1)),
            pl.BlockSpec((None, TILE, D_MODEL), lambda j, i: (prev(i), 0, 0)),
            pl.BlockSpec((te, D_MODEL), lambda j, i: (j, 0)),
            pl.BlockSpec((te, D_MODEL), lambda j, i: (n_j + j, 0)),
            grp_spec(e1.shape[1], 2 * SSM_STATE), grp_spec(e2.shape[1], 2 * SSM_STATE),
            grp_spec(SSM_GROUP, 2 * SSM_STATE), grp_spec(SSM_GROUP, 2 * SSM_STATE),
            grp_spec(2 * SSM_STATE, SSM_GROUP), grp_spec(2 * SSM_STATE, SSM_GROUP),
            grp_spec(SSM_GROUP, 1),
            row_spec, row_spec, hs_spec, hs_spec, wg_in, wo_in,
        ],
        out_specs=(sp_spec, sp_spec, act_spec, act_spec, hs_spec, hs_spec, hp_spec, hp_spec,
                   wg_out, wo_out),
        out_shape=(sp_shape, sp_shape, act_shape, act_shape, hs_shape, hs_shape,
                   hp_shape, hp_shape, wg_shape, wo_shape),
        scratch_shapes=[pltpu.VMEM((2 * te, D_MODEL), BF16),
                        pltpu.VMEM((gb, CHUNK_ROWS, CHUNK_ROWS), BF16),
                        pltpu.VMEM((gb, 2 * SSM_STATE, CHUNK_ROWS), BF16),
                        pltpu.VMEM((gb, CHUNK_ROWS, 2 * SSM_STATE), BF16),
                        pltpu.VMEM((gb, CHUNK_ROWS, LANES), BF16),
                        pltpu.VMEM((LANES, rows), F32), pltpu.VMEM((LANES, rows), F32),
                        pltpu.VMEM((LANES, rows), F32), pltpu.VMEM((LANES, rows), F32),
                        pltpu.VMEM((8, rows), F32), pltpu.VMEM((8, rows), F32)],
        compiler_params=_params(2, VMEM_LIMIT_LARGE),
        name="s5_scan",
    )(xs_sp, xs_p, wuz_t, wuz_t, e1, e2, c1, c2, apt, bb, dv, are, aim, sre, sim, w_glu, w_out)


def _glu_out_kernel(gt_ref, szt_ref, wg_ref, bg_ref, wo_ref, h_ref, nw_ref, o_ref, acc_scr, *,
                    te, lb):
    e = pl.program_id(2)

    @pl.when(e == 0)
    def _():
        acc_scr[...] = jnp.zeros_like(acc_scr)

    gate = jnp.dot(wg_ref[...], gt_ref[...], preferred_element_type=F32) + bg_ref[...]
    ge = gt_ref[pl.ds(pl.multiple_of(e * te, te), te), :].astype(F32)
    y3 = (ge * jax.nn.sigmoid(gate)) * szt_ref[...].astype(F32)
    acc_scr[...] += jnp.dot(wo_ref[...], y3.astype(BF16), preferred_element_type=F32)

    @pl.when(e == pl.num_programs(2) - 1)
    def _():
        o = acc_scr[...].T
        ln = h_ref.shape[1]
        for t in range(lb // ln):
            h = h_ref[t] + o[ln * t:ln * (t + 1), :]
            o_ref[:, D_MODEL * t:D_MODEL * (t + 1)] = _rms(h, nw_ref[...])


def _glu_out(gt, szt, wg_t, b_glu, wo_t, h_all, norm_w):
    n_tiles, _, lanes, _ = h_all.shape
    te = TE_GLU
    lb = LB_GLU
    n_l = CHUNK * lanes // lb
    n_e = D_INNER // te
    tpb = lb // lanes
    return pl.pallas_call(
        functools.partial(_glu_out_kernel, te=te, lb=lb),
        grid=(n_tiles, n_l, n_e),
        in_specs=[
            pl.BlockSpec((None, D_INNER, lb), lambda i, l, e: (i, 0, l)),
            pl.BlockSpec((None, te, lb), lambda i, l, e: (i, e, l)),
            pl.BlockSpec((te, D_INNER), lambda i, l, e: (e, 0)),
            pl.BlockSpec((te, 1), lambda i, l, e: (e, 0)),
            pl.BlockSpec((D_MODEL, te), lambda i, l, e: (0, e)),
            pl.BlockSpec((None, tpb, lanes, D_MODEL), lambda i, l, e: (i, l, 0, 0)),
            pl.BlockSpec((1, D_MODEL), lambda i, l, e: (0, 0)),
        ],
        out_specs=pl.BlockSpec((lanes, tpb * D_MODEL), lambda i, l, e: (i, l)),
        out_shape=jax.ShapeDtypeStruct((n_tiles * lanes, CHUNK * D_MODEL), F32),
        scratch_shapes=[pltpu.VMEM((D_MODEL, lb), F32)],
        compiler_params=_params(3),
        name="glu_out",
    )(gt, szt, wg_t, b_glu, wo_t, h_all, norm_w)


def _to_rows_kernel(o_ref, y_ref, slab_scr):
    nc = o_ref.shape[0]
    n_slabs = D_MODEL // LANES
    for t in range(CHUNK):
        for k in range(n_slabs):
            slab_scr[k, pl.ds(t, nc, stride=CHUNK), :] = (
                o_ref[:, D_MODEL * t + LANES * k:D_MODEL * t + LANES * (k + 1)])
    for k in range(n_slabs):
        y_ref[:, LANES * k:LANES * (k + 1)] = slab_scr[k]


def _to_rows(o):
    n_chunks = o.shape[0]
    tm = TM_ROWS
    nc = tm // CHUNK
    return pl.pallas_call(
        _to_rows_kernel,
        grid=(n_chunks // nc,),
        in_specs=[pl.BlockSpec((nc, CHUNK * D_MODEL), lambda i: (i, 0))],
        out_specs=pl.BlockSpec((None, tm, D_MODEL), lambda i: (0, i, 0)),
        out_shape=jax.ShapeDtypeStruct((1, n_chunks * CHUNK, D_MODEL), F32),
        scratch_shapes=[pltpu.VMEM((D_MODEL // LANES, tm, LANES), F32)],
        compiler_params=_params(1),
        name="to_rows",
    )(o)


def _ssm_operators(a_re, a_im, b_re, b_im, c_re, c_im, d_vec, log_dt):
    a_re, a_im = a_re.astype(F32), a_im.astype(F32)
    dt = jnp.exp(log_dt.astype(F32))[:, None]
    th_re = jnp.concatenate([a_re * dt] * 2, axis=1)[:, None, :]
    th_im = jnp.concatenate([a_im * dt] * 2, axis=1)[:, None, :]
    n_tau = 24
    taus = jnp.arange(n_tau, dtype=F32)[None, :, None]
    used = taus <= CHUNK
    mag = jnp.exp(jnp.where(used, th_re * taus, 0.0))
    ang = jnp.where(used, th_im * taus, 0.0)
    ap_re = mag * jnp.cos(ang)
    ap_im = mag * jnp.sin(ang)
    sign = jnp.concatenate([jnp.ones((SSM_STATE,), F32), -jnp.ones((SSM_STATE,), F32)])
    e1 = ap_re * sign
    e2 = -ap_im
    c_re, c_im = c_re.astype(F32), c_im.astype(F32)
    c1 = jnp.concatenate([c_re, c_im], axis=2)
    c2 = jnp.concatenate([c_im, c_re], axis=2)

    ab_re, ab_im = ap_re[:, 1, :SSM_STATE], ap_im[:, 1, :SSM_STATE]
    nr, ni = ab_re - 1.0, ab_im
    den = a_re * a_re + a_im * a_im
    q_re = ((nr * a_re + ni * a_im) / den)[..., None]
    q_im = ((ni * a_re - nr * a_im) / den)[..., None]
    b_re, b_im = b_re.astype(F32), b_im.astype(F32)
    bb = jnp.concatenate([q_re * b_re - q_im * b_im, q_re * b_im + q_im * b_re], axis=1)

    rev = slice(CHUNK - 1, None, -1)
    apt = jnp.concatenate([ap_re[:, rev, :SSM_STATE].transpose(0, 2, 1),
                           ap_im[:, rev, :SSM_STATE].transpose(0, 2, 1)], axis=1)
    dv = d_vec.astype(F32).reshape(N_GROUPS, SSM_GROUP, 1)
    are = ap_re[:, CHUNK, :SSM_STATE].reshape(1, N_STATE_ROWS)
    aim = ap_im[:, CHUNK, :SSM_STATE].reshape(1, N_STATE_ROWS)
    return e1, e2, c1, c2, apt, bb, dv, are, aim


def kernel(x_prompt, x_sample, cache_conv, state_ssm_re, state_ssm_im, meta_tokens, norm_w,
           final_norm_w, conv_w_in, conv_w, conv_b, conv_w_out, ssm_w_in, ssm_a_re, ssm_a_im,
           ssm_b_re, ssm_b_im, ssm_c_re, ssm_c_im, ssm_d, ssm_log_dt, ssm_w_glu, ssm_b_glu,
           ssm_w_out):
    n_seq, seq_len = x_sample.shape[0], x_sample.shape[1]
    n_prompt_rows = x_prompt.shape[1]
    n_ptiles = n_prompt_rows // TILE
    n_sample_rows = n_seq * seq_len
    assert x_prompt.shape[0] == 1 and seq_len == 2 * CHUNK and N_META == CHUNK
    assert n_prompt_rows % TILE == 0 and N_META + n_sample_rows <= SPECIAL_ROWS

    w_in0 = conv_w_in.astype(F32)
    conv_w8 = jnp.zeros((8, D_INNER), F32).at[0:3].set(conv_w[0].astype(F32))
    conv_b2 = conv_b[0].astype(F32).reshape(1, D_INNER)
    b_glu = ssm_b_glu[0].astype(F32).reshape(D_INNER, 1)
    nw0 = norm_w[0].astype(F32).reshape(1, D_MODEL)
    nw1 = norm_w[1].astype(F32).reshape(1, D_MODEL)
    nwf = final_norm_w.astype(F32).reshape(1, D_MODEL)
    ssm_ops = _ssm_operators(
        ssm_a_re[0], ssm_a_im[0], ssm_b_re[0], ssm_b_im[0], ssm_c_re[0], ssm_c_im[0],
        ssm_d[0], ssm_log_dt[0])
    sre = state_ssm_re[0].astype(F32).reshape(n_seq, N_STATE_ROWS)
    sim = state_ssm_im[0].astype(F32).reshape(n_seq, N_STATE_ROWS)

    xp = x_prompt.astype(F32)

    zeros8 = jnp.zeros((8, D_INNER), F32)
    x_sp, xn_sp = _special_rows(meta_tokens.astype(F32),
                                x_sample.astype(F32).reshape(n_sample_rows, D_MODEL), nw0)
    xn_p = _rmsnorm(xp, nw0, tm=TM_CONV)
    sp_real = N_META + n_sample_rows
    y_sp, cv_sp = _conv_proj(xn_sp, w_in0, conv_w8, conv_b2, zeros8,
                             cache_conv[0, :, 0, :].astype(F32), cache_conv[0, :, 1, :].astype(F32),
                             real_rows=sp_real)
    init8 = zeros8.at[6:8].set(cv_sp[N_META - 2:N_META])
    y_p, tail_p, w_out0, wuz_t = _conv_proj(
        xn_p, w_in0, conv_w8, conv_b2, init8,
        cast_w=conv_w_out.astype(F32), cast_w_t=ssm_w_in.astype(F32))
    h1_p, xs_p = _out_proj(y_p, w_out0, xp, nw1, lanes=LANES)
    h1_sp, xs_sp = _out_proj(y_sp, w_out0, x_sp, nw1, lanes=SPECIAL_ROWS // CHUNK,
                             real_rows=sp_real)

    gt_sp, szt_sp, gt_p, szt_p, hs_re, hs_im, hp_re, hp_im, wg_t, wo_t = _ssm(
        xs_sp.reshape(SPECIAL_ROWS, D_MODEL), xs_p.reshape(n_ptiles, TILE, D_MODEL),
        wuz_t, *ssm_ops, sre, sim, ssm_w_glu.astype(F32), ssm_w_out.astype(F32))
    n_sc = n_sample_rows // CHUNK
    o_sp = _glu_out(gt_sp, szt_sp, wg_t, b_glu, wo_t, h1_sp[:, :, 1:1 + n_sc], nwf)
    o_p = _glu_out(gt_p, szt_p, wg_t, b_glu, wo_t, h1_p, nwf)

    y_prompt = _to_rows(o_p)
    y_sample = o_sp.reshape(n_seq, seq_len, D_MODEL)
    new_conv_prompt = tail_p[-1, 6:8].reshape(1, 1, 2, D_INNER)
    cv_s = cv_sp[N_META:N_META + n_sample_rows].reshape(n_seq, seq_len, D_INNER)
    new_conv_sample = cv_s[:, seq_len - 2:].reshape(1, n_seq, 2, D_INNER)
    p_shape = (1, 1, N_GROUPS, SSM_STATE)
    s_shape = (1, n_seq, N_GROUPS, SSM_STATE)
    return (y_prompt, y_sample, new_conv_prompt, new_conv_sample,
            hp_re[0].reshape(p_shape), hp_im[0].reshape(p_shape),
            hs_re.reshape(s_shape), hs_im.reshape(s_shape))
```

```python
import functools

import jax
import jax.numpy as jnp
from jax import lax
from jax.experimental import pallas as pl
from jax.experimental.pallas import tpu as pltpu

F32 = jnp.float32
BF16 = jnp.bfloat16

D_MODEL = 2048
D_INNER = 4096
N_META = 16
SSM_GROUP = 16
N_GROUPS = D_INNER // SSM_GROUP
SSM_STATE = 64
N_STATE_ROWS = N_GROUPS * SSM_STATE
RMS_EPS = 1e-6

CHUNK = 16
CHUNK_ROWS = CHUNK * SSM_GROUP
LANES = 128
TILE = CHUNK * LANES
SPECIAL_ROWS = 1024

TM_CONV = 1024
TE_CONV = 512
TM_OUT = 512
TE_SSM = 256
TE_GLU = 512
LB_GLU = 512
TM_ROWS = 1024
CAST_ROWS = 512

VMEM_LIMIT = 56 * 1024 * 1024
VMEM_LIMIT_LARGE = 62 * 1024 * 1024


def _params(n_axes, vmem=VMEM_LIMIT):
    return pltpu.CompilerParams(dimension_semantics=("arbitrary",) * n_axes,
                                vmem_limit_bytes=vmem)


def _rms(x, w):
    ms = jnp.mean(x * x, axis=-1, keepdims=True)
    return x * lax.rsqrt(ms + RMS_EPS) * w


def _hosted_cast(w, n_steps, step_of, *, transpose):
    _, r, c = w.shape
    bc = (r * c) // (n_steps * CAST_ROWS)
    nc = c // bc
    assert bc % LANES == 0 and (r // CAST_ROWS) * nc == n_steps
    in_spec = pl.BlockSpec((None, CAST_ROWS, bc),
                           lambda *g: (0, step_of(*g) // nc, step_of(*g) % nc))
    if transpose:
        out_spec = pl.BlockSpec((bc, CAST_ROWS),
                                lambda *g: (step_of(*g) % nc, step_of(*g) // nc))
        return in_spec, out_spec, jax.ShapeDtypeStruct((c, r), BF16)
    out_spec = pl.BlockSpec((CAST_ROWS, bc), lambda *g: (step_of(*g) // nc, step_of(*g) % nc))
    return in_spec, out_spec, jax.ShapeDtypeStruct((r, c), BF16)


def _hosted_cast_step(src_ref, dst_ref, *, transpose):
    x = src_ref[...]
    dst_ref[...] = (x.T if transpose else x).astype(BF16)


def _rmsnorm_kernel(x_ref, w_ref, o_ref):
    o_ref[...] = _rms(x_ref[...], w_ref[...]).astype(BF16)


def _rmsnorm(x, w, *, tm):
    rows = x.shape[1]
    return pl.pallas_call(
        _rmsnorm_kernel, grid=(rows // tm,),
        in_specs=[pl.BlockSpec((None, tm, D_MODEL), lambda i: (0, i, 0)),
                  pl.BlockSpec((1, D_MODEL), lambda i: (0, 0))],
        out_specs=pl.BlockSpec((tm, D_MODEL), lambda i: (i, 0)),
        out_shape=jax.ShapeDtypeStruct((rows, D_MODEL), BF16),
        compiler_params=_params(1), name="rmsnorm",
    )(x, w)


def _special_rows_kernel(meta_ref, xs_ref, w_ref, x_ref, xn_ref):
    n_meta = meta_ref.shape[0]
    n_real = n_meta + xs_ref.shape[0]
    n_pad = x_ref.shape[0] - n_real
    for ref, lo, hi in ((meta_ref, 0, n_meta), (xs_ref, n_meta, n_real)):
        x = ref[...]
        x_ref[lo:hi, :] = x
        xn_ref[lo:hi, :] = _rms(x, w_ref[...]).astype(BF16)
    x_ref[n_real:, :] = jnp.zeros((n_pad, D_MODEL), F32)
    xn_ref[n_real:, :] = jnp.zeros((n_pad, D_MODEL), BF16)


def _special_rows(meta, xs, w):
    whole = lambda a: pl.BlockSpec(a.shape, lambda i: (0,) * a.ndim)
    return pl.pallas_call(
        _special_rows_kernel, grid=(1,),
        in_specs=[whole(meta), whole(xs), whole(w)],
        out_specs=(pl.BlockSpec((None, SPECIAL_ROWS, D_MODEL), lambda i: (0, 0, 0)),
                   pl.BlockSpec((SPECIAL_ROWS, D_MODEL), lambda i: (0, 0))),
        out_shape=(jax.ShapeDtypeStruct((1, SPECIAL_ROWS, D_MODEL), F32),
                   jax.ShapeDtypeStruct((SPECIAL_ROWS, D_MODEL), BF16)),
        compiler_params=_params(1), name="special_rows",
    )(meta, xs, w)


def _conv_proj_kernel(*refs, tm, te, n_pieces, special, real_rows):
    if special:
        (xn_ref, wb_ref, wc_ref, wv_ref, wz_ref, cw_ref, cb_ref, init_ref, c0_ref, c1_ref,
         y_ref, cv_ref, w_scr, s_scr, p1_scr, p2_scr) = refs
    else:
        (xn_ref, wb_ref, wc_ref, wv_ref, wz_ref, cw_ref, cb_ref, init_ref, wp_ref, wt_ref,
         y_ref, tail_ref, wp_bf_ref, wt_bf_ref, w_scr, s_scr) = refs
    i = pl.program_id(1)

    @pl.when(i == 0)
    def _():
        for p, w_ref in enumerate((wb_ref, wc_ref, wv_ref, wz_ref)):
            w_scr[:, p * te:(p + 1) * te] = w_ref[...].astype(BF16)
        s_scr[0:8, :] = init_ref[...]

    if special:
        pm = real_rows
        cv_ref[pm:tm, :] = jnp.zeros((tm - pm, te), F32)
        y_ref[pm:tm, :] = jnp.zeros((tm - pm, te), BF16)
    else:
        pm = tm // n_pieces
    for r in range(n_pieces):
        lo = r * pm
        if not special and r == n_pieces - 1:
            _hosted_cast_step(wp_ref, wp_bf_ref, transpose=False)
            _hosted_cast_step(wt_ref, wt_bf_ref, transpose=True)
        proj = jnp.dot(xn_ref[lo:lo + pm, :], w_scr[...], preferred_element_type=F32)
        bg = proj[:, 0 * te:1 * te]
        cg = proj[:, 1 * te:2 * te]
        vv = proj[:, 2 * te:3 * te]
        zz = proj[:, 3 * te:4 * te]
        cv = cg * vv
        s_scr[8 + lo:8 + lo + pm, :] = cv
        if special:
            cv_ref[lo:lo + pm, :] = cv
            p1_scr[...] = s_scr[7 + lo:7 + lo + pm, :]
            p2_scr[...] = s_scr[6 + lo:6 + lo + pm, :]
            for q in range(c0_ref.shape[0]):
                r0 = N_META + 32 * q
                p1_scr[r0:r0 + 1, :] = c1_ref[q:q + 1, :]
                p2_scr[r0:r0 + 1, :] = c0_ref[q:q + 1, :]
                p2_scr[r0 + 1:r0 + 2, :] = c1_ref[q:q + 1, :]
            p1 = p1_scr[...]
            p2 = p2_scr[...]
        else:
            p1 = s_scr[7 + lo:7 + lo + pm, :]
            p2 = s_scr[6 + lo:6 + lo + pm, :]
        conv = cb_ref[...] + cw_ref[0:1, :] * p2
        conv = conv + cw_ref[1:2, :] * p1
        conv = conv + cw_ref[2:3, :] * cv
        y_ref[lo:lo + pm, :] = (bg * conv * jax.nn.silu(zz)).astype(BF16)

    if not special:
        tail = s_scr[tm:tm + 8, :]
        s_scr[0:8, :] = tail
        tail_ref[...] = tail


def _conv_proj(xn, w_in, conv_w8, conv_b, init8, cache0=None, cache1=None, real_rows=None,
               cast_w=None, cast_w_t=None):
    rows = xn.shape[0]
    tm, te = TM_CONV, TE_CONV
    n_j = D_INNER // te
    n_i = rows // tm
    special = cache0 is not None
    w_spec = lambda p: pl.BlockSpec((None, D_MODEL, te),
                                    lambda j, i, p=p: (0, 0, p * n_j + j))
    in_specs = [pl.BlockSpec((tm, D_MODEL), lambda j, i: (i, 0)),
                w_spec(0), w_spec(1), w_spec(2), w_spec(3),
                pl.BlockSpec((8, te), lambda j, i: (0, j)),
                pl.BlockSpec((1, te), lambda j, i: (0, j)),
                pl.BlockSpec((8, te), lambda j, i: (0, j))]
    args = [xn, w_in, w_in, w_in, w_in, conv_w8, conv_b, init8]
    scratch = [pltpu.VMEM((D_MODEL, 4 * te), BF16), pltpu.VMEM((tm + 8, te), F32)]
    y_shape = jax.ShapeDtypeStruct((rows, D_INNER), BF16)
    y_spec = pl.BlockSpec((tm, te), lambda j, i: (i, j))
    if special:
        assert n_i == 1
        nq = cache0.shape[0]
        in_specs += [pl.BlockSpec((nq, te), lambda j, i: (0, j)),
                     pl.BlockSpec((nq, te), lambda j, i: (0, j))]
        args += [cache0, cache1]
        out_shape = (y_shape, jax.ShapeDtypeStruct((rows, D_INNER), F32))
        out_specs = (y_spec, pl.BlockSpec((tm, te), lambda j, i: (i, j)))
        scratch += [pltpu.VMEM((real_rows, te), F32), pltpu.VMEM((real_rows, te), F32)]
    else:
        step_of = lambda j, i: j * n_i + i
        wp_in, wp_out, wp_shape = _hosted_cast(cast_w, n_j * n_i, step_of, transpose=False)
        wt_in, wt_out, wt_shape = _hosted_cast(cast_w_t, n_j * n_i, step_of, transpose=True)
        in_specs += [wp_in, wt_in]
        args += [cast_w, cast_w_t]
        out_shape = (y_shape, jax.ShapeDtypeStruct((n_i, 8, D_INNER), F32), wp_shape, wt_shape)
        out_specs = (y_spec, pl.BlockSpec((None, 8, te), lambda j, i: (i, 0, j)), wp_out, wt_out)
    return pl.pallas_call(
        functools.partial(_conv_proj_kernel, tm=tm, te=te, n_pieces=1 if special else 2,
                          special=special, real_rows=real_rows),
        grid=(n_j, n_i), in_specs=in_specs, out_specs=out_specs, out_shape=out_shape,
        scratch_shapes=scratch, compiler_params=_params(2, VMEM_LIMIT_LARGE),
        name="conv_proj_special" if special else "conv_proj",
    )(*args)


def _out_proj_kernel(y_ref, w_ref, x_ref, nw_ref, h_ref, xs_ref, slab_scr, *, real_rows, n_steps):
    tm = y_ref.shape[0]
    nc = tm // CHUNK
    n_slabs = D_MODEL // LANES

    def block(r):
        if r == 0:
            h = jnp.zeros((tm, D_MODEL), F32)
        else:
            h = x_ref[0:r, :] + jnp.dot(y_ref[0:r, :], w_ref[...], preferred_element_type=F32)
            if r < tm:
                h = jnp.concatenate([h, jnp.zeros((tm - r, D_MODEL), F32)], axis=0)
        for k in range(n_slabs):
            slab_scr[k] = h[:, LANES * k:LANES * (k + 1)]
        for s in range(CHUNK):
            hs = jnp.concatenate(
                [slab_scr[k, pl.ds(s, nc, stride=CHUNK), :] for k in range(n_slabs)], axis=1)
            h_ref[s] = hs
            xs_ref[s] = _rms(hs, nw_ref[...]).astype(BF16)

    if real_rows is None:
        block(tm)
    else:
        i = pl.program_id(0)
        for b in range(n_steps):
            pl.when(i == b)(functools.partial(block, min(max(real_rows - b * tm, 0), tm)))


def _out_proj(y, w_out, x, norm_w, *, lanes, real_rows=None):
    tm = TM_OUT
    nc = tm // CHUNK
    per_tile = lanes // nc
    n_i = x.shape[1] // tm
    tile_map = lambda i: (i // per_tile, 0, i % per_tile, 0)
    tile_shape = (n_i // per_tile, CHUNK, lanes, D_MODEL)
    return pl.pallas_call(
        functools.partial(_out_proj_kernel, real_rows=real_rows, n_steps=n_i),
        grid=(n_i,),
        in_specs=[
            pl.BlockSpec((tm, D_INNER), lambda i: (i, 0)),
            pl.BlockSpec((D_INNER, D_MODEL), lambda i: (0, 0), pipeline_mode=pl.Buffered(1)),
            pl.BlockSpec((None, tm, D_MODEL), lambda i: (0, i, 0)),
            pl.BlockSpec((1, D_MODEL), lambda i: (0, 0)),
        ],
        out_specs=(pl.BlockSpec((None, CHUNK, nc, D_MODEL), tile_map),
                   pl.BlockSpec((None, CHUNK, nc, D_MODEL), tile_map)),
        out_shape=(jax.ShapeDtypeStruct(tile_shape, F32),
                   jax.ShapeDtypeStruct(tile_shape, BF16)),
        scratch_shapes=[pltpu.VMEM((D_MODEL // LANES, tm, LANES), F32)],
        compiler_params=_params(1, VMEM_LIMIT_LARGE),
        name="out_proj",
    )(y, w_out, x, norm_w)


def _split_bf16(x):
    hi = x.astype(BF16)
    return hi, (x - hi.astype(F32)).astype(BF16)


def _ssm_kernel(xsp_ref, xs_ref, wu_ref, wz_ref, e1_ref, e2_ref, c1_ref, c2_ref, apt_ref, bb_ref,
                dv_ref, are_ref, aim_ref, sre_ref, sim_ref, wg_ref, wo_ref,
                gtsp_ref, sztsp_ref, gt_ref, szt_ref, hsre_ref, hsim_ref, hpre_ref, hpim_ref,
                wgt_ref, wot_ref, w_scr, tk_scr, win_scr, cout_scr, zt_scr, xre_scr, xim_scr, pre_scr, pim_scr,
                cre_scr, cim_scr, *, te, n_tiles):
    i = pl.program_id(1)
    gb = te // SSM_GROUP
    rows = gb * SSM_STATE
    n_seq = sre_ref.shape[0]
    bdims = (((2,), (1,)), ((0,), (0,)))

    def build_operators():
        w_scr[0:te, :] = wu_ref[...]
        w_scr[te:2 * te, :] = wz_ref[...]

        c1 = c1_ref[...]
        c2 = c2_ref[...]
        cap = [c1 * e1_ref[:, t:t + 1, :] + c2 * e2_ref[:, t:t + 1, :] for t in range(CHUNK + 1)]
        ca = jnp.concatenate(cap[:CHUNK], axis=1)
        cout_scr[...] = jnp.concatenate(cap[1:], axis=1).astype(BF16)

        lane = lax.broadcasted_iota(jnp.int32, (SSM_GROUP, CHUNK_ROWS), 1)
        row = lax.broadcasted_iota(jnp.int32, (SSM_GROUP, CHUNK_ROWS), 0)
        diag = row == (lane & (SSM_GROUP - 1))
        tile_c = diag.astype(BF16)
        spread_s = (row == lane // SSM_GROUP).astype(BF16)
        expand = lambda b, m, n_rows: jnp.dot(b, m, preferred_element_type=F32).reshape(
            gb, n_rows, CHUNK_ROWS)
        bhi, blo = _split_bf16(bb_ref[...].reshape(gb * 2 * SSM_STATE, SSM_GROUP))
        bhi_t = expand(bhi, tile_c, 2 * SSM_STATE)
        blo_t = expand(blo, tile_c, 2 * SSM_STATE)

        phi, plo = _split_bf16(apt_ref[...].reshape(gb * 2 * SSM_STATE, SSM_GROUP))
        ap_t = expand(phi, spread_s, 2 * SSM_STATE) + expand(plo, spread_s, 2 * SSM_STATE)
        b_t = bhi_t + blo_t
        pr, pi = ap_t[:, :SSM_STATE], ap_t[:, SSM_STATE:]
        br, bi = b_t[:, :SSM_STATE], b_t[:, SSM_STATE:]
        win_scr[:, 0:SSM_STATE, :] = (pr * br - pi * bi).astype(BF16)
        win_scr[:, SSM_STATE:, :] = (pr * bi + pi * br).astype(BF16)

        bhi_t, blo_t = bhi_t.astype(BF16), blo_t.astype(BF16)
        ahi, alo = _split_bf16(ca)
        kw = (lax.dot_general(ahi, bhi_t, bdims, preferred_element_type=F32)
              + lax.dot_general(ahi, blo_t, bdims, preferred_element_type=F32)
              + lax.dot_general(alo, bhi_t, bdims, preferred_element_type=F32))
        kw0 = (kw[:, 0:SSM_GROUP, :] + jnp.where(diag, dv_ref[...], 0.0)).astype(BF16)
        kw = kw.astype(BF16)
        tk_scr[...] = jnp.zeros_like(tk_scr)
        for s in range(CHUNK):
            lo = SSM_GROUP * s
            hi = lo + SSM_GROUP
            tk_scr[:, lo:hi, lo:hi] = kw0[:, :, lo:hi]
            if hi < CHUNK_ROWS:
                tk_scr[:, hi:CHUNK_ROWS, lo:hi] = kw[:, SSM_GROUP:CHUNK_ROWS - lo, lo:hi]

    nt_dims = (((1,), (1,)), ((), ()))
    half = LANES // 2
    low_half = lax.broadcasted_iota(jnp.int32, (1, LANES), 1) < half

    ar = are_ref[...]
    ai = aim_ref[...]

    def step(hr, hi, n):
        xr = xre_scr[pl.ds(n, 1), :]
        xi = xim_scr[pl.ds(n, 1), :]
        return ar * hr - ai * hi + xr, ar * hi + ai * hr + xi

    def chunk_inputs():
        zt = zt_scr[...]
        yt = lax.dot_general(tk_scr[...], zt, bdims, preferred_element_type=F32)
        xt = lax.dot_general(win_scr[...], zt, bdims, preferred_element_type=F32)
        xre_scr[...] = xt[:, 0:SSM_STATE, :].reshape(rows, LANES).T
        xim_scr[...] = xt[:, SSM_STATE:, :].reshape(rows, LANES).T
        return yt

    def gelu_pieces(yt):
        hp = jnp.concatenate([pre_scr[...].T.reshape(gb, SSM_STATE, LANES),
                              pim_scr[...].T.reshape(gb, SSM_STATE, LANES)], axis=1)
        ycorr = lax.dot_general(cout_scr[...], hp.astype(BF16), bdims,
                                preferred_element_type=F32)
        g = jax.nn.gelu(yt + ycorr)
        return [g[:, SSM_GROUP * t:SSM_GROUP * (t + 1), :].reshape(te, LANES)
                for t in range(CHUNK)]

    def pack_samples(out_ref, block_of, first_lane_of):
        quarter = LANES // 4
        lane_q = lax.broadcasted_iota(jnp.int32, (1, LANES), 1) // quarter
        for c in range(CHUNK // 4):
            out = jnp.zeros((te, LANES), F32)
            for k in range(4):
                t = 4 * c + k
                shift = (quarter * k - first_lane_of(t)) % LANES
                blk = block_of(t)
                out = jnp.where(lane_q == k, pltpu.roll(blk, shift, 1) if shift else blk, out)
            out_ref[:, LANES * c:LANES * (c + 1)] = out.astype(BF16)

    @pl.when(i == 0)
    def _():
        build_operators()
        uz = lax.dot_general(w_scr[...], xsp_ref[...], nt_dims,
                             preferred_element_type=F32)
        sz = jax.nn.silu(uz[te:, :])
        pack_samples(sztsp_ref, lambda t: sz[:, LANES * (t // 2):LANES * (t // 2 + 1)],
                     lambda t: half * (t % 2) + 1)
        for s in range(CHUNK):
            blk = uz[:te, LANES * (s // 2):LANES * (s // 2 + 1)]
            if s % 2:
                blk = pltpu.roll(blk, half, 1)
            zt_scr[:, CHUNK * s:CHUNK * (s + 1), :] = (
                jnp.where(low_half, blk, 0.0).astype(BF16).reshape(gb, SSM_GROUP, LANES))
        yt = chunk_inputs()

        pre_scr[...] = jnp.zeros_like(pre_scr)
        pim_scr[...] = jnp.zeros_like(pim_scr)
        zero = jnp.zeros((1, rows), F32)
        hr, hi = step(zero, zero, 0)
        cre_scr[0:1, :] = hr
        cim_scr[0:1, :] = hi
        for q in range(n_seq):
            hr = sre_ref[q:q + 1, :]
            hi = sim_ref[q:q + 1, :]
            for e in range(2):
                n = 1 + 2 * q + e
                pre_scr[n:n + 1, :] = hr
                pim_scr[n:n + 1, :] = hi
                hr, hi = step(hr, hi, n)
            hsre_ref[q:q + 1, :] = hr
            hsim_ref[q:q + 1, :] = hi

        pieces = gelu_pieces(yt)
        pack_samples(gtsp_ref, lambda t: pieces[t], lambda t: 1)

    @pl.when(i > 0)
    def _():
        uz = lax.dot_general(w_scr[...], xs_ref[...], nt_dims,
                             preferred_element_type=F32)
        szt_ref[...] = jax.nn.silu(uz[te:, :]).astype(BF16)
        ub = uz[:te, :].astype(BF16)
        for s in range(CHUNK):
            zt_scr[:, CHUNK * s:CHUNK * (s + 1), :] = (
                ub[:, LANES * s:LANES * (s + 1)].reshape(gb, SSM_GROUP, LANES))
        yt = chunk_inputs()
        _hosted_cast_step(wg_ref, wgt_ref, transpose=True)
        _hosted_cast_step(wo_ref, wot_ref, transpose=True)

        hr, hi = cre_scr[0:1, :], cim_scr[0:1, :]
        for n in range(LANES):
            pre_scr[n:n + 1, :] = hr
            pim_scr[n:n + 1, :] = hi
            hr, hi = step(hr, hi, n)
        cre_scr[0:1, :] = hr
        cim_scr[0:1, :] = hi

        @pl.when(i == n_tiles - 1)
        def _():
            hpre_ref[...] = jnp.broadcast_to(hr, hpre_ref.shape)
            hpim_ref[...] = jnp.broadcast_to(hi, hpim_ref.shape)

        pieces = gelu_pieces(yt)
        for t in range(CHUNK):
            gt_ref[:, LANES * t:LANES * (t + 1)] = pieces[t].astype(BF16)


def _ssm(xs_sp, xs_p, wuz_t, e1, e2, c1, c2, apt, bb, dv, are, aim, sre, sim, w_glu, w_out):
    n_ptiles = xs_p.shape[0]
    n_tiles = 1 + n_ptiles
    te = TE_SSM
    n_j = D_INNER // te
    gb = te // SSM_GROUP
    rows = gb * SSM_STATE
    n_seq = sre.shape[0]
    prev = lambda i: jnp.maximum(i - 1, 0)
    act_shape = jax.ShapeDtypeStruct((n_ptiles, D_INNER, TILE), BF16)
    act_spec = pl.BlockSpec((None, te, TILE), lambda j, i: (prev(i), j, 0))
    assert 2 * n_seq == LANES // 4
    sp_slots = CHUNK * 2 * n_seq
    sp_shape = jax.ShapeDtypeStruct((1, D_INNER, sp_slots), BF16)
    sp_spec = pl.BlockSpec((None, te, sp_slots), lambda j, i: (0, j, 0))
    hs_shape = jax.ShapeDtypeStruct((n_seq, N_STATE_ROWS), F32)
    hs_spec = pl.BlockSpec((n_seq, rows), lambda j, i: (0, j))
    hp_shape = jax.ShapeDtypeStruct((8, N_STATE_ROWS), F32)
    hp_spec = pl.BlockSpec((8, rows), lambda j, i: (0, j))
    row_spec = pl.BlockSpec((1, rows), lambda j, i: (0, j))
    grp_spec = lambda r, c: pl.BlockSpec((gb, r, c), lambda j, i: (j, 0, 0))
    prompt_step = lambda j, i: j * n_ptiles + prev(i)
    wg_in, wg_out, wg_shape = _hosted_cast(w_glu, n_j * n_ptiles, prompt_step, transpose=True)
    wo_in, wo_out, wo_shape = _hosted_cast(w_out, n_j * n_ptiles, prompt_step, transpose=True)
    return pl.pallas_call(
        functools.partial(_ssm_kernel, te=te, n_tiles=n_tiles),
        grid=(n_j, n_tiles),
        in_specs=[
            pl.BlockSpec((TILE // 2, D_MODEL), lambda j, i: (0, 0), pipeline_mode=pl.Buffered(1)),
            pl.BlockSpec((None, TILE, D_MODEL), lambda j, i: (prev(i), 0, 0)),
            pl.BlockSpec((te, D_MODEL), lambda j, i: (j, 0)),
            pl.BlockSpec((te, D_MODEL), lambda j, i: (n_j + j, 0)),
            grp_spec(e1.shape[1], 2 * SSM_STATE), grp_spec(e2.shape[1], 2 * SSM_STATE),
            grp_spec(SSM_GROUP, 2 * SSM_STATE), grp_spec(SSM_GROUP, 2 * SSM_STATE),
            grp_spec(2 * SSM_STATE, SSM_GROUP), grp_spec(2 * SSM_STATE, SSM_GROUP),
            grp_spec(SSM_GROUP, 1),
            row_spec, row_spec, hs_spec, hs_spec, wg_in, wo_in,
        ],
        out_specs=(sp_spec, sp_spec, act_spec, act_spec, hs_spec, hs_spec, hp_spec, hp_spec,
                   wg_out, wo_out),
        out_shape=(sp_shape, sp_shape, act_shape, act_shape, hs_shape, hs_shape,
                   hp_shape, hp_shape, wg_shape, wo_shape),
        scratch_shapes=[pltpu.VMEM((2 * te, D_MODEL), BF16),
                        pltpu.VMEM((gb, CHUNK_ROWS, CHUNK_ROWS), BF16),
                        pltpu.VMEM((gb, 2 * SSM_STATE, CHUNK_ROWS), BF16),
                        pltpu.VMEM((gb, CHUNK_ROWS, 2 * SSM_STATE), BF16),
                        pltpu.VMEM((gb, CHUNK_ROWS, LANES), BF16),
                        pltpu.VMEM((LANES, rows), F32), pltpu.VMEM((LANES, rows), F32),
                        pltpu.VMEM((LANES, rows), F32), pltpu.VMEM((LANES, rows), F32),
                        pltpu.VMEM((8, rows), F32), pltpu.VMEM((8, rows), F32)],
        compiler_params=_params(2, VMEM_LIMIT_LARGE),
        name="s5_scan",
    )(xs_sp, xs_p, wuz_t, wuz_t, e1, e2, c1, c2, apt, bb, dv, are, aim, sre, sim, w_glu, w_out)


def _glu_out_kernel(gt_ref, szt_ref, wg_ref, bg_ref, wo_ref, h_ref, nw_ref, o_ref, acc_scr, *,
                    te, lb):
    e = pl.program_id(2)

    @pl.when(e == 0)
    def _():
        acc_scr[...] = jnp.zeros_like(acc_scr)

    gate = jnp.dot(wg_ref[...], gt_ref[...], preferred_element_type=F32) + bg_ref[...]
    ge = gt_ref[pl.ds(pl.multiple_of(e * te, te), te), :].astype(F32)
    y3 = (ge * jax.nn.sigmoid(gate)) * szt_ref[...].astype(F32)
    acc_scr[...] += jnp.dot(wo_ref[...], y3.astype(BF16), preferred_element_type=F32)

    @pl.when(e == pl.num_programs(2) - 1)
    def _():
        o = acc_scr[...].T
        ln = h_ref.shape[1]
        for t in range(lb // ln):
            h = h_ref[t] + o[ln * t:ln * (t + 1), :]
            o_ref[:, D_MODEL * t:D_MODEL * (t + 1)] = _rms(h, nw_ref[...])


def _glu_out(gt, szt, wg_t, b_glu, wo_t, h_all, norm_w):
    n_tiles, _, lanes, _ = h_all.shape
    te = TE_GLU
    lb = LB_GLU
    n_l = CHUNK * lanes // lb
    n_e = D_INNER // te
    tpb = lb // lanes
    return pl.pallas_call(
        functools.partial(_glu_out_kernel, te=te, lb=lb),
        grid=(n_tiles, n_l, n_e),
        in_specs=[
            pl.BlockSpec((None, D_INNER, lb), lambda i, l, e: (i, 0, l)),
            pl.BlockSpec((None, te, lb), lambda i, l, e: (i, e, l)),
            pl.BlockSpec((te, D_INNER), lambda i, l, e: (e, 0)),
            pl.BlockSpec((te, 1), lambda i, l, e: (e, 0)),
            pl.BlockSpec((D_MODEL, te), lambda i, l, e: (0, e)),
            pl.BlockSpec((None, tpb, lanes, D_MODEL), lambda i, l, e: (i, l, 0, 0)),
            pl.BlockSpec((1, D_MODEL), lambda i, l, e: (0, 0)),
        ],
        out_specs=pl.BlockSpec((lanes, tpb * D_MODEL), lambda i, l, e: (i, l)),
        out_shape=jax.ShapeDtypeStruct((n_tiles * lanes, CHUNK * D_MODEL), F32),
        scratch_shapes=[pltpu.VMEM((D_MODEL, lb), F32)],
        compiler_params=_params(3),
        name="glu_out",
    )(gt, szt, wg_t, b_glu, wo_t, h_all, norm_w)


def _to_rows_kernel(o_ref, y_ref, slab_scr):
    nc = o_ref.shape[0]
    n_slabs = D_MODEL // LANES
    for t in range(CHUNK):
        for k in range(n_slabs):
            slab_scr[k, pl.ds(t, nc, stride=CHUNK), :] = (
                o_ref[:, D_MODEL * t + LANES * k:D_MODEL * t + LANES * (k + 1)])
    for k in range(n_slabs):
        y_ref[:, LANES * k:LANES * (k + 1)] = slab_scr[k]


def _to_rows(o):
    n_chunks = o.shape[0]
    tm = TM_ROWS
    nc = tm // CHUNK
    return pl.pallas_call(
        _to_rows_kernel,
        grid=(n_chunks // nc,),
        in_specs=[pl.BlockSpec((nc, CHUNK * D_MODEL), lambda i: (i, 0))],
        out_specs=pl.BlockSpec((None, tm, D_MODEL), lambda i: (0, i, 0)),
        out_shape=jax.ShapeDtypeStruct((1, n_chunks * CHUNK, D_MODEL), F32),
        scratch_shapes=[pltpu.VMEM((D_MODEL // LANES, tm, LANES), F32)],
        compiler_params=_params(1),
        name="to_rows",
    )(o)


def _ssm_operators(a_re, a_im, b_re, b_im, c_re, c_im, d_vec, log_dt):
    a_re, a_im = a_re.astype(F32), a_im.astype(F32)
    dt = jnp.exp(log_dt.astype(F32))[:, None]
    th_re = jnp.concatenate([a_re * dt] * 2, axis=1)
    th_im = jnp.concatenate([a_im * dt] * 2, axis=1)
    mag = jnp.exp(th_re)
    one_re, one_im = mag * jnp.cos(th_im), mag * jnp.sin(th_im)
    n_tau = 24
    pw_re, pw_im = [jnp.ones_like(one_re)], [jnp.zeros_like(one_im)]
    for _ in range(CHUNK):
        r, i = pw_re[-1], pw_im[-1]
        pw_re.append(r * one_re - i * one_im)
        pw_im.append(r * one_im + i * one_re)
    pad = [jnp.zeros_like(one_re)] * (n_tau - CHUNK - 1)
    ap_re = jnp.stack(pw_re + pad, axis=1)
    ap_im = jnp.stack(pw_im + pad, axis=1)
    sign = jnp.concatenate([jnp.ones((SSM_STATE,), F32), -jnp.ones((SSM_STATE,), F32)])
    e1 = ap_re * sign
    e2 = -ap_im
    c_re, c_im = c_re.astype(F32), c_im.astype(F32)
    c1 = jnp.concatenate([c_re, c_im], axis=2)
    c2 = jnp.concatenate([c_im, c_re], axis=2)

    ab_re, ab_im = ap_re[:, 1, :SSM_STATE], ap_im[:, 1, :SSM_STATE]
    nr, ni = ab_re - 1.0, ab_im
    den = a_re * a_re + a_im * a_im
    q_re = ((nr * a_re + ni * a_im) / den)[..., None]
    q_im = ((ni * a_re - nr * a_im) / den)[..., None]
    b_re, b_im = b_re.astype(F32), b_im.astype(F32)
    bb = jnp.concatenate([q_re * b_re - q_im * b_im, q_re * b_im + q_im * b_re], axis=1)

    rev = slice(CHUNK - 1, None, -1)
    apt = jnp.concatenate([ap_re[:, rev, :SSM_STATE].transpose(0, 2, 1),
                           ap_im[:, rev, :SSM_STATE].transpose(0, 2, 1)], axis=1)
    dv = d_vec.astype(F32).reshape(N_GROUPS, SSM_GROUP, 1)
    are = ap_re[:, CHUNK, :SSM_STATE].reshape(1, N_STATE_ROWS)
    aim = ap_im[:, CHUNK, :SSM_STATE].reshape(1, N_STATE_ROWS)
    return e1, e2, c1, c2, apt, bb, dv, are, aim


def kernel(x_prompt, x_sample, cache_conv, state_ssm_re, state_ssm_im, meta_tokens, norm_w,
           final_norm_w, conv_w_in, conv_w, conv_b, conv_w_out, ssm_w_in, ssm_a_re, ssm_a_im,
           ssm_b_re, ssm_b_im, ssm_c_re, ssm_c_im, ssm_d, ssm_log_dt, ssm_w_glu, ssm_b_glu,
           ssm_w_out):
    n_seq, seq_len = x_sample.shape[0], x_sample.shape[1]
    n_prompt_rows = x_prompt.shape[1]
    n_ptiles = n_prompt_rows // TILE
    n_sample_rows = n_seq * seq_len
    assert x_prompt.shape[0] == 1 and seq_len == 2 * CHUNK and N_META == CHUNK
    assert n_prompt_rows % TILE == 0 and N_META + n_sample_rows <= SPECIAL_ROWS

    w_in0 = conv_w_in.astype(F32)
    conv_w8 = jnp.zeros((8, D_INNER), F32).at[0:3].set(conv_w[0].astype(F32))
    conv_b2 = conv_b[0].astype(F32).reshape(1, D_INNER)
    b_glu = ssm_b_glu[0].astype(F32).reshape(D_INNER, 1)
    nw0 = norm_w[0].astype(F32).reshape(1, D_MODEL)
    nw1 = norm_w[1].astype(F32).reshape(1, D_MODEL)
    nwf = final_norm_w.astype(F32).reshape(1, D_MODEL)
    ssm_ops = _ssm_operators(
        ssm_a_re[0], ssm_a_im[0], ssm_b_re[0], ssm_b_im[0], ssm_c_re[0], ssm_c_im[0],
        ssm_d[0], ssm_log_dt[0])
    sre = state_ssm_re[0].astype(F32).reshape(n_seq, N_STATE_ROWS)
    sim = state_ssm_im[0].astype(F32).reshape(n_seq, N_STATE_ROWS)

    xp = x_prompt.astype(F32)

    zeros8 = jnp.zeros((8, D_INNER), F32)
    x_sp, xn_sp = _special_rows(meta_tokens.astype(F32),
                                x_sample.astype(F32).reshape(n_sample_rows, D_MODEL), nw0)
    xn_p = _rmsnorm(xp, nw0, tm=TM_CONV)
    sp_real = N_META + n_sample_rows
    y_sp, cv_sp = _conv_proj(xn_sp, w_in0, conv_w8, conv_b2, zeros8,
                             cache_conv[0, :, 0, :].astype(F32), cache_conv[0, :, 1, :].astype(F32),
                             real_rows=sp_real)
    init8 = zeros8.at[6:8].set(cv_sp[N_META - 2:N_META])
    y_p, tail_p, w_out0, wuz_t = _conv_proj(
        xn_p, w_in0, conv_w8, conv_b2, init8,
        cast_w=conv_w_out.astype(F32), cast_w_t=ssm_w_in.astype(F32))
    h1_p, xs_p = _out_proj(y_p, w_out0, xp, nw1, lanes=LANES)
    h1_sp, xs_sp = _out_proj(y_sp, w_out0, x_sp, nw1, lanes=SPECIAL_ROWS // CHUNK,
                             real_rows=sp_real)

    gt_sp, szt_sp, gt_p, szt_p, hs_re, hs_im, hp_re, hp_im, wg_t, wo_t = _ssm(
        xs_sp.reshape(SPECIAL_ROWS, D_MODEL), xs_p.reshape(n_ptiles, TILE, D_MODEL),
        wuz_t, *ssm_ops, sre, sim, ssm_w_glu.astype(F32), ssm_w_out.astype(F32))
    n_sc = n_sample_rows // CHUNK
    o_sp = _glu_out(gt_sp, szt_sp, wg_t, b_glu, wo_t, h1_sp[:, :, 1:1 + n_sc], nwf)
    o_p = _glu_out(gt_p, szt_p, wg_t, b_glu, wo_t, h1_p, nwf)

    y_prompt = _to_rows(o_p)
    y_sample = o_sp.reshape(n_seq, seq_len, D_MODEL)
    new_conv_prompt = tail_p[-1, 6:8].reshape(1, 1, 2, D_INNER)
    cv_s = cv_sp[N_META:N_META + n_sample_rows].reshape(n_seq, seq_len, D_INNER)
    new_conv_sample = cv_s[:, seq_len - 2:].reshape(1, n_seq, 2, D_INNER)
    p_shape = (1, 1, N_GROUPS, SSM_STATE)
    s_shape = (1, n_seq, N_GROUPS, SSM_STATE)
    return (y_prompt, y_sample, new_conv_prompt, new_conv_sample,
            hp_re[0].reshape(p_shape), hp_im[0].reshape(p_shape),
            hs_re.reshape(s_shape), hs_im.reshape(s_shape))
```

```python
import functools

import jax
import jax.numpy as jnp
from jax import lax
from jax.experimental import pallas as pl
from jax.experimental.pallas import tpu as pltpu

F32 = jnp.float32
BF16 = jnp.bfloat16

D_MODEL = 2048
D_INNER = 4096
N_META = 16
SSM_GROUP = 16
N_GROUPS = D_INNER // SSM_GROUP
SSM_STATE = 64
N_STATE_ROWS = N_GROUPS * SSM_STATE
RMS_EPS = 1e-6

CHUNK = 16
SLAB_PITCH = 20
CHUNK_ROWS = CHUNK * SSM_GROUP
LANES = 128
TILE = CHUNK * LANES
SPECIAL_ROWS = 1024

TM_CONV = 1024
TE_CONV = 512
TM_OUT = 512
TE_SSM = 256
TE_GLU = 512
LB_GLU = 512
TM_ROWS = 1024
CAST_ROWS = 512

VMEM_LIMIT = 56 * 1024 * 1024
VMEM_LIMIT_LARGE = 62 * 1024 * 1024


def _params(n_axes, vmem=VMEM_LIMIT):
    return pltpu.CompilerParams(dimension_semantics=("arbitrary",) * n_axes,
                                vmem_limit_bytes=vmem)


def _rms(x, w):
    ms = jnp.mean(x * x, axis=-1, keepdims=True)
    return x * lax.rsqrt(ms + RMS_EPS) * w


def _hosted_cast(w, n_steps, step_of, *, transpose):
    _, r, c = w.shape
    bc = (r * c) // (n_steps * CAST_ROWS)
    nc = c // bc
    assert bc % LANES == 0 and (r // CAST_ROWS) * nc == n_steps
    in_spec = pl.BlockSpec((None, CAST_ROWS, bc),
                           lambda *g: (0, step_of(*g) // nc, step_of(*g) % nc))
    if transpose:
        out_spec = pl.BlockSpec((bc, CAST_ROWS),
                                lambda *g: (step_of(*g) % nc, step_of(*g) // nc))
        return in_spec, out_spec, jax.ShapeDtypeStruct((c, r), BF16)
    out_spec = pl.BlockSpec((CAST_ROWS, bc), lambda *g: (step_of(*g) // nc, step_of(*g) % nc))
    return in_spec, out_spec, jax.ShapeDtypeStruct((r, c), BF16)


def _hosted_cast_step(src_ref, dst_ref, *, transpose):
    x = src_ref[...]
    dst_ref[...] = (x.T if transpose else x).astype(BF16)


def _rmsnorm_kernel(x_ref, w_ref, o_ref):
    o_ref[...] = _rms(x_ref[...], w_ref[...]).astype(BF16)


def _rmsnorm(x, w, *, tm):
    rows = x.shape[1]
    return pl.pallas_call(
        _rmsnorm_kernel, grid=(rows // tm,),
        in_specs=[pl.BlockSpec((None, tm, D_MODEL), lambda i: (0, i, 0)),
                  pl.BlockSpec((1, D_MODEL), lambda i: (0, 0))],
        out_specs=pl.BlockSpec((tm, D_MODEL), lambda i: (i, 0)),
        out_shape=jax.ShapeDtypeStruct((rows, D_MODEL), BF16),
        compiler_params=_params(1), name="rmsnorm",
    )(x, w)


def _special_rows_kernel(meta_ref, xs_ref, w_ref, x_ref, xn_ref):
    n_meta = meta_ref.shape[0]
    n_real = n_meta + xs_ref.shape[0]
    n_pad = x_ref.shape[0] - n_real
    for ref, lo, hi in ((meta_ref, 0, n_meta), (xs_ref, n_meta, n_real)):
        x = ref[...]
        x_ref[lo:hi, :] = x
        xn_ref[lo:hi, :] = _rms(x, w_ref[...]).astype(BF16)
    x_ref[n_real:, :] = jnp.zeros((n_pad, D_MODEL), F32)
    xn_ref[n_real:, :] = jnp.zeros((n_pad, D_MODEL), BF16)


def _special_rows(meta, xs, w):
    whole = lambda a: pl.BlockSpec(a.shape, lambda i: (0,) * a.ndim)
    return pl.pallas_call(
        _special_rows_kernel, grid=(1,),
        in_specs=[whole(meta), whole(xs), whole(w)],
        out_specs=(pl.BlockSpec((None, SPECIAL_ROWS, D_MODEL), lambda i: (0, 0, 0)),
                   pl.BlockSpec((SPECIAL_ROWS, D_MODEL), lambda i: (0, 0))),
        out_shape=(jax.ShapeDtypeStruct((1, SPECIAL_ROWS, D_MODEL), F32),
                   jax.ShapeDtypeStruct((SPECIAL_ROWS, D_MODEL), BF16)),
        compiler_params=_params(1), name="special_rows",
    )(meta, xs, w)


def _conv_proj_kernel(*refs, tm, te, n_pieces, special, real_rows):
    if special:
        (xn_ref, wb_ref, wc_ref, wv_ref, wz_ref, cw_ref, cb_ref, init_ref, c0_ref, c1_ref,
         y_ref, cv_ref, w_scr, s_scr, p1_scr, p2_scr) = refs
    else:
        (xn_ref, wb_ref, wc_ref, wv_ref, wz_ref, cw_ref, cb_ref, init_ref, wp_ref, wt_ref,
         y_ref, tail_ref, wp_bf_ref, wt_bf_ref, w_scr, s_scr) = refs
    i = pl.program_id(1)

    @pl.when(i == 0)
    def _():
        for p, w_ref in enumerate((wb_ref, wc_ref, wv_ref, wz_ref)):
            w_scr[:, p * te:(p + 1) * te] = w_ref[...].astype(BF16)
        s_scr[0:8, :] = init_ref[...]

    if special:
        pm = real_rows
        cv_ref[pm:tm, :] = jnp.zeros((tm - pm, te), F32)
        y_ref[pm:tm, :] = jnp.zeros((tm - pm, te), BF16)
    else:
        pm = tm // n_pieces
    for r in range(n_pieces):
        lo = r * pm
        if not special and r == n_pieces - 1:
            _hosted_cast_step(wp_ref, wp_bf_ref, transpose=False)
            _hosted_cast_step(wt_ref, wt_bf_ref, transpose=True)
        proj = jnp.dot(xn_ref[lo:lo + pm, :], w_scr[...], preferred_element_type=F32)
        bg = proj[:, 0 * te:1 * te]
        cg = proj[:, 1 * te:2 * te]
        vv = proj[:, 2 * te:3 * te]
        zz = proj[:, 3 * te:4 * te]
        cv = cg * vv
        s_scr[8 + lo:8 + lo + pm, :] = cv
        if special:
            cv_ref[lo:lo + pm, :] = cv
            p1_scr[...] = s_scr[7 + lo:7 + lo + pm, :]
            p2_scr[...] = s_scr[6 + lo:6 + lo + pm, :]
            for q in range(c0_ref.shape[0]):
                r0 = N_META + 32 * q
                p1_scr[r0:r0 + 1, :] = c1_ref[q:q + 1, :]
                p2_scr[r0:r0 + 1, :] = c0_ref[q:q + 1, :]
                p2_scr[r0 + 1:r0 + 2, :] = c1_ref[q:q + 1, :]
            p1 = p1_scr[...]
            p2 = p2_scr[...]
        else:
            p1 = s_scr[7 + lo:7 + lo + pm, :]
            p2 = s_scr[6 + lo:6 + lo + pm, :]
        conv = cb_ref[...] + cw_ref[0:1, :] * p2
        conv = conv + cw_ref[1:2, :] * p1
        conv = conv + cw_ref[2:3, :] * cv
        y_ref[lo:lo + pm, :] = (bg * conv * jax.nn.silu(zz)).astype(BF16)

    if not special:
        tail = s_scr[tm:tm + 8, :]
        s_scr[0:8, :] = tail
        tail_ref[...] = tail


def _conv_proj(xn, w_in, conv_w8, conv_b, init8, cache0=None, cache1=None, real_rows=None,
               cast_w=None, cast_w_t=None):
    rows = xn.shape[0]
    tm, te = TM_CONV, TE_CONV
    n_j = D_INNER // te
    n_i = rows // tm
    special = cache0 is not None
    w_spec = lambda p: pl.BlockSpec((None, D_MODEL, te),
                                    lambda j, i, p=p: (0, 0, p * n_j + j))
    in_specs = [pl.BlockSpec((tm, D_MODEL), lambda j, i: (i, 0)),
                w_spec(0), w_spec(1), w_spec(2), w_spec(3),
                pl.BlockSpec((8, te), lambda j, i: (0, j)),
                pl.BlockSpec((1, te), lambda j, i: (0, j)),
                pl.BlockSpec((8, te), lambda j, i: (0, j))]
    args = [xn, w_in, w_in, w_in, w_in, conv_w8, conv_b, init8]
    scratch = [pltpu.VMEM((D_MODEL, 4 * te), BF16), pltpu.VMEM((tm + 8, te), F32)]
    y_shape = jax.ShapeDtypeStruct((rows, D_INNER), BF16)
    y_spec = pl.BlockSpec((tm, te), lambda j, i: (i, j))
    if special:
        assert n_i == 1
        nq = cache0.shape[0]
        in_specs += [pl.BlockSpec((nq, te), lambda j, i: (0, j)),
                     pl.BlockSpec((nq, te), lambda j, i: (0, j))]
        args += [cache0, cache1]
        out_shape = (y_shape, jax.ShapeDtypeStruct((rows, D_INNER), F32))
        out_specs = (y_spec, pl.BlockSpec((tm, te), lambda j, i: (i, j)))
        scratch += [pltpu.VMEM((real_rows, te), F32), pltpu.VMEM((real_rows, te), F32)]
    else:
        step_of = lambda j, i: j * n_i + i
        wp_in, wp_out, wp_shape = _hosted_cast(cast_w, n_j * n_i, step_of, transpose=False)
        wt_in, wt_out, wt_shape = _hosted_cast(cast_w_t, n_j * n_i, step_of, transpose=True)
        in_specs += [wp_in, wt_in]
        args += [cast_w, cast_w_t]
        out_shape = (y_shape, jax.ShapeDtypeStruct((n_i, 8, D_INNER), F32), wp_shape, wt_shape)
        out_specs = (y_spec, pl.BlockSpec((None, 8, te), lambda j, i: (i, 0, j)), wp_out, wt_out)
    return pl.pallas_call(
        functools.partial(_conv_proj_kernel, tm=tm, te=te, n_pieces=1 if special else 2,
                          special=special, real_rows=real_rows),
        grid=(n_j, n_i), in_specs=in_specs, out_specs=out_specs, out_shape=out_shape,
        scratch_shapes=scratch, compiler_params=_params(2, VMEM_LIMIT_LARGE),
        name="conv_proj_special" if special else "conv_proj",
    )(*args)


def _out_proj_kernel(y_ref, w_ref, x_ref, nw_ref, h_ref, xs_ref, slab_scr, *, real_rows, n_steps):
    tm = y_ref.shape[0]
    nc = tm // CHUNK
    n_slabs = D_MODEL // LANES

    def block(r):
        if r == 0:
            h = jnp.zeros((tm, D_MODEL), F32)
        else:
            h = x_ref[0:r, :] + jnp.dot(y_ref[0:r, :], w_ref[...], preferred_element_type=F32)
            if r < tm:
                h = jnp.concatenate([h, jnp.zeros((tm - r, D_MODEL), F32)], axis=0)
        for k in range(n_slabs):
            for n in range(nc):
                slab_scr[k, SLAB_PITCH * n:SLAB_PITCH * n + CHUNK, :] = (
                    h[CHUNK * n:CHUNK * (n + 1), LANES * k:LANES * (k + 1)])
        for s in range(CHUNK):
            hs = jnp.concatenate(
                [slab_scr[k, pl.ds(s, nc, stride=SLAB_PITCH), :] for k in range(n_slabs)], axis=1)
            h_ref[s] = hs
            xs_ref[s] = _rms(hs, nw_ref[...]).astype(BF16)

    if real_rows is None:
        block(tm)
    else:
        i = pl.program_id(0)
        for b in range(n_steps):
            pl.when(i == b)(functools.partial(block, min(max(real_rows - b * tm, 0), tm)))


def _out_proj(y, w_out, x, norm_w, *, lanes, real_rows=None):
    tm = TM_OUT
    nc = tm // CHUNK
    per_tile = lanes // nc
    n_i = x.shape[1] // tm
    tile_map = lambda i: (i // per_tile, 0, i % per_tile, 0)
    tile_shape = (n_i // per_tile, CHUNK, lanes, D_MODEL)
    return pl.pallas_call(
        functools.partial(_out_proj_kernel, real_rows=real_rows, n_steps=n_i),
        grid=(n_i,),
        in_specs=[
            pl.BlockSpec((tm, D_INNER), lambda i: (i, 0)),
            pl.BlockSpec((D_INNER, D_MODEL), lambda i: (0, 0), pipeline_mode=pl.Buffered(1)),
            pl.BlockSpec((None, tm, D_MODEL), lambda i: (0, i, 0)),
            pl.BlockSpec((1, D_MODEL), lambda i: (0, 0)),
        ],
        out_specs=(pl.BlockSpec((None, CHUNK, nc, D_MODEL), tile_map),
                   pl.BlockSpec((None, CHUNK, nc, D_MODEL), tile_map)),
        out_shape=(jax.ShapeDtypeStruct(tile_shape, F32),
                   jax.ShapeDtypeStruct(tile_shape, BF16)),
        scratch_shapes=[pltpu.VMEM((D_MODEL // LANES, nc * SLAB_PITCH, LANES), F32)],
        compiler_params=_params(1, VMEM_LIMIT_LARGE),
        name="out_proj",
    )(y, w_out, x, norm_w)


def _split_bf16(x):
    hi = x.astype(BF16)
    return hi, (x - hi.astype(F32)).astype(BF16)


def _ssm_kernel(xsp_ref, xs_ref, wu_ref, wz_ref, e1_ref, e2_ref, c1_ref, c2_ref, apt_ref, bb_ref,
                dv_ref, are_ref, aim_ref, sre_ref, sim_ref, wg_ref, wo_ref,
                gtsp_ref, sztsp_ref, gt_ref, szt_ref, hsre_ref, hsim_ref, hpre_ref, hpim_ref,
                wgt_ref, wot_ref, w_scr, tk_scr, win_scr, cout_scr, zt_scr, xre_scr, xim_scr, pre_scr, pim_scr,
                cre_scr, cim_scr, *, te, n_tiles):
    i = pl.program_id(1)
    gb = te // SSM_GROUP
    rows = gb * SSM_STATE
    n_seq = sre_ref.shape[0]
    bdims = (((2,), (1,)), ((0,), (0,)))

    def build_operators():
        w_scr[0:te, :] = wu_ref[...]
        w_scr[te:2 * te, :] = wz_ref[...]

        c1 = c1_ref[...]
        c2 = c2_ref[...]
        cap = [c1 * e1_ref[:, t:t + 1, :] + c2 * e2_ref[:, t:t + 1, :] for t in range(CHUNK + 1)]
        ca = jnp.concatenate(cap[:CHUNK], axis=1)
        cout_scr[...] = jnp.concatenate(cap[1:], axis=1).astype(BF16)

        lane = lax.broadcasted_iota(jnp.int32, (SSM_GROUP, CHUNK_ROWS), 1)
        row = lax.broadcasted_iota(jnp.int32, (SSM_GROUP, CHUNK_ROWS), 0)
        diag = row == (lane & (SSM_GROUP - 1))
        tile_c = diag.astype(BF16)
        spread_s = (row == lane // SSM_GROUP).astype(BF16)
        expand = lambda b, m, n_rows: jnp.dot(b, m, preferred_element_type=F32).reshape(
            gb, n_rows, CHUNK_ROWS)
        bhi, blo = _split_bf16(bb_ref[...].reshape(gb * 2 * SSM_STATE, SSM_GROUP))
        bhi_t = expand(bhi, tile_c, 2 * SSM_STATE)
        blo_t = expand(blo, tile_c, 2 * SSM_STATE)

        phi, plo = _split_bf16(apt_ref[...].reshape(gb * 2 * SSM_STATE, SSM_GROUP))
        ap_t = expand(phi, spread_s, 2 * SSM_STATE) + expand(plo, spread_s, 2 * SSM_STATE)
        b_t = bhi_t + blo_t
        pr, pi = ap_t[:, :SSM_STATE], ap_t[:, SSM_STATE:]
        br, bi = b_t[:, :SSM_STATE], b_t[:, SSM_STATE:]
        win_scr[:, 0:SSM_STATE, :] = (pr * br - pi * bi).astype(BF16)
        win_scr[:, SSM_STATE:, :] = (pr * bi + pi * br).astype(BF16)

        bhi_t, blo_t = bhi_t.astype(BF16), blo_t.astype(BF16)
        ahi, alo = _split_bf16(ca)
        kw = (lax.dot_general(ahi, bhi_t, bdims, preferred_element_type=F32)
              + lax.dot_general(ahi, blo_t, bdims, preferred_element_type=F32)
              + lax.dot_general(alo, bhi_t, bdims, preferred_element_type=F32))
        kw0 = (kw[:, 0:SSM_GROUP, :] + jnp.where(diag, dv_ref[...], 0.0)).astype(BF16)
        kw = kw.astype(BF16)
        tk_scr[...] = jnp.zeros_like(tk_scr)
        for s in range(CHUNK):
            lo = SSM_GROUP * s
            hi = lo + SSM_GROUP
            tk_scr[:, lo:hi, lo:hi] = kw0[:, :, lo:hi]
            if hi < CHUNK_ROWS:
                tk_scr[:, hi:CHUNK_ROWS, lo:hi] = kw[:, SSM_GROUP:CHUNK_ROWS - lo, lo:hi]

    nt_dims = (((1,), (1,)), ((), ()))
    half = LANES // 2
    low_half = lax.broadcasted_iota(jnp.int32, (1, LANES), 1) < half

    ar = are_ref[...]
    ai = aim_ref[...]

    def step(hr, hi, n):
        xr = xre_scr[pl.ds(n, 1), :]
        xi = xim_scr[pl.ds(n, 1), :]
        return ar * hr - ai * hi + xr, ar * hi + ai * hr + xi

    def chunk_inputs():
        zt = zt_scr[...]
        yt = lax.dot_general(tk_scr[...], zt, bdims, preferred_element_type=F32)
        xt = lax.dot_general(win_scr[...], zt, bdims, preferred_element_type=F32)
        xre_scr[...] = xt[:, 0:SSM_STATE, :].reshape(rows, LANES).T
        xim_scr[...] = xt[:, SSM_STATE:, :].reshape(rows, LANES).T
        return yt

    def gelu_pieces(yt):
        hp = jnp.concatenate([pre_scr[...].T.reshape(gb, SSM_STATE, LANES),
                              pim_scr[...].T.reshape(gb, SSM_STATE, LANES)], axis=1)
        ycorr = lax.dot_general(cout_scr[...], hp.astype(BF16), bdims,
                                preferred_element_type=F32)
        g = jax.nn.gelu(yt + ycorr)
        return [g[:, SSM_GROUP * t:SSM_GROUP * (t + 1), :].reshape(te, LANES)
                for t in range(CHUNK)]

    def pack_samples(out_ref, block_of, first_lane_of):
        quarter = LANES // 4
        lane_q = lax.broadcasted_iota(jnp.int32, (1, LANES), 1) // quarter
        for c in range(CHUNK // 4):
            out = jnp.zeros((te, LANES), F32)
            for k in range(4):
                t = 4 * c + k
                shift = (quarter * k - first_lane_of(t)) % LANES
                blk = block_of(t)
                out = jnp.where(lane_q == k, pltpu.roll(blk, shift, 1) if shift else blk, out)
            out_ref[:, LANES * c:LANES * (c + 1)] = out.astype(BF16)

    @pl.when(i == 0)
    def _():
        build_operators()
        uz = lax.dot_general(w_scr[...], xsp_ref[...], nt_dims,
                             preferred_element_type=F32)
        sz = jax.nn.silu(uz[te:, :])
        pack_samples(sztsp_ref, lambda t: sz[:, LANES * (t // 2):LANES * (t // 2 + 1)],
                     lambda t: half * (t % 2) + 1)
        for s in range(CHUNK):
            blk = uz[:te, LANES * (s // 2):LANES * (s // 2 + 1)]
            if s % 2:
                blk = pltpu.roll(blk, half, 1)
            zt_scr[:, CHUNK * s:CHUNK * (s + 1), :] = (
                jnp.where(low_half, blk, 0.0).astype(BF16).reshape(gb, SSM_GROUP, LANES))
        yt = chunk_inputs()

        pre_scr[...] = jnp.zeros_like(pre_scr)
        pim_scr[...] = jnp.zeros_like(pim_scr)
        zero = jnp.zeros((1, rows), F32)
        hr, hi = step(zero, zero, 0)
        cre_scr[0:1, :] = hr
        cim_scr[0:1, :] = hi
        for q in range(n_seq):
            hr = sre_ref[q:q + 1, :]
            hi = sim_ref[q:q + 1, :]
            for e in range(2):
                n = 1 + 2 * q + e
                pre_scr[n:n + 1, :] = hr
                pim_scr[n:n + 1, :] = hi
                hr, hi = step(hr, hi, n)
            hsre_ref[q:q + 1, :] = hr
            hsim_ref[q:q + 1, :] = hi

        pieces = gelu_pieces(yt)
        pack_samples(gtsp_ref, lambda t: pieces[t], lambda t: 1)

    @pl.when(i > 0)
    def _():
        uz = lax.dot_general(w_scr[...], xs_ref[...], nt_dims,
                             preferred_element_type=F32)
        szt_ref[...] = jax.nn.silu(uz[te:, :]).astype(BF16)
        ub = uz[:te, :].astype(BF16)
        for s in range(CHUNK):
            zt_scr[:, CHUNK * s:CHUNK * (s + 1), :] = (
                ub[:, LANES * s:LANES * (s + 1)].reshape(gb, SSM_GROUP, LANES))
        yt = chunk_inputs()
        _hosted_cast_step(wg_ref, wgt_ref, transpose=True)
        _hosted_cast_step(wo_ref, wot_ref, transpose=True)

        hr, hi = cre_scr[0:1, :], cim_scr[0:1, :]
        for n in range(LANES):
            pre_scr[n:n + 1, :] = hr
            pim_scr[n:n + 1, :] = hi
            hr, hi = step(hr, hi, n)
        cre_scr[0:1, :] = hr
        cim_scr[0:1, :] = hi

        @pl.when(i == n_tiles - 1)
        def _():
            hpre_ref[...] = jnp.broadcast_to(hr, hpre_ref.shape)
            hpim_ref[...] = jnp.broadcast_to(hi, hpim_ref.shape)

        pieces = gelu_pieces(yt)
        for t in range(CHUNK):
            gt_ref[:, LANES * t:LANES * (t + 1)] = pieces[t].astype(BF16)


def _ssm(xs_sp, xs_p, wuz_t, e1, e2, c1, c2, apt, bb, dv, are, aim, sre, sim, w_glu, w_out):
    n_ptiles = xs_p.shape[0]
    n_tiles = 1 + n_ptiles
    te = TE_SSM
    n_j = D_INNER // te
    gb = te // SSM_GROUP
    rows = gb * SSM_STATE
    n_seq = sre.shape[0]
    prev = lambda i: jnp.maximum(i - 1, 0)
    act_shape = jax.ShapeDtypeStruct((n_ptiles, D_INNER, TILE), BF16)
    act_spec = pl.BlockSpec((None, te, TILE), lambda j, i: (prev(i), j, 0))
    assert 2 * n_seq == LANES // 4
    sp_slots = CHUNK * 2 * n_seq
    sp_shape = jax.ShapeDtypeStruct((1, D_INNER, sp_slots), BF16)
    sp_spec = pl.BlockSpec((None, te, sp_slots), lambda j, i: (0, j, 0))
    hs_shape = jax.ShapeDtypeStruct((n_seq, N_STATE_ROWS), F32)
    hs_spec = pl.BlockSpec((n_seq, rows), lambda j, i: (0, j))
    hp_shape = jax.ShapeDtypeStruct((8, N_STATE_ROWS), F32)
    hp_spec = pl.BlockSpec((8, rows), lambda j, i: (0, j))
    row_spec = pl.BlockSpec((1, rows), lambda j, i: (0, j))
    grp_spec = lambda r, c: pl.BlockSpec((gb, r, c), lambda j, i: (j, 0, 0))
    prompt_step = lambda j, i: j * n_ptiles + prev(i)
    wg_in, wg_out, wg_shape = _hosted_cast(w_glu, n_j * n_ptiles, prompt_step, transpose=True)
    wo_in, wo_out, wo_shape = _hosted_cast(w_out, n_j * n_ptiles, prompt_step, transpose=True)
    return pl.pallas_call(
        functools.partial(_ssm_kernel, te=te, n_tiles=n_tiles),
        grid=(n_j, n_tiles),
        in_specs=[
            pl.BlockSpec((TILE // 2, D_MODEL), lambda j, i: (0, 0), pipeline_mode=pl.Buffered(1)),
            pl.BlockSpec((None, TILE, D_MODEL), lambda j, i: (prev(i), 0, 0)),
            pl.BlockSpec((te, D_MODEL), lambda j, i: (j, 0)),
            pl.BlockSpec((te, D_MODEL), lambda j, i: (n_j + j, 0)),
            grp_spec(e1.shape[1], 2 * SSM_STATE), grp_spec(e2.shape[1], 2 * SSM_STATE),
            grp_spec(SSM_GROUP, 2 * SSM_STATE), grp_spec(SSM_GROUP, 2 * SSM_STATE),
            grp_spec(2 * SSM_STATE, SSM_GROUP), grp_spec(2 * SSM_STATE, SSM_GROUP),
            grp_spec(SSM_GROUP, 1),
            row_spec, row_spec, hs_spec, hs_spec, wg_in, wo_in,
        ],
        out_specs=(sp_spec, sp_spec, act_spec, act_spec, hs_spec, hs_spec, hp_spec, hp_spec,
                   wg_out, wo_out),
        out_shape=(sp_shape, sp_shape, act_shape, act_shape, hs_shape, hs_shape,
                   hp_shape, hp_shape, wg_shape, wo_shape),
        scratch_shapes=[pltpu.VMEM((2 * te, D_MODEL), BF16),
                        pltpu.VMEM((gb, CHUNK_ROWS, CHUNK_ROWS), BF16),
                        pltpu.VMEM((gb, 2 * SSM_STATE, CHUNK_ROWS), BF16),
                        pltpu.VMEM((gb, CHUNK_ROWS, 2 * SSM_STATE), BF16),
                        pltpu.VMEM((gb, CHUNK_ROWS, LANES), BF16),
                        pltpu.VMEM((LANES, rows), F32), pltpu.VMEM((LANES, rows), F32),
                        pltpu.VMEM((LANES, rows), F32), pltpu.VMEM((LANES, rows), F32),
                        pltpu.VMEM((8, rows), F32), pltpu.VMEM((8, rows), F32)],
        compiler_params=_params(2, VMEM_LIMIT_LARGE),
        name="s5_scan",
    )(xs_sp, xs_p, wuz_t, wuz_t, e1, e2, c1, c2, apt, bb, dv, are, aim, sre, sim, w_glu, w_out)


def _glu_out_kernel(gt_ref, szt_ref, wg_ref, bg_ref, wo_ref, h_ref, nw_ref, o_ref, acc_scr, *,
                    te, lb):
    e = pl.program_id(2)

    @pl.when(e == 0)
    def _():
        acc_scr[...] = jnp.zeros_like(acc_scr)

    gate = jnp.dot(wg_ref[...], gt_ref[...], preferred_element_type=F32) + bg_ref[...]
    ge = gt_ref[pl.ds(pl.multiple_of(e * te, te), te), :].astype(F32)
    y3 = (ge * jax.nn.sigmoid(gate)) * szt_ref[...].astype(F32)
    acc_scr[...] += jnp.dot(wo_ref[...], y3.astype(BF16), preferred_element_type=F32)

    @pl.when(e == pl.num_programs(2) - 1)
    def _():
        o = acc_scr[...].T
        ln = h_ref.shape[1]
        for t in range(lb // ln):
            h = h_ref[t] + o[ln * t:ln * (t + 1), :]
            o_ref[:, D_MODEL * t:D_MODEL * (t + 1)] = _rms(h, nw_ref[...])


def _glu_out(gt, szt, wg_t, b_glu, wo_t, h_all, norm_w):
    n_tiles, _, lanes, _ = h_all.shape
    te = TE_GLU
    lb = LB_GLU
    n_l = CHUNK * lanes // lb
    n_e = D_INNER // te
    tpb = lb // lanes
    return pl.pallas_call(
        functools.partial(_glu_out_kernel, te=te, lb=lb),
        grid=(n_tiles, n_l, n_e),
        in_specs=[
            pl.BlockSpec((None, D_INNER, lb), lambda i, l, e: (i, 0, l)),
            pl.BlockSpec((None, te, lb), lambda i, l, e: (i, e, l)),
            pl.BlockSpec((te, D_INNER), lambda i, l, e: (e, 0)),
            pl.BlockSpec((te, 1), lambda i, l, e: (e, 0)),
            pl.BlockSpec((D_MODEL, te), lambda i, l, e: (0, e)),
            pl.BlockSpec((None, tpb, lanes, D_MODEL), lambda i, l, e: (i, l, 0, 0)),
            pl.BlockSpec((1, D_MODEL), lambda i, l, e: (0, 0)),
        ],
        out_specs=pl.BlockSpec((lanes, tpb * D_MODEL), lambda i, l, e: (i, l)),
        out_shape=jax.ShapeDtypeStruct((n_tiles * lanes, CHUNK * D_MODEL), F32),
        scratch_shapes=[pltpu.VMEM((D_MODEL, lb), F32)],
        compiler_params=_params(3),
        name="glu_out",
    )(gt, szt, wg_t, b_glu, wo_t, h_all, norm_w)


def _to_rows_kernel(o_ref, y_ref, slab_scr):
    nc = o_ref.shape[0]
    n_slabs = D_MODEL // LANES
    for t in range(CHUNK):
        for k in range(n_slabs):
            slab_scr[k, pl.ds(t, nc, stride=SLAB_PITCH), :] = (
                o_ref[:, D_MODEL * t + LANES * k:D_MODEL * t + LANES * (k + 1)])
    for k in range(n_slabs):
        for n in range(nc):
            y_ref[CHUNK * n:CHUNK * (n + 1), LANES * k:LANES * (k + 1)] = (
                slab_scr[k, SLAB_PITCH * n:SLAB_PITCH * n + CHUNK, :])


def _to_rows(o):
    n_chunks = o.shape[0]
    tm = TM_ROWS
    nc = tm // CHUNK
    return pl.pallas_call(
        _to_rows_kernel,
        grid=(n_chunks // nc,),
        in_specs=[pl.BlockSpec((nc, CHUNK * D_MODEL), lambda i: (i, 0))],
        out_specs=pl.BlockSpec((None, tm, D_MODEL), lambda i: (0, i, 0)),
        out_shape=jax.ShapeDtypeStruct((1, n_chunks * CHUNK, D_MODEL), F32),
        scratch_shapes=[pltpu.VMEM((D_MODEL // LANES, nc * SLAB_PITCH, LANES), F32)],
        compiler_params=_params(1),
        name="to_rows",
    )(o)


def _ssm_operators(a_re, a_im, b_re, b_im, c_re, c_im, d_vec, log_dt):
    a_re, a_im = a_re.astype(F32), a_im.astype(F32)
    dt = jnp.exp(log_dt.astype(F32))[:, None]
    th_re = jnp.concatenate([a_re * dt] * 2, axis=1)[:, None, :]
    th_im = jnp.concatenate([a_im * dt] * 2, axis=1)[:, None, :]
    n_tau = 24
    taus = jnp.arange(n_tau, dtype=F32)[None, :, None]
    used = taus <= CHUNK
    mag = jnp.exp(jnp.where(used, th_re * taus, 0.0))
    ang = jnp.where(used, th_im * taus, 0.0)
    ap_re = mag * jnp.cos(ang)
    ap_im = mag * jnp.sin(ang)
    sign = jnp.concatenate([jnp.ones((SSM_STATE,), F32), -jnp.ones((SSM_STATE,), F32)])
    e1 = ap_re * sign
    e2 = -ap_im
    c_re, c_im = c_re.astype(F32), c_im.astype(F32)
    c1 = jnp.concatenate([c_re, c_im], axis=2)
    c2 = jnp.concatenate([c_im, c_re], axis=2)

    ab_re, ab_im = ap_re[:, 1, :SSM_STATE], ap_im[:, 1, :SSM_STATE]
    nr, ni = ab_re - 1.0, ab_im
    den = a_re * a_re + a_im * a_im
    q_re = ((nr * a_re + ni * a_im) / den)[..., None]
    q_im = ((ni * a_re - nr * a_im) / den)[..., None]
    b_re, b_im = b_re.astype(F32), b_im.astype(F32)
    bb = jnp.concatenate([q_re * b_re - q_im * b_im, q_re * b_im + q_im * b_re], axis=1)

    rev = slice(CHUNK - 1, None, -1)
    apt = jnp.concatenate([ap_re[:, rev, :SSM_STATE].transpose(0, 2, 1),
                           ap_im[:, rev, :SSM_STATE].transpose(0, 2, 1)], axis=1)
    dv = d_vec.astype(F32).reshape(N_GROUPS, SSM_GROUP, 1)
    are = ap_re[:, CHUNK, :SSM_STATE].reshape(1, N_STATE_ROWS)
    aim = ap_im[:, CHUNK, :SSM_STATE].reshape(1, N_STATE_ROWS)
    return e1, e2, c1, c2, apt, bb, dv, are, aim


def kernel(x_prompt, x_sample, cache_conv, state_ssm_re, state_ssm_im, meta_tokens, norm_w,
           final_norm_w, conv_w_in, conv_w, conv_b, conv_w_out, ssm_w_in, ssm_a_re, ssm_a_im,
           ssm_b_re, ssm_b_im, ssm_c_re, ssm_c_im, ssm_d, ssm_log_dt, ssm_w_glu, ssm_b_glu,
           ssm_w_out):
    n_seq, seq_len = x_sample.shape[0], x_sample.shape[1]
    n_prompt_rows = x_prompt.shape[1]
    n_ptiles = n_prompt_rows // TILE
    n_sample_rows = n_seq * seq_len
    assert x_prompt.shape[0] == 1 and seq_len == 2 * CHUNK and N_META == CHUNK
    assert n_prompt_rows % TILE == 0 and N_META + n_sample_rows <= SPECIAL_ROWS

    w_in0 = conv_w_in.astype(F32)
    conv_w8 = jnp.zeros((8, D_INNER), F32).at[0:3].set(conv_w[0].astype(F32))
    conv_b2 = conv_b[0].astype(F32).reshape(1, D_INNER)
    b_glu = ssm_b_glu[0].astype(F32).reshape(D_INNER, 1)
    nw0 = norm_w[0].astype(F32).reshape(1, D_MODEL)
    nw1 = norm_w[1].astype(F32).reshape(1, D_MODEL)
    nwf = final_norm_w.astype(F32).reshape(1, D_MODEL)
    ssm_ops = _ssm_operators(
        ssm_a_re[0], ssm_a_im[0], ssm_b_re[0], ssm_b_im[0], ssm_c_re[0], ssm_c_im[0],
        ssm_d[0], ssm_log_dt[0])
    sre = state_ssm_re[0].astype(F32).reshape(n_seq, N_STATE_ROWS)
    sim = state_ssm_im[0].astype(F32).reshape(n_seq, N_STATE_ROWS)

    xp = x_prompt.astype(F32)

    zeros8 = jnp.zeros((8, D_INNER), F32)
    x_sp, xn_sp = _special_rows(meta_tokens.astype(F32),
                                x_sample.astype(F32).reshape(n_sample_rows, D_MODEL), nw0)
    xn_p = _rmsnorm(xp, nw0, tm=TM_CONV)
    sp_real = N_META + n_sample_rows
    y_sp, cv_sp = _conv_proj(xn_sp, w_in0, conv_w8, conv_b2, zeros8,
                             cache_conv[0, :, 0, :].astype(F32), cache_conv[0, :, 1, :].astype(F32),
                             real_rows=sp_real)
    init8 = zeros8.at[6:8].set(cv_sp[N_META - 2:N_META])
    y_p, tail_p, w_out0, wuz_t = _conv_proj(
        xn_p, w_in0, conv_w8, conv_b2, init8,
        cast_w=conv_w_out.astype(F32), cast_w_t=ssm_w_in.astype(F32))
    h1_p, xs_p = _out_proj(y_p, w_out0, xp, nw1, lanes=LANES)
    h1_sp, xs_sp = _out_proj(y_sp, w_out0, x_sp, nw1, lanes=SPECIAL_ROWS // CHUNK,
                             real_rows=sp_real)

    gt_sp, szt_sp, gt_p, szt_p, hs_re, hs_im, hp_re, hp_im, wg_t, wo_t = _ssm(
        xs_sp.reshape(SPECIAL_ROWS, D_MODEL), xs_p.reshape(n_ptiles, TILE, D_MODEL),
        wuz_t, *ssm_ops, sre, sim, ssm_w_glu.astype(F32), ssm_w_out.astype(F32))
    n_sc = n_sample_rows // CHUNK
    o_sp = _glu_out(gt_sp, szt_sp, wg_t, b_glu, wo_t, h1_sp[:, :, 1:1 + n_sc], nwf)
    o_p = _glu_out(gt_p, szt_p, wg_t, b_glu, wo_t, h1_p, nwf)

    y_prompt = _to_rows(o_p)
    y_sample = o_sp.reshape(n_seq, seq_len, D_MODEL)
    new_conv_prompt = tail_p[-1, 6:8].reshape(1, 1, 2, D_INNER)
    cv_s = cv_sp[N_META:N_META + n_sample_rows].reshape(n_seq, seq_len, D_INNER)
    new_conv_sample = cv_s[:, seq_len - 2:].reshape(1, n_seq, 2, D_INNER)
    p_shape = (1, 1, N_GROUPS, SSM_STATE)
    s_shape = (1, n_seq, N_GROUPS, SSM_STATE)
    return (y_prompt, y_sample, new_conv_prompt, new_conv_sample,
            hp_re[0].reshape(p_shape), hp_im[0].reshape(p_shape),
            hs_re.reshape(s_shape), hs_im.reshape(s_shape))
```
